```python
import jax
import jax.numpy as jnp
from jax import lax
import numpy as np

D_MODEL = 1024
BATCH = 16
SEQ = 2048
DEPTH = 1

N_META = 16
CHUNK = 64
N_PAD = CHUNK - N_META
EPS = 1e-6

HG_HEADS = 4
HG_EXPAND = 128
HG_VDIM = 128
HG_FDIM = HG_HEADS * HG_EXPAND
HG_WIDTH = HG_HEADS * HG_VDIM

ML_HEADS = 4
ML_QKDIM = 64
ML_VDIM = 128
ML_WIDTH = ML_HEADS * ML_VDIM
ML_CONV = 4

N_BRANCH = 2
IN_SPLITS = (
    HG_FDIM,
    2 * HG_FDIM,
    2 * HG_FDIM + HG_WIDTH,
    2 * HG_FDIM + 2 * HG_WIDTH,
    2 * HG_FDIM + 2 * HG_WIDTH + ML_WIDTH,
    2 * HG_FDIM + 2 * HG_WIDTH + 2 * ML_WIDTH,
    2 * HG_FDIM + 2 * HG_WIDTH + 2 * ML_WIDTH + 2 * ML_HEADS,
)
IN_TOTAL = IN_SPLITS[-1] + N_BRANCH * D_MODEL

N_EXPERTS = 32
TOP_K = 4
D_FF = 1024
SWIGLU_LIMIT = 7.0
SWIGLU_ALPHA = 1.702
MOE_BLOCK = 128

kernel_name = 'hybrid_hgrn2_mlstm_moe'


def _rms(x):
    xf = x.astype(jnp.float32)
    y = xf * lax.rsqrt(jnp.mean(xf * xf, axis=-1, keepdims=True) + EPS)
    return y.astype(x.dtype)


def rmsnorm(x, w):
    return _rms(x) * w


def head_rmsnorm(o, w):
    b, t, h, d = o.shape
    return _rms(o).reshape(b, t, h * d) * w


def to_chunks(a):
    b, t = a.shape[:2]
    a = a.reshape((b, t // CHUNK, CHUNK) + a.shape[2:])
    perm = (1, 0, 3, 2) + tuple(range(4, a.ndim))
    return a.transpose(perm)


def from_chunks(o):
    nc, b, h, l, d = o.shape
    return o.transpose(1, 0, 3, 2, 4).reshape(b, nc * l, h, d)


def causal_depthwise_conv(u, w, bias):
    c = u.shape[-1]
    y = lax.conv_general_dilated(
        u, w[:, None, :].astype(u.dtype), window_strides=(1,),
        padding=[(ML_CONV - 1, 0)], dimension_numbers=('NWC', 'WIO', 'NWC'),
        feature_group_count=c)
    return y + bias


def hgrn2_chunkwise(q, k, v, log_f):
    f32 = jnp.float32
    qc, kc, vc, gc = (to_chunks(a.astype(f32)) for a in (q, k, v, log_f))
    cum = jnp.cumsum(gc, axis=-2)
    causal = jnp.tril(jnp.ones((CHUNK, CHUNK), bool))[:, :, None]

    def step(state, inp):
        q_, k_, v_, b_ = inp
        diff = b_[..., :, None, :] - b_[..., None, :, :]
        decay = jnp.exp(jnp.where(causal, diff, -jnp.inf))
        scores = jnp.einsum('bhtd,bhsd,bhtsd->bhts', q_, k_, decay)
        o = (jnp.einsum('bhts,bhsv->bhtv', scores, v_)
             + jnp.einsum('bhtd,bhdv->bhtv', q_ * jnp.exp(b_), state))
        b_last = b_[..., -1:, :]
        new_state = (jnp.exp(b_last[..., 0, :])[..., None] * state
                     + jnp.einsum('bhsd,bhsv->bhdv', k_ * jnp.exp(b_last - b_), v_))
        return new_state, o

    bsz, _, heads, dk = q.shape
    state0 = jnp.zeros((bsz, heads, dk, v.shape[-1]), f32)
    _, o = lax.scan(step, state0, (qc, kc, vc, cum))
    return from_chunks(o)


def mlstm_chunkwise(q, k, v, log_i, log_f):
    f32 = jnp.float32
    dk = q.shape[-1]
    qc, kc, vc = (to_chunks(a.astype(f32)) for a in (q * (dk ** -0.5), k, v))
    ic, fc = (to_chunks(a.astype(f32)) for a in (log_i, log_f))
    cum = jnp.cumsum(fc, axis=-1)
    causal = jnp.tril(jnp.ones((CHUNK, CHUNK), bool))

    def step(carry, inp):
        s, n, m = carry
        q_, k_, v_, li, b_ = inp
        d = jnp.where(causal, b_[..., :, None] - b_[..., None, :] + li[..., None, :], -jnp.inf)
        a = b_ + m[..., None]
        m_t = jnp.maximum(a, jnp.max(d, axis=-1))
        w_intra = jnp.exp(d - m_t[..., None])
        w_inter = jnp.exp(a - m_t)
        qk = jnp.einsum('bhtd,bhsd->bhts', q_, k_) * w_intra
        num = (jnp.einsum('bhts,bhsv->bhtv', qk, v_)
               + w_inter[..., None] * jnp.einsum('bhtd,bhdv->bhtv', q_, s))
        den = jnp.sum(qk, axis=-1) + w_inter * jnp.einsum('bhtd,bhd->bht', q_, n)
        h = num / jnp.maximum(jnp.abs(den), jnp.exp(-m_t))[..., None]
        g_last = b_[..., -1]
        e = g_last[..., None] - b_ + li
        m_new = jnp.maximum(g_last + m, jnp.max(e, axis=-1))
        w_s = jnp.exp(e - m_new[..., None])
        w_p = jnp.exp(g_last + m - m_new)
        s_new = w_p[..., None, None] * s + jnp.einsum('bhs,bhsd,bhsv->bhdv', w_s, k_, v_)
        n_new = w_p[..., None] * n + jnp.einsum('bhs,bhsd->bhd', w_s, k_)
        return (s_new, n_new, m_new), h

    bsz, _, heads, _ = q.shape
    carry0 = (jnp.zeros((bsz, heads, dk, v.shape[-1]), f32),
              jnp.zeros((bsz, heads, dk), f32),
              jnp.zeros((bsz, heads), f32))
    _, h = lax.scan(step, carry0, (qc, kc, vc, ic, cum))
    return from_chunks(h)


def moe_ffn(xn, router_w, router_b, w_gu, b_gu, w_dn, b_dn):
    bsz, t, d = xn.shape
    n_tok = bsz * t
    xt = xn.reshape(n_tok, d)
    logits = (xt @ router_w + router_b).astype(jnp.float32)
    top_v, top_e = lax.top_k(logits, TOP_K)
    gate_w = jax.nn.softmax(top_v, axis=-1)
    n_assign = n_tok * TOP_K
    e_flat = top_e.reshape(n_assign).astype(jnp.int32)
    tok_flat = jnp.arange(n_assign, dtype=jnp.int32) // TOP_K
    w_flat = gate_w.reshape(n_assign)
    order = jnp.argsort(e_flat)
    e_sorted = e_flat[order]
    tok_sorted = tok_flat[order]
    w_sorted = w_flat[order]
    counts = jnp.bincount(e_flat, length=N_EXPERTS).astype(jnp.int32)
    starts = jnp.cumsum(counts) - counts
    padded = ((counts + MOE_BLOCK - 1) // MOE_BLOCK) * MOE_BLOCK
    pend = jnp.cumsum(padded)
    pstart = pend - padded
    dest = pstart[e_sorted] + (jnp.arange(n_assign, dtype=jnp.int32) - starts[e_sorted])
    n_blocks = -(-n_assign // MOE_BLOCK) + N_EXPERTS
    n_rows = n_blocks * MOE_BLOCK
    row_tok = jnp.full((n_rows,), n_tok, jnp.int32).at[dest].set(tok_sorted)
    row_w = jnp.zeros((n_rows,), jnp.float32).at[dest].set(w_sorted)
    block_e = jnp.minimum(
        jnp.searchsorted(pend, jnp.arange(n_blocks, dtype=jnp.int32) * MOE_BLOCK, side='right'),
        N_EXPERTS - 1)
    x_pad = jnp.concatenate([xt, jnp.zeros((1, d), xt.dtype)], axis=0)
    xb = x_pad[row_tok].reshape(n_blocks, MOE_BLOCK, d)

    def expert_block(args):
        xblk, e = args
        gu = xblk @ w_gu[e] + b_gu[e]
        gate = jnp.minimum(gu[:, 0::2], SWIGLU_LIMIT)
        up = jnp.clip(gu[:, 1::2], -SWIGLU_LIMIT, SWIGLU_LIMIT)
        act = (up + 1.0) * gate * jax.nn.sigmoid(SWIGLU_ALPHA * gate)
        return act @ w_dn[e] + b_dn[e]

    yb = lax.map(expert_block, (xb, block_e)).reshape(n_rows, d)
    y = jnp.zeros((n_tok + 1, d), jnp.float32).at[row_tok].add(
        yb.astype(jnp.float32) * row_w[:, None])
    return y[:n_tok].reshape(bsz, t, d).astype(xn.dtype)


def setup_inputs(seed: int = 0) -> dict:
    key = jax.random.key(seed)
    ks = jax.random.split(key, 32)
    f32 = jnp.float32

    def nrm(k, shape, scale):
        return jax.random.normal(k, shape, f32) * scale

    def gain(k, shape):
        return 1.0 + 0.02 * jax.random.normal(k, shape, f32)

    ml_gate_b = jnp.concatenate([
        nrm(ks[10], (DEPTH, ML_HEADS), 0.1),
        jnp.linspace(3.0, 6.0, ML_HEADS, dtype=f32)[None, :] + nrm(ks[11], (DEPTH, ML_HEADS), 0.1),
    ], axis=-1)
    return {
        'x': nrm(ks[0], (BATCH, SEQ, D_MODEL), 1.0),
        'meta_tokens': nrm(ks[1], (N_META, D_MODEL), 1.0),
        'hg_lb_logits': nrm(ks[2], (DEPTH + 1, HG_FDIM), 0.5),
        'norm_mix': gain(ks[3], (DEPTH, D_MODEL)),
        'w_in': nrm(ks[4], (DEPTH, D_MODEL, IN_TOTAL), D_MODEL ** -0.5),
        'hg_norm': gain(ks[5], (DEPTH, HG_WIDTH)),
        'ml_conv_w': nrm(ks[6], (DEPTH, ML_CONV, ML_WIDTH), ML_CONV ** -0.5),
        'ml_conv_b': nrm(ks[7], (DEPTH, ML_WIDTH), 0.01),
        'ml_wq': nrm(ks[8], (DEPTH, ML_HEADS, ML_VDIM, ML_QKDIM), ML_VDIM ** -0.5),
        'ml_wk': nrm(ks[9], (DEPTH, ML_HEADS, ML_VDIM, ML_QKDIM), ML_VDIM ** -0.5),
        'ml_wv': nrm(ks[12], (DEPTH, ML_HEADS, ML_VDIM, ML_VDIM), ML_VDIM ** -0.5),
        'ml_gate_b': ml_gate_b,
        'ml_norm': gain(ks[13], (DEPTH, ML_WIDTH)),
        'ml_skip': gain(ks[14], (DEPTH, ML_WIDTH)),
        'w_branch_hg': nrm(ks[15], (DEPTH, HG_WIDTH, D_MODEL), HG_WIDTH ** -0.5),
        'w_branch_ml': nrm(ks[16], (DEPTH, ML_WIDTH, D_MODEL), ML_WIDTH ** -0.5),
        'w_out': nrm(ks[17], (DEPTH, D_MODEL, D_MODEL), D_MODEL ** -0.5),
        'norm_ffn': gain(ks[18], (DEPTH, D_MODEL)),
        'router_w': nrm(ks[19], (DEPTH, D_MODEL, N_EXPERTS), D_MODEL ** -0.5),
        'router_b': nrm(ks[20], (DEPTH, N_EXPERTS), 0.01),
        'exp_w_gu': nrm(ks[21], (DEPTH, N_EXPERTS, D_MODEL, 2 * D_FF), D_MODEL ** -0.5),
        'exp_b_gu': nrm(ks[22], (DEPTH, N_EXPERTS, 2 * D_FF), 0.01),
        'exp_w_down': nrm(ks[23], (DEPTH, N_EXPERTS, D_FF, D_MODEL), D_FF ** -0.5),
        'exp_b_down': nrm(ks[24], (DEPTH, N_EXPERTS, D_MODEL), 0.01),
        'norm_final': gain(ks[25], (D_MODEL,)),
    }


def reference(x, meta_tokens, hg_lb_logits, norm_mix, w_in, hg_norm, ml_conv_w, ml_conv_b,
              ml_wq, ml_wk, ml_wv, ml_gate_b, ml_norm, ml_skip, w_branch_hg, w_branch_ml,
              w_out, norm_ffn, router_w, router_b, exp_w_gu, exp_b_gu, exp_w_down,
              exp_b_down, norm_final):
    f32 = jnp.float32
    bsz = x.shape[0]
    dt = x.dtype
    h = jnp.concatenate([
        jnp.zeros((bsz, N_PAD, D_MODEL), dt),
        jnp.broadcast_to(meta_tokens.astype(dt)[None], (bsz, N_META, D_MODEL)),
        x,
    ], axis=1)
    t = h.shape[1]
    valid = (jnp.arange(t) >= N_PAD)[None, :, None]
    lower_bounds = jnp.cumsum(jax.nn.softmax(hg_lb_logits.astype(f32), axis=0), axis=0)

    for l in range(DEPTH):
        xn = rmsnorm(h, norm_mix[l])
        proj = xn @ w_in[l]
        hq, hf, hi, hg, mm, mo, mif, gates = jnp.split(proj, IN_SPLITS, axis=-1)

        lb = lower_bounds[l]
        f = lb + (1.0 - lb) * jax.nn.sigmoid(hf.astype(f32))
        f = jnp.where(valid, f, 1.0)
        q_h = jax.nn.silu(hq).reshape(bsz, t, HG_HEADS, HG_EXPAND)
        k_h = (1.0 - f).reshape(bsz, t, HG_HEADS, HG_EXPAND)
        logf_h = jnp.log(f).reshape(bsz, t, HG_HEADS, HG_EXPAND)
        v_h = hi.reshape(bsz, t, HG_HEADS, HG_VDIM)
        o_h = hgrn2_chunkwise(q_h, k_h, v_h, logf_h).astype(dt)
        y_hg = head_rmsnorm(o_h, hg_norm[l]) * jax.nn.silu(hg)

        mm = jnp.where(valid, mm, 0.0)
        c = jax.nn.silu(causal_depthwise_conv(mm, ml_conv_w[l], ml_conv_b[l]))
        ch = c.reshape(bsz, t, ML_HEADS, ML_VDIM)
        mh = mm.reshape(bsz, t, ML_HEADS, ML_VDIM)
        q_m = jnp.einsum('bthc,hcd->bthd', ch, ml_wq[l])
        k_m = jnp.einsum('bthc,hcd->bthd', ch, ml_wk[l])
        v_m = jnp.einsum('bthc,hcd->bthd', mh, ml_wv[l])
        gi, gf = jnp.split(mif.astype(f32) + ml_gate_b[l].astype(f32), 2, axis=-1)
        log_i = jnp.where(valid, gi, -jnp.inf)
        log_fm = jnp.where(valid, jax.nn.log_sigmoid(gf), 0.0)
        o_m = mlstm_chunkwise(q_m, k_m, v_m, log_i, log_fm).astype(dt)
        y_ml = (head_rmsnorm(o_m, ml_norm[l]) + ml_skip[l] * c) * jax.nn.sigmoid(mo)

        g = jax.nn.sigmoid(gates).reshape(bsz, t, N_BRANCH, D_MODEL)
        merged = g[:, :, 0] * (y_hg @ w_branch_hg[l]) + g[:, :, 1] * (y_ml @ w_branch_ml[l])
        h = h + merged @ w_out[l]

        ffn_out = moe_ffn(rmsnorm(h[:, N_PAD:], norm_ffn[l]), router_w[l], router_b[l],
                          exp_w_gu[l], exp_b_gu[l], exp_w_down[l], exp_b_down[l])
        h = h.at[:, N_PAD:].add(ffn_out)

    return rmsnorm(h[:, N_PAD + N_META:], norm_final)
```

```python
import functools

import numpy as np
import jax
import jax.numpy as jnp
from jax import lax
from jax.experimental import pallas as pl
from jax.experimental.pallas import tpu as pltpu

F32 = jnp.float32
BF16 = jnp.bfloat16

N_META = 16
CHUNK = 64
N_PAD = CHUNK - N_META
EPS = 1e-6

N_HEADS = 4
HG_DK = 128
HG_DV = 128
HG_W = N_HEADS * HG_DV
ML_DK = 64
ML_DV = 128
ML_W = N_HEADS * ML_DV
ML_CONV = 4
TOP_K = 4
SWIGLU_LIMIT = 7.0
SWIGLU_ALPHA = 1.702

LANES = 128
VMEM_LIMIT_BYTES = 56 * 1024 * 1024

HG_LEVELS = (32, 16, 8, 4, 2, 1)


def _cparams(sem):
    return pltpu.CompilerParams(dimension_semantics=sem, vmem_limit_bytes=VMEM_LIMIT_BYTES)


def _sigmoid(x):
    return 1.0 / (1.0 + jnp.exp(-x))


def _split3(x):
    x1 = x.astype(BF16)
    r1 = x - x1.astype(F32)
    x2 = r1.astype(BF16)
    x3 = (r1 - x2.astype(F32)).astype(BF16)
    return x1, x2, x3


def _dot(a, b):
    return jnp.dot(a, b, preferred_element_type=F32)


def _dot_nt(a, b):
    return lax.dot_general(a, b, (((1,), (1,)), ((), ())), preferred_element_type=F32)


def _dot_exact_lhs(m_bf16, x):
    x1, x2, x3 = _split3(x)
    return _dot(m_bf16, x1) + _dot(m_bf16, x2) + _dot(m_bf16, x3)


def _in_proj_kernel(h_ref, nw_ref, wa_ref, wg_ref, wm_ref,
                    q_ref, f_ref, i_ref, g_ref, mm_ref, mo_ref, gates_ref, mif_ref):
    x = h_ref[...]
    ms = jnp.mean(x * x, axis=-1, keepdims=True)
    xb = ((x * lax.rsqrt(ms + EPS)) * nw_ref[...]).astype(BF16)
    outs = (q_ref, f_ref, i_ref, g_ref, mm_ref, mo_ref)
    for j, o_ref in enumerate(outs):
        o_ref[...] = _dot(xb, wa_ref[:, j * HG_W:(j + 1) * HG_W]).astype(o_ref.dtype)
    d_model = gates_ref.shape[1] // 2
    for j in range(2):
        gates_ref[:, j * d_model:(j + 1) * d_model] = _dot(
            xb, wg_ref[:, j * d_model:(j + 1) * d_model]).astype(gates_ref.dtype)
    mif_ref[...] = _dot(xb, wm_ref[...])


def _in_proj(h2d, norm_w, w_a, w_g, w_m, tm):
    m, d = h2d.shape
    row = lambda n: pl.BlockSpec((tm, n), lambda i: (i, 0))
    full = lambda a: pl.BlockSpec(a.shape, lambda i: (0, 0))
    out_shape = [
        jax.ShapeDtypeStruct((m, HG_W), BF16),
        jax.ShapeDtypeStruct((m, HG_W), F32),
        jax.ShapeDtypeStruct((m, HG_W), BF16),
        jax.ShapeDtypeStruct((m, HG_W), BF16),
        jax.ShapeDtypeStruct((m, ML_W), BF16),
        jax.ShapeDtypeStruct((m, ML_W), BF16),
        jax.ShapeDtypeStruct((m, 2 * d), BF16),
        jax.ShapeDtypeStruct((m, LANES), F32),
    ]
    return pl.pallas_call(
        _in_proj_kernel,
        grid=(m // tm,),
        in_specs=[row(d), full(norm_w), full(w_a), full(w_g), full(w_m)],
        out_specs=[row(s.shape[1]) for s in out_shape],
        out_shape=out_shape,
        compiler_params=_cparams(("parallel",)),
        name="in_proj",
    )(h2d, norm_w, w_a, w_g, w_m)


def _hgrn_matrices():
    n = CHUNK
    t = np.arange(n)[:, None]
    u = np.arange(n)[None, :]
    blocks = [(u <= t), (u > t)]
    for m in HG_LEVELS:
        r = (t // (2 * m)) * (2 * m) + m - 1
        upper = (t % (2 * m)) >= m
        blocks.append(upper & (u > r) & (u <= t))
        blocks.append((~upper) & (u > t) & (u <= r))
    return np.concatenate(blocks, axis=0).astype(np.float32)


def _hgrn2_kernel(q_ref, f_ref, v_ref, g_ref, mat_ref, lb_ref, nw_ref, y_ref, st_ref):
    c = pl.program_id(1)
    bb = q_ref.shape[0]

    @pl.when(c == 0)
    def _():
        st_ref[...] = jnp.zeros_like(st_ref)

    pos = c * CHUNK + lax.broadcasted_iota(jnp.int32, (CHUNK, 1), 0)
    valid = pos >= N_PAD
    ti = lax.broadcasted_iota(jnp.int32, (CHUNK, CHUNK), 0)
    si = lax.broadcasted_iota(jnp.int32, (CHUNK, CHUNK), 1)
    diag_mask = ti == si
    level_masks = []
    for m in HG_LEVELS:
        same_pair = (ti & ~(2 * m - 1)) == (si & ~(2 * m - 1))
        level_masks.append(same_pair & ((ti & m) != 0) & ((si & m) == 0))

    lb = lb_ref[...]
    mat = mat_ref[...]
    for b in range(bb):
        hf = f_ref[b]
        f = lb + (1.0 - lb) * _sigmoid(hf)
        f = jnp.where(valid, f, 1.0)
        logf = jnp.log(f)
        k_all = 1.0 - f
        hq = q_ref[b].astype(F32)
        q_all = hq * _sigmoid(hq)
        expo = _dot_exact_lhs(mat, logf)
        e_b = jnp.exp(expo[0:CHUNK])
        e_bl = jnp.exp(expo[CHUNK:2 * CHUNK])
        v_all = v_ref[b]
        hg = g_ref[b].astype(F32)
        gate = hg * _sigmoid(hg)
        for h in range(N_HEADS):
            sl = slice(h * HG_DK, (h + 1) * HG_DK)
            q = q_all[:, sl]
            k = k_all[:, sl]
            scores = jnp.where(diag_mask, _dot_nt(q.astype(BF16), k.astype(BF16)), 0.0)
            for li, mask in enumerate(level_masks):
                r0 = (2 + 2 * li) * CHUNK
                qd = (q * jnp.exp(expo[r0:r0 + CHUNK, sl])).astype(BF16)
                kd = (k * jnp.exp(expo[r0 + CHUNK:r0 + 2 * CHUNK, sl])).astype(BF16)
                scores = jnp.where(mask, _dot_nt(qd, kd), scores)
            v = v_all[:, sl]
            st = st_ref[b, h]
            o = _dot(scores.astype(BF16), v) + _dot_nt((q * e_b[:, sl]).astype(BF16), st.astype(BF16))
            kl = (k * e_bl[:, sl]).astype(BF16)
            vt = v.astype(F32).T.astype(BF16)
            st_ref[b, h] = e_b[CHUNK - 1:CHUNK, sl] * st + _dot(vt, kl)
            ms = jnp.mean(o * o, axis=-1, keepdims=True)
            y = (o * lax.rsqrt(ms + EPS)) * nw_ref[:, sl] * gate[:, sl]
            y_ref[b, :, sl] = y.astype(y_ref.dtype)


def _hgrn2(hq, hf, hi, hg, mats, lb, norm_w, bb):
    bsz, t, _ = hq.shape
    nc = t // CHUNK
    blk = pl.BlockSpec((bb, CHUNK, HG_W), lambda i, c: (i, c, 0))
    full = lambda a: pl.BlockSpec(a.shape, lambda i, c: (0, 0))
    return pl.pallas_call(
        _hgrn2_kernel,
        grid=(bsz // bb, nc),
        in_specs=[blk, blk, blk, blk, full(mats), full(lb), full(norm_w)],
        out_specs=blk,
        out_shape=jax.ShapeDtypeStruct((bsz, t, HG_W), BF16),
        scratch_shapes=[pltpu.VMEM((bb, N_HEADS, HG_DV, HG_DK), F32)],
        compiler_params=_cparams(("parallel", "arbitrary")),
        name="hgrn2",
    )(hq, hf, hi, hg, mats, lb, norm_w)


def _log_sigmoid(x):
    return jnp.minimum(x, 0.0) - jnp.log(1.0 + jnp.exp(-jnp.abs(x)))


def _mlstm_kernel(mm_ref, mo_ref, gc_ref, gr_ref, tri_ref, cw_ref, cb_ref, wq_ref, wk_ref, wv_ref,
                  gbc_ref, gbr_ref, nw_ref, sk_ref, y_ref, s_ref, m_ref, tail_ref):
    c = pl.program_id(1)
    bb = mm_ref.shape[0]

    @pl.when(c == 0)
    def _():
        s_ref[...] = jnp.zeros_like(s_ref)
        m_ref[...] = jnp.zeros_like(m_ref)
        tail_ref[...] = jnp.zeros_like(tail_ref)

    pos_c = c * CHUNK + lax.broadcasted_iota(jnp.int32, (CHUNK, 1), 0)
    valid_c = pos_c >= N_PAD
    pos_r = c * CHUNK + lax.broadcasted_iota(jnp.int32, (1, CHUNK), 1)
    valid_r = pos_r >= N_PAD
    ti = lax.broadcasted_iota(jnp.int32, (CHUNK, CHUNK), 0)
    si = lax.broadcasted_iota(jnp.int32, (CHUNK, CHUNK), 1)
    causal = si <= ti
    tri = tri_ref[...]
    ones_v = jnp.ones((CHUNK, ML_DV), BF16)
    neg_inf = -jnp.inf

    for b in range(bb):
        mm = jnp.where(valid_c, mm_ref[b].astype(F32), 0.0)
        ext = jnp.concatenate([tail_ref[b], mm], axis=0)
        tail_ref[b] = mm[CHUNK - 8:CHUNK]
        conv = cb_ref[...]
        for j in range(ML_CONV):
            off = 8 - (ML_CONV - 1) + j
            conv = conv + cw_ref[j:j + 1, :] * ext[off:off + CHUNK]
        cact = conv * _sigmoid(conv)
        cact_b = cact.astype(BF16)
        mm_b = mm.astype(BF16)

        gcol = gc_ref[b] + gbc_ref[...]
        li_col = jnp.where(valid_c, gcol, neg_inf)
        lf_col = jnp.where(valid_c, _log_sigmoid(gcol), 0.0)
        b_col = _dot_exact_lhs(tri, lf_col)
        grow = gr_ref[b, 0] + gbr_ref[...]
        li_row = jnp.where(valid_r, grow, neg_inf)
        lf_row = jnp.where(valid_r, _log_sigmoid(grow), 0.0)
        r1, r2, r3 = _split3(lf_row)
        b_row = _dot_nt(r1, tri) + _dot_nt(r2, tri) + _dot_nt(r3, tri)

        mo = mo_ref[b].astype(F32)
        ogate = _sigmoid(mo)
        for h in range(N_HEADS):
            sl = slice(h * ML_DV, (h + 1) * ML_DV)
            q = (_dot(cact_b[:, sl], wq_ref[h]) * (ML_DK ** -0.5)).astype(BF16)
            k = _dot(cact_b[:, sl], wk_ref[h])
            v = _dot(mm_b[:, sl], wv_ref[h]).astype(BF16)
            v_aug = jnp.concatenate([v, ones_v], axis=1)
            bc = b_col[:, N_HEADS + h:N_HEADS + h + 1]
            lic = li_col[:, h:h + 1]
            br = b_row[N_HEADS + h:N_HEADS + h + 1, :]
            lir = li_row[h:h + 1, :]
            m_prev = m_ref[b, h][0:1, 0:1]
            d = jnp.where(causal, bc + (lir - br), neg_inf)
            a = bc + m_prev
            m_t = jnp.maximum(a, jnp.max(d, axis=-1, keepdims=True))
            w_intra = jnp.exp(d - m_t)
            w_inter = jnp.exp(a - m_t)
            qk = _dot_nt(q, k.astype(BF16)) * w_intra
            s_aug = s_ref[b, h]
            numden = _dot(qk.astype(BF16), v_aug) + w_inter * _dot(q, s_aug.astype(BF16))
            num = numden[:, :ML_DV]
            den = numden[:, ML_DV:]
            o = num / jnp.maximum(jnp.abs(den), jnp.exp(-m_t))
            g_last = bc[CHUNK - 1:CHUNK, :]
            e = g_last - bc + lic
            m_new = jnp.maximum(g_last + m_prev, jnp.max(e, axis=0, keepdims=True))
            w_s = jnp.exp(e - m_new)
            w_p = jnp.exp(g_last + m_prev - m_new)
            kw_t = (k * w_s).T.astype(BF16)
            s_ref[b, h] = w_p * s_aug + _dot(kw_t, v_aug)
            m_ref[b, h] = jnp.broadcast_to(m_new, m_ref.shape[2:])
            ms = jnp.mean(o * o, axis=-1, keepdims=True)
            y = ((o * lax.rsqrt(ms + EPS)) * nw_ref[:, sl] + sk_ref[:, sl] * cact[:, sl]) * ogate[:, sl]
            y_ref[b, :, sl] = y.astype(y_ref.dtype)


def _mlstm(mm, mo, g_col, g_row, tri, conv_w, conv_b, wq, wk, wv, gb_col, gb_row, norm_w, skip, bb):
    bsz, t, _ = mm.shape
    nc = t // CHUNK
    blk = pl.BlockSpec((bb, CHUNK, ML_W), lambda i, c: (i, c, 0))
    gcb = pl.BlockSpec((bb, CHUNK, LANES), lambda i, c: (i, c, 0))
    grb = pl.BlockSpec((bb, 1, 8, CHUNK), lambda i, c: (i, c, 0, 0))

    def full(a):
        nd = a.ndim
        return pl.BlockSpec(a.shape, lambda i, c: (0,) * nd)

    params = (tri, conv_w, conv_b, wq, wk, wv, gb_col, gb_row, norm_w, skip)
    return pl.pallas_call(
        _mlstm_kernel,
        grid=(bsz // bb, nc),
        in_specs=[blk, blk, gcb, grb] + [full(p) for p in params],
        out_specs=blk,
        out_shape=jax.ShapeDtypeStruct((bsz, t, ML_W), BF16),
        scratch_shapes=[
            pltpu.VMEM((bb, N_HEADS, ML_DK, 2 * ML_DV), F32),
            pltpu.VMEM((bb, N_HEADS, 8, LANES), F32),
            pltpu.VMEM((bb, 8, ML_W), F32),
        ],
        compiler_params=_cparams(("parallel", "arbitrary")),
        name="mlstm",
    )(mm, mo, g_col, g_row, *params)


def _merge_route_kernel(h_ref, yh_ref, ym_ref, gates_ref, wbh_ref, wbm_ref, wo_ref, nf_ref,
                        rw_ref, rb_ref, h2_ref, xn_ref, te_ref, tw_ref, *, n_experts):
    d = h_ref.shape[1]
    g0 = _sigmoid(gates_ref[:, :d].astype(F32))
    g1 = _sigmoid(gates_ref[:, d:].astype(F32))
    merged = g0 * _dot(yh_ref[...], wbh_ref[...]) + g1 * _dot(ym_ref[...], wbm_ref[...])
    h2 = h_ref[...] + _dot(merged.astype(BF16), wo_ref[...])
    h2_ref[...] = h2
    ms = jnp.mean(h2 * h2, axis=-1, keepdims=True)
    xn = (h2 * lax.rsqrt(ms + EPS)) * nf_ref[...]
    xn_ref[...] = xn
    x1, x2, _ = _split3(xn)
    logits = (_dot(x1, rw_ref[0]) + _dot(x1, rw_ref[1]) + _dot(x2, rw_ref[0])) + rb_ref[...]
    lane = lax.broadcasted_iota(jnp.int32, logits.shape, 1)
    work = jnp.where(lane < n_experts, logits, -jnp.inf)
    vals, idxs = [], []
    for _ in range(TOP_K):
        vmax = jnp.max(work, axis=-1, keepdims=True)
        imax = jnp.min(jnp.where(work == vmax, lane, LANES), axis=-1, keepdims=True)
        vals.append(vmax)
        idxs.append(imax)
        work = jnp.where(lane == imax, -jnp.inf, work)
    exps = [jnp.exp(v - vals[0]) for v in vals]
    tot = exps[0] + exps[1] + exps[2] + exps[3]
    te = jnp.zeros(logits.shape, jnp.int32)
    tw = jnp.zeros(logits.shape, F32)
    for kk in range(TOP_K):
        te = jnp.where(lane == kk, idxs[kk], te)
        tw = jnp.where(lane == kk, exps[kk] / tot, tw)
    te_ref[...] = te[:, :TOP_K]
    tw_ref[...] = tw[:, :TOP_K]


def _merge_route(h2d, y_hg, y_ml, gates, wbh, wbm, wo, norm_ffn, rw_split, rb, n_experts, tm):
    m, d = h2d.shape
    row = lambda n: pl.BlockSpec((tm, n), lambda i: (i, 0))

    def full(a):
        nd = a.ndim
        return pl.BlockSpec(a.shape, lambda i: (0,) * nd)

    return pl.pallas_call(
        functools.partial(_merge_route_kernel, n_experts=n_experts),
        grid=(m // tm,),
        in_specs=[row(d), row(HG_W), row(ML_W), row(2 * d), full(wbh), full(wbm), full(wo),
                  full(norm_ffn), full(rw_split), full(rb)],
        out_specs=[row(d), row(d), row(TOP_K), row(TOP_K)],
        out_shape=[
            jax.ShapeDtypeStruct((m, d), F32),
            jax.ShapeDtypeStruct((m, d), F32),
            jax.ShapeDtypeStruct((m, TOP_K), jnp.int32),
            jax.ShapeDtypeStruct((m, TOP_K), F32),
        ],
        compiler_params=_cparams(("parallel",)),
        name="merge_route",
    )(h2d, y_hg, y_ml, gates, wbh, wbm, wo, norm_ffn, rw_split, rb)


def _experts_kernel(be_ref, nu_ref, idx_ref, idxn_ref, rw_ref, x_hbm, wg_ref, wu_ref, bg_ref, bu_ref,
                    wd_ref, bd_ref, y_ref, xbuf, sem):
    i = pl.program_id(0)
    n_used = nu_ref[0]
    tm = xbuf.shape[1]
    slot = i % 2

    def start_gather(src_idx_ref, dst_slot):
        def body(r, carry):
            pltpu.make_async_copy(
                x_hbm.at[pl.ds(src_idx_ref[0, 0, r], 1), :],
                xbuf.at[dst_slot, pl.ds(r, 1), :],
                sem.at[dst_slot]).start()
            return carry
        lax.fori_loop(0, tm, body, 0, unroll=8)

    @pl.when(i == 0)
    def _():
        start_gather(idx_ref, 0)

    @pl.when(i + 1 < n_used)
    def _():
        start_gather(idxn_ref, 1 - slot)

    @pl.when(i < n_used)
    def _():
        pltpu.make_async_copy(xbuf.at[slot], xbuf.at[slot], sem.at[slot]).wait()
        xb = xbuf[slot].astype(BF16)
        g = _dot(xb, wg_ref[0]) + bg_ref[0]
        u = _dot(xb, wu_ref[0]) + bu_ref[0]
        gate = jnp.minimum(g, SWIGLU_LIMIT)
        up = jnp.clip(u, -SWIGLU_LIMIT, SWIGLU_LIMIT)
        act = (up + 1.0) * gate * _sigmoid(SWIGLU_ALPHA * gate)
        y = _dot(act.astype(BF16), wd_ref[0]) + bd_ref[0]
        y_ref[...] = y * rw_ref[...]

    @pl.when(i >= n_used)
    def _():
        y_ref[...] = jnp.zeros_like(y_ref)


def _experts(block_e, n_used, row_src, row_w, xn, w_g, w_u, b_g, b_u, w_d, b_d, tm):
    n_blocks = block_e.shape[0]
    d = xn.shape[1]
    dff = w_g.shape[2]
    idx3 = row_src.reshape(n_blocks, 1, tm)
    last = n_blocks - 1
    smem_blk = lambda f: pl.BlockSpec((1, 1, tm), f, memory_space=pltpu.SMEM)
    wspec = lambda k, n: pl.BlockSpec((1, k, n), lambda i, be, nu: (be[i], 0, 0))
    grid_spec = pltpu.PrefetchScalarGridSpec(
        num_scalar_prefetch=2,
        grid=(n_blocks,),
        in_specs=[
            smem_blk(lambda i, be, nu: (i, 0, 0)),
            smem_blk(lambda i, be, nu: (jnp.minimum(i + 1, last), 0, 0)),
            pl.BlockSpec((tm, 1), lambda i, be, nu: (i, 0)),
            pl.BlockSpec(memory_space=pl.ANY),
            wspec(d, dff), wspec(d, dff), wspec(1, dff), wspec(1, dff),
            wspec(dff, d), wspec(1, d),
        ],
        out_specs=pl.BlockSpec((tm, d), lambda i, be, nu: (i, 0)),
        scratch_shapes=[pltpu.VMEM((2, tm, d), F32), pltpu.SemaphoreType.DMA((2,))],
    )
    return pl.pallas_call(
        _experts_kernel,
        grid_spec=grid_spec,
        out_shape=jax.ShapeDtypeStruct((n_blocks * tm, d), F32),
        compiler_params=_cparams(("arbitrary",)),
        name="experts",
    )(block_e, n_used, idx3, idx3, row_w, xn, w_g, w_u, b_g, b_u, w_d, b_d)


def _combine_kernel(idx_ref, idxn_ref, hsrc_ref, hsrcn_ref, y_hbm, h_hbm, nw_ref, o_ref, ybuf, hbuf, sem):
    i = pl.program_id(0)
    n = pl.num_programs(0)
    tt = hbuf.shape[1]
    slot = i % 2

    def start(src_idx_ref, src_h_ref, dst_slot):
        def body(r, carry):
            pltpu.make_async_copy(
                y_hbm.at[pl.ds(src_idx_ref[0, 0, r], 1), :],
                ybuf.at[dst_slot, pl.ds(r, 1), :],
                sem.at[dst_slot]).start()
            return carry
        lax.fori_loop(0, TOP_K * tt, body, 0, unroll=8)
        h_row = pl.multiple_of(src_h_ref[0, 0, 0], 8)
        pltpu.make_async_copy(
            h_hbm.at[pl.ds(h_row, tt), :], hbuf.at[dst_slot], sem.at[dst_slot]).start()

    @pl.when(i == 0)
    def _():
        start(idx_ref, hsrc_ref, 0)

    @pl.when(i + 1 < n)
    def _():
        start(idxn_ref, hsrcn_ref, 1 - slot)

    pltpu.make_async_copy(ybuf.at[slot], ybuf.at[slot], sem.at[slot]).wait()
    pltpu.make_async_copy(hbuf.at[slot], hbuf.at[slot], sem.at[slot]).wait()
    acc = hbuf[slot]
    for kk in range(TOP_K):
        acc = acc + ybuf[slot, kk * tt:(kk + 1) * tt, :]
    ms = jnp.mean(acc * acc, axis=-1, keepdims=True)
    o_ref[...] = (acc * lax.rsqrt(ms + EPS)) * nw_ref[...]


def _combine(pos_tiles, h_src, y_rows, h2, norm_w, tt):
    n_tiles = pos_tiles.shape[0]
    d = h2.shape[1]
    last = n_tiles - 1
    cur = lambda i: (i, 0, 0)
    nxt = lambda i: (jnp.minimum(i + 1, last), 0, 0)
    return pl.pallas_call(
        _combine_kernel,
        grid=(n_tiles,),
        in_specs=[
            pl.BlockSpec((1, 1, TOP_K * tt), cur, memory_space=pltpu.SMEM),
            pl.BlockSpec((1, 1, TOP_K * tt), nxt, memory_space=pltpu.SMEM),
            pl.BlockSpec((1, 1, 1), cur, memory_space=pltpu.SMEM),
            pl.BlockSpec((1, 1, 1), nxt, memory_space=pltpu.SMEM),
            pl.BlockSpec(memory_space=pl.ANY),
            pl.BlockSpec(memory_space=pl.ANY),
            pl.BlockSpec(norm_w.shape, lambda i: (0, 0)),
        ],
        out_specs=pl.BlockSpec((tt, d), lambda i: (i, 0)),
        out_shape=jax.ShapeDtypeStruct((n_tiles * tt, d), F32),
        scratch_shapes=[pltpu.VMEM((2, TOP_K * tt, d), F32), pltpu.VMEM((2, tt, d), F32),
                        pltpu.SemaphoreType.DMA((2,))],
        compiler_params=_cparams(("arbitrary",)),
        name="combine",
    )(pos_tiles, pos_tiles, h_src, h_src, y_rows, h2, norm_w)


def _pick(n, prefs):
    for p in prefs:
        if n % p == 0:
            return p
    raise ValueError(f"no tile in {prefs} divides {n}")


def kernel(x, meta_tokens, hg_lb_logits, norm_mix, w_in, hg_norm, ml_conv_w, ml_conv_b, ml_wq, ml_wk, ml_wv,
           ml_gate_b, ml_norm, ml_skip, w_branch_hg, w_branch_ml, w_out, norm_ffn, router_w, router_b,
           exp_w_gu, exp_b_gu, exp_w_down, exp_b_down, norm_final):
    bsz, seq, d = x.shape
    assert norm_mix.shape[0] == 1, "single-layer block"
    assert seq % CHUNK == 0 and d % LANES == 0
    t = CHUNK + seq
    m_rows = bsz * t
    n_experts = router_w.shape[-1]
    dff = exp_w_down.shape[2]
    assert n_experts <= LANES

    h = jnp.concatenate([
        jnp.zeros((bsz, N_PAD, d), x.dtype),
        jnp.broadcast_to(meta_tokens.astype(x.dtype)[None], (bsz, N_META, d)),
        x,
    ], axis=1)
    h2d = h.reshape(m_rows, d)
    lower_bounds = jnp.cumsum(jax.nn.softmax(hg_lb_logits.astype(F32), axis=0), axis=0)

    w = w_in[0]
    n_a = 4 * HG_W + 2 * ML_W
    w_a = w[:, :n_a].astype(BF16)
    w_m = jnp.pad(w[:, n_a:n_a + 2 * N_HEADS], ((0, 0), (0, LANES - 2 * N_HEADS))).astype(BF16)
    w_g = w[:, n_a + 2 * N_HEADS:].astype(BF16)
    tm1 = _pick(m_rows, (512, 256, 128, 64))
    hq, hf, hi, hg, mm, mo, gates, mif = _in_proj(h2d, norm_mix[0][None], w_a, w_g, w_m, tm1)

    bb = _pick(bsz, (4, 2, 1))
    r3 = lambda a: a.reshape(bsz, t, a.shape[-1])
    mats = jnp.asarray(_hgrn_matrices(), BF16)
    y_hg = _hgrn2(r3(hq), r3(hf), r3(hi), r3(hg), mats, lower_bounds[0][None], hg_norm[0][None], bb)

    nc = t // CHUNK
    g_col = r3(mif)
    g_row = jnp.swapaxes(g_col[:, :, :2 * N_HEADS].reshape(bsz, nc, CHUNK, 2 * N_HEADS), 2, 3)
    tri = jnp.asarray(np.tril(np.ones((CHUNK, CHUNK), np.float32)), BF16)
    gb = ml_gate_b[0].astype(F32)
    gb_col = jnp.pad(gb, (0, LANES - 2 * N_HEADS))[None]
    gb_row = jnp.broadcast_to(gb[:, None], (2 * N_HEADS, CHUNK))
    y_ml = _mlstm(r3(mm), r3(mo), g_col, g_row, tri, ml_conv_w[0], ml_conv_b[0][None],
                  ml_wq[0].astype(BF16), ml_wk[0].astype(BF16), ml_wv[0].astype(BF16),
                  gb_col, gb_row, ml_norm[0][None], ml_skip[0][None], bb)

    rw = jnp.pad(router_w[0].astype(F32), ((0, 0), (0, LANES - n_experts)))
    rw1 = rw.astype(BF16)
    rw2 = (rw - rw1.astype(F32)).astype(BF16)
    rw_split = jnp.stack([rw1, rw2])
    rb = jnp.pad(router_b[0].astype(F32), (0, LANES - n_experts))[None]
    tm4 = _pick(m_rows, (256, 128, 64))
    h2, xn2, top_e, top_w = _merge_route(
        h2d, y_hg.reshape(m_rows, HG_W), y_ml.reshape(m_rows, ML_W), gates,
        w_branch_hg[0].astype(BF16), w_branch_ml[0].astype(BF16), w_out[0].astype(BF16),
        norm_ffn[0][None], rw_split, rb, n_experts, tm4)

    tm6 = 256
    n_tok = bsz * (t - N_PAD)
    n_assign = n_tok * TOP_K
    tok_rows = (jnp.arange(bsz, dtype=jnp.int32)[:, None] * t
                + jnp.arange(N_PAD, t, dtype=jnp.int32)[None, :]).reshape(n_tok)
    e_flat = top_e.reshape(bsz, t, TOP_K)[:, N_PAD:].reshape(n_assign)
    w_flat = top_w.reshape(bsz, t, TOP_K)[:, N_PAD:].reshape(n_assign)
    order = jnp.argsort(e_flat)
    e_sorted = e_flat[order]
    counts = jnp.bincount(e_flat, length=n_experts).astype(jnp.int32)
    starts = jnp.cumsum(counts) - counts
    padded = ((counts + tm6 - 1) // tm6) * tm6
    pend = jnp.cumsum(padded)
    pstart = pend - padded
    dest = pstart[e_sorted] + (jnp.arange(n_assign, dtype=jnp.int32) - starts[e_sorted])
    n_blocks = -(-n_assign // tm6) + n_experts
    n_rows = n_blocks * tm6
    row_src = jnp.full((n_rows,), tok_rows[0], jnp.int32).at[dest].set(tok_rows[order // TOP_K])
    row_w = jnp.zeros((n_rows,), F32).at[dest].set(w_flat[order])
    block_e = jnp.minimum(
        jnp.searchsorted(pend, jnp.arange(n_blocks, dtype=jnp.int32) * tm6, side='right'),
        n_experts - 1).astype(jnp.int32)
    n_used = (pend[-1] // tm6).astype(jnp.int32)[None]
    pos = jnp.zeros((n_assign,), jnp.int32).at[order].set(dest)

    wgu = exp_w_gu[0]
    w_gate = wgu[:, :, 0::2].astype(BF16)
    w_up = wgu[:, :, 1::2].astype(BF16)
    bgu = exp_b_gu[0]
    b_gate = bgu[:, None, 0::2]
    b_up = bgu[:, None, 1::2]
    y_rows = _experts(block_e, n_used, row_src, row_w[:, None], xn2, w_gate, w_up, b_gate, b_up,
                      exp_w_down[0].astype(BF16), exp_b_down[0][:, None, :], tm6)

    tt = _pick(seq, (128, 64))
    tiles_per_b = seq // tt
    n_tiles = bsz * tiles_per_b
    pos_seq = pos.reshape(bsz, t - N_PAD, TOP_K)[:, N_META:]
    pos_tiles = pos_seq.reshape(bsz, tiles_per_b, tt, TOP_K).transpose(0, 1, 3, 2).reshape(
        n_tiles, 1, TOP_K * tt)
    h_src = (jnp.arange(bsz, dtype=jnp.int32)[:, None] * t + CHUNK
             + jnp.arange(tiles_per_b, dtype=jnp.int32)[None, :] * tt).reshape(n_tiles, 1, 1)
    out = _combine(pos_tiles, h_src, y_rows, h2, norm_final[None], tt)
    return out.reshape(bsz, seq, d)
```

```python
import functools

import numpy as np
import jax
import jax.numpy as jnp
from jax import lax
from jax.experimental import pallas as pl
from jax.experimental.pallas import tpu as pltpu

F32 = jnp.float32
BF16 = jnp.bfloat16

N_META = 16
CHUNK = 64
N_PAD = CHUNK - N_META
EPS = 1e-6

N_HEADS = 4
HG_DK = 128
HG_DV = 128
HG_W = N_HEADS * HG_DV
ML_DK = 64
ML_DV = 128
ML_W = N_HEADS * ML_DV
ML_CONV = 4
TOP_K = 4
SWIGLU_LIMIT = 7.0
SWIGLU_ALPHA = 1.702

LANES = 128
VMEM_LIMIT_BYTES = 56 * 1024 * 1024

HG_LEVELS = (32, 16, 8, 4, 2, 1)


def _cparams(sem):
    return pltpu.CompilerParams(dimension_semantics=sem, vmem_limit_bytes=VMEM_LIMIT_BYTES)


def _sigmoid(x):
    return 1.0 / (1.0 + jnp.exp(-x))


def _split3(x):
    x1 = x.astype(BF16)
    r1 = x - x1.astype(F32)
    x2 = r1.astype(BF16)
    x3 = (r1 - x2.astype(F32)).astype(BF16)
    return x1, x2, x3


def _dot(a, b):
    return jnp.dot(a, b, preferred_element_type=F32)


def _dot_nt(a, b):
    return lax.dot_general(a, b, (((1,), (1,)), ((), ())), preferred_element_type=F32)


def _dot_exact_lhs(m_bf16, x):
    x1, x2, x3 = _split3(x)
    return _dot(m_bf16, x1) + _dot(m_bf16, x2) + _dot(m_bf16, x3)


def _in_proj_kernel(h_ref, nw_ref, wa_ref, wg_ref, wm_ref,
                    q_ref, f_ref, i_ref, g_ref, mm_ref, mo_ref, gates_ref, mif_ref):
    x = h_ref[...]
    ms = jnp.mean(x * x, axis=-1, keepdims=True)
    xb = ((x * lax.rsqrt(ms + EPS)) * nw_ref[...]).astype(BF16)
    outs = (q_ref, f_ref, i_ref, g_ref, mm_ref, mo_ref)
    for j, o_ref in enumerate(outs):
        o_ref[...] = _dot(xb, wa_ref[:, j * HG_W:(j + 1) * HG_W]).astype(o_ref.dtype)
    d_model = gates_ref.shape[1] // 2
    for j in range(2):
        gates_ref[:, j * d_model:(j + 1) * d_model] = _dot(
            xb, wg_ref[:, j * d_model:(j + 1) * d_model]).astype(gates_ref.dtype)
    mif_ref[...] = _dot(xb, wm_ref[...])


def _in_proj(h2d, norm_w, w_a, w_g, w_m, tm):
    m, d = h2d.shape
    row = lambda n: pl.BlockSpec((tm, n), lambda i: (i, 0))
    full = lambda a: pl.BlockSpec(a.shape, lambda i: (0, 0))
    out_shape = [
        jax.ShapeDtypeStruct((m, HG_W), BF16),
        jax.ShapeDtypeStruct((m, HG_W), F32),
        jax.ShapeDtypeStruct((m, HG_W), BF16),
        jax.ShapeDtypeStruct((m, HG_W), BF16),
        jax.ShapeDtypeStruct((m, ML_W), BF16),
        jax.ShapeDtypeStruct((m, ML_W), BF16),
        jax.ShapeDtypeStruct((m, 2 * d), BF16),
        jax.ShapeDtypeStruct((m, LANES), F32),
    ]
    return pl.pallas_call(
        _in_proj_kernel,
        grid=(m // tm,),
        in_specs=[row(d), full(norm_w), full(w_a), full(w_g), full(w_m)],
        out_specs=[row(s.shape[1]) for s in out_shape],
        out_shape=out_shape,
        compiler_params=_cparams(("parallel",)),
        name="in_proj",
    )(h2d, norm_w, w_a, w_g, w_m)


def _hgrn_matrices():
    n = CHUNK
    t = np.arange(n)[:, None]
    u = np.arange(n)[None, :]
    blocks = [(u <= t), (u > t)]
    for m in HG_LEVELS:
        r = (t // (2 * m)) * (2 * m) + m - 1
        upper = (t % (2 * m)) >= m
        blocks.append(upper & (u > r) & (u <= t))
        blocks.append((~upper) & (u > t) & (u <= r))
    return np.concatenate(blocks, axis=0).astype(np.float32)


def _hgrn2_kernel(q_ref, f_ref, v_ref, g_ref, mat_ref, lb_ref, nw_ref, y_ref, st_ref):
    c = pl.program_id(1)
    bb = q_ref.shape[0]

    @pl.when(c == 0)
    def _():
        st_ref[...] = jnp.zeros_like(st_ref)

    pos = c * CHUNK + lax.broadcasted_iota(jnp.int32, (CHUNK, 1), 0)
    valid = pos >= N_PAD
    ti = lax.broadcasted_iota(jnp.int32, (CHUNK, CHUNK), 0)
    si = lax.broadcasted_iota(jnp.int32, (CHUNK, CHUNK), 1)
    diag_mask = ti == si
    level_masks = []
    for m in HG_LEVELS:
        same_pair = (ti & ~(2 * m - 1)) == (si & ~(2 * m - 1))
        level_masks.append(same_pair & ((ti & m) != 0) & ((si & m) == 0))

    lb = lb_ref[...]
    mat = mat_ref[...]
    for b in range(bb):
        hf = f_ref[b]
        f = lb + (1.0 - lb) * _sigmoid(hf)
        f = jnp.where(valid, f, 1.0)
        logf = jnp.log(f)
        k_all = 1.0 - f
        hq = q_ref[b].astype(F32)
        q_all = hq * _sigmoid(hq)
        expo = _dot_exact_lhs(mat, logf)
        e_b = jnp.exp(expo[0:CHUNK])
        e_bl = jnp.exp(expo[CHUNK:2 * CHUNK])
        v_all = v_ref[b]
        hg = g_ref[b].astype(F32)
        gate = hg * _sigmoid(hg)
        for h in range(N_HEADS):
            sl = slice(h * HG_DK, (h + 1) * HG_DK)
            q = q_all[:, sl]
            k = k_all[:, sl]
            scores = jnp.where(diag_mask, _dot_nt(q.astype(BF16), k.astype(BF16)), 0.0)
            for li, mask in enumerate(level_masks):
                r0 = (2 + 2 * li) * CHUNK
                qd = (q * jnp.exp(expo[r0:r0 + CHUNK, sl])).astype(BF16)
                kd = (k * jnp.exp(expo[r0 + CHUNK:r0 + 2 * CHUNK, sl])).astype(BF16)
                scores = jnp.where(mask, _dot_nt(qd, kd), scores)
            v = v_all[:, sl]
            st = st_ref[b, h]
            o = _dot(scores.astype(BF16), v) + _dot_nt((q * e_b[:, sl]).astype(BF16), st.astype(BF16))
            kl = (k * e_bl[:, sl]).astype(BF16)
            vt = v.astype(F32).T.astype(BF16)
            st_ref[b, h] = e_b[CHUNK - 1:CHUNK, sl] * st + _dot(vt, kl)
            ms = jnp.mean(o * o, axis=-1, keepdims=True)
            y = (o * lax.rsqrt(ms + EPS)) * nw_ref[:, sl] * gate[:, sl]
            y_ref[b, :, sl] = y.astype(y_ref.dtype)


def _hgrn2(hq, hf, hi, hg, mats, lb, norm_w, bb):
    bsz, t, _ = hq.shape
    nc = t // CHUNK
    blk = pl.BlockSpec((bb, CHUNK, HG_W), lambda i, c: (i, c, 0))
    full = lambda a: pl.BlockSpec(a.shape, lambda i, c: (0, 0))
    return pl.pallas_call(
        _hgrn2_kernel,
        grid=(bsz // bb, nc),
        in_specs=[blk, blk, blk, blk, full(mats), full(lb), full(norm_w)],
        out_specs=blk,
        out_shape=jax.ShapeDtypeStruct((bsz, t, HG_W), BF16),
        scratch_shapes=[pltpu.VMEM((bb, N_HEADS, HG_DV, HG_DK), F32)],
        compiler_params=_cparams(("parallel", "arbitrary")),
        name="hgrn2",
    )(hq, hf, hi, hg, mats, lb, norm_w)


def _log_sigmoid(x):
    return jnp.minimum(x, 0.0) - jnp.log(1.0 + jnp.exp(-jnp.abs(x)))


def _mlstm_kernel(mm_ref, mo_ref, gc_ref, gr_ref, tri_ref, cw_ref, cb_ref, wq_ref, wk_ref, wv_ref,
                  gbc_ref, gbr_ref, nw_ref, sk_ref, y_ref, s_ref, m_ref, tail_ref):
    c = pl.program_id(1)
    bb = mm_ref.shape[0]

    @pl.when(c == 0)
    def _():
        s_ref[...] = jnp.zeros_like(s_ref)
        m_ref[...] = jnp.zeros_like(m_ref)
        tail_ref[...] = jnp.zeros_like(tail_ref)

    pos_c = c * CHUNK + lax.broadcasted_iota(jnp.int32, (CHUNK, 1), 0)
    valid_c = pos_c >= N_PAD
    pos_r = c * CHUNK + lax.broadcasted_iota(jnp.int32, (1, CHUNK), 1)
    valid_r = pos_r >= N_PAD
    ti = lax.broadcasted_iota(jnp.int32, (CHUNK, CHUNK), 0)
    si = lax.broadcasted_iota(jnp.int32, (CHUNK, CHUNK), 1)
    causal = si <= ti
    tri = tri_ref[...]
    ones_v = jnp.ones((CHUNK, ML_DV), BF16)
    neg_inf = -jnp.inf

    for b in range(bb):
        mm = jnp.where(valid_c, mm_ref[b].astype(F32), 0.0)
        ext = jnp.concatenate([tail_ref[b], mm], axis=0)
        tail_ref[b] = mm[CHUNK - 8:CHUNK]
        conv = cb_ref[...]
        for j in range(ML_CONV):
            off = 8 - (ML_CONV - 1) + j
            conv = conv + cw_ref[j:j + 1, :] * ext[off:off + CHUNK]
        cact = conv * _sigmoid(conv)
        cact_b = cact.astype(BF16)
        mm_b = mm.astype(BF16)

        gcol = gc_ref[b] + gbc_ref[...]
        li_col = jnp.where(valid_c, gcol, neg_inf)
        lf_col = jnp.where(valid_c, _log_sigmoid(gcol), 0.0)
        b_col = _dot_exact_lhs(tri, lf_col)
        grow = gr_ref[b, 0] + gbr_ref[...]
        li_row = jnp.where(valid_r, grow, neg_inf)
        lf_row = jnp.where(valid_r, _log_sigmoid(grow), 0.0)
        r1, r2, r3 = _split3(lf_row)
        b_row = _dot_nt(r1, tri) + _dot_nt(r2, tri) + _dot_nt(r3, tri)

        mo = mo_ref[b].astype(F32)
        ogate = _sigmoid(mo)
        for h in range(N_HEADS):
            sl = slice(h * ML_DV, (h + 1) * ML_DV)
            q = (_dot(cact_b[:, sl], wq_ref[h]) * (ML_DK ** -0.5)).astype(BF16)
            k = _dot(cact_b[:, sl], wk_ref[h])
            v = _dot(mm_b[:, sl], wv_ref[h]).astype(BF16)
            v_aug = jnp.concatenate([v, ones_v], axis=1)
            bc = b_col[:, N_HEADS + h:N_HEADS + h + 1]
            lic = li_col[:, h:h + 1]
            br = b_row[N_HEADS + h:N_HEADS + h + 1, :]
            lir = li_row[h:h + 1, :]
            m_prev = m_ref[b, h][0:1, 0:1]
            d = jnp.where(causal, bc + (lir - br), neg_inf)
            a = bc + m_prev
            m_t = jnp.maximum(a, jnp.max(d, axis=-1, keepdims=True))
            w_intra = jnp.exp(d - m_t)
            w_inter = jnp.exp(a - m_t)
            qk = _dot_nt(q, k.astype(BF16)) * w_intra
            s_aug = s_ref[b, h]
            numden = _dot(qk.astype(BF16), v_aug) + w_inter * _dot(q, s_aug.astype(BF16))
            num = numden[:, :ML_DV]
            den = numden[:, ML_DV:]
            o = num / jnp.maximum(jnp.abs(den), jnp.exp(-m_t))
            g_last = bc[CHUNK - 1:CHUNK, :]
            e = g_last - bc + lic
            m_new = jnp.maximum(g_last + m_prev, jnp.max(e, axis=0, keepdims=True))
            w_s = jnp.exp(e - m_new)
            w_p = jnp.exp(g_last + m_prev - m_new)
            kw_t = (k * w_s).T.astype(BF16)
            s_ref[b, h] = w_p * s_aug + _dot(kw_t, v_aug)
            m_ref[b, h] = jnp.broadcast_to(m_new, m_ref.shape[2:])
            ms = jnp.mean(o * o, axis=-1, keepdims=True)
            y = ((o * lax.rsqrt(ms + EPS)) * nw_ref[:, sl] + sk_ref[:, sl] * cact[:, sl]) * ogate[:, sl]
            y_ref[b, :, sl] = y.astype(y_ref.dtype)


def _mlstm(mm, mo, g_col, g_row, tri, conv_w, conv_b, wq, wk, wv, gb_col, gb_row, norm_w, skip, bb):
    bsz, t, _ = mm.shape
    nc = t // CHUNK
    blk = pl.BlockSpec((bb, CHUNK, ML_W), lambda i, c: (i, c, 0))
    gcb = pl.BlockSpec((bb, CHUNK, LANES), lambda i, c: (i, c, 0))
    grb = pl.BlockSpec((bb, 1, 8, CHUNK), lambda i, c: (i, c, 0, 0))

    def full(a):
        nd = a.ndim
        return pl.BlockSpec(a.shape, lambda i, c: (0,) * nd)

    params = (tri, conv_w, conv_b, wq, wk, wv, gb_col, gb_row, norm_w, skip)
    return pl.pallas_call(
        _mlstm_kernel,
        grid=(bsz // bb, nc),
        in_specs=[blk, blk, gcb, grb] + [full(p) for p in params],
        out_specs=blk,
        out_shape=jax.ShapeDtypeStruct((bsz, t, ML_W), BF16),
        scratch_shapes=[
            pltpu.VMEM((bb, N_HEADS, ML_DK, 2 * ML_DV), F32),
            pltpu.VMEM((bb, N_HEADS, 8, LANES), F32),
            pltpu.VMEM((bb, 8, ML_W), F32),
        ],
        compiler_params=_cparams(("parallel", "arbitrary")),
        name="mlstm",
    )(mm, mo, g_col, g_row, *params)


def _merge_route_kernel(h_ref, yh_ref, ym_ref, gates_ref, wbh_ref, wbm_ref, wo_ref, nf_ref,
                        rw_ref, rb_ref, h2_ref, xn_ref, te_ref, tw_ref, rk_ref, cnt_ref, *, n_experts):
    d = h_ref.shape[1]
    tq = h_ref.shape[0]
    j = pl.program_id(1)

    @pl.when(jnp.logical_and(pl.program_id(0) == 0, j == 0))
    def _():
        cnt_ref[...] = jnp.zeros_like(cnt_ref)

    g0 = _sigmoid(gates_ref[:, :d].astype(F32))
    g1 = _sigmoid(gates_ref[:, d:].astype(F32))
    merged = g0 * _dot(yh_ref[...], wbh_ref[...]) + g1 * _dot(ym_ref[...], wbm_ref[...])
    h2 = h_ref[...] + _dot(merged.astype(BF16), wo_ref[...])
    h2_ref[...] = h2
    ms = jnp.mean(h2 * h2, axis=-1, keepdims=True)
    xn = (h2 * lax.rsqrt(ms + EPS)) * nf_ref[...]
    xn_ref[...] = xn
    x1, x2, _ = _split3(xn)
    logits = (_dot(x1, rw_ref[0]) + _dot(x1, rw_ref[1]) + _dot(x2, rw_ref[0])) + rb_ref[...]
    lane = lax.broadcasted_iota(jnp.int32, logits.shape, 1)
    work = jnp.where(lane < n_experts, logits, -jnp.inf)
    vals, idxs = [], []
    for _ in range(TOP_K):
        vmax = jnp.max(work, axis=-1, keepdims=True)
        imax = jnp.min(jnp.where(work == vmax, lane, LANES), axis=-1, keepdims=True)
        vals.append(vmax)
        idxs.append(imax)
        work = jnp.where(lane == imax, -jnp.inf, work)
    exps = [jnp.exp(v - vals[0]) for v in vals]
    tot = exps[0] + exps[1] + exps[2] + exps[3]
    te = jnp.zeros(logits.shape, jnp.int32)
    tw = jnp.zeros(logits.shape, F32)
    for kk in range(TOP_K):
        te = jnp.where(lane == kk, idxs[kk], te)
        tw = jnp.where(lane == kk, exps[kk] / tot, tw)
    te_ref[...] = te[:, :TOP_K]
    tw_ref[...] = tw[:, :TOP_K]

    valid = (j * tq + lax.broadcasted_iota(jnp.int32, (tq, 1), 0)) >= N_PAD
    onehots = [jnp.where(jnp.logical_and(lane == idxs[kk], valid), 1.0, 0.0) for kk in range(TOP_K)]
    oh_all = onehots[0] + onehots[1] + onehots[2] + onehots[3]
    ri = lax.broadcasted_iota(jnp.int32, (tq, tq), 0)
    ci = lax.broadcasted_iota(jnp.int32, (tq, tq), 1)
    earlier = jnp.where(ci < ri, 1.0, 0.0).astype(BF16)
    before = _dot(earlier, oh_all.astype(BF16)) + cnt_ref[...]
    rk = jnp.zeros(logits.shape, F32)
    for kk in range(TOP_K):
        rank_k = jnp.sum(jnp.where(lane == idxs[kk], before, 0.0), axis=-1, keepdims=True)
        rk = jnp.where(lane == kk, rank_k, rk)
        before = before + onehots[kk]
    rk_ref[...] = rk[:, :TOP_K].astype(jnp.int32)
    cnt_ref[...] = cnt_ref[...] + jnp.sum(oh_all, axis=0, keepdims=True)


def _merge_route(h2d, y_hg, y_ml, gates, wbh, wbm, wo, norm_ffn, rw_split, rb, n_experts, bsz, tq):
    m, d = h2d.shape
    nj = m // (bsz * tq)
    row = lambda n: pl.BlockSpec((tq, n), lambda b, j: (b * nj + j, 0))

    def full(a):
        nd = a.ndim
        return pl.BlockSpec(a.shape, lambda b, j: (0,) * nd)

    return pl.pallas_call(
        functools.partial(_merge_route_kernel, n_experts=n_experts),
        grid=(bsz, nj),
        in_specs=[row(d), row(HG_W), row(ML_W), row(2 * d), full(wbh), full(wbm), full(wo),
                  full(norm_ffn), full(rw_split), full(rb)],
        out_specs=[row(d), row(d), row(TOP_K), row(TOP_K), row(TOP_K),
                   pl.BlockSpec((1, LANES), lambda b, j: (0, 0))],
        out_shape=[
            jax.ShapeDtypeStruct((m, d), F32),
            jax.ShapeDtypeStruct((m, d), F32),
            jax.ShapeDtypeStruct((m, TOP_K), jnp.int32),
            jax.ShapeDtypeStruct((m, TOP_K), F32),
            jax.ShapeDtypeStruct((m, TOP_K), jnp.int32),
            jax.ShapeDtypeStruct((1, LANES), F32),
        ],
        compiler_params=_cparams(("arbitrary", "arbitrary")),
        name="merge_route",
    )(h2d, y_hg, y_ml, gates, wbh, wbm, wo, norm_ffn, rw_split, rb)


def _dispatch_kernel(ps_ref, pe_ref, e_ref, rk_ref, x_ref, xs_hbm, zbuf, sem, *, n_experts, tm):
    b = pl.program_id(0)
    j = pl.program_id(1)
    tq = x_ref.shape[0]

    @pl.when(jnp.logical_and(b == 0, j == 0))
    def _():
        zbuf[...] = jnp.zeros_like(zbuf)
        for e in range(n_experts):
            @pl.when(pe_ref[e] > ps_ref[e])
            def _():
                dst = pl.multiple_of(pe_ref[e] - tm, tm)
                pltpu.make_async_copy(zbuf, xs_hbm.at[pl.ds(dst, tm), :], sem).start()
        for e in range(n_experts):
            @pl.when(pe_ref[e] > ps_ref[e])
            def _():
                pltpu.make_async_copy(zbuf, xs_hbm.at[pl.ds(0, tm), :], sem).wait()

        def zero_tail(blk, carry):
            dst = pl.multiple_of(blk * tm, tm)
            cp = pltpu.make_async_copy(zbuf, xs_hbm.at[pl.ds(dst, tm), :], sem)
            cp.start()
            cp.wait()
            return carry
        lax.fori_loop(pe_ref[n_experts - 1] // tm, xs_hbm.shape[0] // tm, zero_tail, 0)

    def scatter_rows(lo):
        def body(r, carry):
            for kk in range(TOP_K):
                dst = ps_ref[e_ref[0, 0, r * TOP_K + kk]] + rk_ref[0, 0, r * TOP_K + kk]
                pltpu.make_async_copy(
                    x_ref.at[pl.ds(r, 1), :], xs_hbm.at[pl.ds(dst, 1), :], sem).start()
            return carry
        lax.fori_loop(lo, tq, body, 0)
        n = tq - lo
        for _ in range(TOP_K):
            pltpu.make_async_copy(x_ref.at[pl.ds(0, n), :], xs_hbm.at[pl.ds(0, n), :], sem).wait()

    @pl.when(j == 0)
    def _():
        scatter_rows(N_PAD)

    @pl.when(j != 0)
    def _():
        scatter_rows(0)


def _dispatch(pstart, pend, top_e, rank, xn, n_rows, bsz, tq, tm):
    m, d = xn.shape
    nj = m // (bsz * tq)
    n_experts = pstart.shape[0]
    sm = lambda a: a.reshape(bsz * nj, 1, tq * TOP_K)
    smem_blk = pl.BlockSpec((1, 1, tq * TOP_K), lambda b, j, ps, pe: (b * nj + j, 0, 0),
                            memory_space=pltpu.SMEM)
    grid_spec = pltpu.PrefetchScalarGridSpec(
        num_scalar_prefetch=2,
        grid=(bsz, nj),
        in_specs=[smem_blk, smem_blk, pl.BlockSpec((tq, d), lambda b, j, ps, pe: (b * nj + j, 0))],
        out_specs=pl.BlockSpec(memory_space=pl.ANY),
        scratch_shapes=[pltpu.VMEM((tm, d), F32), pltpu.SemaphoreType.DMA(())],
    )
    return pl.pallas_call(
        functools.partial(_dispatch_kernel, n_experts=n_experts, tm=tm),
        grid_spec=grid_spec,
        out_shape=jax.ShapeDtypeStruct((n_rows, d), F32),
        compiler_params=_cparams(("arbitrary", "arbitrary")),
        name="dispatch",
    )(pstart, pend, sm(top_e), sm(rank), xn)


CAST_ROWS = 256


def _experts_kernel(be_ref, nu_ref, x_ref, wgu_ref, bg_ref, bu_ref, wd_ref, bd_ref, y_ref,
                    wg_s, wu_s, wd_s):
    i = pl.program_id(0)
    n_used = nu_ref[0]
    dff = wg_s.shape[0]
    new_expert = jnp.logical_or(i == 0, be_ref[i] != be_ref[jnp.maximum(i - 1, 0)])

    @pl.when(jnp.logical_and(new_expert, i < n_used))
    def _():
        for c in range(dff // CAST_ROWS):
            rows = pl.ds(c * CAST_ROWS, CAST_ROWS)
            wd_s[rows, :] = wd_ref[0, rows, :].astype(BF16)
            for s in range(wgu_ref.shape[1]):
                lanes = pl.ds(s * LANES, LANES)
                wg_s[rows, lanes] = wgu_ref[
                    0, s, pl.ds(2 * c * CAST_ROWS, CAST_ROWS, stride=2), :].astype(BF16)
                wu_s[rows, lanes] = wgu_ref[
                    0, s, pl.ds(2 * c * CAST_ROWS + 1, CAST_ROWS, stride=2), :].astype(BF16)

    @pl.when(i < n_used)
    def _():
        xb = x_ref[...].astype(BF16)
        g = _dot_nt(xb, wg_s[...]) + bg_ref[0]
        u = _dot_nt(xb, wu_s[...]) + bu_ref[0]
        gate = jnp.minimum(g, SWIGLU_LIMIT)
        up = jnp.clip(u, -SWIGLU_LIMIT, SWIGLU_LIMIT)
        act = (up + 1.0) * gate * _sigmoid(SWIGLU_ALPHA * gate)
        y_ref[...] = _dot(act.astype(BF16), wd_s[...]) + bd_ref[0]

    @pl.when(i >= n_used)
    def _():
        y_ref[...] = jnp.zeros_like(y_ref)


def _experts(block_e, n_used, xs, w_gu_t, b_g, b_u, w_d, b_d, tm):
    n_blocks = block_e.shape[0]
    d = xs.shape[1]
    dff = w_d.shape[1]
    assert dff % CAST_ROWS == 0
    wspec = lambda k, n: pl.BlockSpec((1, k, n), lambda i, be, nu: (be[i], 0, 0))
    grid_spec = pltpu.PrefetchScalarGridSpec(
        num_scalar_prefetch=2,
        grid=(n_blocks,),
        in_specs=[
            pl.BlockSpec((tm, d), lambda i, be, nu: (jnp.minimum(i, nu[0] - 1), 0)),
            pl.BlockSpec((1, d // LANES, 2 * dff, LANES), lambda i, be, nu: (be[i], 0, 0, 0)),
            wspec(1, dff), wspec(1, dff), wspec(dff, d), wspec(1, d),
        ],
        out_specs=pl.BlockSpec((tm, d), lambda i, be, nu: (i, 0)),
        scratch_shapes=[pltpu.VMEM((dff, d), BF16), pltpu.VMEM((dff, d), BF16), pltpu.VMEM((dff, d), BF16)],
    )
    return pl.pallas_call(
        _experts_kernel,
        grid_spec=grid_spec,
        out_shape=jax.ShapeDtypeStruct((n_blocks * tm, d), F32),
        compiler_params=_cparams(("arbitrary",)),
        name="experts",
    )(block_e, n_used, xs, w_gu_t, b_g, b_u, w_d, b_d)


def _combine_kernel(ps_ref, e_ref, en_ref, rk_ref, rkn_ref, hsrc_ref, hsrcn_ref, tw_ref, y_hbm, h_hbm,
                    nw_ref, o_ref, ybuf, hbuf, sem):
    i = pl.program_id(0)
    n = pl.num_programs(0)
    tt = hbuf.shape[1]
    slot = i % 2

    def start(src_e_ref, src_rk_ref, src_h_ref, dst_slot):
        def body(r, carry):
            for kk in range(TOP_K):
                src = ps_ref[src_e_ref[0, 0, r * TOP_K + kk]] + src_rk_ref[0, 0, r * TOP_K + kk]
                pltpu.make_async_copy(
                    y_hbm.at[pl.ds(src, 1), :],
                    ybuf.at[dst_slot, pl.ds(kk * tt + r, 1), :],
                    sem.at[dst_slot]).start()
            return carry
        lax.fori_loop(0, tt, body, 0, unroll=2)
        h_row = pl.multiple_of(src_h_ref[0, 0, 0], 8)
        pltpu.make_async_copy(
            h_hbm.at[pl.ds(h_row, tt), :], hbuf.at[dst_slot], sem.at[dst_slot]).start()

    @pl.when(i == 0)
    def _():
        start(e_ref, rk_ref, hsrc_ref, 0)

    @pl.when(i + 1 < n)
    def _():
        start(en_ref, rkn_ref, hsrcn_ref, 1 - slot)

    pltpu.make_async_copy(ybuf.at[slot], ybuf.at[slot], sem.at[slot]).wait()
    pltpu.make_async_copy(hbuf.at[slot], hbuf.at[slot], sem.at[slot]).wait()
    acc = hbuf[slot]
    tw = tw_ref[...]
    for kk in range(TOP_K):
        acc = acc + tw[:, kk:kk + 1] * ybuf[slot, kk * tt:(kk + 1) * tt, :]
    ms = jnp.mean(acc * acc, axis=-1, keepdims=True)
    o_ref[...] = (acc * lax.rsqrt(ms + EPS)) * nw_ref[...]


def _combine(pstart, e_seq, rk_seq, tw_seq, h_src, y_rows, h2, norm_w, tt):
    n_tiles = h_src.shape[0]
    d = h2.shape[1]
    last = n_tiles - 1
    cur = lambda i, ps: (i, 0, 0)
    nxt = lambda i, ps: (jnp.minimum(i + 1, last), 0, 0)
    sm = lambda a: a.reshape(n_tiles, 1, tt * TOP_K)
    idx_blk = lambda f: pl.BlockSpec((1, 1, TOP_K * tt), f, memory_space=pltpu.SMEM)
    one_blk = lambda f: pl.BlockSpec((1, 1, 1), f, memory_space=pltpu.SMEM)
    grid_spec = pltpu.PrefetchScalarGridSpec(
        num_scalar_prefetch=1,
        grid=(n_tiles,),
        in_specs=[
            idx_blk(cur), idx_blk(nxt), idx_blk(cur), idx_blk(nxt), one_blk(cur), one_blk(nxt),
            pl.BlockSpec((tt, TOP_K), lambda i, ps: (i, 0)),
            pl.BlockSpec(memory_space=pl.ANY),
            pl.BlockSpec(memory_space=pl.ANY),
            pl.BlockSpec(norm_w.shape, lambda i, ps: (0, 0)),
        ],
        out_specs=pl.BlockSpec((tt, d), lambda i, ps: (i, 0)),
        scratch_shapes=[pltpu.VMEM((2, TOP_K * tt, d), F32), pltpu.VMEM((2, tt, d), F32),
                        pltpu.SemaphoreType.DMA((2,))],
    )
    return pl.pallas_call(
        _combine_kernel,
        grid_spec=grid_spec,
        out_shape=jax.ShapeDtypeStruct((n_tiles * tt, d), F32),
        compiler_params=_cparams(("arbitrary",)),
        name="combine",
    )(pstart, sm(e_seq), sm(e_seq), sm(rk_seq), sm(rk_seq), h_src, h_src, tw_seq, y_rows, h2, norm_w)


def _pick(n, prefs):
    for p in prefs:
        if n % p == 0:
            return p
    raise ValueError(f"no tile in {prefs} divides {n}")


def kernel(x, meta_tokens, hg_lb_logits, norm_mix, w_in, hg_norm, ml_conv_w, ml_conv_b, ml_wq, ml_wk, ml_wv,
           ml_gate_b, ml_norm, ml_skip, w_branch_hg, w_branch_ml, w_out, norm_ffn, router_w, router_b,
           exp_w_gu, exp_b_gu, exp_w_down, exp_b_down, norm_final):
    bsz, seq, d = x.shape
    assert norm_mix.shape[0] == 1, "single-layer block"
    assert seq % CHUNK == 0 and d % LANES == 0
    t = CHUNK + seq
    m_rows = bsz * t
    n_experts = router_w.shape[-1]
    dff = exp_w_down.shape[2]
    assert n_experts <= LANES

    h = jnp.concatenate([
        jnp.zeros((bsz, N_PAD, d), x.dtype),
        jnp.broadcast_to(meta_tokens.astype(x.dtype)[None], (bsz, N_META, d)),
        x,
    ], axis=1)
    h2d = h.reshape(m_rows, d)
    lower_bounds = jnp.cumsum(jax.nn.softmax(hg_lb_logits.astype(F32), axis=0), axis=0)

    w = w_in[0]
    n_a = 4 * HG_W + 2 * ML_W
    w_a = w[:, :n_a].astype(BF16)
    w_m = jnp.pad(w[:, n_a:n_a + 2 * N_HEADS], ((0, 0), (0, LANES - 2 * N_HEADS))).astype(BF16)
    w_g = w[:, n_a + 2 * N_HEADS:].astype(BF16)
    tm1 = _pick(m_rows, (512, 256, 128, 64))
    hq, hf, hi, hg, mm, mo, gates, mif = _in_proj(h2d, norm_mix[0][None], w_a, w_g, w_m, tm1)

    bb = _pick(bsz, (4, 2, 1))
    r3 = lambda a: a.reshape(bsz, t, a.shape[-1])
    mats = jnp.asarray(_hgrn_matrices(), BF16)
    y_hg = _hgrn2(r3(hq), r3(hf), r3(hi), r3(hg), mats, lower_bounds[0][None], hg_norm[0][None], bb)

    nc = t // CHUNK
    g_col = r3(mif)
    g_row = jnp.swapaxes(g_col[:, :, :2 * N_HEADS].reshape(bsz, nc, CHUNK, 2 * N_HEADS), 2, 3)
    tri = jnp.asarray(np.tril(np.ones((CHUNK, CHUNK), np.float32)), BF16)
    gb = ml_gate_b[0].astype(F32)
    gb_col = jnp.pad(gb, (0, LANES - 2 * N_HEADS))[None]
    gb_row = jnp.broadcast_to(gb[:, None], (2 * N_HEADS, CHUNK))
    y_ml = _mlstm(r3(mm), r3(mo), g_col, g_row, tri, ml_conv_w[0], ml_conv_b[0][None],
                  ml_wq[0].astype(BF16), ml_wk[0].astype(BF16), ml_wv[0].astype(BF16),
                  gb_col, gb_row, ml_norm[0][None], ml_skip[0][None], bb)

    rw = jnp.pad(router_w[0].astype(F32), ((0, 0), (0, LANES - n_experts)))
    rw1 = rw.astype(BF16)
    rw2 = (rw - rw1.astype(F32)).astype(BF16)
    rw_split = jnp.stack([rw1, rw2])
    rb = jnp.pad(router_b[0].astype(F32), (0, LANES - n_experts))[None]
    tq = _pick(t, (352, 192, 64))
    h2, xn2, top_e, top_w, rank, cnt = _merge_route(
        h2d, y_hg.reshape(m_rows, HG_W), y_ml.reshape(m_rows, ML_W), gates,
        w_branch_hg[0].astype(BF16), w_branch_ml[0].astype(BF16), w_out[0].astype(BF16),
        norm_ffn[0][None], rw_split, rb, n_experts, bsz, tq)

    tm6 = 256
    n_assign = bsz * (t - N_PAD) * TOP_K
    counts = cnt[0, :n_experts].astype(jnp.int32)
    padded = ((counts + tm6 - 1) // tm6) * tm6
    pend = jnp.cumsum(padded).astype(jnp.int32)
    pstart = pend - padded
    n_blocks = -(-n_assign // tm6) + n_experts
    n_rows = n_blocks * tm6
    blk_start = jnp.arange(n_blocks, dtype=jnp.int32) * tm6
    block_e = jnp.minimum(jnp.sum((blk_start[:, None] >= pend[None, :]).astype(jnp.int32), axis=1),
                          n_experts - 1)
    n_used = (pend[-1] // tm6)[None]
    xs = _dispatch(pstart, pend, top_e, rank, xn2, n_rows, bsz, tq, tm6)

    w_gu_t = jnp.swapaxes(exp_w_gu[0].reshape(n_experts, d // LANES, LANES, 2 * dff), 2, 3)
    bgu = exp_b_gu[0]
    y_rows = _experts(block_e, n_used, xs, w_gu_t, bgu[:, None, 0::2], bgu[:, None, 1::2],
                      exp_w_down[0], exp_b_down[0][:, None, :], tm6)

    tt = _pick(seq, (128, 64))
    tiles_per_b = seq // tt
    n_tiles = bsz * tiles_per_b
    seq_part = lambda a: a.reshape(bsz, t, TOP_K)[:, CHUNK:].reshape(bsz * seq, TOP_K)
    h_src = (jnp.arange(bsz, dtype=jnp.int32)[:, None] * t + CHUNK
             + jnp.arange(tiles_per_b, dtype=jnp.int32)[None, :] * tt).reshape(n_tiles, 1, 1)
    out = _combine(pstart, seq_part(top_e), seq_part(rank), seq_part(top_w), h_src, y_rows, h2,
                   norm_final[None], tt)
    return out.reshape(bsz, seq, d)
```

```python
import functools

import numpy as np
import jax
import jax.numpy as jnp
from jax import lax
from jax.experimental import pallas as pl
from jax.experimental.pallas import tpu as pltpu

F32 = jnp.float32
BF16 = jnp.bfloat16

N_META = 16
CHUNK = 64
N_PAD = CHUNK - N_META
EPS = 1e-6

N_HEADS = 4
HG_DK = 128
HG_DV = 128
HG_W = N_HEADS * HG_DV
ML_DK = 64
ML_DV = 128
ML_W = N_HEADS * ML_DV
ML_CONV = 4
TOP_K = 4
SWIGLU_LIMIT = 7.0
SWIGLU_ALPHA = 1.702

LANES = 128
VMEM_LIMIT_BYTES = 56 * 1024 * 1024

HG_LEVELS = (32, 16, 8, 4, 2, 1)


def _cparams(sem):
    return pltpu.CompilerParams(dimension_semantics=sem, vmem_limit_bytes=VMEM_LIMIT_BYTES)


def _sigmoid(x):
    return 1.0 / (1.0 + jnp.exp(-x))


def _split3(x):
    x1 = x.astype(BF16)
    r1 = x - x1.astype(F32)
    x2 = r1.astype(BF16)
    x3 = (r1 - x2.astype(F32)).astype(BF16)
    return x1, x2, x3


def _dot(a, b):
    return jnp.dot(a, b, preferred_element_type=F32)


TOK_ROWS = 8


def _store_token_tiles(ref, x):
    n = x.shape[0]
    for s in range(TOK_ROWS):
        ref[pl.ds(s, n, stride=TOK_ROWS), :] = x[:, s * LANES:(s + 1) * LANES]


def _load_token_tiles(ref, n):
    return jnp.concatenate([ref[pl.ds(s, n, stride=TOK_ROWS), :] for s in range(TOK_ROWS)], axis=1)


def _dot_nt(a, b):
    return lax.dot_general(a, b, (((1,), (1,)), ((), ())), preferred_element_type=F32)


def _dot_exact_lhs(m_bf16, x):
    x1, x2, x3 = _split3(x)
    return _dot(m_bf16, x1) + _dot(m_bf16, x2) + _dot(m_bf16, x3)


def _in_proj_kernel(h_ref, nw_ref, wa_ref, wg_ref, wm_ref,
                    q_ref, f_ref, i_ref, g_ref, mm_ref, mo_ref, gates_ref, mif_ref):
    x = h_ref[...]
    ms = jnp.mean(x * x, axis=-1, keepdims=True)
    xb = ((x * lax.rsqrt(ms + EPS)) * nw_ref[...]).astype(BF16)
    outs = (q_ref, f_ref, i_ref, g_ref, mm_ref, mo_ref)
    for j, o_ref in enumerate(outs):
        o_ref[...] = _dot(xb, wa_ref[:, j * HG_W:(j + 1) * HG_W]).astype(o_ref.dtype)
    d_model = gates_ref.shape[1] // 2
    for j in range(2):
        gates_ref[:, j * d_model:(j + 1) * d_model] = _dot(
            xb, wg_ref[:, j * d_model:(j + 1) * d_model]).astype(gates_ref.dtype)
    mif_ref[...] = _dot(xb, wm_ref[...])


def _in_proj(h2d, norm_w, w_a, w_g, w_m, tm):
    m, d = h2d.shape
    row = lambda n: pl.BlockSpec((tm, n), lambda i: (i, 0))
    full = lambda a: pl.BlockSpec(a.shape, lambda i: (0, 0))
    out_shape = [
        jax.ShapeDtypeStruct((m, HG_W), BF16),
        jax.ShapeDtypeStruct((m, HG_W), F32),
        jax.ShapeDtypeStruct((m, HG_W), BF16),
        jax.ShapeDtypeStruct((m, HG_W), BF16),
        jax.ShapeDtypeStruct((m, ML_W), BF16),
        jax.ShapeDtypeStruct((m, ML_W), BF16),
        jax.ShapeDtypeStruct((m, 2 * d), BF16),
        jax.ShapeDtypeStruct((m, LANES), F32),
    ]
    return pl.pallas_call(
        _in_proj_kernel,
        grid=(m // tm,),
        in_specs=[row(d), full(norm_w), full(w_a), full(w_g), full(w_m)],
        out_specs=[row(s.shape[1]) for s in out_shape],
        out_shape=out_shape,
        compiler_params=_cparams(("parallel",)),
        name="in_proj",
    )(h2d, norm_w, w_a, w_g, w_m)


def _hgrn_matrices():
    n = CHUNK
    t = np.arange(n)[:, None]
    u = np.arange(n)[None, :]
    blocks = [(u <= t), (u > t)]
    for m in HG_LEVELS:
        r = (t // (2 * m)) * (2 * m) + m - 1
        upper = (t % (2 * m)) >= m
        blocks.append(upper & (u > r) & (u <= t))
        blocks.append((~upper) & (u > t) & (u <= r))
    return np.concatenate(blocks, axis=0).astype(np.float32)


def _hgrn2_kernel(q_ref, f_ref, v_ref, g_ref, mat_ref, lb_ref, nw_ref, y_ref, st_ref):
    c = pl.program_id(1)
    bb = q_ref.shape[0]

    @pl.when(c == 0)
    def _():
        st_ref[...] = jnp.zeros_like(st_ref)

    pos = c * CHUNK + lax.broadcasted_iota(jnp.int32, (CHUNK, 1), 0)
    valid = pos >= N_PAD
    ti = lax.broadcasted_iota(jnp.int32, (CHUNK, CHUNK), 0)
    si = lax.broadcasted_iota(jnp.int32, (CHUNK, CHUNK), 1)
    diag_mask = ti == si
    level_masks = []
    for m in HG_LEVELS:
        same_pair = (ti & ~(2 * m - 1)) == (si & ~(2 * m - 1))
        level_masks.append(same_pair & ((ti & m) != 0) & ((si & m) == 0))

    lb = lb_ref[...]
    mat = mat_ref[...]
    for b in range(bb):
        hf = f_ref[b]
        f = lb + (1.0 - lb) * _sigmoid(hf)
        f = jnp.where(valid, f, 1.0)
        logf = jnp.log(f)
        k_all = 1.0 - f
        hq = q_ref[b].astype(F32)
        q_all = hq * _sigmoid(hq)
        expo = _dot_exact_lhs(mat, logf)
        e_b = jnp.exp(expo[0:CHUNK])
        e_bl = jnp.exp(expo[CHUNK:2 * CHUNK])
        v_all = v_ref[b]
        hg = g_ref[b].astype(F32)
        gate = hg * _sigmoid(hg)
        for h in range(N_HEADS):
            sl = slice(h * HG_DK, (h + 1) * HG_DK)
            q = q_all[:, sl]
            k = k_all[:, sl]
            scores = jnp.where(diag_mask, _dot_nt(q.astype(BF16), k.astype(BF16)), 0.0)
            for li, mask in enumerate(level_masks):
                r0 = (2 + 2 * li) * CHUNK
                qd = (q * jnp.exp(expo[r0:r0 + CHUNK, sl])).astype(BF16)
                kd = (k * jnp.exp(expo[r0 + CHUNK:r0 + 2 * CHUNK, sl])).astype(BF16)
                scores = jnp.where(mask, _dot_nt(qd, kd), scores)
            v = v_all[:, sl]
            st = st_ref[b, h]
            o = _dot(scores.astype(BF16), v) + _dot_nt((q * e_b[:, sl]).astype(BF16), st.astype(BF16))
            kl = (k * e_bl[:, sl]).astype(BF16)
            vt = v.astype(F32).T.astype(BF16)
            st_ref[b, h] = e_b[CHUNK - 1:CHUNK, sl] * st + _dot(vt, kl)
            ms = jnp.mean(o * o, axis=-1, keepdims=True)
            y = (o * lax.rsqrt(ms + EPS)) * nw_ref[:, sl] * gate[:, sl]
            y_ref[b, :, sl] = y.astype(y_ref.dtype)


def _hgrn2(hq, hf, hi, hg, mats, lb, norm_w, bb):
    bsz, t, _ = hq.shape
    nc = t // CHUNK
    blk = pl.BlockSpec((bb, CHUNK, HG_W), lambda i, c: (i, c, 0))
    full = lambda a: pl.BlockSpec(a.shape, lambda i, c: (0, 0))
    return pl.pallas_call(
        _hgrn2_kernel,
        grid=(bsz // bb, nc),
        in_specs=[blk, blk, blk, blk, full(mats), full(lb), full(norm_w)],
        out_specs=blk,
        out_shape=jax.ShapeDtypeStruct((bsz, t, HG_W), BF16),
        scratch_shapes=[pltpu.VMEM((bb, N_HEADS, HG_DV, HG_DK), F32)],
        compiler_params=_cparams(("parallel", "arbitrary")),
        name="hgrn2",
    )(hq, hf, hi, hg, mats, lb, norm_w)


def _log_sigmoid(x):
    return jnp.minimum(x, 0.0) - jnp.log(1.0 + jnp.exp(-jnp.abs(x)))


def _mlstm_kernel(mm_ref, mo_ref, gc_ref, gr_ref, tri_ref, cw_ref, cb_ref, wq_ref, wk_ref, wv_ref,
                  gbc_ref, gbr_ref, nw_ref, sk_ref, y_ref, s_ref, m_ref, tail_ref):
    c = pl.program_id(1)
    bb = mm_ref.shape[0]

    @pl.when(c == 0)
    def _():
        s_ref[...] = jnp.zeros_like(s_ref)
        m_ref[...] = jnp.zeros_like(m_ref)
        tail_ref[...] = jnp.zeros_like(tail_ref)

    pos_c = c * CHUNK + lax.broadcasted_iota(jnp.int32, (CHUNK, 1), 0)
    valid_c = pos_c >= N_PAD
    pos_r = c * CHUNK + lax.broadcasted_iota(jnp.int32, (1, CHUNK), 1)
    valid_r = pos_r >= N_PAD
    ti = lax.broadcasted_iota(jnp.int32, (CHUNK, CHUNK), 0)
    si = lax.broadcasted_iota(jnp.int32, (CHUNK, CHUNK), 1)
    causal = si <= ti
    tri = tri_ref[...]
    ones_v = jnp.ones((CHUNK, ML_DV), BF16)
    neg_inf = -jnp.inf

    for b in range(bb):
        mm = jnp.where(valid_c, mm_ref[b].astype(F32), 0.0)
        ext = jnp.concatenate([tail_ref[b], mm], axis=0)
        tail_ref[b] = mm[CHUNK - 8:CHUNK]
        conv = cb_ref[...]
        for j in range(ML_CONV):
            off = 8 - (ML_CONV - 1) + j
            conv = conv + cw_ref[j:j + 1, :] * ext[off:off + CHUNK]
        cact = conv * _sigmoid(conv)
        cact_b = cact.astype(BF16)
        mm_b = mm.astype(BF16)

        gcol = gc_ref[b] + gbc_ref[...]
        li_col = jnp.where(valid_c, gcol, neg_inf)
        lf_col = jnp.where(valid_c, _log_sigmoid(gcol), 0.0)
        b_col = _dot_exact_lhs(tri, lf_col)
        grow = gr_ref[b, 0] + gbr_ref[...]
        li_row = jnp.where(valid_r, grow, neg_inf)
        lf_row = jnp.where(valid_r, _log_sigmoid(grow), 0.0)
        r1, r2, r3 = _split3(lf_row)
        b_row = _dot_nt(r1, tri) + _dot_nt(r2, tri) + _dot_nt(r3, tri)

        mo = mo_ref[b].astype(F32)
        ogate = _sigmoid(mo)
        for h in range(N_HEADS):
            sl = slice(h * ML_DV, (h + 1) * ML_DV)
            q = (_dot(cact_b[:, sl], wq_ref[h]) * (ML_DK ** -0.5)).astype(BF16)
            k = _dot(cact_b[:, sl], wk_ref[h])
            v = _dot(mm_b[:, sl], wv_ref[h]).astype(BF16)
            v_aug = jnp.concatenate([v, ones_v], axis=1)
            bc = b_col[:, N_HEADS + h:N_HEADS + h + 1]
            lic = li_col[:, h:h + 1]
            br = b_row[N_HEADS + h:N_HEADS + h + 1, :]
            lir = li_row[h:h + 1, :]
            m_prev = m_ref[b, h][0:1, 0:1]
            d = jnp.where(causal, bc + (lir - br), neg_inf)
            a = bc + m_prev
            m_t = jnp.maximum(a, jnp.max(d, axis=-1, keepdims=True))
            w_intra = jnp.exp(d - m_t)
            w_inter = jnp.exp(a - m_t)
            qk = _dot_nt(q, k.astype(BF16)) * w_intra
            s_aug = s_ref[b, h]
            numden = _dot(qk.astype(BF16), v_aug) + w_inter * _dot(q, s_aug.astype(BF16))
            num = numden[:, :ML_DV]
            den = numden[:, ML_DV:]
            o = num / jnp.maximum(jnp.abs(den), jnp.exp(-m_t))
            g_last = bc[CHUNK - 1:CHUNK, :]
            e = g_last - bc + lic
            m_new = jnp.maximum(g_last + m_prev, jnp.max(e, axis=0, keepdims=True))
            w_s = jnp.exp(e - m_new)
            w_p = jnp.exp(g_last + m_prev - m_new)
            kw_t = (k * w_s).T.astype(BF16)
            s_ref[b, h] = w_p * s_aug + _dot(kw_t, v_aug)
            m_ref[b, h] = jnp.broadcast_to(m_new, m_ref.shape[2:])
            ms = jnp.mean(o * o, axis=-1, keepdims=True)
            y = ((o * lax.rsqrt(ms + EPS)) * nw_ref[:, sl] + sk_ref[:, sl] * cact[:, sl]) * ogate[:, sl]
            y_ref[b, :, sl] = y.astype(y_ref.dtype)


def _mlstm(mm, mo, g_col, g_row, tri, conv_w, conv_b, wq, wk, wv, gb_col, gb_row, norm_w, skip, bb):
    bsz, t, _ = mm.shape
    nc = t // CHUNK
    blk = pl.BlockSpec((bb, CHUNK, ML_W), lambda i, c: (i, c, 0))
    gcb = pl.BlockSpec((bb, CHUNK, LANES), lambda i, c: (i, c, 0))
    grb = pl.BlockSpec((bb, 1, 8, CHUNK), lambda i, c: (i, c, 0, 0))

    def full(a):
        nd = a.ndim
        return pl.BlockSpec(a.shape, lambda i, c: (0,) * nd)

    params = (tri, conv_w, conv_b, wq, wk, wv, gb_col, gb_row, norm_w, skip)
    return pl.pallas_call(
        _mlstm_kernel,
        grid=(bsz // bb, nc),
        in_specs=[blk, blk, gcb, grb] + [full(p) for p in params],
        out_specs=blk,
        out_shape=jax.ShapeDtypeStruct((bsz, t, ML_W), BF16),
        scratch_shapes=[
            pltpu.VMEM((bb, N_HEADS, ML_DK, 2 * ML_DV), F32),
            pltpu.VMEM((bb, N_HEADS, 8, LANES), F32),
            pltpu.VMEM((bb, 8, ML_W), F32),
        ],
        compiler_params=_cparams(("parallel", "arbitrary")),
        name="mlstm",
    )(mm, mo, g_col, g_row, *params)


def _merge_route_kernel(h_ref, yh_ref, ym_ref, gates_ref, wbh_ref, wbm_ref, wo_ref, nf_ref,
                        rw_ref, rb_ref, h2_ref, xn_ref, te_ref, tw_ref, rk_ref, cnt_ref, *, n_experts):
    d = h_ref.shape[1]
    tq = h_ref.shape[0]
    j = pl.program_id(1)

    @pl.when(jnp.logical_and(pl.program_id(0) == 0, j == 0))
    def _():
        cnt_ref[...] = jnp.zeros_like(cnt_ref)

    g0 = _sigmoid(gates_ref[:, :d].astype(F32))
    g1 = _sigmoid(gates_ref[:, d:].astype(F32))
    merged = g0 * _dot(yh_ref[...], wbh_ref[...]) + g1 * _dot(ym_ref[...], wbm_ref[...])
    h2 = h_ref[...] + _dot(merged.astype(BF16), wo_ref[...])
    _store_token_tiles(h2_ref, h2)
    ms = jnp.mean(h2 * h2, axis=-1, keepdims=True)
    xn = (h2 * lax.rsqrt(ms + EPS)) * nf_ref[...]
    _store_token_tiles(xn_ref, xn)
    x1, x2, _ = _split3(xn)
    logits = (_dot(x1, rw_ref[0]) + _dot(x1, rw_ref[1]) + _dot(x2, rw_ref[0])) + rb_ref[...]
    lane = lax.broadcasted_iota(jnp.int32, logits.shape, 1)
    work = jnp.where(lane < n_experts, logits, -jnp.inf)
    vals, idxs = [], []
    for _ in range(TOP_K):
        vmax = jnp.max(work, axis=-1, keepdims=True)
        imax = jnp.min(jnp.where(work == vmax, lane, LANES), axis=-1, keepdims=True)
        vals.append(vmax)
        idxs.append(imax)
        work = jnp.where(lane == imax, -jnp.inf, work)
    exps = [jnp.exp(v - vals[0]) for v in vals]
    tot = exps[0] + exps[1] + exps[2] + exps[3]
    te = jnp.zeros(logits.shape, jnp.int32)
    tw = jnp.zeros(logits.shape, F32)
    for kk in range(TOP_K):
        te = jnp.where(lane == kk, idxs[kk], te)
        tw = jnp.where(lane == kk, exps[kk] / tot, tw)
    te_ref[...] = te[:, :TOP_K]
    tw_ref[...] = tw[:, :TOP_K]

    valid = (j * tq + lax.broadcasted_iota(jnp.int32, (tq, 1), 0)) >= N_PAD
    onehots = [jnp.where(jnp.logical_and(lane == idxs[kk], valid), 1.0, 0.0) for kk in range(TOP_K)]
    oh_all = onehots[0] + onehots[1] + onehots[2] + onehots[3]
    ri = lax.broadcasted_iota(jnp.int32, (tq, tq), 0)
    ci = lax.broadcasted_iota(jnp.int32, (tq, tq), 1)
    earlier = jnp.where(ci < ri, 1.0, 0.0).astype(BF16)
    before = _dot(earlier, oh_all.astype(BF16)) + cnt_ref[...]
    rk = jnp.zeros(logits.shape, F32)
    for kk in range(TOP_K):
        rank_k = jnp.sum(jnp.where(lane == idxs[kk], before, 0.0), axis=-1, keepdims=True)
        rk = jnp.where(lane == kk, rank_k, rk)
        before = before + onehots[kk]
    rk_ref[...] = rk[:, :TOP_K].astype(jnp.int32)
    cnt_ref[...] = cnt_ref[...] + jnp.sum(oh_all, axis=0, keepdims=True)


def _merge_route(h2d, y_hg, y_ml, gates, wbh, wbm, wo, norm_ffn, rw_split, rb, n_experts, bsz, tq):
    m, d = h2d.shape
    nj = m // (bsz * tq)
    assert d == TOK_ROWS * LANES, "token-tile layout assumes one (8,128) tile per token"
    row = lambda n: pl.BlockSpec((tq, n), lambda b, j: (b * nj + j, 0))
    tiles = pl.BlockSpec((tq * TOK_ROWS, LANES), lambda b, j: (b * nj + j, 0))

    def full(a):
        nd = a.ndim
        return pl.BlockSpec(a.shape, lambda b, j: (0,) * nd)

    return pl.pallas_call(
        functools.partial(_merge_route_kernel, n_experts=n_experts),
        grid=(bsz, nj),
        in_specs=[row(d), row(HG_W), row(ML_W), row(2 * d), full(wbh), full(wbm), full(wo),
                  full(norm_ffn), full(rw_split), full(rb)],
        out_specs=[tiles, tiles, row(TOP_K), row(TOP_K), row(TOP_K),
                   pl.BlockSpec((1, LANES), lambda b, j: (0, 0))],
        out_shape=[
            jax.ShapeDtypeStruct((m * TOK_ROWS, LANES), F32),
            jax.ShapeDtypeStruct((m * TOK_ROWS, LANES), F32),
            jax.ShapeDtypeStruct((m, TOP_K), jnp.int32),
            jax.ShapeDtypeStruct((m, TOP_K), F32),
            jax.ShapeDtypeStruct((m, TOP_K), jnp.int32),
            jax.ShapeDtypeStruct((1, LANES), F32),
        ],
        compiler_params=_cparams(("arbitrary", "arbitrary")),
        name="merge_route",
    )(h2d, y_hg, y_ml, gates, wbh, wbm, wo, norm_ffn, rw_split, rb)


def _toks(first, n=1):
    return pl.ds(pl.multiple_of(first * TOK_ROWS, TOK_ROWS), n * TOK_ROWS)


def _dispatch_kernel(ps_ref, pe_ref, e_ref, rk_ref, x_ref, xs_hbm, zbuf, sem, *, n_experts, tm):
    b = pl.program_id(0)
    j = pl.program_id(1)
    tq = x_ref.shape[0] // TOK_ROWS

    @pl.when(jnp.logical_and(b == 0, j == 0))
    def _():
        zbuf[...] = jnp.zeros_like(zbuf)
        for e in range(n_experts):
            @pl.when(pe_ref[e] > ps_ref[e])
            def _():
                pltpu.make_async_copy(zbuf, xs_hbm.at[_toks(pe_ref[e] - tm, tm), :], sem).start()
        for e in range(n_experts):
            @pl.when(pe_ref[e] > ps_ref[e])
            def _():
                pltpu.make_async_copy(zbuf, xs_hbm.at[_toks(0, tm), :], sem).wait()

        def zero_tail(blk, carry):
            cp = pltpu.make_async_copy(zbuf, xs_hbm.at[_toks(blk * tm, tm), :], sem)
            cp.start()
            cp.wait()
            return carry
        lax.fori_loop(pe_ref[n_experts - 1] // tm, xs_hbm.shape[0] // (tm * TOK_ROWS), zero_tail, 0)

    def scatter_rows(lo):
        def body(r, carry):
            for kk in range(TOP_K):
                dst = ps_ref[e_ref[0, 0, r * TOP_K + kk]] + rk_ref[0, 0, r * TOP_K + kk]
                pltpu.make_async_copy(x_ref.at[_toks(r), :], xs_hbm.at[_toks(dst), :], sem).start()
            return carry
        lax.fori_loop(lo, tq, body, 0, unroll=2)
        n = tq - lo
        for _ in range(TOP_K):
            pltpu.make_async_copy(x_ref.at[_toks(0, n), :], xs_hbm.at[_toks(0, n), :], sem).wait()

    @pl.when(j == 0)
    def _():
        scatter_rows(N_PAD)

    @pl.when(j != 0)
    def _():
        scatter_rows(0)


def _dispatch(pstart, pend, top_e, rank, xn_t, n_rows, bsz, tq, tm):
    m = xn_t.shape[0] // TOK_ROWS
    nj = m // (bsz * tq)
    n_experts = pstart.shape[0]
    sm = lambda a: a.reshape(bsz * nj, 1, tq * TOP_K)
    smem_blk = pl.BlockSpec((1, 1, tq * TOP_K), lambda b, j, ps, pe: (b * nj + j, 0, 0),
                            memory_space=pltpu.SMEM)
    grid_spec = pltpu.PrefetchScalarGridSpec(
        num_scalar_prefetch=2,
        grid=(bsz, nj),
        in_specs=[smem_blk, smem_blk,
                  pl.BlockSpec((tq * TOK_ROWS, LANES), lambda b, j, ps, pe: (b * nj + j, 0))],
        out_specs=pl.BlockSpec(memory_space=pl.ANY),
        scratch_shapes=[pltpu.VMEM((tm * TOK_ROWS, LANES), F32), pltpu.SemaphoreType.DMA(())],
    )
    return pl.pallas_call(
        functools.partial(_dispatch_kernel, n_experts=n_experts, tm=tm),
        grid_spec=grid_spec,
        out_shape=jax.ShapeDtypeStruct((n_rows * TOK_ROWS, LANES), F32),
        compiler_params=_cparams(("arbitrary", "arbitrary")),
        name="dispatch",
    )(pstart, pend, sm(top_e), sm(rank), xn_t)


CAST_ROWS = 256


def _experts_kernel(be_ref, nu_ref, x_ref, wgu_ref, bg_ref, bu_ref, wd_ref, bd_ref, y_ref,
                    wg_s, wu_s, wd_s):
    i = pl.program_id(0)
    n_used = nu_ref[0]
    dff = wg_s.shape[0]
    new_expert = jnp.logical_or(i == 0, be_ref[i] != be_ref[jnp.maximum(i - 1, 0)])

    @pl.when(jnp.logical_and(new_expert, i < n_used))
    def _():
        for c in range(dff // CAST_ROWS):
            rows = pl.ds(c * CAST_ROWS, CAST_ROWS)
            wd_s[rows, :] = wd_ref[0, rows, :].astype(BF16)
            for s in range(wgu_ref.shape[1]):
                lanes = pl.ds(s * LANES, LANES)
                wg_s[rows, lanes] = wgu_ref[
                    0, s, pl.ds(2 * c * CAST_ROWS, CAST_ROWS, stride=2), :].astype(BF16)
                wu_s[rows, lanes] = wgu_ref[
                    0, s, pl.ds(2 * c * CAST_ROWS + 1, CAST_ROWS, stride=2), :].astype(BF16)

    @pl.when(i < n_used)
    def _():
        tm = x_ref.shape[0] // TOK_ROWS
        xb = _load_token_tiles(x_ref, tm).astype(BF16)
        g = _dot_nt(xb, wg_s[...]) + bg_ref[0]
        u = _dot_nt(xb, wu_s[...]) + bu_ref[0]
        gate = jnp.minimum(g, SWIGLU_LIMIT)
        up = jnp.clip(u, -SWIGLU_LIMIT, SWIGLU_LIMIT)
        act = (up + 1.0) * gate * _sigmoid(SWIGLU_ALPHA * gate)
        _store_token_tiles(y_ref, _dot(act.astype(BF16), wd_s[...]) + bd_ref[0])

    @pl.when(i >= n_used)
    def _():
        y_ref[...] = jnp.zeros_like(y_ref)


def _experts(block_e, n_used, xs, w_gu_t, b_g, b_u, w_d, b_d, tm):
    n_blocks = block_e.shape[0]
    dff, d = w_d.shape[1:]
    tile_blk = lambda f: pl.BlockSpec((tm * TOK_ROWS, LANES), f)
    assert dff % CAST_ROWS == 0
    wspec = lambda k, n: pl.BlockSpec((1, k, n), lambda i, be, nu: (be[i], 0, 0))
    grid_spec = pltpu.PrefetchScalarGridSpec(
        num_scalar_prefetch=2,
        grid=(n_blocks,),
        in_specs=[
            tile_blk(lambda i, be, nu: (jnp.minimum(i, nu[0] - 1), 0)),
            pl.BlockSpec((1, d // LANES, 2 * dff, LANES), lambda i, be, nu: (be[i], 0, 0, 0)),
            wspec(1, dff), wspec(1, dff), wspec(dff, d), wspec(1, d),
        ],
        out_specs=tile_blk(lambda i, be, nu: (i, 0)),
        scratch_shapes=[pltpu.VMEM((dff, d), BF16), pltpu.VMEM((dff, d), BF16), pltpu.VMEM((dff, d), BF16)],
    )
    return pl.pallas_call(
        _experts_kernel,
        grid_spec=grid_spec,
        out_shape=jax.ShapeDtypeStruct((n_blocks * tm * TOK_ROWS, LANES), F32),
        compiler_params=_cparams(("arbitrary",)),
        name="experts",
    )(block_e, n_used, xs, w_gu_t, b_g, b_u, w_d, b_d)


def _combine_kernel(ps_ref, e_ref, en_ref, rk_ref, rkn_ref, hsrc_ref, hsrcn_ref, tw_ref, y_hbm, h_hbm,
                    nw_ref, o_ref, ybuf, hbuf, obuf, sem):
    i = pl.program_id(0)
    n = pl.num_programs(0)
    tt = o_ref.shape[0]
    d = o_ref.shape[1]
    slot = i % 2

    def start(src_e_ref, src_rk_ref, src_h_ref, dst_slot):
        def body(r, carry):
            for kk in range(TOP_K):
                src = ps_ref[src_e_ref[0, 0, r * TOP_K + kk]] + src_rk_ref[0, 0, r * TOP_K + kk]
                pltpu.make_async_copy(
                    y_hbm.at[_toks(src), :],
                    ybuf.at[dst_slot, _toks(kk * tt + r), :],
                    sem.at[dst_slot]).start()
            return carry
        lax.fori_loop(0, tt, body, 0, unroll=2)
        pltpu.make_async_copy(
            h_hbm.at[_toks(src_h_ref[0, 0, 0], tt), :], hbuf.at[dst_slot], sem.at[dst_slot]).start()

    @pl.when(i == 0)
    def _():
        start(e_ref, rk_ref, hsrc_ref, 0)

    @pl.when(i + 1 < n)
    def _():
        start(en_ref, rkn_ref, hsrcn_ref, 1 - slot)

    pltpu.make_async_copy(ybuf.at[slot], ybuf.at[slot], sem.at[slot]).wait()
    pltpu.make_async_copy(hbuf.at[slot], hbuf.at[slot], sem.at[slot]).wait()
    acc = hbuf[slot]
    tw = tw_ref[...]
    nt = tt * TOK_ROWS
    for kk in range(TOP_K):
        acc = acc + tw[:, kk:kk + 1] * ybuf[slot, kk * nt:(kk + 1) * nt, :]
    a3 = acc.reshape(tt, TOK_ROWS, LANES)
    ssq = jnp.sum(jnp.sum(a3 * a3, axis=2, keepdims=True), axis=1, keepdims=True)
    o3 = (a3 * lax.rsqrt(ssq * (1.0 / d) + EPS)) * nw_ref[...][None]
    obuf[...] = o3.reshape(nt, LANES)
    for s in range(TOK_ROWS):
        o_ref[:, s * LANES:(s + 1) * LANES] = obuf[pl.ds(s, tt, stride=TOK_ROWS), :]


def _combine(pstart, e_seq, rk_seq, tw_rows, h_src, y_rows, h2_t, norm_w_t, d, tt):
    n_tiles = h_src.shape[0]
    last = n_tiles - 1
    nt = tt * TOK_ROWS
    cur = lambda i, ps: (i, 0, 0)
    nxt = lambda i, ps: (jnp.minimum(i + 1, last), 0, 0)
    sm = lambda a: a.reshape(n_tiles, 1, tt * TOP_K)
    idx_blk = lambda f: pl.BlockSpec((1, 1, TOP_K * tt), f, memory_space=pltpu.SMEM)
    one_blk = lambda f: pl.BlockSpec((1, 1, 1), f, memory_space=pltpu.SMEM)
    grid_spec = pltpu.PrefetchScalarGridSpec(
        num_scalar_prefetch=1,
        grid=(n_tiles,),
        in_specs=[
            idx_blk(cur), idx_blk(nxt), idx_blk(cur), idx_blk(nxt), one_blk(cur), one_blk(nxt),
            pl.BlockSpec((nt, TOP_K), lambda i, ps: (i, 0)),
            pl.BlockSpec(memory_space=pl.ANY),
            pl.BlockSpec(memory_space=pl.ANY),
            pl.BlockSpec(norm_w_t.shape, lambda i, ps: (0, 0)),
        ],
        out_specs=pl.BlockSpec((tt, d), lambda i, ps: (i, 0)),
        scratch_shapes=[pltpu.VMEM((2, TOP_K * nt, LANES), F32), pltpu.VMEM((2, nt, LANES), F32),
                        pltpu.VMEM((nt, LANES), F32), pltpu.SemaphoreType.DMA((2,))],
    )
    return pl.pallas_call(
        _combine_kernel,
        grid_spec=grid_spec,
        out_shape=jax.ShapeDtypeStruct((n_tiles * tt, d), F32),
        compiler_params=_cparams(("arbitrary",)),
        name="combine",
    )(pstart, sm(e_seq), sm(e_seq), sm(rk_seq), sm(rk_seq), h_src, h_src, tw_rows, y_rows, h2_t, norm_w_t)


def _pick(n, prefs):
    for p in prefs:
        if n % p == 0:
            return p
    raise ValueError(f"no tile in {prefs} divides {n}")


def kernel(x, meta_tokens, hg_lb_logits, norm_mix, w_in, hg_norm, ml_conv_w, ml_conv_b, ml_wq, ml_wk, ml_wv,
           ml_gate_b, ml_norm, ml_skip, w_branch_hg, w_branch_ml, w_out, norm_ffn, router_w, router_b,
           exp_w_gu, exp_b_gu, exp_w_down, exp_b_down, norm_final):
    bsz, seq, d = x.shape
    assert norm_mix.shape[0] == 1, "single-layer block"
    assert seq % CHUNK == 0 and d % LANES == 0
    t = CHUNK + seq
    m_rows = bsz * t
    n_experts = router_w.shape[-1]
    dff = exp_w_down.shape[2]
    assert n_experts <= LANES

    h = jnp.concatenate([
        jnp.zeros((bsz, N_PAD, d), x.dtype),
        jnp.broadcast_to(meta_tokens.astype(x.dtype)[None], (bsz, N_META, d)),
        x,
    ], axis=1)
    h2d = h.reshape(m_rows, d)
    lower_bounds = jnp.cumsum(jax.nn.softmax(hg_lb_logits.astype(F32), axis=0), axis=0)

    w = w_in[0]
    n_a = 4 * HG_W + 2 * ML_W
    w_a = w[:, :n_a].astype(BF16)
    w_m = jnp.pad(w[:, n_a:n_a + 2 * N_HEADS], ((0, 0), (0, LANES - 2 * N_HEADS))).astype(BF16)
    w_g = w[:, n_a + 2 * N_HEADS:].astype(BF16)
    tm1 = _pick(m_rows, (512, 256, 128, 64))
    hq, hf, hi, hg, mm, mo, gates, mif = _in_proj(h2d, norm_mix[0][None], w_a, w_g, w_m, tm1)

    bb = _pick(bsz, (4, 2, 1))
    r3 = lambda a: a.reshape(bsz, t, a.shape[-1])
    mats = jnp.asarray(_hgrn_matrices(), BF16)
    y_hg = _hgrn2(r3(hq), r3(hf), r3(hi), r3(hg), mats, lower_bounds[0][None], hg_norm[0][None], bb)

    nc = t // CHUNK
    g_col = r3(mif)
    g_row = jnp.swapaxes(g_col[:, :, :2 * N_HEADS].reshape(bsz, nc, CHUNK, 2 * N_HEADS), 2, 3)
    tri = jnp.asarray(np.tril(np.ones((CHUNK, CHUNK), np.float32)), BF16)
    gb = ml_gate_b[0].astype(F32)
    gb_col = jnp.pad(gb, (0, LANES - 2 * N_HEADS))[None]
    gb_row = jnp.broadcast_to(gb[:, None], (2 * N_HEADS, CHUNK))
    y_ml = _mlstm(r3(mm), r3(mo), g_col, g_row, tri, ml_conv_w[0], ml_conv_b[0][None],
                  ml_wq[0].astype(BF16), ml_wk[0].astype(BF16), ml_wv[0].astype(BF16),
                  gb_col, gb_row, ml_norm[0][None], ml_skip[0][None], bb)

    rw = jnp.pad(router_w[0].astype(F32), ((0, 0), (0, LANES - n_experts)))
    rw1 = rw.astype(BF16)
    rw2 = (rw - rw1.astype(F32)).astype(BF16)
    rw_split = jnp.stack([rw1, rw2])
    rb = jnp.pad(router_b[0].astype(F32), (0, LANES - n_experts))[None]
    tq = _pick(t, (352, 192, 64))
    h2, xn2, top_e, top_w, rank, cnt = _merge_route(
        h2d, y_hg.reshape(m_rows, HG_W), y_ml.reshape(m_rows, ML_W), gates,
        w_branch_hg[0].astype(BF16), w_branch_ml[0].astype(BF16), w_out[0].astype(BF16),
        norm_ffn[0][None], rw_split, rb, n_experts, bsz, tq)

    tm6 = 256
    n_assign = bsz * (t - N_PAD) * TOP_K
    counts = cnt[0, :n_experts].astype(jnp.int32)
    padded = ((counts + tm6 - 1) // tm6) * tm6
    pend = jnp.cumsum(padded).astype(jnp.int32)
    pstart = pend - padded
    n_blocks = -(-n_assign // tm6) + n_experts
    n_rows = n_blocks * tm6
    blk_start = jnp.arange(n_blocks, dtype=jnp.int32) * tm6
    block_e = jnp.minimum(jnp.sum((blk_start[:, None] >= pend[None, :]).astype(jnp.int32), axis=1),
                          n_experts - 1)
    n_used = (pend[-1] // tm6)[None]
    xs = _dispatch(pstart, pend, top_e, rank, xn2, n_rows, bsz, tq, tm6)

    w_gu_t = jnp.swapaxes(exp_w_gu[0].reshape(n_experts, d // LANES, LANES, 2 * dff), 2, 3)
    bgu = exp_b_gu[0]
    y_rows = _experts(block_e, n_used, xs, w_gu_t, bgu[:, None, 0::2], bgu[:, None, 1::2],
                      exp_w_down[0], exp_b_down[0][:, None, :], tm6)

    tt = _pick(seq, (128, 64))
    tiles_per_b = seq // tt
    n_tiles = bsz * tiles_per_b
    seq_part = lambda a: a.reshape(bsz, t, TOP_K)[:, CHUNK:].reshape(bsz * seq, TOP_K)
    h_src = (jnp.arange(bsz, dtype=jnp.int32)[:, None] * t + CHUNK
             + jnp.arange(tiles_per_b, dtype=jnp.int32)[None, :] * tt).reshape(n_tiles, 1, 1)
    tw_rows = jnp.repeat(seq_part(top_w), TOK_ROWS, axis=0)
    out = _combine(pstart, seq_part(top_e), seq_part(rank), tw_rows, h_src, y_rows, h2,
                   norm_final.reshape(TOK_ROWS, LANES), d, tt)
    return out.reshape(bsz, seq, d)
```

```python
import functools

import numpy as np
import jax
import jax.numpy as jnp
from jax import lax
from jax.experimental import pallas as pl
from jax.experimental.pallas import tpu as pltpu

F32 = jnp.float32
BF16 = jnp.bfloat16

N_META = 16
CHUNK = 64
N_PAD = CHUNK - N_META
EPS = 1e-6

N_HEADS = 4
HG_DK = 128
HG_DV = 128
HG_W = N_HEADS * HG_DV
ML_DK = 64
ML_DV = 128
ML_W = N_HEADS * ML_DV
ML_CONV = 4
TOP_K = 4
SWIGLU_LIMIT = 7.0
SWIGLU_ALPHA = 1.702

LANES = 128
VMEM_LIMIT_BYTES = 56 * 1024 * 1024

HG_LEVELS = (32, 16, 8, 4, 2, 1)


def _cparams(sem):
    return pltpu.CompilerParams(dimension_semantics=sem, vmem_limit_bytes=VMEM_LIMIT_BYTES)


def _sigmoid(x):
    return 1.0 / (1.0 + jnp.exp(-x))


def _split3(x):
    x1 = x.astype(BF16)
    r1 = x - x1.astype(F32)
    x2 = r1.astype(BF16)
    x3 = (r1 - x2.astype(F32)).astype(BF16)
    return x1, x2, x3


def _dot(a, b):
    return jnp.dot(a, b, preferred_element_type=F32)


TOK_ROWS = 8


def _store_token_tiles(ref, x):
    n = x.shape[0]
    for s in range(TOK_ROWS):
        ref[pl.ds(s, n, stride=TOK_ROWS), :] = x[:, s * LANES:(s + 1) * LANES]


def _load_token_tiles(ref, n):
    return jnp.concatenate([ref[pl.ds(s, n, stride=TOK_ROWS), :] for s in range(TOK_ROWS)], axis=1)


def _dot_nt(a, b):
    return lax.dot_general(a, b, (((1,), (1,)), ((), ())), preferred_element_type=F32)


def _dot_exact_lhs(m_bf16, x):
    x1, x2, x3 = _split3(x)
    return _dot(m_bf16, x1) + _dot(m_bf16, x2) + _dot(m_bf16, x3)


def _in_proj_kernel(h_ref, nw_ref, wa_ref, wg_ref, wm_ref,
                    q_ref, f_ref, i_ref, g_ref, mm_ref, mo_ref, gates_ref, mif_ref):
    x = h_ref[...]
    ms = jnp.mean(x * x, axis=-1, keepdims=True)
    xb = ((x * lax.rsqrt(ms + EPS)) * nw_ref[...]).astype(BF16)
    outs = (q_ref, f_ref, i_ref, g_ref, mm_ref, mo_ref)
    for j, o_ref in enumerate(outs):
        o_ref[...] = _dot(xb, wa_ref[:, j * HG_W:(j + 1) * HG_W]).astype(o_ref.dtype)
    d_model = gates_ref.shape[1] // 2
    for j in range(2):
        gates_ref[:, j * d_model:(j + 1) * d_model] = _dot(
            xb, wg_ref[:, j * d_model:(j + 1) * d_model]).astype(gates_ref.dtype)
    mif_ref[...] = _dot(xb, wm_ref[...])


def _in_proj(h2d, norm_w, w_a, w_g, w_m, tm):
    m, d = h2d.shape
    row = lambda n: pl.BlockSpec((tm, n), lambda i: (i, 0))
    full = lambda a: pl.BlockSpec(a.shape, lambda i: (0, 0))
    out_shape = [
        jax.ShapeDtypeStruct((m, HG_W), BF16),
        jax.ShapeDtypeStruct((m, HG_W), F32),
        jax.ShapeDtypeStruct((m, HG_W), BF16),
        jax.ShapeDtypeStruct((m, HG_W), BF16),
        jax.ShapeDtypeStruct((m, ML_W), BF16),
        jax.ShapeDtypeStruct((m, ML_W), BF16),
        jax.ShapeDtypeStruct((m, 2 * d), BF16),
        jax.ShapeDtypeStruct((m, LANES), F32),
    ]
    return pl.pallas_call(
        _in_proj_kernel,
        grid=(m // tm,),
        in_specs=[row(d), full(norm_w), full(w_a), full(w_g), full(w_m)],
        out_specs=[row(s.shape[1]) for s in out_shape],
        out_shape=out_shape,
        compiler_params=_cparams(("parallel",)),
        name="in_proj",
    )(h2d, norm_w, w_a, w_g, w_m)


def _hgrn2_kernel(q_ref, f_ref, v_ref, g_ref, tri_ref, lb_ref, nw_ref, y_ref, *st_refs):
    c = pl.program_id(1)
    bb = q_ref.shape[0]

    @pl.when(c == 0)
    def _():
        for st_ref in st_refs:
            st_ref[...] = jnp.zeros_like(st_ref)

    row = lax.broadcasted_iota(jnp.int32, (CHUNK, 1), 0)
    valid = (c * CHUNK + row) >= N_PAD
    ti = lax.broadcasted_iota(jnp.int32, (CHUNK, CHUNK), 0)
    si = lax.broadcasted_iota(jnp.int32, (CHUNK, CHUNK), 1)
    diag_mask = ti == si
    level_masks = {}
    for m in HG_LEVELS:
        same_pair = (ti & ~(2 * m - 1)) == (si & ~(2 * m - 1))
        level_masks[m] = same_pair & ((ti & m) != 0) & ((si & m) == 0)

    lb = lb_ref[...]
    tri = tri_ref[...]
    per_b = []
    for b in range(bb):
        hf = f_ref[b]
        f = lb + (1.0 - lb) * _sigmoid(hf)
        f = jnp.where(valid, f, 1.0)
        logf = jnp.log(f)
        k_all = 1.0 - f
        hq = q_ref[b].astype(F32)
        q_all = hq * _sigmoid(hq)
        b_cum = _dot_exact_lhs(tri, logf)
        e_b = jnp.exp(b_cum)
        e_bl = jnp.exp(b_cum[CHUNK - 1:CHUNK] - b_cum)

        q_fac, k_fac = {}, {}
        for m in HG_LEVELS:
            if m >= 4:
                grp = b_cum.reshape(CHUNK // (2 * m), 2 * m, HG_W)
                e = (grp - grp[:, m - 1:m, :]).reshape(CHUNK, HG_W)
                q_fac[m] = jnp.exp(jnp.minimum(e, 0.0))
                k_fac[m] = jnp.exp(jnp.minimum(-e, 0.0))
        f_prev = pltpu.roll(f, 1, 0)
        f_next = pltpu.roll(f, CHUNK - 1, 0)
        r4 = row & 3
        q_fac[2] = jnp.where(r4 == 2, f, jnp.where(r4 == 3, f * f_prev, 1.0))
        k_fac[2] = jnp.where(r4 == 0, f_next, 1.0)
        q_fac[1] = jnp.where((row & 1) == 1, f, 1.0)

        hg = g_ref[b].astype(F32)
        per_b.append((q_all, k_all, q_fac, k_fac, e_b, e_bl, v_ref[b], hg * _sigmoid(hg)))

    pairs = [(b, h) for b in range(bb) for h in range(N_HEADS)]
    sls = [slice(h * HG_DK, (h + 1) * HG_DK) for h in range(N_HEADS)]

    scores_all = []
    for b, h in pairs:
        q_all, k_all, q_fac, k_fac = per_b[b][:4]
        q = q_all[:, sls[h]]
        k = k_all[:, sls[h]]
        kb = k.astype(BF16)
        scores = jnp.where(diag_mask, _dot_nt(q.astype(BF16), kb), 0.0)
        for m in HG_LEVELS:
            qd = (q * q_fac[m][:, sls[h]]).astype(BF16)
            kd = (k * k_fac[m][:, sls[h]]).astype(BF16) if m in k_fac else kb
            scores = jnp.where(level_masks[m], _dot_nt(qd, kd), scores)
        scores_all.append(scores.astype(BF16))

    sts = [st_ref[...] for st_ref in st_refs]
    outs = []
    for p, (b, h) in enumerate(pairs):
        q_all, e_b, v_all = per_b[b][0], per_b[b][4], per_b[b][6]
        qe = (q_all[:, sls[h]] * e_b[:, sls[h]]).astype(BF16)
        outs.append(_dot(scores_all[p], v_all[:, sls[h]]) + _dot_nt(qe, sts[p].astype(BF16)))

    for p, (b, h) in enumerate(pairs):
        k_all, e_b, e_bl, v_all = per_b[b][1], per_b[b][4], per_b[b][5], per_b[b][6]
        kl = (k_all[:, sls[h]] * e_bl[:, sls[h]]).astype(BF16)
        vt = v_all[:, sls[h]].astype(F32).T.astype(BF16)
        st_refs[p][...] = e_b[CHUNK - 1:CHUNK, sls[h]] * sts[p] + _dot(vt, kl)

    for p, (b, h) in enumerate(pairs):
        o = outs[p]
        ms = jnp.mean(o * o, axis=-1, keepdims=True)
        y = (o * lax.rsqrt(ms + EPS)) * nw_ref[:, sls[h]] * per_b[b][7][:, sls[h]]
        y_ref[b, :, sls[h]] = y.astype(y_ref.dtype)


def _hgrn2(hq, hf, hi, hg, tri, lb, norm_w, bb):
    bsz, t, _ = hq.shape
    nc = t // CHUNK
    blk = pl.BlockSpec((bb, CHUNK, HG_W), lambda i, c: (i, c, 0))
    full = lambda a: pl.BlockSpec(a.shape, lambda i, c: (0, 0))
    return pl.pallas_call(
        _hgrn2_kernel,
        grid=(bsz // bb, nc),
        in_specs=[blk, blk, blk, blk, full(tri), full(lb), full(norm_w)],
        out_specs=blk,
        out_shape=jax.ShapeDtypeStruct((bsz, t, HG_W), BF16),
        scratch_shapes=[pltpu.VMEM((HG_DV, HG_DK), F32)] * (bb * N_HEADS),
        compiler_params=_cparams(("parallel", "arbitrary")),
        name="hgrn2",
    )(hq, hf, hi, hg, tri, lb, norm_w)


def _log_sigmoid(x):
    return jnp.minimum(x, 0.0) - jnp.log(1.0 + jnp.exp(-jnp.abs(x)))


def _mlstm_kernel(mm_ref, mo_ref, gc_ref, gr_ref, tri_ref, cw_ref, cb_ref, wq_ref, wk_ref, wv_ref,
                  gbc_ref, gbr_ref, nw_ref, sk_ref, y_ref, *scratch):
    c = pl.program_id(1)
    bb = mm_ref.shape[0]
    n_pairs = bb * N_HEADS
    s_refs = scratch[:n_pairs]
    m_refs = scratch[n_pairs:2 * n_pairs]
    tail_refs = scratch[2 * n_pairs:]

    @pl.when(c == 0)
    def _():
        for ref in scratch:
            ref[...] = jnp.zeros_like(ref)

    pos_c = c * CHUNK + lax.broadcasted_iota(jnp.int32, (CHUNK, 1), 0)
    valid_c = pos_c >= N_PAD
    pos_r = c * CHUNK + lax.broadcasted_iota(jnp.int32, (1, CHUNK), 1)
    valid_r = pos_r >= N_PAD
    ti = lax.broadcasted_iota(jnp.int32, (CHUNK, CHUNK), 0)
    si = lax.broadcasted_iota(jnp.int32, (CHUNK, CHUNK), 1)
    causal = si <= ti
    tri = tri_ref[...]
    ones_v = jnp.ones((CHUNK, ML_DV), BF16)
    neg_inf = -jnp.inf

    per_b = []
    for b in range(bb):
        mm = jnp.where(valid_c, mm_ref[b].astype(F32), 0.0)
        ext = jnp.concatenate([tail_refs[b][...], mm], axis=0)
        tail_refs[b][...] = mm[CHUNK - 8:CHUNK]
        conv = cb_ref[...]
        for j in range(ML_CONV):
            off = 8 - (ML_CONV - 1) + j
            conv = conv + cw_ref[j:j + 1, :] * ext[off:off + CHUNK]
        cact = conv * _sigmoid(conv)
        cact_b = cact.astype(BF16)
        mm_b = mm.astype(BF16)

        gcol = gc_ref[b] + gbc_ref[...]
        li_col = jnp.where(valid_c, gcol, neg_inf)
        lf_col = jnp.where(valid_c, _log_sigmoid(gcol), 0.0)
        b_col = _dot_exact_lhs(tri, lf_col)
        grow = gr_ref[b, 0] + gbr_ref[...]
        li_row = jnp.where(valid_r, grow, neg_inf)
        lf_row = jnp.where(valid_r, _log_sigmoid(grow), 0.0)
        r1, r2, r3 = _split3(lf_row)
        b_row = _dot_nt(r1, tri) + _dot_nt(r2, tri) + _dot_nt(r3, tri)

        ogate = _sigmoid(mo_ref[b].astype(F32))
        per_b.append((cact, cact_b, mm_b, li_col, b_col, li_row, b_row, ogate))

    pairs = [(b, h) for b in range(bb) for h in range(N_HEADS)]
    sls = [slice(h * ML_DV, (h + 1) * ML_DV) for h in range(N_HEADS)]

    qs, ks, vs = [], [], []
    for b, h in pairs:
        cact_b, mm_b = per_b[b][1], per_b[b][2]
        qs.append((_dot(cact_b[:, sls[h]], wq_ref[h]) * (ML_DK ** -0.5)).astype(BF16))
        ks.append(_dot(cact_b[:, sls[h]], wk_ref[h]))
        v = _dot(mm_b[:, sls[h]], wv_ref[h]).astype(BF16)
        vs.append(jnp.concatenate([v, ones_v], axis=1))

    bcs, lics, m_prevs, m_ts, w_intras, w_inters = [], [], [], [], [], []
    for p, (b, h) in enumerate(pairs):
        li_col, b_col, li_row, b_row = per_b[b][3:7]
        bc = b_col[:, N_HEADS + h:N_HEADS + h + 1]
        br = b_row[N_HEADS + h:N_HEADS + h + 1, :]
        lir = li_row[h:h + 1, :]
        m_prev = m_refs[p][0:1, 0:1]
        d = jnp.where(causal, bc + (lir - br), neg_inf)
        a = bc + m_prev
        m_t = jnp.maximum(a, jnp.max(d, axis=-1, keepdims=True))
        bcs.append(bc)
        lics.append(li_col[:, h:h + 1])
        m_prevs.append(m_prev)
        m_ts.append(m_t)
        w_intras.append(jnp.exp(d - m_t))
        w_inters.append(jnp.exp(a - m_t))

    qks = [(_dot_nt(qs[p], ks[p].astype(BF16)) * w_intras[p]).astype(BF16) for p in range(len(pairs))]
    s_augs = [s_refs[p][...] for p in range(len(pairs))]
    numdens = [_dot(qks[p], vs[p]) + w_inters[p] * _dot(qs[p], s_augs[p].astype(BF16))
               for p in range(len(pairs))]

    for p, (b, h) in enumerate(pairs):
        bc, lic, m_prev = bcs[p], lics[p], m_prevs[p]
        g_last = bc[CHUNK - 1:CHUNK, :]
        e = g_last - bc + lic
        m_new = jnp.maximum(g_last + m_prev, jnp.max(e, axis=0, keepdims=True))
        w_s = jnp.exp(e - m_new)
        w_p = jnp.exp(g_last + m_prev - m_new)
        kw_t = (ks[p] * w_s).T.astype(BF16)
        s_refs[p][...] = w_p * s_augs[p] + _dot(kw_t, vs[p])
        m_refs[p][...] = jnp.broadcast_to(m_new, m_refs[p].shape)

    for p, (b, h) in enumerate(pairs):
        cact, ogate = per_b[b][0], per_b[b][7]
        num = numdens[p][:, :ML_DV]
        den = numdens[p][:, ML_DV:]
        o = num / jnp.maximum(jnp.abs(den), jnp.exp(-m_ts[p]))
        ms = jnp.mean(o * o, axis=-1, keepdims=True)
        sl = sls[h]
        y = ((o * lax.rsqrt(ms + EPS)) * nw_ref[:, sl] + sk_ref[:, sl] * cact[:, sl]) * ogate[:, sl]
        y_ref[b, :, sl] = y.astype(y_ref.dtype)


def _mlstm(mm, mo, g_col, g_row, tri, conv_w, conv_b, wq, wk, wv, gb_col, gb_row, norm_w, skip, bb):
    bsz, t, _ = mm.shape
    nc = t // CHUNK
    blk = pl.BlockSpec((bb, CHUNK, ML_W), lambda i, c: (i, c, 0))
    gcb = pl.BlockSpec((bb, CHUNK, LANES), lambda i, c: (i, c, 0))
    grb = pl.BlockSpec((bb, 1, 8, CHUNK), lambda i, c: (i, c, 0, 0))

    def full(a):
        nd = a.ndim
        return pl.BlockSpec(a.shape, lambda i, c: (0,) * nd)

    params = (tri, conv_w, conv_b, wq, wk, wv, gb_col, gb_row, norm_w, skip)
    return pl.pallas_call(
        _mlstm_kernel,
        grid=(bsz // bb, nc),
        in_specs=[blk, blk, gcb, grb] + [full(p) for p in params],
        out_specs=blk,
        out_shape=jax.ShapeDtypeStruct((bsz, t, ML_W), BF16),
        scratch_shapes=([pltpu.VMEM((ML_DK, 2 * ML_DV), F32)] * (bb * N_HEADS)
                        + [pltpu.VMEM((8, LANES), F32)] * (bb * N_HEADS)
                        + [pltpu.VMEM((8, ML_W), F32)] * bb),
        compiler_params=_cparams(("parallel", "arbitrary")),
        name="mlstm",
    )(mm, mo, g_col, g_row, *params)


def _merge_route_kernel(h_ref, yh_ref, ym_ref, gates_ref, wbh_ref, wbm_ref, wo_ref, nf_ref,
                        rw_ref, rb_ref, h2_ref, xn_ref, te_ref, tw_ref, rk_ref, cnt_ref, *, n_experts):
    d = h_ref.shape[1]
    tq = h_ref.shape[0]
    j = pl.program_id(1)

    @pl.when(jnp.logical_and(pl.program_id(0) == 0, j == 0))
    def _():
        cnt_ref[...] = jnp.zeros_like(cnt_ref)

    g0 = _sigmoid(gates_ref[:, :d].astype(F32))
    g1 = _sigmoid(gates_ref[:, d:].astype(F32))
    merged = g0 * _dot(yh_ref[...], wbh_ref[...]) + g1 * _dot(ym_ref[...], wbm_ref[...])
    h2 = h_ref[...] + _dot(merged.astype(BF16), wo_ref[...])
    _store_token_tiles(h2_ref, h2)
    ms = jnp.mean(h2 * h2, axis=-1, keepdims=True)
    xn = (h2 * lax.rsqrt(ms + EPS)) * nf_ref[...]
    _store_token_tiles(xn_ref, xn)
    x1, x2, _ = _split3(xn)
    logits = (_dot(x1, rw_ref[0]) + _dot(x1, rw_ref[1]) + _dot(x2, rw_ref[0])) + rb_ref[...]
    lane = lax.broadcasted_iota(jnp.int32, logits.shape, 1)
    work = jnp.where(lane < n_experts, logits, -jnp.inf)
    vals, idxs = [], []
    for _ in range(TOP_K):
        vmax = jnp.max(work, axis=-1, keepdims=True)
        imax = jnp.min(jnp.where(work == vmax, lane, LANES), axis=-1, keepdims=True)
        vals.append(vmax)
        idxs.append(imax)
        work = jnp.where(lane == imax, -jnp.inf, work)
    exps = [jnp.exp(v - vals[0]) for v in vals]
    tot = exps[0] + exps[1] + exps[2] + exps[3]
    te = jnp.zeros(logits.shape, jnp.int32)
    tw = jnp.zeros(logits.shape, F32)
    for kk in range(TOP_K):
        te = jnp.where(lane == kk, idxs[kk], te)
        tw = jnp.where(lane == kk, exps[kk] / tot, tw)
    te_ref[...] = te[:, :TOP_K]
    tw_ref[...] = tw[:, :TOP_K]

    valid = (j * tq + lax.broadcasted_iota(jnp.int32, (tq, 1), 0)) >= N_PAD
    onehots = [jnp.where(jnp.logical_and(lane == idxs[kk], valid), 1.0, 0.0) for kk in range(TOP_K)]
    oh_all = onehots[0] + onehots[1] + onehots[2] + onehots[3]
    ri = lax.broadcasted_iota(jnp.int32, (tq, tq), 0)
    ci = lax.broadcasted_iota(jnp.int32, (tq, tq), 1)
    earlier = jnp.where(ci < ri, 1.0, 0.0).astype(BF16)
    before = _dot(earlier, oh_all.astype(BF16)) + cnt_ref[...]
    rk = jnp.zeros(logits.shape, F32)
    for kk in range(TOP_K):
        rank_k = jnp.sum(jnp.where(lane == idxs[kk], before, 0.0), axis=-1, keepdims=True)
        rk = jnp.where(lane == kk, rank_k, rk)
        before = before + onehots[kk]
    rk_ref[...] = rk[:, :TOP_K].astype(jnp.int32)
    cnt_ref[...] = cnt_ref[...] + jnp.sum(oh_all, axis=0, keepdims=True)


def _merge_route(h2d, y_hg, y_ml, gates, wbh, wbm, wo, norm_ffn, rw_split, rb, n_experts, bsz, tq):
    m, d = h2d.shape
    nj = m // (bsz * tq)
    assert d == TOK_ROWS * LANES, "token-tile layout assumes one (8,128) tile per token"
    row = lambda n: pl.BlockSpec((tq, n), lambda b, j: (b * nj + j, 0))
    tiles = pl.BlockSpec((tq * TOK_ROWS, LANES), lambda b, j: (b * nj + j, 0))

    def full(a):
        nd = a.ndim
        return pl.BlockSpec(a.shape, lambda b, j: (0,) * nd)

    return pl.pallas_call(
        functools.partial(_merge_route_kernel, n_experts=n_experts),
        grid=(bsz, nj),
        in_specs=[row(d), row(HG_W), row(ML_W), row(2 * d), full(wbh), full(wbm), full(wo),
                  full(norm_ffn), full(rw_split), full(rb)],
        out_specs=[tiles, tiles, row(TOP_K), row(TOP_K), row(TOP_K),
                   pl.BlockSpec((1, LANES), lambda b, j: (0, 0))],
        out_shape=[
            jax.ShapeDtypeStruct((m * TOK_ROWS, LANES), F32),
            jax.ShapeDtypeStruct((m * TOK_ROWS, LANES), F32),
            jax.ShapeDtypeStruct((m, TOP_K), jnp.int32),
            jax.ShapeDtypeStruct((m, TOP_K), F32),
            jax.ShapeDtypeStruct((m, TOP_K), jnp.int32),
            jax.ShapeDtypeStruct((1, LANES), F32),
        ],
        compiler_params=_cparams(("arbitrary", "arbitrary")),
        name="merge_route",
    )(h2d, y_hg, y_ml, gates, wbh, wbm, wo, norm_ffn, rw_split, rb)


def _toks(first, n=1):
    return pl.ds(pl.multiple_of(first * TOK_ROWS, TOK_ROWS), n * TOK_ROWS)


def _dispatch_kernel(ps_ref, pe_ref, e_ref, rk_ref, x_ref, xs_hbm, zbuf, sem, *, n_experts, tm):
    b = pl.program_id(0)
    j = pl.program_id(1)
    tq = x_ref.shape[0] // TOK_ROWS

    @pl.when(jnp.logical_and(b == 0, j == 0))
    def _():
        zbuf[...] = jnp.zeros_like(zbuf)
        for e in range(n_experts):
            @pl.when(pe_ref[e] > ps_ref[e])
            def _():
                pltpu.make_async_copy(zbuf, xs_hbm.at[_toks(pe_ref[e] - tm, tm), :], sem).start()
        for e in range(n_experts):
            @pl.when(pe_ref[e] > ps_ref[e])
            def _():
                pltpu.make_async_copy(zbuf, xs_hbm.at[_toks(0, tm), :], sem).wait()

        def zero_tail(blk, carry):
            cp = pltpu.make_async_copy(zbuf, xs_hbm.at[_toks(blk * tm, tm), :], sem)
            cp.start()
            cp.wait()
            return carry
        lax.fori_loop(pe_ref[n_experts - 1] // tm, xs_hbm.shape[0] // (tm * TOK_ROWS), zero_tail, 0)

    def scatter_rows(lo):
        def body(r, carry):
            for kk in range(TOP_K):
                dst = ps_ref[e_ref[0, 0, r * TOP_K + kk]] + rk_ref[0, 0, r * TOP_K + kk]
                pltpu.make_async_copy(x_ref.at[_toks(r), :], xs_hbm.at[_toks(dst), :], sem).start()
            return carry
        lax.fori_loop(lo, tq, body, 0, unroll=2)
        n = tq - lo
        for _ in range(TOP_K):
            pltpu.make_async_copy(x_ref.at[_toks(0, n), :], xs_hbm.at[_toks(0, n), :], sem).wait()

    @pl.when(j == 0)
    def _():
        scatter_rows(N_PAD)

    @pl.when(j != 0)
    def _():
        scatter_rows(0)


def _dispatch(pstart, pend, top_e, rank, xn_t, n_rows, bsz, tq, tm):
    m = xn_t.shape[0] // TOK_ROWS
    nj = m // (bsz * tq)
    n_experts = pstart.shape[0]
    sm = lambda a: a.reshape(bsz * nj, 1, tq * TOP_K)
    smem_blk = pl.BlockSpec((1, 1, tq * TOP_K), lambda b, j, ps, pe: (b * nj + j, 0, 0),
                            memory_space=pltpu.SMEM)
    grid_spec = pltpu.PrefetchScalarGridSpec(
        num_scalar_prefetch=2,
        grid=(bsz, nj),
        in_specs=[smem_blk, smem_blk,
                  pl.BlockSpec((tq * TOK_ROWS, LANES), lambda b, j, ps, pe: (b * nj + j, 0))],
        out_specs=pl.BlockSpec(memory_space=pl.ANY),
        scratch_shapes=[pltpu.VMEM((tm * TOK_ROWS, LANES), F32), pltpu.SemaphoreType.DMA(())],
    )
    return pl.pallas_call(
        functools.partial(_dispatch_kernel, n_experts=n_experts, tm=tm),
        grid_spec=grid_spec,
        out_shape=jax.ShapeDtypeStruct((n_rows * TOK_ROWS, LANES), F32),
        compiler_params=_cparams(("arbitrary", "arbitrary")),
        name="dispatch",
    )(pstart, pend, sm(top_e), sm(rank), xn_t)


CAST_ROWS = 256


def _experts_kernel(be_ref, nu_ref, x_ref, wgu_ref, bg_ref, bu_ref, wd_ref, bd_ref, y_ref,
                    wg_s, wu_s, wd_s):
    i = pl.program_id(0)
    n_used = nu_ref[0]
    dff = wg_s.shape[0]
    new_expert = jnp.logical_or(i == 0, be_ref[i] != be_ref[jnp.maximum(i - 1, 0)])

    @pl.when(jnp.logical_and(new_expert, i < n_used))
    def _():
        for c in range(dff // CAST_ROWS):
            rows = pl.ds(c * CAST_ROWS, CAST_ROWS)
            wd_s[rows, :] = wd_ref[0, rows, :].astype(BF16)
            for s in range(wgu_ref.shape[1]):
                lanes = pl.ds(s * LANES, LANES)
                wg_s[rows, lanes] = wgu_ref[
                    0, s, pl.ds(2 * c * CAST_ROWS, CAST_ROWS, stride=2), :].astype(BF16)
                wu_s[rows, lanes] = wgu_ref[
                    0, s, pl.ds(2 * c * CAST_ROWS + 1, CAST_ROWS, stride=2), :].astype(BF16)

    @pl.when(i < n_used)
    def _():
        tm = x_ref.shape[0] // TOK_ROWS
        xb = _load_token_tiles(x_ref, tm).astype(BF16)
        g = _dot_nt(xb, wg_s[...]) + bg_ref[0]
        u = _dot_nt(xb, wu_s[...]) + bu_ref[0]
        gate = jnp.minimum(g, SWIGLU_LIMIT)
        up = jnp.clip(u, -SWIGLU_LIMIT, SWIGLU_LIMIT)
        act = (up + 1.0) * gate * _sigmoid(SWIGLU_ALPHA * gate)
        _store_token_tiles(y_ref, _dot(act.astype(BF16), wd_s[...]) + bd_ref[0])

    @pl.when(i >= n_used)
    def _():
        y_ref[...] = jnp.zeros_like(y_ref)


def _experts(block_e, n_used, xs, w_gu_t, b_g, b_u, w_d, b_d, tm):
    n_blocks = block_e.shape[0]
    dff, d = w_d.shape[1:]
    tile_blk = lambda f: pl.BlockSpec((tm * TOK_ROWS, LANES), f)
    assert dff % CAST_ROWS == 0
    wspec = lambda k, n: pl.BlockSpec((1, k, n), lambda i, be, nu: (be[i], 0, 0))
    grid_spec = pltpu.PrefetchScalarGridSpec(
        num_scalar_prefetch=2,
        grid=(n_blocks,),
        in_specs=[
            tile_blk(lambda i, be, nu: (jnp.minimum(i, nu[0] - 1), 0)),
            pl.BlockSpec((1, d // LANES, 2 * dff, LANES), lambda i, be, nu: (be[i], 0, 0, 0)),
            wspec(1, dff), wspec(1, dff), wspec(dff, d), wspec(1, d),
        ],
        out_specs=tile_blk(lambda i, be, nu: (i, 0)),
        scratch_shapes=[pltpu.VMEM((dff, d), BF16), pltpu.VMEM((dff, d), BF16), pltpu.VMEM((dff, d), BF16)],
    )
    return pl.pallas_call(
        _experts_kernel,
        grid_spec=grid_spec,
        out_shape=jax.ShapeDtypeStruct((n_blocks * tm * TOK_ROWS, LANES), F32),
        compiler_params=_cparams(("arbitrary",)),
        name="experts",
    )(block_e, n_used, xs, w_gu_t, b_g, b_u, w_d, b_d)


def _combine_kernel(ps_ref, e_ref, en_ref, rk_ref, rkn_ref, hsrc_ref, hsrcn_ref, tw_ref, y_hbm, h_hbm,
                    nw_ref, o_ref, ybuf, hbuf, obuf, sem):
    i = pl.program_id(0)
    n = pl.num_programs(0)
    tt = o_ref.shape[0]
    d = o_ref.shape[1]
    slot = i % 2

    def start(src_e_ref, src_rk_ref, src_h_ref, dst_slot):
        def body(r, carry):
            for kk in range(TOP_K):
                src = ps_ref[src_e_ref[0, 0, r * TOP_K + kk]] + src_rk_ref[0, 0, r * TOP_K + kk]
                pltpu.make_async_copy(
                    y_hbm.at[_toks(src), :],
                    ybuf.at[dst_slot, _toks(kk * tt + r), :],
                    sem.at[dst_slot]).start()
            return carry
        lax.fori_loop(0, tt, body, 0, unroll=2)
        pltpu.make_async_copy(
            h_hbm.at[_toks(src_h_ref[0, 0, 0], tt), :], hbuf.at[dst_slot], sem.at[dst_slot]).start()

    @pl.when(i == 0)
    def _():
        start(e_ref, rk_ref, hsrc_ref, 0)

    @pl.when(i + 1 < n)
    def _():
        start(en_ref, rkn_ref, hsrcn_ref, 1 - slot)

    pltpu.make_async_copy(ybuf.at[slot], ybuf.at[slot], sem.at[slot]).wait()
    pltpu.make_async_copy(hbuf.at[slot], hbuf.at[slot], sem.at[slot]).wait()
    acc = hbuf[slot]
    tw = tw_ref[...]
    nt = tt * TOK_ROWS
    for kk in range(TOP_K):
        acc = acc + tw[:, kk:kk + 1] * ybuf[slot, kk * nt:(kk + 1) * nt, :]
    a3 = acc.reshape(tt, TOK_ROWS, LANES)
    ssq = jnp.sum(jnp.sum(a3 * a3, axis=2, keepdims=True), axis=1, keepdims=True)
    o3 = (a3 * lax.rsqrt(ssq * (1.0 / d) + EPS)) * nw_ref[...][None]
    obuf[...] = o3.reshape(nt, LANES)
    for s in range(TOK_ROWS):
        o_ref[:, s * LANES:(s + 1) * LANES] = obuf[pl.ds(s, tt, stride=TOK_ROWS), :]


def _combine(pstart, e_seq, rk_seq, tw_rows, h_src, y_rows, h2_t, norm_w_t, d, tt):
    n_tiles = h_src.shape[0]
    last = n_tiles - 1
    nt = tt * TOK_ROWS
    cur = lambda i, ps: (i, 0, 0)
    nxt = lambda i, ps: (jnp.minimum(i + 1, last), 0, 0)
    sm = lambda a: a.reshape(n_tiles, 1, tt * TOP_K)
    idx_blk = lambda f: pl.BlockSpec((1, 1, TOP_K * tt), f, memory_space=pltpu.SMEM)
    one_blk = lambda f: pl.BlockSpec((1, 1, 1), f, memory_space=pltpu.SMEM)
    grid_spec = pltpu.PrefetchScalarGridSpec(
        num_scalar_prefetch=1,
        grid=(n_tiles,),
        in_specs=[
            idx_blk(cur), idx_blk(nxt), idx_blk(cur), idx_blk(nxt), one_blk(cur), one_blk(nxt),
            pl.BlockSpec((nt, TOP_K), lambda i, ps: (i, 0)),
            pl.BlockSpec(memory_space=pl.ANY),
            pl.BlockSpec(memory_space=pl.ANY),
            pl.BlockSpec(norm_w_t.shape, lambda i, ps: (0, 0)),
        ],
        out_specs=pl.BlockSpec((tt, d), lambda i, ps: (i, 0)),
        scratch_shapes=[pltpu.VMEM((2, TOP_K * nt, LANES), F32), pltpu.VMEM((2, nt, LANES), F32),
                        pltpu.VMEM((nt, LANES), F32), pltpu.SemaphoreType.DMA((2,))],
    )
    return pl.pallas_call(
        _combine_kernel,
        grid_spec=grid_spec,
        out_shape=jax.ShapeDtypeStruct((n_tiles * tt, d), F32),
        compiler_params=_cparams(("arbitrary",)),
        name="combine",
    )(pstart, sm(e_seq), sm(e_seq), sm(rk_seq), sm(rk_seq), h_src, h_src, tw_rows, y_rows, h2_t, norm_w_t)


def _pick(n, prefs):
    for p in prefs:
        if n % p == 0:
            return p
    raise ValueError(f"no tile in {prefs} divides {n}")


def kernel(x, meta_tokens, hg_lb_logits, norm_mix, w_in, hg_norm, ml_conv_w, ml_conv_b, ml_wq, ml_wk, ml_wv,
           ml_gate_b, ml_norm, ml_skip, w_branch_hg, w_branch_ml, w_out, norm_ffn, router_w, router_b,
           exp_w_gu, exp_b_gu, exp_w_down, exp_b_down, norm_final):
    bsz, seq, d = x.shape
    assert norm_mix.shape[0] == 1, "single-layer block"
    assert seq % CHUNK == 0 and d % LANES == 0
    t = CHUNK + seq
    m_rows = bsz * t
    n_experts = router_w.shape[-1]
    dff = exp_w_down.shape[2]
    assert n_experts <= LANES

    h = jnp.concatenate([
        jnp.zeros((bsz, N_PAD, d), x.dtype),
        jnp.broadcast_to(meta_tokens.astype(x.dtype)[None], (bsz, N_META, d)),
        x,
    ], axis=1)
    h2d = h.reshape(m_rows, d)
    lower_bounds = jnp.cumsum(jax.nn.softmax(hg_lb_logits.astype(F32), axis=0), axis=0)

    w = w_in[0]
    n_a = 4 * HG_W + 2 * ML_W
    w_a = w[:, :n_a].astype(BF16)
    w_m = jnp.pad(w[:, n_a:n_a + 2 * N_HEADS], ((0, 0), (0, LANES - 2 * N_HEADS))).astype(BF16)
    w_g = w[:, n_a + 2 * N_HEADS:].astype(BF16)
    tm1 = _pick(m_rows, (512, 256, 128, 64))
    hq, hf, hi, hg, mm, mo, gates, mif = _in_proj(h2d, norm_mix[0][None], w_a, w_g, w_m, tm1)

    bb = _pick(bsz, (4, 2, 1))
    r3 = lambda a: a.reshape(bsz, t, a.shape[-1])
    tri = jnp.asarray(np.tril(np.ones((CHUNK, CHUNK), np.float32)), BF16)
    y_hg = _hgrn2(r3(hq), r3(hf), r3(hi), r3(hg), tri, lower_bounds[0][None], hg_norm[0][None], bb)

    nc = t // CHUNK
    g_col = r3(mif)
    g_row = jnp.swapaxes(g_col[:, :, :2 * N_HEADS].reshape(bsz, nc, CHUNK, 2 * N_HEADS), 2, 3)
    gb = ml_gate_b[0].astype(F32)
    gb_col = jnp.pad(gb, (0, LANES - 2 * N_HEADS))[None]
    gb_row = jnp.broadcast_to(gb[:, None], (2 * N_HEADS, CHUNK))
    y_ml = _mlstm(r3(mm), r3(mo), g_col, g_row, tri, ml_conv_w[0], ml_conv_b[0][None],
                  ml_wq[0].astype(BF16), ml_wk[0].astype(BF16), ml_wv[0].astype(BF16),
                  gb_col, gb_row, ml_norm[0][None], ml_skip[0][None], bb)

    rw = jnp.pad(router_w[0].astype(F32), ((0, 0), (0, LANES - n_experts)))
    rw1 = rw.astype(BF16)
    rw2 = (rw - rw1.astype(F32)).astype(BF16)
    rw_split = jnp.stack([rw1, rw2])
    rb = jnp.pad(router_b[0].astype(F32), (0, LANES - n_experts))[None]
    tq = _pick(t, (352, 192, 64))
    h2, xn2, top_e, top_w, rank, cnt = _merge_route(
        h2d, y_hg.reshape(m_rows, HG_W), y_ml.reshape(m_rows, ML_W), gates,
        w_branch_hg[0].astype(BF16), w_branch_ml[0].astype(BF16), w_out[0].astype(BF16),
        norm_ffn[0][None], rw_split, rb, n_experts, bsz, tq)

    tm6 = 256
    n_assign = bsz * (t - N_PAD) * TOP_K
    counts = cnt[0, :n_experts].astype(jnp.int32)
    padded = ((counts + tm6 - 1) // tm6) * tm6
    pend = jnp.cumsum(padded).astype(jnp.int32)
    pstart = pend - padded
    n_blocks = -(-n_assign // tm6) + n_experts
    n_rows = n_blocks * tm6
    blk_start = jnp.arange(n_blocks, dtype=jnp.int32) * tm6
    block_e = jnp.minimum(jnp.sum((blk_start[:, None] >= pend[None, :]).astype(jnp.int32), axis=1),
                          n_experts - 1)
    n_used = (pend[-1] // tm6)[None]
    xs = _dispatch(pstart, pend, top_e, rank, xn2, n_rows, bsz, tq, tm6)

    w_gu_t = jnp.swapaxes(exp_w_gu[0].reshape(n_experts, d // LANES, LANES, 2 * dff), 2, 3)
    bgu = exp_b_gu[0]
    y_rows = _experts(block_e, n_used, xs, w_gu_t, bgu[:, None, 0::2], bgu[:, None, 1::2],
                      exp_w_down[0], exp_b_down[0][:, None, :], tm6)

    tt = _pick(seq, (128, 64))
    tiles_per_b = seq // tt
    n_tiles = bsz * tiles_per_b
    seq_part = lambda a: a.reshape(bsz, t, TOP_K)[:, CHUNK:].reshape(bsz * seq, TOP_K)
    h_src = (jnp.arange(bsz, dtype=jnp.int32)[:, None] * t + CHUNK
             + jnp.arange(tiles_per_b, dtype=jnp.int32)[None, :] * tt).reshape(n_tiles, 1, 1)
    tw_rows = jnp.repeat(seq_part(top_w), TOK_ROWS, axis=0)
    out = _combine(pstart, seq_part(top_e), seq_part(rank), tw_rows, h_src, y_rows, h2,
                   norm_final.reshape(TOK_ROWS, LANES), d, tt)
    return out.reshape(bsz, seq, d)
```

```python
import functools

import numpy as np
import jax
import jax.numpy as jnp
from jax import lax
from jax.experimental import pallas as pl
from jax.experimental.pallas import tpu as pltpu

F32 = jnp.float32
BF16 = jnp.bfloat16

N_META = 16
CHUNK = 64
N_PAD = CHUNK - N_META
EPS = 1e-6

N_HEADS = 4
HG_DK = 128
HG_DV = 128
HG_W = N_HEADS * HG_DV
ML_DK = 64
ML_DV = 128
ML_W = N_HEADS * ML_DV
ML_CONV = 4
TOP_K = 4
SWIGLU_LIMIT = 7.0
SWIGLU_ALPHA = 1.702

LANES = 128
VMEM_LIMIT_BYTES = 56 * 1024 * 1024

HG_LEVELS = (32, 16, 8, 4, 2, 1)


def _cparams(sem):
    return pltpu.CompilerParams(dimension_semantics=sem, vmem_limit_bytes=VMEM_LIMIT_BYTES)


def _sigmoid(x):
    return 1.0 / (1.0 + jnp.exp(-x))


def _split3(x):
    x1 = x.astype(BF16)
    r1 = x - x1.astype(F32)
    x2 = r1.astype(BF16)
    x3 = (r1 - x2.astype(F32)).astype(BF16)
    return x1, x2, x3


def _dot(a, b):
    return jnp.dot(a, b, preferred_element_type=F32)


TOK_ROWS = 8


def _store_token_tiles(ref, x):
    n = x.shape[0]
    for s in range(TOK_ROWS):
        ref[pl.ds(s, n, stride=TOK_ROWS), :] = x[:, s * LANES:(s + 1) * LANES]


def _load_token_tiles(ref, n):
    return jnp.concatenate([ref[pl.ds(s, n, stride=TOK_ROWS), :] for s in range(TOK_ROWS)], axis=1)


def _dot_nt(a, b):
    return lax.dot_general(a, b, (((1,), (1,)), ((), ())), preferred_element_type=F32)


def _dot_exact_lhs(m_bf16, x):
    x1, x2, x3 = _split3(x)
    return _dot(m_bf16, x1) + _dot(m_bf16, x2) + _dot(m_bf16, x3)


def _in_proj_kernel(h_ref, nw_ref, wa_ref, wg_ref, wm_ref,
                    q_ref, f_ref, i_ref, g_ref, mm_ref, mo_ref, gates_ref, mif_ref):
    x = h_ref[...]
    ms = jnp.mean(x * x, axis=-1, keepdims=True)
    xb = ((x * lax.rsqrt(ms + EPS)) * nw_ref[...]).astype(BF16)
    outs = (q_ref, f_ref, i_ref, g_ref, mm_ref, mo_ref)
    for j, o_ref in enumerate(outs):
        o_ref[...] = _dot(xb, wa_ref[:, j * HG_W:(j + 1) * HG_W]).astype(o_ref.dtype)
    d_model = gates_ref.shape[1] // 2
    for j in range(2):
        gates_ref[:, j * d_model:(j + 1) * d_model] = _dot(
            xb, wg_ref[:, j * d_model:(j + 1) * d_model]).astype(gates_ref.dtype)
    mif_ref[...] = _dot(xb, wm_ref[...])


def _in_proj(h2d, norm_w, w_a, w_g, w_m, tm):
    m, d = h2d.shape
    row = lambda n: pl.BlockSpec((tm, n), lambda i: (i, 0))
    full = lambda a: pl.BlockSpec(a.shape, lambda i: (0, 0))
    out_shape = [
        jax.ShapeDtypeStruct((m, HG_W), BF16),
        jax.ShapeDtypeStruct((m, HG_W), F32),
        jax.ShapeDtypeStruct((m, HG_W), BF16),
        jax.ShapeDtypeStruct((m, HG_W), BF16),
        jax.ShapeDtypeStruct((m, ML_W), BF16),
        jax.ShapeDtypeStruct((m, ML_W), BF16),
        jax.ShapeDtypeStruct((m, 2 * d), BF16),
        jax.ShapeDtypeStruct((m, LANES), F32),
    ]
    return pl.pallas_call(
        _in_proj_kernel,
        grid=(m // tm,),
        in_specs=[row(d), full(norm_w), full(w_a), full(w_g), full(w_m)],
        out_specs=[row(s.shape[1]) for s in out_shape],
        out_shape=out_shape,
        compiler_params=_cparams(("parallel",)),
        name="in_proj",
    )(h2d, norm_w, w_a, w_g, w_m)


def _hgrn2_kernel(q_ref, f_ref, v_ref, g_ref, tri_ref, lb_ref, nw_ref, y_ref, *st_refs):
    c = pl.program_id(1)
    bb = q_ref.shape[0]

    @pl.when(c == 0)
    def _():
        for st_ref in st_refs:
            st_ref[...] = jnp.zeros_like(st_ref)

    row = lax.broadcasted_iota(jnp.int32, (CHUNK, 1), 0)
    valid = (c * CHUNK + row) >= N_PAD
    ti = lax.broadcasted_iota(jnp.int32, (CHUNK, CHUNK), 0)
    si = lax.broadcasted_iota(jnp.int32, (CHUNK, CHUNK), 1)
    diag_mask = ti == si
    level_masks = {}
    for m in HG_LEVELS:
        same_pair = (ti & ~(2 * m - 1)) == (si & ~(2 * m - 1))
        level_masks[m] = same_pair & ((ti & m) != 0) & ((si & m) == 0)

    lb = lb_ref[...]
    tri = tri_ref[...]
    per_b = []
    for b in range(bb):
        hf = f_ref[b]
        f = lb + (1.0 - lb) * _sigmoid(hf)
        f = jnp.where(valid, f, 1.0)
        logf = jnp.log(f)
        k_all = 1.0 - f
        hq = q_ref[b].astype(F32)
        q_all = hq * _sigmoid(hq)
        b_cum = _dot_exact_lhs(tri, logf)
        e_b = jnp.exp(b_cum)
        e_bl = jnp.exp(b_cum[CHUNK - 1:CHUNK] - b_cum)

        q_fac, k_fac = {}, {}
        for m in HG_LEVELS:
            if m >= 4:
                grp = b_cum.reshape(CHUNK // (2 * m), 2 * m, HG_W)
                e = (grp - grp[:, m - 1:m, :]).reshape(CHUNK, HG_W)
                q_fac[m] = jnp.exp(jnp.minimum(e, 0.0))
                k_fac[m] = jnp.exp(jnp.minimum(-e, 0.0))
        f_prev = pltpu.roll(f, 1, 0)
        f_next = pltpu.roll(f, CHUNK - 1, 0)
        r4 = row & 3
        q_fac[2] = jnp.where(r4 == 2, f, jnp.where(r4 == 3, f * f_prev, 1.0))
        k_fac[2] = jnp.where(r4 == 0, f_next, 1.0)
        q_fac[1] = jnp.where((row & 1) == 1, f, 1.0)

        hg = g_ref[b].astype(F32)
        per_b.append((q_all, k_all, q_fac, k_fac, e_b, e_bl, v_ref[b], hg * _sigmoid(hg)))

    pairs = [(b, h) for b in range(bb) for h in range(N_HEADS)]
    sls = [slice(h * HG_DK, (h + 1) * HG_DK) for h in range(N_HEADS)]

    scores_all = []
    for b, h in pairs:
        q_all, k_all, q_fac, k_fac = per_b[b][:4]
        q = q_all[:, sls[h]]
        k = k_all[:, sls[h]]
        kb = k.astype(BF16)
        scores = jnp.where(diag_mask, _dot_nt(q.astype(BF16), kb), 0.0)
        for m in HG_LEVELS:
            qd = (q * q_fac[m][:, sls[h]]).astype(BF16)
            kd = (k * k_fac[m][:, sls[h]]).astype(BF16) if m in k_fac else kb
            scores = jnp.where(level_masks[m], _dot_nt(qd, kd), scores)
        scores_all.append(scores.astype(BF16))

    sts = [st_ref[...] for st_ref in st_refs]
    outs = []
    for p, (b, h) in enumerate(pairs):
        q_all, e_b, v_all = per_b[b][0], per_b[b][4], per_b[b][6]
        qe = (q_all[:, sls[h]] * e_b[:, sls[h]]).astype(BF16)
        outs.append(_dot(scores_all[p], v_all[:, sls[h]]) + _dot_nt(qe, sts[p].astype(BF16)))

    for p, (b, h) in enumerate(pairs):
        k_all, e_b, e_bl, v_all = per_b[b][1], per_b[b][4], per_b[b][5], per_b[b][6]
        kl = (k_all[:, sls[h]] * e_bl[:, sls[h]]).astype(BF16)
        vt = v_all[:, sls[h]].astype(F32).T.astype(BF16)
        st_refs[p][...] = e_b[CHUNK - 1:CHUNK, sls[h]] * sts[p] + _dot(vt, kl)

    for p, (b, h) in enumerate(pairs):
        o = outs[p]
        ms = jnp.mean(o * o, axis=-1, keepdims=True)
        y = (o * lax.rsqrt(ms + EPS)) * nw_ref[:, sls[h]] * per_b[b][7][:, sls[h]]
        y_ref[b, :, sls[h]] = y.astype(y_ref.dtype)


def _hgrn2(hq, hf, hi, hg, tri, lb, norm_w, bb):
    bsz, t, _ = hq.shape
    nc = t // CHUNK
    blk = pl.BlockSpec((bb, CHUNK, HG_W), lambda i, c: (i, c, 0))
    full = lambda a: pl.BlockSpec(a.shape, lambda i, c: (0, 0))
    return pl.pallas_call(
        _hgrn2_kernel,
        grid=(bsz // bb, nc),
        in_specs=[blk, blk, blk, blk, full(tri), full(lb), full(norm_w)],
        out_specs=blk,
        out_shape=jax.ShapeDtypeStruct((bsz, t, HG_W), BF16),
        scratch_shapes=[pltpu.VMEM((HG_DV, HG_DK), F32)] * (bb * N_HEADS),
        compiler_params=_cparams(("parallel", "arbitrary")),
        name="hgrn2",
    )(hq, hf, hi, hg, tri, lb, norm_w)


def _log_sigmoid(x):
    return jnp.minimum(x, 0.0) - jnp.log(1.0 + jnp.exp(-jnp.abs(x)))


def _mlstm_kernel(mm_ref, mo_ref, gc_ref, gr_ref, tri_ref, cw_ref, cb_ref, wq_ref, wk_ref, wv_ref,
                  gbc_ref, gbr_ref, nw_ref, sk_ref, y_ref, *scratch):
    c = pl.program_id(1)
    bb = mm_ref.shape[0]
    n_pairs = bb * N_HEADS
    s_refs = scratch[:n_pairs]
    m_refs = scratch[n_pairs:2 * n_pairs]
    tail_refs = scratch[2 * n_pairs:]

    @pl.when(c == 0)
    def _():
        for ref in scratch:
            ref[...] = jnp.zeros_like(ref)

    pos_c = c * CHUNK + lax.broadcasted_iota(jnp.int32, (CHUNK, 1), 0)
    valid_c = pos_c >= N_PAD
    pos_r = c * CHUNK + lax.broadcasted_iota(jnp.int32, (1, CHUNK), 1)
    valid_r = pos_r >= N_PAD
    ti = lax.broadcasted_iota(jnp.int32, (CHUNK, CHUNK), 0)
    si = lax.broadcasted_iota(jnp.int32, (CHUNK, CHUNK), 1)
    causal = si <= ti
    tri = tri_ref[...]
    ones_v = jnp.ones((CHUNK, ML_DV), BF16)
    neg_inf = -jnp.inf

    per_b = []
    for b in range(bb):
        mm = jnp.where(valid_c, mm_ref[b].astype(F32), 0.0)
        ext = jnp.concatenate([tail_refs[b][...], mm], axis=0)
        tail_refs[b][...] = mm[CHUNK - 8:CHUNK]
        conv = cb_ref[...]
        for j in range(ML_CONV):
            off = 8 - (ML_CONV - 1) + j
            conv = conv + cw_ref[j:j + 1, :] * ext[off:off + CHUNK]
        cact = conv * _sigmoid(conv)
        cact_b = cact.astype(BF16)
        mm_b = mm.astype(BF16)

        gcol = gc_ref[b] + gbc_ref[...]
        li_col = jnp.where(valid_c, gcol, neg_inf)
        lf_col = jnp.where(valid_c, _log_sigmoid(gcol), 0.0)
        b_col = _dot_exact_lhs(tri, lf_col)
        grow = gr_ref[b, 0] + gbr_ref[...]
        li_row = jnp.where(valid_r, grow, neg_inf)
        lf_row = jnp.where(valid_r, _log_sigmoid(grow), 0.0)
        r1, r2, r3 = _split3(lf_row)
        b_row = _dot_nt(r1, tri) + _dot_nt(r2, tri) + _dot_nt(r3, tri)

        ogate = _sigmoid(mo_ref[b].astype(F32))
        per_b.append((cact, cact_b, mm_b, li_col, b_col, li_row, b_row, ogate))

    pairs = [(b, h) for b in range(bb) for h in range(N_HEADS)]
    sls = [slice(h * ML_DV, (h + 1) * ML_DV) for h in range(N_HEADS)]

    qs, ks, vs = [], [], []
    for b, h in pairs:
        cact_b, mm_b = per_b[b][1], per_b[b][2]
        qs.append((_dot(cact_b[:, sls[h]], wq_ref[h]) * (ML_DK ** -0.5)).astype(BF16))
        ks.append(_dot(cact_b[:, sls[h]], wk_ref[h]))
        v = _dot(mm_b[:, sls[h]], wv_ref[h]).astype(BF16)
        vs.append(jnp.concatenate([v, ones_v], axis=1))

    bcs, lics, m_prevs, m_ts, w_intras, w_inters = [], [], [], [], [], []
    for p, (b, h) in enumerate(pairs):
        li_col, b_col, li_row, b_row = per_b[b][3:7]
        bc = b_col[:, N_HEADS + h:N_HEADS + h + 1]
        br = b_row[N_HEADS + h:N_HEADS + h + 1, :]
        lir = li_row[h:h + 1, :]
        m_prev = m_refs[p][0:1, 0:1]
        d = jnp.where(causal, bc + (lir - br), neg_inf)
        a = bc + m_prev
        m_t = jnp.maximum(a, jnp.max(d, axis=-1, keepdims=True))
        bcs.append(bc)
        lics.append(li_col[:, h:h + 1])
        m_prevs.append(m_prev)
        m_ts.append(m_t)
        w_intras.append(jnp.exp(d - m_t))
        w_inters.append(jnp.exp(a - m_t))

    qks = [(_dot_nt(qs[p], ks[p].astype(BF16)) * w_intras[p]).astype(BF16) for p in range(len(pairs))]
    s_augs = [s_refs[p][...] for p in range(len(pairs))]
    numdens = [_dot(qks[p], vs[p]) + w_inters[p] * _dot(qs[p], s_augs[p].astype(BF16))
               for p in range(len(pairs))]

    for p, (b, h) in enumerate(pairs):
        bc, lic, m_prev = bcs[p], lics[p], m_prevs[p]
        g_last = bc[CHUNK - 1:CHUNK, :]
        e = g_last - bc + lic
        m_new = jnp.maximum(g_last + m_prev, jnp.max(e, axis=0, keepdims=True))
        w_s = jnp.exp(e - m_new)
        w_p = jnp.exp(g_last + m_prev - m_new)
        kw_t = (ks[p] * w_s).T.astype(BF16)
        s_refs[p][...] = w_p * s_augs[p] + _dot(kw_t, vs[p])
        m_refs[p][...] = jnp.broadcast_to(m_new, m_refs[p].shape)

    for p, (b, h) in enumerate(pairs):
        cact, ogate = per_b[b][0], per_b[b][7]
        num = numdens[p][:, :ML_DV]
        den = numdens[p][:, ML_DV:]
        o = num / jnp.maximum(jnp.abs(den), jnp.exp(-m_ts[p]))
        ms = jnp.mean(o * o, axis=-1, keepdims=True)
        sl = sls[h]
        y = ((o * lax.rsqrt(ms + EPS)) * nw_ref[:, sl] + sk_ref[:, sl] * cact[:, sl]) * ogate[:, sl]
        y_ref[b, :, sl] = y.astype(y_ref.dtype)


def _mlstm(mm, mo, g_col, g_row, tri, conv_w, conv_b, wq, wk, wv, gb_col, gb_row, norm_w, skip, bb):
    bsz, t, _ = mm.shape
    nc = t // CHUNK
    blk = pl.BlockSpec((bb, CHUNK, ML_W), lambda i, c: (i, c, 0))
    gcb = pl.BlockSpec((bb, CHUNK, LANES), lambda i, c: (i, c, 0))
    grb = pl.BlockSpec((bb, 1, 8, CHUNK), lambda i, c: (i, c, 0, 0))

    def full(a):
        nd = a.ndim
        return pl.BlockSpec(a.shape, lambda i, c: (0,) * nd)

    params = (tri, conv_w, conv_b, wq, wk, wv, gb_col, gb_row, norm_w, skip)
    return pl.pallas_call(
        _mlstm_kernel,
        grid=(bsz // bb, nc),
        in_specs=[blk, blk, gcb, grb] + [full(p) for p in params],
        out_specs=blk,
        out_shape=jax.ShapeDtypeStruct((bsz, t, ML_W), BF16),
        scratch_shapes=([pltpu.VMEM((ML_DK, 2 * ML_DV), F32)] * (bb * N_HEADS)
                        + [pltpu.VMEM((8, LANES), F32)] * (bb * N_HEADS)
                        + [pltpu.VMEM((8, ML_W), F32)] * bb),
        compiler_params=_cparams(("parallel", "arbitrary")),
        name="mlstm",
    )(mm, mo, g_col, g_row, *params)


def _merge_route_kernel(h_ref, yh_ref, ym_ref, gates_ref, wbh_ref, wbm_ref, wo_ref, nf_ref,
                        rw_ref, rb_ref, h2_ref, xn_ref, te_ref, tw_ref, rk_ref, cnt_ref, *, n_experts):
    d = h_ref.shape[1]
    tq = h_ref.shape[0]
    j = pl.program_id(1)

    @pl.when(jnp.logical_and(pl.program_id(0) == 0, j == 0))
    def _():
        cnt_ref[...] = jnp.zeros_like(cnt_ref)

    g0 = _sigmoid(gates_ref[:, :d].astype(F32))
    g1 = _sigmoid(gates_ref[:, d:].astype(F32))
    merged = g0 * _dot(yh_ref[...], wbh_ref[...]) + g1 * _dot(ym_ref[...], wbm_ref[...])
    h2 = h_ref[...] + _dot(merged.astype(BF16), wo_ref[...])
    _store_token_tiles(h2_ref, h2)
    ms = jnp.mean(h2 * h2, axis=-1, keepdims=True)
    xn = (h2 * lax.rsqrt(ms + EPS)) * nf_ref[...]
    _store_token_tiles(xn_ref, xn)
    x1, x2, _ = _split3(xn)
    logits = (_dot(x1, rw_ref[0]) + _dot(x1, rw_ref[1]) + _dot(x2, rw_ref[0])) + rb_ref[...]
    lane = lax.broadcasted_iota(jnp.int32, logits.shape, 1)
    work = jnp.where(lane < n_experts, logits, -jnp.inf)
    vals, idxs = [], []
    for _ in range(TOP_K):
        vmax = jnp.max(work, axis=-1, keepdims=True)
        imax = jnp.min(jnp.where(work == vmax, lane, LANES), axis=-1, keepdims=True)
        vals.append(vmax)
        idxs.append(imax)
        work = jnp.where(lane == imax, -jnp.inf, work)
    exps = [jnp.exp(v - vals[0]) for v in vals]
    tot = exps[0] + exps[1] + exps[2] + exps[3]
    te = jnp.zeros(logits.shape, jnp.int32)
    tw = jnp.zeros(logits.shape, F32)
    for kk in range(TOP_K):
        te = jnp.where(lane == kk, idxs[kk], te)
        tw = jnp.where(lane == kk, exps[kk] / tot, tw)
    te_ref[...] = te[:, :TOP_K]
    tw_ref[...] = tw[:, :TOP_K]

    valid = (j * tq + lax.broadcasted_iota(jnp.int32, (tq, 1), 0)) >= N_PAD
    onehots = [jnp.where(jnp.logical_and(lane == idxs[kk], valid), 1.0, 0.0) for kk in range(TOP_K)]
    oh_all = onehots[0] + onehots[1] + onehots[2] + onehots[3]
    ri = lax.broadcasted_iota(jnp.int32, (tq, tq), 0)
    ci = lax.broadcasted_iota(jnp.int32, (tq, tq), 1)
    earlier = jnp.where(ci < ri, 1.0, 0.0).astype(BF16)
    before = _dot(earlier, oh_all.astype(BF16)) + cnt_ref[...]
    rk = jnp.zeros(logits.shape, F32)
    for kk in range(TOP_K):
        rank_k = jnp.sum(jnp.where(lane == idxs[kk], before, 0.0), axis=-1, keepdims=True)
        rk = jnp.where(lane == kk, rank_k, rk)
        before = before + onehots[kk]
    rk_ref[...] = rk[:, :TOP_K].astype(jnp.int32)
    cnt_ref[...] = cnt_ref[...] + jnp.sum(oh_all, axis=0, keepdims=True)


def _merge_route(h2d, y_hg, y_ml, gates, wbh, wbm, wo, norm_ffn, rw_split, rb, n_experts, bsz, tq):
    m, d = h2d.shape
    nj = m // (bsz * tq)
    assert d == TOK_ROWS * LANES, "token-tile layout assumes one (8,128) tile per token"
    row = lambda n: pl.BlockSpec((tq, n), lambda b, j: (b * nj + j, 0))
    tiles = pl.BlockSpec((tq * TOK_ROWS, LANES), lambda b, j: (b * nj + j, 0))

    def full(a):
        nd = a.ndim
        return pl.BlockSpec(a.shape, lambda b, j: (0,) * nd)

    return pl.pallas_call(
        functools.partial(_merge_route_kernel, n_experts=n_experts),
        grid=(bsz, nj),
        in_specs=[row(d), row(HG_W), row(ML_W), row(2 * d), full(wbh), full(wbm), full(wo),
                  full(norm_ffn), full(rw_split), full(rb)],
        out_specs=[tiles, tiles, row(TOP_K), row(TOP_K), row(TOP_K),
                   pl.BlockSpec((1, LANES), lambda b, j: (0, 0))],
        out_shape=[
            jax.ShapeDtypeStruct((m * TOK_ROWS, LANES), F32),
            jax.ShapeDtypeStruct((m * TOK_ROWS, LANES), F32),
            jax.ShapeDtypeStruct((m, TOP_K), jnp.int32),
            jax.ShapeDtypeStruct((m, TOP_K), F32),
            jax.ShapeDtypeStruct((m, TOP_K), jnp.int32),
            jax.ShapeDtypeStruct((1, LANES), F32),
        ],
        compiler_params=_cparams(("arbitrary", "arbitrary")),
        name="merge_route",
    )(h2d, y_hg, y_ml, gates, wbh, wbm, wo, norm_ffn, rw_split, rb)


def _toks(first, n=1):
    return pl.ds(pl.multiple_of(first * TOK_ROWS, TOK_ROWS), n * TOK_ROWS)


def _dispatch_kernel(ps_ref, pe_ref, pos_ref, x_ref, xs_hbm, zbuf, sem, *, n_experts, tm):
    b = pl.program_id(0)
    j = pl.program_id(1)
    tq = x_ref.shape[0] // TOK_ROWS

    @pl.when(jnp.logical_and(b == 0, j == 0))
    def _():
        zbuf[...] = jnp.zeros_like(zbuf)
        for e in range(n_experts):
            @pl.when(pe_ref[e] > ps_ref[e])
            def _():
                pltpu.make_async_copy(zbuf, xs_hbm.at[_toks(pe_ref[e] - tm, tm), :], sem).start()
        for e in range(n_experts):
            @pl.when(pe_ref[e] > ps_ref[e])
            def _():
                pltpu.make_async_copy(zbuf, xs_hbm.at[_toks(0, tm), :], sem).wait()

        def zero_tail(blk, carry):
            cp = pltpu.make_async_copy(zbuf, xs_hbm.at[_toks(blk * tm, tm), :], sem)
            cp.start()
            cp.wait()
            return carry
        lax.fori_loop(pe_ref[n_experts - 1] // tm, xs_hbm.shape[0] // (tm * TOK_ROWS), zero_tail, 0)

    def scatter_rows(lo):
        def body(r, carry):
            for kk in range(TOP_K):
                dst = pos_ref[0, 0, r * TOP_K + kk]
                pltpu.make_async_copy(x_ref.at[_toks(r), :], xs_hbm.at[_toks(dst), :], sem).start()
            return carry
        lax.fori_loop(lo, tq, body, 0, unroll=2)
        n = tq - lo
        for _ in range(TOP_K):
            pltpu.make_async_copy(x_ref.at[_toks(0, n), :], xs_hbm.at[_toks(0, n), :], sem).wait()

    @pl.when(j == 0)
    def _():
        scatter_rows(N_PAD)

    @pl.when(j != 0)
    def _():
        scatter_rows(0)


def _dispatch(pstart, pend, pos, xn_t, n_rows, bsz, tq, tm):
    m = xn_t.shape[0] // TOK_ROWS
    nj = m // (bsz * tq)
    n_experts = pstart.shape[0]
    sm = lambda a: a.reshape(bsz * nj, 1, tq * TOP_K)
    smem_blk = pl.BlockSpec((1, 1, tq * TOP_K), lambda b, j, ps, pe: (b * nj + j, 0, 0),
                            memory_space=pltpu.SMEM)
    grid_spec = pltpu.PrefetchScalarGridSpec(
        num_scalar_prefetch=2,
        grid=(bsz, nj),
        in_specs=[smem_blk,
                  pl.BlockSpec((tq * TOK_ROWS, LANES), lambda b, j, ps, pe: (b * nj + j, 0))],
        out_specs=pl.BlockSpec(memory_space=pl.ANY),
        scratch_shapes=[pltpu.VMEM((tm * TOK_ROWS, LANES), F32), pltpu.SemaphoreType.DMA(())],
    )
    return pl.pallas_call(
        functools.partial(_dispatch_kernel, n_experts=n_experts, tm=tm),
        grid_spec=grid_spec,
        out_shape=jax.ShapeDtypeStruct((n_rows * TOK_ROWS, LANES), F32),
        compiler_params=_cparams(("arbitrary", "arbitrary")),
        name="dispatch",
    )(pstart, pend, sm(pos), xn_t)


CAST_ROWS = 256


def _experts_kernel(be_ref, nu_ref, x_ref, wgu_ref, bg_ref, bu_ref, wd_ref, bd_ref, y_ref,
                    wg_s, wu_s, wd_s):
    i = pl.program_id(0)
    n_used = nu_ref[0]
    dff = wg_s.shape[0]
    new_expert = jnp.logical_or(i == 0, be_ref[i] != be_ref[jnp.maximum(i - 1, 0)])

    @pl.when(jnp.logical_and(new_expert, i < n_used))
    def _():
        for c in range(dff // CAST_ROWS):
            rows = pl.ds(c * CAST_ROWS, CAST_ROWS)
            wd_s[rows, :] = wd_ref[0, rows, :].astype(BF16)
            for s in range(wgu_ref.shape[1]):
                lanes = pl.ds(s * LANES, LANES)
                wg_s[rows, lanes] = wgu_ref[
                    0, s, pl.ds(2 * c * CAST_ROWS, CAST_ROWS, stride=2), :].astype(BF16)
                wu_s[rows, lanes] = wgu_ref[
                    0, s, pl.ds(2 * c * CAST_ROWS + 1, CAST_ROWS, stride=2), :].astype(BF16)

    @pl.when(i < n_used)
    def _():
        tm = x_ref.shape[0] // TOK_ROWS
        xb = _load_token_tiles(x_ref, tm).astype(BF16)
        g = _dot_nt(xb, wg_s[...]) + bg_ref[0]
        u = _dot_nt(xb, wu_s[...]) + bu_ref[0]
        gate = jnp.minimum(g, SWIGLU_LIMIT)
        up = jnp.clip(u, -SWIGLU_LIMIT, SWIGLU_LIMIT)
        act = (up + 1.0) * gate * _sigmoid(SWIGLU_ALPHA * gate)
        _store_token_tiles(y_ref, _dot(act.astype(BF16), wd_s[...]) + bd_ref[0])

    @pl.when(i >= n_used)
    def _():
        y_ref[...] = jnp.zeros_like(y_ref)


def _experts(block_e, n_used, xs, w_gu_t, b_g, b_u, w_d, b_d, tm):
    n_blocks = block_e.shape[0]
    dff, d = w_d.shape[1:]
    tile_blk = lambda f: pl.BlockSpec((tm * TOK_ROWS, LANES), f)
    assert dff % CAST_ROWS == 0
    wspec = lambda k, n: pl.BlockSpec((1, k, n), lambda i, be, nu: (be[i], 0, 0))
    grid_spec = pltpu.PrefetchScalarGridSpec(
        num_scalar_prefetch=2,
        grid=(n_blocks,),
        in_specs=[
            tile_blk(lambda i, be, nu: (jnp.minimum(i, nu[0] - 1), 0)),
            pl.BlockSpec((1, d // LANES, 2 * dff, LANES), lambda i, be, nu: (be[i], 0, 0, 0)),
            wspec(1, dff), wspec(1, dff), wspec(dff, d), wspec(1, d),
        ],
        out_specs=tile_blk(lambda i, be, nu: (i, 0)),
        scratch_shapes=[pltpu.VMEM((dff, d), BF16), pltpu.VMEM((dff, d), BF16), pltpu.VMEM((dff, d), BF16)],
    )
    return pl.pallas_call(
        _experts_kernel,
        grid_spec=grid_spec,
        out_shape=jax.ShapeDtypeStruct((n_blocks * tm * TOK_ROWS, LANES), F32),
        compiler_params=_cparams(("arbitrary",)),
        name="experts",
    )(block_e, n_used, xs, w_gu_t, b_g, b_u, w_d, b_d)


def _combine_kernel(pos_ref, posn_ref, hsrc_ref, hsrcn_ref, tw_ref, y_hbm, h_hbm,
                    nw_ref, o_ref, ybuf, hbuf, obuf, sem):
    i = pl.program_id(0)
    n = pl.num_programs(0)
    tt = o_ref.shape[0]
    d = o_ref.shape[1]
    slot = i % 2

    def start(src_pos_ref, src_h_ref, dst_slot):
        def body(r, carry):
            for kk in range(TOP_K):
                src = src_pos_ref[0, 0, r * TOP_K + kk]
                pltpu.make_async_copy(
                    y_hbm.at[_toks(src), :],
                    ybuf.at[dst_slot, _toks(kk * tt + r), :],
                    sem.at[dst_slot]).start()
            return carry
        lax.fori_loop(0, tt, body, 0, unroll=2)
        pltpu.make_async_copy(
            h_hbm.at[_toks(src_h_ref[0, 0, 0], tt), :], hbuf.at[dst_slot], sem.at[dst_slot]).start()

    @pl.when(i == 0)
    def _():
        start(pos_ref, hsrc_ref, 0)

    @pl.when(i + 1 < n)
    def _():
        start(posn_ref, hsrcn_ref, 1 - slot)

    pltpu.make_async_copy(ybuf.at[slot], ybuf.at[slot], sem.at[slot]).wait()
    pltpu.make_async_copy(hbuf.at[slot], hbuf.at[slot], sem.at[slot]).wait()
    acc = hbuf[slot]
    tw = tw_ref[...]
    nt = tt * TOK_ROWS
    for kk in range(TOP_K):
        acc = acc + tw[:, kk:kk + 1] * ybuf[slot, kk * nt:(kk + 1) * nt, :]
    a3 = acc.reshape(tt, TOK_ROWS, LANES)
    ssq = jnp.sum(jnp.sum(a3 * a3, axis=2, keepdims=True), axis=1, keepdims=True)
    o3 = (a3 * lax.rsqrt(ssq * (1.0 / d) + EPS)) * nw_ref[...][None]
    obuf[...] = o3.reshape(nt, LANES)
    for s in range(TOK_ROWS):
        o_ref[:, s * LANES:(s + 1) * LANES] = obuf[pl.ds(s, tt, stride=TOK_ROWS), :]


def _combine(pos_seq, tw_rows, h_src, y_rows, h2_t, norm_w_t, d, tt):
    n_tiles = h_src.shape[0]
    last = n_tiles - 1
    nt = tt * TOK_ROWS
    cur = lambda i: (i, 0, 0)
    nxt = lambda i: (jnp.minimum(i + 1, last), 0, 0)
    sm = lambda a: a.reshape(n_tiles, 1, tt * TOP_K)
    idx_blk = lambda f: pl.BlockSpec((1, 1, TOP_K * tt), f, memory_space=pltpu.SMEM)
    one_blk = lambda f: pl.BlockSpec((1, 1, 1), f, memory_space=pltpu.SMEM)
    return pl.pallas_call(
        _combine_kernel,
        grid=(n_tiles,),
        in_specs=[
            idx_blk(cur), idx_blk(nxt), one_blk(cur), one_blk(nxt),
            pl.BlockSpec((nt, TOP_K), lambda i: (i, 0)),
            pl.BlockSpec(memory_space=pl.ANY),
            pl.BlockSpec(memory_space=pl.ANY),
            pl.BlockSpec(norm_w_t.shape, lambda i: (0, 0)),
        ],
        out_specs=pl.BlockSpec((tt, d), lambda i: (i, 0)),
        scratch_shapes=[pltpu.VMEM((2, TOP_K * nt, LANES), F32), pltpu.VMEM((2, nt, LANES), F32),
                        pltpu.VMEM((nt, LANES), F32), pltpu.SemaphoreType.DMA((2,))],
        out_shape=jax.ShapeDtypeStruct((n_tiles * tt, d), F32),
        compiler_params=_cparams(("arbitrary",)),
        name="combine",
    )(sm(pos_seq), sm(pos_seq), h_src, h_src, tw_rows, y_rows, h2_t, norm_w_t)


def _pick(n, prefs):
    for p in prefs:
        if n % p == 0:
            return p
    raise ValueError(f"no tile in {prefs} divides {n}")


def kernel(x, meta_tokens, hg_lb_logits, norm_mix, w_in, hg_norm, ml_conv_w, ml_conv_b, ml_wq, ml_wk, ml_wv,
           ml_gate_b, ml_norm, ml_skip, w_branch_hg, w_branch_ml, w_out, norm_ffn, router_w, router_b,
           exp_w_gu, exp_b_gu, exp_w_down, exp_b_down, norm_final):
    bsz, seq, d = x.shape
    assert norm_mix.shape[0] == 1, "single-layer block"
    assert seq % CHUNK == 0 and d % LANES == 0
    t = CHUNK + seq
    m_rows = bsz * t
    n_experts = router_w.shape[-1]
    dff = exp_w_down.shape[2]
    assert n_experts <= LANES

    h = jnp.concatenate([
        jnp.zeros((bsz, N_PAD, d), x.dtype),
        jnp.broadcast_to(meta_tokens.astype(x.dtype)[None], (bsz, N_META, d)),
        x,
    ], axis=1)
    h2d = h.reshape(m_rows, d)
    lower_bounds = jnp.cumsum(jax.nn.softmax(hg_lb_logits.astype(F32), axis=0), axis=0)

    w = w_in[0]
    n_a = 4 * HG_W + 2 * ML_W
    w_a = w[:, :n_a].astype(BF16)
    w_m = jnp.pad(w[:, n_a:n_a + 2 * N_HEADS], ((0, 0), (0, LANES - 2 * N_HEADS))).astype(BF16)
    w_g = w[:, n_a + 2 * N_HEADS:].astype(BF16)
    tm1 = _pick(m_rows, (512, 256, 128, 64))
    hq, hf, hi, hg, mm, mo, gates, mif = _in_proj(h2d, norm_mix[0][None], w_a, w_g, w_m, tm1)

    bb = _pick(bsz, (4, 2, 1))
    r3 = lambda a: a.reshape(bsz, t, a.shape[-1])
    tri = jnp.asarray(np.tril(np.ones((CHUNK, CHUNK), np.float32)), BF16)
    y_hg = _hgrn2(r3(hq), r3(hf), r3(hi), r3(hg), tri, lower_bounds[0][None], hg_norm[0][None], bb)

    nc = t // CHUNK
    g_col = r3(mif)
    g_row = jnp.swapaxes(g_col[:, :, :2 * N_HEADS].reshape(bsz, nc, CHUNK, 2 * N_HEADS), 2, 3)
    gb = ml_gate_b[0].astype(F32)
    gb_col = jnp.pad(gb, (0, LANES - 2 * N_HEADS))[None]
    gb_row = jnp.broadcast_to(gb[:, None], (2 * N_HEADS, CHUNK))
    y_ml = _mlstm(r3(mm), r3(mo), g_col, g_row, tri, ml_conv_w[0], ml_conv_b[0][None],
                  ml_wq[0].astype(BF16), ml_wk[0].astype(BF16), ml_wv[0].astype(BF16),
                  gb_col, gb_row, ml_norm[0][None], ml_skip[0][None], bb)

    rw = jnp.pad(router_w[0].astype(F32), ((0, 0), (0, LANES - n_experts)))
    rw1 = rw.astype(BF16)
    rw2 = (rw - rw1.astype(F32)).astype(BF16)
    rw_split = jnp.stack([rw1, rw2])
    rb = jnp.pad(router_b[0].astype(F32), (0, LANES - n_experts))[None]
    tq = _pick(t, (352, 192, 64))
    h2, xn2, top_e, top_w, rank, cnt = _merge_route(
        h2d, y_hg.reshape(m_rows, HG_W), y_ml.reshape(m_rows, ML_W), gates,
        w_branch_hg[0].astype(BF16), w_branch_ml[0].astype(BF16), w_out[0].astype(BF16),
        norm_ffn[0][None], rw_split, rb, n_experts, bsz, tq)

    tm6 = 512
    n_assign = bsz * (t - N_PAD) * TOP_K
    counts = cnt[0, :n_experts].astype(jnp.int32)
    padded = ((counts + tm6 - 1) // tm6) * tm6
    pend = jnp.cumsum(padded).astype(jnp.int32)
    pstart = pend - padded
    n_blocks = -(-n_assign // tm6) + n_experts
    n_rows = n_blocks * tm6
    blk_start = jnp.arange(n_blocks, dtype=jnp.int32) * tm6
    block_e = jnp.minimum(jnp.sum((blk_start[:, None] >= pend[None, :]).astype(jnp.int32), axis=1),
                          n_experts - 1)
    n_used = (pend[-1] // tm6)[None]
    pos = rank + jnp.sum(jnp.where(top_e[:, :, None] == jnp.arange(n_experts, dtype=jnp.int32),
                                   pstart[None, None, :], 0), axis=-1)
    xs = _dispatch(pstart, pend, pos, xn2, n_rows, bsz, tq, tm6)

    w_gu_t = jnp.swapaxes(exp_w_gu[0].reshape(n_experts, d // LANES, LANES, 2 * dff), 2, 3)
    bgu = exp_b_gu[0]
    y_rows = _experts(block_e, n_used, xs, w_gu_t, bgu[:, None, 0::2], bgu[:, None, 1::2],
                      exp_w_down[0], exp_b_down[0][:, None, :], tm6)

    tt = _pick(seq, (128, 64))
    tiles_per_b = seq // tt
    n_tiles = bsz * tiles_per_b
    seq_part = lambda a: a.reshape(bsz, t, TOP_K)[:, CHUNK:].reshape(bsz * seq, TOP_K)
    h_src = (jnp.arange(bsz, dtype=jnp.int32)[:, None] * t + CHUNK
             + jnp.arange(tiles_per_b, dtype=jnp.int32)[None, :] * tt).reshape(n_tiles, 1, 1)
    tw_rows = jnp.repeat(seq_part(top_w), TOK_ROWS, axis=0)
    out = _combine(seq_part(pos), tw_rows, h_src, y_rows, h2,
                   norm_final.reshape(TOK_ROWS, LANES), d, tt)
    return out.reshape(bsz, seq, d)
```

```python
import functools

import numpy as np
import jax
import jax.numpy as jnp
from jax import lax
from jax.experimental import pallas as pl
from jax.experimental.pallas import tpu as pltpu

F32 = jnp.float32
BF16 = jnp.bfloat16

N_META = 16
CHUNK = 64
N_PAD = CHUNK - N_META
EPS = 1e-6

N_HEADS = 4
HG_DK = 128
HG_DV = 128
HG_W = N_HEADS * HG_DV
ML_DK = 64
ML_DV = 128
ML_W = N_HEADS * ML_DV
ML_CONV = 4
TOP_K = 4
SWIGLU_LIMIT = 7.0
SWIGLU_ALPHA = 1.702

LANES = 128
VMEM_LIMIT_BYTES = 56 * 1024 * 1024

HG_LEVELS = (32, 16, 8, 4, 2, 1)


def _cparams(sem):
    return pltpu.CompilerParams(dimension_semantics=sem, vmem_limit_bytes=VMEM_LIMIT_BYTES)


def _sigmoid(x):
    return 1.0 / (1.0 + jnp.exp(-x))


def _split3(x):
    x1 = x.astype(BF16)
    r1 = x - x1.astype(F32)
    x2 = r1.astype(BF16)
    x3 = (r1 - x2.astype(F32)).astype(BF16)
    return x1, x2, x3


def _dot(a, b):
    return jnp.dot(a, b, preferred_element_type=F32)


TOK_ROWS = 8


def _store_token_tiles(ref, x):
    n = x.shape[0]
    for s in range(TOK_ROWS):
        ref[pl.ds(s, n, stride=TOK_ROWS), :] = x[:, s * LANES:(s + 1) * LANES]


def _load_token_tiles(ref, n):
    return jnp.concatenate([ref[pl.ds(s, n, stride=TOK_ROWS), :] for s in range(TOK_ROWS)], axis=1)


def _dot_nt(a, b):
    return lax.dot_general(a, b, (((1,), (1,)), ((), ())), preferred_element_type=F32)


def _dot_exact_lhs(m_bf16, x):
    x1, x2, x3 = _split3(x)
    return _dot(m_bf16, x1) + _dot(m_bf16, x2) + _dot(m_bf16, x3)


def _in_proj_kernel(h_ref, nw_ref, wa_ref, wg_ref, wm_ref,
                    q_ref, f_ref, i_ref, g_ref, mm_ref, mo_ref, gates_ref, mif_ref):
    x = h_ref[...]
    ms = jnp.mean(x * x, axis=-1, keepdims=True)
    xb = ((x * lax.rsqrt(ms + EPS)) * nw_ref[...]).astype(BF16)
    outs = (q_ref, f_ref, i_ref, g_ref, mm_ref, mo_ref)
    for j, o_ref in enumerate(outs):
        o_ref[...] = _dot(xb, wa_ref[:, j * HG_W:(j + 1) * HG_W]).astype(o_ref.dtype)
    d_model = gates_ref.shape[1] // 2
    for j in range(2):
        gates_ref[:, j * d_model:(j + 1) * d_model] = _dot(
            xb, wg_ref[:, j * d_model:(j + 1) * d_model]).astype(gates_ref.dtype)
    mif_ref[...] = _dot(xb, wm_ref[...])


def _in_proj(h2d, norm_w, w_a, w_g, w_m, tm):
    m, d = h2d.shape
    row = lambda n: pl.BlockSpec((tm, n), lambda i: (i, 0))
    full = lambda a: pl.BlockSpec(a.shape, lambda i: (0, 0))
    out_shape = [
        jax.ShapeDtypeStruct((m, HG_W), BF16),
        jax.ShapeDtypeStruct((m, HG_W), F32),
        jax.ShapeDtypeStruct((m, HG_W), BF16),
        jax.ShapeDtypeStruct((m, HG_W), BF16),
        jax.ShapeDtypeStruct((m, ML_W), BF16),
        jax.ShapeDtypeStruct((m, ML_W), BF16),
        jax.ShapeDtypeStruct((m, 2 * d), BF16),
        jax.ShapeDtypeStruct((m, LANES), F32),
    ]
    return pl.pallas_call(
        _in_proj_kernel,
        grid=(m // tm,),
        in_specs=[row(d), full(norm_w), full(w_a), full(w_g), full(w_m)],
        out_specs=[row(s.shape[1]) for s in out_shape],
        out_shape=out_shape,
        compiler_params=_cparams(("parallel",)),
        name="in_proj",
    )(h2d, norm_w, w_a, w_g, w_m)


def _hgrn2_kernel(q_ref, f_ref, v_ref, g_ref, tri_ref, lb_ref, nw_ref, y_ref, *st_refs):
    c = pl.program_id(1)
    bb = q_ref.shape[0]

    @pl.when(c == 0)
    def _():
        for st_ref in st_refs:
            st_ref[...] = jnp.zeros_like(st_ref)

    row = lax.broadcasted_iota(jnp.int32, (CHUNK, 1), 0)
    valid = (c * CHUNK + row) >= N_PAD
    ti = lax.broadcasted_iota(jnp.int32, (CHUNK, CHUNK), 0)
    si = lax.broadcasted_iota(jnp.int32, (CHUNK, CHUNK), 1)
    diag_mask = ti == si
    level_masks = {}
    for m in HG_LEVELS:
        same_pair = (ti & ~(2 * m - 1)) == (si & ~(2 * m - 1))
        level_masks[m] = same_pair & ((ti & m) != 0) & ((si & m) == 0)

    lb = lb_ref[...]
    tri = tri_ref[...]
    per_b = []
    for b in range(bb):
        hf = f_ref[b]
        f = lb + (1.0 - lb) * _sigmoid(hf)
        f = jnp.where(valid, f, 1.0)
        logf = jnp.log(f)
        k_all = 1.0 - f
        hq = q_ref[b].astype(F32)
        q_all = hq * _sigmoid(hq)
        b_cum = _dot_exact_lhs(tri, logf)
        e_b = jnp.exp(b_cum)
        e_bl = jnp.exp(b_cum[CHUNK - 1:CHUNK] - b_cum)

        q_fac, k_fac = {}, {}
        for m in HG_LEVELS:
            if m >= 4:
                grp = b_cum.reshape(CHUNK // (2 * m), 2 * m, HG_W)
                e = (grp - grp[:, m - 1:m, :]).reshape(CHUNK, HG_W)
                q_fac[m] = jnp.exp(jnp.minimum(e, 0.0))
                k_fac[m] = jnp.exp(jnp.minimum(-e, 0.0))
        f_prev = pltpu.roll(f, 1, 0)
        f_next = pltpu.roll(f, CHUNK - 1, 0)
        r4 = row & 3
        q_fac[2] = jnp.where(r4 == 2, f, jnp.where(r4 == 3, f * f_prev, 1.0))
        k_fac[2] = jnp.where(r4 == 0, f_next, 1.0)
        q_fac[1] = jnp.where((row & 1) == 1, f, 1.0)

        hg = g_ref[b].astype(F32)
        per_b.append((q_all, k_all, q_fac, k_fac, e_b, e_bl, v_ref[b], hg * _sigmoid(hg)))

    pairs = [(b, h) for b in range(bb) for h in range(N_HEADS)]
    sls = [slice(h * HG_DK, (h + 1) * HG_DK) for h in range(N_HEADS)]

    scores_all = []
    for b, h in pairs:
        q_all, k_all, q_fac, k_fac = per_b[b][:4]
        q = q_all[:, sls[h]]
        k = k_all[:, sls[h]]
        kb = k.astype(BF16)
        scores = jnp.where(diag_mask, _dot_nt(q.astype(BF16), kb), 0.0)
        for m in HG_LEVELS:
            qd = (q * q_fac[m][:, sls[h]]).astype(BF16)
            kd = (k * k_fac[m][:, sls[h]]).astype(BF16) if m in k_fac else kb
            scores = jnp.where(level_masks[m], _dot_nt(qd, kd), scores)
        scores_all.append(scores.astype(BF16))

    sts = [st_ref[...] for st_ref in st_refs]
    outs = []
    for p, (b, h) in enumerate(pairs):
        q_all, e_b, v_all = per_b[b][0], per_b[b][4], per_b[b][6]
        qe = (q_all[:, sls[h]] * e_b[:, sls[h]]).astype(BF16)
        outs.append(_dot(scores_all[p], v_all[:, sls[h]]) + _dot_nt(qe, sts[p].astype(BF16)))

    for p, (b, h) in enumerate(pairs):
        k_all, e_b, e_bl, v_all = per_b[b][1], per_b[b][4], per_b[b][5], per_b[b][6]
        kl = (k_all[:, sls[h]] * e_bl[:, sls[h]]).astype(BF16)
        vt = v_all[:, sls[h]].astype(F32).T.astype(BF16)
        st_refs[p][...] = e_b[CHUNK - 1:CHUNK, sls[h]] * sts[p] + _dot(vt, kl)

    for p, (b, h) in enumerate(pairs):
        o = outs[p]
        ms = jnp.mean(o * o, axis=-1, keepdims=True)
        y = (o * lax.rsqrt(ms + EPS)) * nw_ref[:, sls[h]] * per_b[b][7][:, sls[h]]
        y_ref[b, :, sls[h]] = y.astype(y_ref.dtype)


def _hgrn2(hq, hf, hi, hg, tri, lb, norm_w, bb):
    bsz, t, _ = hq.shape
    nc = t // CHUNK
    blk = pl.BlockSpec((bb, CHUNK, HG_W), lambda i, c: (i, c, 0))
    full = lambda a: pl.BlockSpec(a.shape, lambda i, c: (0, 0))
    return pl.pallas_call(
        _hgrn2_kernel,
        grid=(bsz // bb, nc),
        in_specs=[blk, blk, blk, blk, full(tri), full(lb), full(norm_w)],
        out_specs=blk,
        out_shape=jax.ShapeDtypeStruct((bsz, t, HG_W), BF16),
        scratch_shapes=[pltpu.VMEM((HG_DV, HG_DK), F32)] * (bb * N_HEADS),
        compiler_params=_cparams(("parallel", "arbitrary")),
        name="hgrn2",
    )(hq, hf, hi, hg, tri, lb, norm_w)


def _log_sigmoid(x):
    return jnp.minimum(x, 0.0) - jnp.log(1.0 + jnp.exp(-jnp.abs(x)))


def _mlstm_kernel(mm_ref, mo_ref, gc_ref, gr_ref, tri_ref, cw_ref, cb_ref, wq_ref, wk_ref, wv_ref,
                  gbc_ref, gbr_ref, nw_ref, sk_ref, y_ref, *scratch):
    c = pl.program_id(1)
    bb = mm_ref.shape[0]
    n_pairs = bb * N_HEADS
    s_refs = scratch[:n_pairs]
    m_refs = scratch[n_pairs:2 * n_pairs]
    tail_refs = scratch[2 * n_pairs:]

    @pl.when(c == 0)
    def _():
        for ref in scratch:
            ref[...] = jnp.zeros_like(ref)

    pos_c = c * CHUNK + lax.broadcasted_iota(jnp.int32, (CHUNK, 1), 0)
    valid_c = pos_c >= N_PAD
    pos_r = c * CHUNK + lax.broadcasted_iota(jnp.int32, (1, CHUNK), 1)
    valid_r = pos_r >= N_PAD
    ti = lax.broadcasted_iota(jnp.int32, (CHUNK, CHUNK), 0)
    si = lax.broadcasted_iota(jnp.int32, (CHUNK, CHUNK), 1)
    causal = si <= ti
    tri = tri_ref[...]
    ones_v = jnp.ones((CHUNK, ML_DV), BF16)
    neg_inf = -jnp.inf

    per_b = []
    for b in range(bb):
        mm = jnp.where(valid_c, mm_ref[b].astype(F32), 0.0)
        ext = jnp.concatenate([tail_refs[b][...], mm], axis=0)
        tail_refs[b][...] = mm[CHUNK - 8:CHUNK]
        conv = cb_ref[...]
        for j in range(ML_CONV):
            off = 8 - (ML_CONV - 1) + j
            conv = conv + cw_ref[j:j + 1, :] * ext[off:off + CHUNK]
        cact = conv * _sigmoid(conv)
        cact_b = cact.astype(BF16)
        mm_b = mm.astype(BF16)

        gcol = gc_ref[b] + gbc_ref[...]
        li_col = jnp.where(valid_c, gcol, neg_inf)
        lf_col = jnp.where(valid_c, _log_sigmoid(gcol), 0.0)
        b_col = _dot_exact_lhs(tri, lf_col)
        grow = gr_ref[b, 0] + gbr_ref[...]
        li_row = jnp.where(valid_r, grow, neg_inf)
        lf_row = jnp.where(valid_r, _log_sigmoid(grow), 0.0)
        r1, r2, r3 = _split3(lf_row)
        b_row = _dot_nt(r1, tri) + _dot_nt(r2, tri) + _dot_nt(r3, tri)

        ogate = _sigmoid(mo_ref[b].astype(F32))
        per_b.append((cact, cact_b, mm_b, li_col, b_col, li_row, b_row, ogate))

    pairs = [(b, h) for b in range(bb) for h in range(N_HEADS)]
    sls = [slice(h * ML_DV, (h + 1) * ML_DV) for h in range(N_HEADS)]

    qs, ks, vs = [], [], []
    for b, h in pairs:
        cact_b, mm_b = per_b[b][1], per_b[b][2]
        qs.append((_dot(cact_b[:, sls[h]], wq_ref[h]) * (ML_DK ** -0.5)).astype(BF16))
        ks.append(_dot(cact_b[:, sls[h]], wk_ref[h]))
        v = _dot(mm_b[:, sls[h]], wv_ref[h]).astype(BF16)
        vs.append(jnp.concatenate([v, ones_v], axis=1))

    n_p = len(pairs)
    blk = lambda p: slice(p * CHUNK, (p + 1) * CHUNK)
    stack = lambda xs: jnp.concatenate(xs, axis=0)
    bc_all = stack([per_b[b][4][:, N_HEADS + h:N_HEADS + h + 1] for b, h in pairs])
    lic_all = stack([per_b[b][3][:, h:h + 1] for b, h in pairs])
    row_all = stack([jnp.broadcast_to(per_b[b][5][h:h + 1, :] - per_b[b][6][N_HEADS + h:N_HEADS + h + 1, :],
                                      (CHUNK, CHUNK)) for b, h in pairs])
    mprev_all = stack([jnp.broadcast_to(m_refs[p][0:1, 0:1], (CHUNK, 1)) for p in range(n_p)])
    glast_all = stack([jnp.broadcast_to(per_b[b][4][CHUNK - 1:CHUNK, N_HEADS + h:N_HEADS + h + 1], (CHUNK, 1))
                       for b, h in pairs])
    causal_all = stack([causal] * n_p)

    d_all = jnp.where(causal_all, bc_all + row_all, neg_inf)
    a_all = bc_all + mprev_all
    m_t_all = jnp.maximum(a_all, jnp.max(d_all, axis=-1, keepdims=True))
    w_intra_all = jnp.exp(d_all - m_t_all)
    w_inter_all = jnp.exp(a_all - m_t_all)

    qk_all = (stack([_dot_nt(qs[p], ks[p].astype(BF16)) for p in range(n_p)]) * w_intra_all).astype(BF16)
    s_augs = [s_refs[p][...] for p in range(n_p)]
    intra_all = stack([_dot(qk_all[blk(p)], vs[p]) for p in range(n_p)])
    inter_all = stack([_dot(qs[p], s_augs[p].astype(BF16)) for p in range(n_p)])
    numden_all = intra_all + w_inter_all * inter_all
    o_all = numden_all[:, :ML_DV] / jnp.maximum(jnp.abs(numden_all[:, ML_DV:]), jnp.exp(-m_t_all))
    ms_all = jnp.mean(o_all * o_all, axis=-1, keepdims=True)
    on_all = o_all * lax.rsqrt(ms_all + EPS)

    e_all = glast_all - bc_all + lic_all
    gm_all = glast_all + mprev_all
    e_max = jnp.max(e_all.reshape(n_p, CHUNK, 1), axis=1, keepdims=True)
    m_new_all = jnp.maximum(gm_all.reshape(n_p, CHUNK, 1), e_max).reshape(n_p * CHUNK, 1)
    w_s_all = jnp.exp(e_all - m_new_all)
    w_p_all = jnp.exp(gm_all - m_new_all)
    kw_all = stack(ks) * w_s_all
    for p in range(n_p):
        kw_t = kw_all[blk(p)].T.astype(BF16)
        s_refs[p][...] = w_p_all[p * CHUNK:p * CHUNK + 1] * s_augs[p] + _dot(kw_t, vs[p])
        m_refs[p][...] = jnp.broadcast_to(m_new_all[p * CHUNK:p * CHUNK + 1], m_refs[p].shape)

    for p, (b, h) in enumerate(pairs):
        cact, ogate = per_b[b][0], per_b[b][7]
        sl = sls[h]
        y = (on_all[blk(p)] * nw_ref[:, sl] + sk_ref[:, sl] * cact[:, sl]) * ogate[:, sl]
        y_ref[b, :, sl] = y.astype(y_ref.dtype)


def _mlstm(mm, mo, g_col, g_row, tri, conv_w, conv_b, wq, wk, wv, gb_col, gb_row, norm_w, skip, bb):
    bsz, t, _ = mm.shape
    nc = t // CHUNK
    blk = pl.BlockSpec((bb, CHUNK, ML_W), lambda i, c: (i, c, 0))
    gcb = pl.BlockSpec((bb, CHUNK, LANES), lambda i, c: (i, c, 0))
    grb = pl.BlockSpec((bb, 1, 8, CHUNK), lambda i, c: (i, c, 0, 0))

    def full(a):
        nd = a.ndim
        return pl.BlockSpec(a.shape, lambda i, c: (0,) * nd)

    params = (tri, conv_w, conv_b, wq, wk, wv, gb_col, gb_row, norm_w, skip)
    return pl.pallas_call(
        _mlstm_kernel,
        grid=(bsz // bb, nc),
        in_specs=[blk, blk, gcb, grb] + [full(p) for p in params],
        out_specs=blk,
        out_shape=jax.ShapeDtypeStruct((bsz, t, ML_W), BF16),
        scratch_shapes=([pltpu.VMEM((ML_DK, 2 * ML_DV), F32)] * (bb * N_HEADS)
                        + [pltpu.VMEM((8, LANES), F32)] * (bb * N_HEADS)
                        + [pltpu.VMEM((8, ML_W), F32)] * bb),
        compiler_params=_cparams(("parallel", "arbitrary")),
        name="mlstm",
    )(mm, mo, g_col, g_row, *params)


def _merge_route_kernel(h_ref, yh_ref, ym_ref, gates_ref, wbh_ref, wbm_ref, wo_ref, nf_ref,
                        rw_ref, rb_ref, h2_ref, xn_ref, te_ref, tw_ref, rk_ref, cnt_ref, *, n_experts):
    d = h_ref.shape[1]
    tq = h_ref.shape[0]
    j = pl.program_id(1)

    @pl.when(jnp.logical_and(pl.program_id(0) == 0, j == 0))
    def _():
        cnt_ref[...] = jnp.zeros_like(cnt_ref)

    g0 = _sigmoid(gates_ref[:, :d].astype(F32))
    g1 = _sigmoid(gates_ref[:, d:].astype(F32))
    merged = g0 * _dot(yh_ref[...], wbh_ref[...]) + g1 * _dot(ym_ref[...], wbm_ref[...])
    h2 = h_ref[...] + _dot(merged.astype(BF16), wo_ref[...])
    _store_token_tiles(h2_ref, h2)
    ms = jnp.mean(h2 * h2, axis=-1, keepdims=True)
    xn = (h2 * lax.rsqrt(ms + EPS)) * nf_ref[...]
    _store_token_tiles(xn_ref, xn)
    x1, x2, _ = _split3(xn)
    logits = (_dot(x1, rw_ref[0]) + _dot(x1, rw_ref[1]) + _dot(x2, rw_ref[0])) + rb_ref[...]
    lane = lax.broadcasted_iota(jnp.int32, logits.shape, 1)
    work = jnp.where(lane < n_experts, logits, -jnp.inf)
    vals, idxs = [], []
    for _ in range(TOP_K):
        vmax = jnp.max(work, axis=-1, keepdims=True)
        imax = jnp.min(jnp.where(work == vmax, lane, LANES), axis=-1, keepdims=True)
        vals.append(vmax)
        idxs.append(imax)
        work = jnp.where(lane == imax, -jnp.inf, work)
    exps = [jnp.exp(v - vals[0]) for v in vals]
    tot = exps[0] + exps[1] + exps[2] + exps[3]
    te = jnp.zeros(logits.shape, jnp.int32)
    tw = jnp.zeros(logits.shape, F32)
    for kk in range(TOP_K):
        te = jnp.where(lane == kk, idxs[kk], te)
        tw = jnp.where(lane == kk, exps[kk] / tot, tw)
    te_ref[...] = te[:, :TOP_K]
    tw_ref[...] = tw[:, :TOP_K]

    valid = (j * tq + lax.broadcasted_iota(jnp.int32, (tq, 1), 0)) >= N_PAD
    onehots = [jnp.where(jnp.logical_and(lane == idxs[kk], valid), 1.0, 0.0) for kk in range(TOP_K)]
    oh_all = onehots[0] + onehots[1] + onehots[2] + onehots[3]
    ri = lax.broadcasted_iota(jnp.int32, (tq, tq), 0)
    ci = lax.broadcasted_iota(jnp.int32, (tq, tq), 1)
    earlier = jnp.where(ci < ri, 1.0, 0.0).astype(BF16)
    before = _dot(earlier, oh_all.astype(BF16)) + cnt_ref[...]
    rk = jnp.zeros(logits.shape, F32)
    for kk in range(TOP_K):
        rank_k = jnp.sum(jnp.where(lane == idxs[kk], before, 0.0), axis=-1, keepdims=True)
        rk = jnp.where(lane == kk, rank_k, rk)
        before = before + onehots[kk]
    rk_ref[...] = rk[:, :TOP_K].astype(jnp.int32)
    cnt_ref[...] = cnt_ref[...] + jnp.sum(oh_all, axis=0, keepdims=True)


def _merge_route(h2d, y_hg, y_ml, gates, wbh, wbm, wo, norm_ffn, rw_split, rb, n_experts, bsz, tq):
    m, d = h2d.shape
    nj = m // (bsz * tq)
    assert d == TOK_ROWS * LANES, "token-tile layout assumes one (8,128) tile per token"
    row = lambda n: pl.BlockSpec((tq, n), lambda b, j: (b * nj + j, 0))
    tiles = pl.BlockSpec((tq * TOK_ROWS, LANES), lambda b, j: (b * nj + j, 0))

    def full(a):
        nd = a.ndim
        return pl.BlockSpec(a.shape, lambda b, j: (0,) * nd)

    return pl.pallas_call(
        functools.partial(_merge_route_kernel, n_experts=n_experts),
        grid=(bsz, nj),
        in_specs=[row(d), row(HG_W), row(ML_W), row(2 * d), full(wbh), full(wbm), full(wo),
                  full(norm_ffn), full(rw_split), full(rb)],
        out_specs=[tiles, tiles, row(TOP_K), row(TOP_K), row(TOP_K),
                   pl.BlockSpec((1, LANES), lambda b, j: (0, 0))],
        out_shape=[
            jax.ShapeDtypeStruct((m * TOK_ROWS, LANES), F32),
            jax.ShapeDtypeStruct((m * TOK_ROWS, LANES), F32),
            jax.ShapeDtypeStruct((m, TOP_K), jnp.int32),
            jax.ShapeDtypeStruct((m, TOP_K), F32),
            jax.ShapeDtypeStruct((m, TOP_K), jnp.int32),
            jax.ShapeDtypeStruct((1, LANES), F32),
        ],
        compiler_params=_cparams(("arbitrary", "arbitrary")),
        name="merge_route",
    )(h2d, y_hg, y_ml, gates, wbh, wbm, wo, norm_ffn, rw_split, rb)


def _toks(first, n=1):
    return pl.ds(pl.multiple_of(first * TOK_ROWS, TOK_ROWS), n * TOK_ROWS)


def _dispatch_kernel(ps_ref, pe_ref, pos_ref, x_ref, xs_hbm, zbuf, sem, *, n_experts, tm):
    b = pl.program_id(0)
    j = pl.program_id(1)
    tq = x_ref.shape[0] // TOK_ROWS

    @pl.when(jnp.logical_and(b == 0, j == 0))
    def _():
        zbuf[...] = jnp.zeros_like(zbuf)
        for e in range(n_experts):
            @pl.when(pe_ref[e] > ps_ref[e])
            def _():
                pltpu.make_async_copy(zbuf, xs_hbm.at[_toks(pe_ref[e] - tm, tm), :], sem).start()
        for e in range(n_experts):
            @pl.when(pe_ref[e] > ps_ref[e])
            def _():
                pltpu.make_async_copy(zbuf, xs_hbm.at[_toks(0, tm), :], sem).wait()

        def zero_tail(blk, carry):
            cp = pltpu.make_async_copy(zbuf, xs_hbm.at[_toks(blk * tm, tm), :], sem)
            cp.start()
            cp.wait()
            return carry
        lax.fori_loop(pe_ref[n_experts - 1] // tm, xs_hbm.shape[0] // (tm * TOK_ROWS), zero_tail, 0)

    def scatter_rows(lo):
        def body(r, carry):
            for kk in range(TOP_K):
                dst = pos_ref[0, 0, r * TOP_K + kk]
                pltpu.make_async_copy(
                    x_ref.at[_toks(r), :], xs_hbm.at[_toks(dst), :], sem).start(priority=kk % 2)
            return carry
        lax.fori_loop(lo, tq, body, 0, unroll=2)
        n = tq - lo
        for _ in range(TOP_K):
            pltpu.make_async_copy(x_ref.at[_toks(0, n), :], xs_hbm.at[_toks(0, n), :], sem).wait()

    @pl.when(j == 0)
    def _():
        scatter_rows(N_PAD)

    @pl.when(j != 0)
    def _():
        scatter_rows(0)


def _dispatch(pstart, pend, pos, xn_t, n_rows, bsz, tq, tm):
    m = xn_t.shape[0] // TOK_ROWS
    nj = m // (bsz * tq)
    n_experts = pstart.shape[0]
    sm = lambda a: a.reshape(bsz * nj, 1, tq * TOP_K)
    smem_blk = pl.BlockSpec((1, 1, tq * TOP_K), lambda b, j, ps, pe: (b * nj + j, 0, 0),
                            memory_space=pltpu.SMEM)
    grid_spec = pltpu.PrefetchScalarGridSpec(
        num_scalar_prefetch=2,
        grid=(bsz, nj),
        in_specs=[smem_blk,
                  pl.BlockSpec((tq * TOK_ROWS, LANES), lambda b, j, ps, pe: (b * nj + j, 0))],
        out_specs=pl.BlockSpec(memory_space=pl.ANY),
        scratch_shapes=[pltpu.VMEM((tm * TOK_ROWS, LANES), F32), pltpu.SemaphoreType.DMA(())],
    )
    return pl.pallas_call(
        functools.partial(_dispatch_kernel, n_experts=n_experts, tm=tm),
        grid_spec=grid_spec,
        out_shape=jax.ShapeDtypeStruct((n_rows * TOK_ROWS, LANES), F32),
        compiler_params=_cparams(("arbitrary", "arbitrary")),
        name="dispatch",
    )(pstart, pend, sm(pos), xn_t)


CAST_ROWS = 256


def _experts_kernel(be_ref, nu_ref, x_ref, wgu_ref, bg_ref, bu_ref, wd_ref, bd_ref, y_ref,
                    wg_s, wu_s, wd_s):
    i = pl.program_id(0)
    n_used = nu_ref[0]
    dff = wg_s.shape[0]
    new_expert = jnp.logical_or(i == 0, be_ref[i] != be_ref[jnp.maximum(i - 1, 0)])

    @pl.when(jnp.logical_and(new_expert, i < n_used))
    def _():
        for c in range(dff // CAST_ROWS):
            rows = pl.ds(c * CAST_ROWS, CAST_ROWS)
            wd_s[rows, :] = wd_ref[0, rows, :].astype(BF16)
            for s in range(wgu_ref.shape[1]):
                lanes = pl.ds(s * LANES, LANES)
                wg_s[rows, lanes] = wgu_ref[
                    0, s, pl.ds(2 * c * CAST_ROWS, CAST_ROWS, stride=2), :].astype(BF16)
                wu_s[rows, lanes] = wgu_ref[
                    0, s, pl.ds(2 * c * CAST_ROWS + 1, CAST_ROWS, stride=2), :].astype(BF16)

    @pl.when(i < n_used)
    def _():
        tm = x_ref.shape[0] // TOK_ROWS
        xb = _load_token_tiles(x_ref, tm).astype(BF16)
        g = _dot_nt(xb, wg_s[...]) + bg_ref[0]
        u = _dot_nt(xb, wu_s[...]) + bu_ref[0]
        gate = jnp.minimum(g, SWIGLU_LIMIT)
        up = jnp.clip(u, -SWIGLU_LIMIT, SWIGLU_LIMIT)
        act = (up + 1.0) * gate * _sigmoid(SWIGLU_ALPHA * gate)
        _store_token_tiles(y_ref, _dot(act.astype(BF16), wd_s[...]) + bd_ref[0])

    @pl.when(i >= n_used)
    def _():
        y_ref[...] = jnp.zeros_like(y_ref)


def _experts(block_e, n_used, xs, w_gu_t, b_g, b_u, w_d, b_d, tm):
    n_blocks = block_e.shape[0]
    dff, d = w_d.shape[1:]
    tile_blk = lambda f: pl.BlockSpec((tm * TOK_ROWS, LANES), f)
    assert dff % CAST_ROWS == 0
    wspec = lambda k, n: pl.BlockSpec((1, k, n), lambda i, be, nu: (be[i], 0, 0))
    grid_spec = pltpu.PrefetchScalarGridSpec(
        num_scalar_prefetch=2,
        grid=(n_blocks,),
        in_specs=[
            tile_blk(lambda i, be, nu: (jnp.minimum(i, nu[0] - 1), 0)),
            pl.BlockSpec((1, d // LANES, 2 * dff, LANES), lambda i, be, nu: (be[i], 0, 0, 0)),
            wspec(1, dff), wspec(1, dff), wspec(dff, d), wspec(1, d),
        ],
        out_specs=tile_blk(lambda i, be, nu: (i, 0)),
        scratch_shapes=[pltpu.VMEM((dff, d), BF16), pltpu.VMEM((dff, d), BF16), pltpu.VMEM((dff, d), BF16)],
    )
    return pl.pallas_call(
        _experts_kernel,
        grid_spec=grid_spec,
        out_shape=jax.ShapeDtypeStruct((n_blocks * tm * TOK_ROWS, LANES), F32),
        compiler_params=_cparams(("arbitrary",)),
        name="experts",
    )(block_e, n_used, xs, w_gu_t, b_g, b_u, w_d, b_d)


def _combine_kernel(pos_ref, posn_ref, hsrc_ref, hsrcn_ref, tw_ref, y_hbm, h_hbm,
                    nw_ref, o_ref, ybuf, hbuf, obuf, sem):
    i = pl.program_id(0)
    n = pl.num_programs(0)
    tt = o_ref.shape[0]
    d = o_ref.shape[1]
    slot = i % 2

    def start(src_pos_ref, src_h_ref, dst_slot):
        def body(r, carry):
            for kk in range(TOP_K):
                src = src_pos_ref[0, 0, r * TOP_K + kk]
                pltpu.make_async_copy(
                    y_hbm.at[_toks(src), :],
                    ybuf.at[dst_slot, _toks(kk * tt + r), :],
                    sem.at[dst_slot]).start(priority=kk % 2)
            return carry
        lax.fori_loop(0, tt, body, 0, unroll=2)
        pltpu.make_async_copy(
            h_hbm.at[_toks(src_h_ref[0, 0, 0], tt), :], hbuf.at[dst_slot], sem.at[dst_slot]).start()

    @pl.when(i == 0)
    def _():
        start(pos_ref, hsrc_ref, 0)

    @pl.when(i + 1 < n)
    def _():
        start(posn_ref, hsrcn_ref, 1 - slot)

    pltpu.make_async_copy(ybuf.at[slot], ybuf.at[slot], sem.at[slot]).wait()
    pltpu.make_async_copy(hbuf.at[slot], hbuf.at[slot], sem.at[slot]).wait()
    acc = hbuf[slot]
    tw = tw_ref[...]
    nt = tt * TOK_ROWS
    for kk in range(TOP_K):
        acc = acc + tw[:, kk:kk + 1] * ybuf[slot, kk * nt:(kk + 1) * nt, :]
    a3 = acc.reshape(tt, TOK_ROWS, LANES)
    ssq = jnp.sum(jnp.sum(a3 * a3, axis=2, keepdims=True), axis=1, keepdims=True)
    o3 = (a3 * lax.rsqrt(ssq * (1.0 / d) + EPS)) * nw_ref[...][None]
    obuf[...] = o3.reshape(nt, LANES)
    for s in range(TOK_ROWS):
        o_ref[:, s * LANES:(s + 1) * LANES] = obuf[pl.ds(s, tt, stride=TOK_ROWS), :]


def _combine(pos_seq, tw_rows, h_src, y_rows, h2_t, norm_w_t, d, tt):
    n_tiles = h_src.shape[0]
    last = n_tiles - 1
    nt = tt * TOK_ROWS
    cur = lambda i: (i, 0, 0)
    nxt = lambda i: (jnp.minimum(i + 1, last), 0, 0)
    sm = lambda a: a.reshape(n_tiles, 1, tt * TOP_K)
    idx_blk = lambda f: pl.BlockSpec((1, 1, TOP_K * tt), f, memory_space=pltpu.SMEM)
    one_blk = lambda f: pl.BlockSpec((1, 1, 1), f, memory_space=pltpu.SMEM)
    return pl.pallas_call(
        _combine_kernel,
        grid=(n_tiles,),
        in_specs=[
            idx_blk(cur), idx_blk(nxt), one_blk(cur), one_blk(nxt),
            pl.BlockSpec((nt, TOP_K), lambda i: (i, 0)),
            pl.BlockSpec(memory_space=pl.ANY),
            pl.BlockSpec(memory_space=pl.ANY),
            pl.BlockSpec(norm_w_t.shape, lambda i: (0, 0)),
        ],
        out_specs=pl.BlockSpec((tt, d), lambda i: (i, 0)),
        scratch_shapes=[pltpu.VMEM((2, TOP_K * nt, LANES), F32), pltpu.VMEM((2, nt, LANES), F32),
                        pltpu.VMEM((nt, LANES), F32), pltpu.SemaphoreType.DMA((2,))],
        out_shape=jax.ShapeDtypeStruct((n_tiles * tt, d), F32),
        compiler_params=_cparams(("arbitrary",)),
        name="combine",
    )(sm(pos_seq), sm(pos_seq), h_src, h_src, tw_rows, y_rows, h2_t, norm_w_t)


def _pick(n, prefs):
    for p in prefs:
        if n % p == 0:
            return p
    raise ValueError(f"no tile in {prefs} divides {n}")


def kernel(x, meta_tokens, hg_lb_logits, norm_mix, w_in, hg_norm, ml_conv_w, ml_conv_b, ml_wq, ml_wk, ml_wv,
           ml_gate_b, ml_norm, ml_skip, w_branch_hg, w_branch_ml, w_out, norm_ffn, router_w, router_b,
           exp_w_gu, exp_b_gu, exp_w_down, exp_b_down, norm_final):
    bsz, seq, d = x.shape
    assert norm_mix.shape[0] == 1, "single-layer block"
    assert seq % CHUNK == 0 and d % LANES == 0
    t = CHUNK + seq
    m_rows = bsz * t
    n_experts = router_w.shape[-1]
    dff = exp_w_down.shape[2]
    assert n_experts <= LANES

    h = jnp.concatenate([
        jnp.zeros((bsz, N_PAD, d), x.dtype),
        jnp.broadcast_to(meta_tokens.astype(x.dtype)[None], (bsz, N_META, d)),
        x,
    ], axis=1)
    h2d = h.reshape(m_rows, d)
    lower_bounds = jnp.cumsum(jax.nn.softmax(hg_lb_logits.astype(F32), axis=0), axis=0)

    w = w_in[0]
    n_a = 4 * HG_W + 2 * ML_W
    w_a = w[:, :n_a].astype(BF16)
    w_m = jnp.pad(w[:, n_a:n_a + 2 * N_HEADS], ((0, 0), (0, LANES - 2 * N_HEADS))).astype(BF16)
    w_g = w[:, n_a + 2 * N_HEADS:].astype(BF16)
    tm1 = _pick(m_rows, (512, 256, 128, 64))
    hq, hf, hi, hg, mm, mo, gates, mif = _in_proj(h2d, norm_mix[0][None], w_a, w_g, w_m, tm1)

    bb = _pick(bsz, (4, 2, 1))
    r3 = lambda a: a.reshape(bsz, t, a.shape[-1])
    tri = jnp.asarray(np.tril(np.ones((CHUNK, CHUNK), np.float32)), BF16)
    y_hg = _hgrn2(r3(hq), r3(hf), r3(hi), r3(hg), tri, lower_bounds[0][None], hg_norm[0][None], bb)

    nc = t // CHUNK
    g_col = r3(mif)
    g_row = jnp.swapaxes(g_col[:, :, :2 * N_HEADS].reshape(bsz, nc, CHUNK, 2 * N_HEADS), 2, 3)
    gb = ml_gate_b[0].astype(F32)
    gb_col = jnp.pad(gb, (0, LANES - 2 * N_HEADS))[None]
    gb_row = jnp.broadcast_to(gb[:, None], (2 * N_HEADS, CHUNK))
    y_ml = _mlstm(r3(mm), r3(mo), g_col, g_row, tri, ml_conv_w[0], ml_conv_b[0][None],
                  ml_wq[0].astype(BF16), ml_wk[0].astype(BF16), ml_wv[0].astype(BF16),
                  gb_col, gb_row, ml_norm[0][None], ml_skip[0][None], bb)

    rw = jnp.pad(router_w[0].astype(F32), ((0, 0), (0, LANES - n_experts)))
    rw1 = rw.astype(BF16)
    rw2 = (rw - rw1.astype(F32)).astype(BF16)
    rw_split = jnp.stack([rw1, rw2])
    rb = jnp.pad(router_b[0].astype(F32), (0, LANES - n_experts))[None]
    tq = _pick(t, (352, 192, 64))
    h2, xn2, top_e, top_w, rank, cnt = _merge_route(
        h2d, y_hg.reshape(m_rows, HG_W), y_ml.reshape(m_rows, ML_W), gates,
        w_branch_hg[0].astype(BF16), w_branch_ml[0].astype(BF16), w_out[0].astype(BF16),
        norm_ffn[0][None], rw_split, rb, n_experts, bsz, tq)

    tm6 = 512
    n_assign = bsz * (t - N_PAD) * TOP_K
    counts = cnt[0, :n_experts].astype(jnp.int32)
    padded = ((counts + tm6 - 1) // tm6) * tm6
    pend = jnp.cumsum(padded).astype(jnp.int32)
    pstart = pend - padded
    n_blocks = -(-n_assign // tm6) + n_experts
    n_rows = n_blocks * tm6
    blk_start = jnp.arange(n_blocks, dtype=jnp.int32) * tm6
    block_e = jnp.minimum(jnp.sum((blk_start[:, None] >= pend[None, :]).astype(jnp.int32), axis=1),
                          n_experts - 1)
    n_used = (pend[-1] // tm6)[None]
    pos = rank + jnp.sum(jnp.where(top_e[:, :, None] == jnp.arange(n_experts, dtype=jnp.int32),
                                   pstart[None, None, :], 0), axis=-1)
    xs = _dispatch(pstart, pend, pos, xn2, n_rows, bsz, tq, tm6)

    w_gu_t = jnp.swapaxes(exp_w_gu[0].reshape(n_experts, d // LANES, LANES, 2 * dff), 2, 3)
    bgu = exp_b_gu[0]
    y_rows = _experts(block_e, n_used, xs, w_gu_t, bgu[:, None, 0::2], bgu[:, None, 1::2],
                      exp_w_down[0], exp_b_down[0][:, None, :], tm6)

    tt = _pick(seq, (128, 64))
    tiles_per_b = seq // tt
    n_tiles = bsz * tiles_per_b
    seq_part = lambda a: a.reshape(bsz, t, TOP_K)[:, CHUNK:].reshape(bsz * seq, TOP_K)
    h_src = (jnp.arange(bsz, dtype=jnp.int32)[:, None] * t + CHUNK
             + jnp.arange(tiles_per_b, dtype=jnp.int32)[None, :] * tt).reshape(n_tiles, 1, 1)
    tw_rows = jnp.repeat(seq_part(top_w), TOK_ROWS, axis=0)
    out = _combine(seq_part(pos), tw_rows, h_src, y_rows, h2,
                   norm_final.reshape(TOK_ROWS, LANES), d, tt)
    return out.reshape(bsz, seq, d)
```

```python
import functools

import numpy as np
import jax
import jax.numpy as jnp
from jax import lax
from jax.experimental import pallas as pl
from jax.experimental.pallas import tpu as pltpu

F32 = jnp.float32
BF16 = jnp.bfloat16

N_META = 16
CHUNK = 64
N_PAD = CHUNK - N_META
EPS = 1e-6

N_HEADS = 4
HG_DK = 128
HG_DV = 128
HG_W = N_HEADS * HG_DV
ML_DK = 64
ML_DV = 128
ML_W = N_HEADS * ML_DV
ML_CONV = 4
TOP_K = 4
SWIGLU_LIMIT = 7.0
SWIGLU_ALPHA = 1.702

LANES = 128
VMEM_LIMIT_BYTES = 56 * 1024 * 1024

HG_LEVELS = (32, 16, 8, 4, 2, 1)


def _cparams(sem):
    return pltpu.CompilerParams(dimension_semantics=sem, vmem_limit_bytes=VMEM_LIMIT_BYTES)


def _sigmoid(x):
    return 1.0 / (1.0 + jnp.exp(-x))


def _split3(x):
    x1 = x.astype(BF16)
    r1 = x - x1.astype(F32)
    x2 = r1.astype(BF16)
    x3 = (r1 - x2.astype(F32)).astype(BF16)
    return x1, x2, x3


def _dot(a, b):
    return jnp.dot(a, b, preferred_element_type=F32)


TOK_ROWS = 8


def _store_token_tiles(ref, x):
    n = x.shape[0]
    for s in range(TOK_ROWS):
        ref[pl.ds(s, n, stride=TOK_ROWS), :] = x[:, s * LANES:(s + 1) * LANES]


def _load_token_tiles(ref, n):
    return jnp.concatenate([ref[pl.ds(s, n, stride=TOK_ROWS), :] for s in range(TOK_ROWS)], axis=1)


def _dot_nt(a, b):
    return lax.dot_general(a, b, (((1,), (1,)), ((), ())), preferred_element_type=F32)


def _dot_exact_lhs(m_bf16, x):
    x1, x2, x3 = _split3(x)
    return _dot(m_bf16, x1) + _dot(m_bf16, x2) + _dot(m_bf16, x3)


def _h_tile(x_ref, head_ref, j):
    x = x_ref[...]
    first = jnp.concatenate([head_ref[...], x[:x.shape[0] - CHUNK]], axis=0)
    return jnp.where(j == 0, first, x)


def _x_tile_spec(tr, d, seq):
    assert seq % 8 == 0 and tr % 8 == 0 and CHUNK % 8 == 0
    return pl.BlockSpec(
        (pl.Element(tr), pl.Element(d)),
        lambda b, j: (pl.multiple_of(b * seq + jnp.maximum(j * tr - CHUNK, 0), 8), 0))


def _in_proj_kernel(x_ref, head_ref, nw_ref, wa_ref, wg_ref, wm_ref,
                    q_ref, f_ref, i_ref, g_ref, mm_ref, mo_ref, gates_ref, mif_ref):
    x = _h_tile(x_ref, head_ref, pl.program_id(1))
    ms = jnp.mean(x * x, axis=-1, keepdims=True)
    xb = ((x * lax.rsqrt(ms + EPS)) * nw_ref[...]).astype(BF16)
    outs = (q_ref, f_ref, i_ref, g_ref, mm_ref, mo_ref)
    for j, o_ref in enumerate(outs):
        o_ref[...] = _dot(xb, wa_ref[:, j * HG_W:(j + 1) * HG_W]).astype(o_ref.dtype)
    d_model = gates_ref.shape[1] // 2
    for j in range(2):
        gates_ref[:, j * d_model:(j + 1) * d_model] = _dot(
            xb, wg_ref[:, j * d_model:(j + 1) * d_model]).astype(gates_ref.dtype)
    mif_ref[...] = _dot(xb, wm_ref[...])


def _in_proj(x2d, head, norm_w, w_a, w_g, w_m, bsz, tr):
    d = x2d.shape[1]
    seq = x2d.shape[0] // bsz
    nj = (seq + CHUNK) // tr
    m = bsz * nj * tr
    row = lambda n: pl.BlockSpec((tr, n), lambda b, j: (b * nj + j, 0))
    full = lambda a: pl.BlockSpec(a.shape, lambda b, j: (0, 0))
    out_shape = [
        jax.ShapeDtypeStruct((m, HG_W), BF16),
        jax.ShapeDtypeStruct((m, HG_W), F32),
        jax.ShapeDtypeStruct((m, HG_W), BF16),
        jax.ShapeDtypeStruct((m, HG_W), BF16),
        jax.ShapeDtypeStruct((m, ML_W), BF16),
        jax.ShapeDtypeStruct((m, ML_W), BF16),
        jax.ShapeDtypeStruct((m, 2 * d), BF16),
        jax.ShapeDtypeStruct((m, LANES), F32),
    ]
    return pl.pallas_call(
        _in_proj_kernel,
        grid=(bsz, nj),
        in_specs=[_x_tile_spec(tr, d, seq), full(head), full(norm_w), full(w_a), full(w_g), full(w_m)],
        out_specs=[row(s.shape[1]) for s in out_shape],
        out_shape=out_shape,
        compiler_params=_cparams(("parallel", "parallel")),
        name="in_proj",
    )(x2d, head, norm_w, w_a, w_g, w_m)


def _hgrn2_kernel(q_ref, f_ref, v_ref, g_ref, tri_ref, lb_ref, nw_ref, y_ref, *st_refs):
    c = pl.program_id(1)
    bb = q_ref.shape[0]

    @pl.when(c == 0)
    def _():
        for st_ref in st_refs:
            st_ref[...] = jnp.zeros_like(st_ref)

    row = lax.broadcasted_iota(jnp.int32, (CHUNK, 1), 0)
    valid = (c * CHUNK + row) >= N_PAD
    ti = lax.broadcasted_iota(jnp.int32, (CHUNK, CHUNK), 0)
    si = lax.broadcasted_iota(jnp.int32, (CHUNK, CHUNK), 1)
    diag_mask = ti == si
    level_masks = {}
    for m in HG_LEVELS:
        same_pair = (ti & ~(2 * m - 1)) == (si & ~(2 * m - 1))
        level_masks[m] = same_pair & ((ti & m) != 0) & ((si & m) == 0)

    lb = lb_ref[...]
    tri = tri_ref[...]
    per_b = []
    for b in range(bb):
        hf = f_ref[b]
        f = lb + (1.0 - lb) * _sigmoid(hf)
        f = jnp.where(valid, f, 1.0)
        logf = jnp.log(f)
        k_all = 1.0 - f
        hq = q_ref[b].astype(F32)
        q_all = hq * _sigmoid(hq)
        b_cum = _dot_exact_lhs(tri, logf)
        e_b = jnp.exp(b_cum)
        e_bl = jnp.exp(b_cum[CHUNK - 1:CHUNK] - b_cum)

        q_fac, k_fac = {}, {}
        for m in HG_LEVELS:
            if m >= 4:
                grp = b_cum.reshape(CHUNK // (2 * m), 2 * m, HG_W)
                e = (grp - grp[:, m - 1:m, :]).reshape(CHUNK, HG_W)
                q_fac[m] = jnp.exp(jnp.minimum(e, 0.0))
                k_fac[m] = jnp.exp(jnp.minimum(-e, 0.0))
        f_prev = pltpu.roll(f, 1, 0)
        f_next = pltpu.roll(f, CHUNK - 1, 0)
        r4 = row & 3
        q_fac[2] = jnp.where(r4 == 2, f, jnp.where(r4 == 3, f * f_prev, 1.0))
        k_fac[2] = jnp.where(r4 == 0, f_next, 1.0)
        q_fac[1] = jnp.where((row & 1) == 1, f, 1.0)

        hg = g_ref[b].astype(F32)
        per_b.append((q_all, k_all, q_fac, k_fac, e_b, e_bl, v_ref[b], hg * _sigmoid(hg)))

    pairs = [(b, h) for b in range(bb) for h in range(N_HEADS)]
    sls = [slice(h * HG_DK, (h + 1) * HG_DK) for h in range(N_HEADS)]

    scores_all = []
    for b, h in pairs:
        q_all, k_all, q_fac, k_fac = per_b[b][:4]
        q = q_all[:, sls[h]]
        k = k_all[:, sls[h]]
        kb = k.astype(BF16)
        scores = jnp.where(diag_mask, _dot_nt(q.astype(BF16), kb), 0.0)
        for m in HG_LEVELS:
            qd = (q * q_fac[m][:, sls[h]]).astype(BF16)
            kd = (k * k_fac[m][:, sls[h]]).astype(BF16) if m in k_fac else kb
            scores = jnp.where(level_masks[m], _dot_nt(qd, kd), scores)
        scores_all.append(scores.astype(BF16))

    sts = [st_ref[...] for st_ref in st_refs]
    outs = []
    for p, (b, h) in enumerate(pairs):
        q_all, e_b, v_all = per_b[b][0], per_b[b][4], per_b[b][6]
        qe = (q_all[:, sls[h]] * e_b[:, sls[h]]).astype(BF16)
        outs.append(_dot(scores_all[p], v_all[:, sls[h]]) + _dot_nt(qe, sts[p].astype(BF16)))

    for p, (b, h) in enumerate(pairs):
        k_all, e_b, e_bl, v_all = per_b[b][1], per_b[b][4], per_b[b][5], per_b[b][6]
        kl = (k_all[:, sls[h]] * e_bl[:, sls[h]]).astype(BF16)
        vt = v_all[:, sls[h]].astype(F32).T.astype(BF16)
        st_refs[p][...] = e_b[CHUNK - 1:CHUNK, sls[h]] * sts[p] + _dot(vt, kl)

    for p, (b, h) in enumerate(pairs):
        o = outs[p]
        ms = jnp.mean(o * o, axis=-1, keepdims=True)
        y = (o * lax.rsqrt(ms + EPS)) * nw_ref[:, sls[h]] * per_b[b][7][:, sls[h]]
        y_ref[b, :, sls[h]] = y.astype(y_ref.dtype)


def _hgrn2(hq, hf, hi, hg, tri, lb, norm_w, bb):
    bsz, t, _ = hq.shape
    nc = t // CHUNK
    blk = pl.BlockSpec((bb, CHUNK, HG_W), lambda i, c: (i, c, 0))
    full = lambda a: pl.BlockSpec(a.shape, lambda i, c: (0, 0))
    return pl.pallas_call(
        _hgrn2_kernel,
        grid=(bsz // bb, nc),
        in_specs=[blk, blk, blk, blk, full(tri), full(lb), full(norm_w)],
        out_specs=blk,
        out_shape=jax.ShapeDtypeStruct((bsz, t, HG_W), BF16),
        scratch_shapes=[pltpu.VMEM((HG_DV, HG_DK), F32)] * (bb * N_HEADS),
        compiler_params=_cparams(("parallel", "arbitrary")),
        name="hgrn2",
    )(hq, hf, hi, hg, tri, lb, norm_w)


def _log_sigmoid(x):
    return jnp.minimum(x, 0.0) - jnp.log(1.0 + jnp.exp(-jnp.abs(x)))


def _mlstm_kernel(mm_ref, mo_ref, gc_ref, gr_ref, tri_ref, cw_ref, cb_ref, wq_ref, wk_ref, wv_ref,
                  gbc_ref, gbr_ref, nw_ref, sk_ref, y_ref, *scratch):
    c = pl.program_id(1)
    bb = mm_ref.shape[0]
    n_pairs = bb * N_HEADS
    s_refs = scratch[:n_pairs]
    m_refs = scratch[n_pairs:2 * n_pairs]
    tail_refs = scratch[2 * n_pairs:]

    @pl.when(c == 0)
    def _():
        for ref in scratch:
            ref[...] = jnp.zeros_like(ref)

    pos_c = c * CHUNK + lax.broadcasted_iota(jnp.int32, (CHUNK, 1), 0)
    valid_c = pos_c >= N_PAD
    pos_r = c * CHUNK + lax.broadcasted_iota(jnp.int32, (1, CHUNK), 1)
    valid_r = pos_r >= N_PAD
    ti = lax.broadcasted_iota(jnp.int32, (CHUNK, CHUNK), 0)
    si = lax.broadcasted_iota(jnp.int32, (CHUNK, CHUNK), 1)
    causal = si <= ti
    tri = tri_ref[...]
    ones_v = jnp.ones((CHUNK, ML_DV), BF16)
    neg_inf = -jnp.inf

    per_b = []
    for b in range(bb):
        mm = jnp.where(valid_c, mm_ref[b].astype(F32), 0.0)
        ext = jnp.concatenate([tail_refs[b][...], mm], axis=0)
        tail_refs[b][...] = mm[CHUNK - 8:CHUNK]
        conv = cb_ref[...]
        for j in range(ML_CONV):
            off = 8 - (ML_CONV - 1) + j
            conv = conv + cw_ref[j:j + 1, :] * ext[off:off + CHUNK]
        cact = conv * _sigmoid(conv)
        cact_b = cact.astype(BF16)
        mm_b = mm.astype(BF16)

        gcol = gc_ref[b] + gbc_ref[...]
        li_col = jnp.where(valid_c, gcol, neg_inf)
        lf_col = jnp.where(valid_c, _log_sigmoid(gcol), 0.0)
        b_col = _dot_exact_lhs(tri, lf_col)
        grow = gr_ref[b, 0] + gbr_ref[...]
        li_row = jnp.where(valid_r, grow, neg_inf)
        lf_row = jnp.where(valid_r, _log_sigmoid(grow), 0.0)
        r1, r2, r3 = _split3(lf_row)
        b_row = _dot_nt(r1, tri) + _dot_nt(r2, tri) + _dot_nt(r3, tri)

        ogate = _sigmoid(mo_ref[b].astype(F32))
        per_b.append((cact, cact_b, mm_b, li_col, b_col, li_row, b_row, ogate))

    pairs = [(b, h) for b in range(bb) for h in range(N_HEADS)]
    sls = [slice(h * ML_DV, (h + 1) * ML_DV) for h in range(N_HEADS)]

    qs, ks, vs = [], [], []
    for b, h in pairs:
        cact_b, mm_b = per_b[b][1], per_b[b][2]
        qs.append((_dot(cact_b[:, sls[h]], wq_ref[h]) * (ML_DK ** -0.5)).astype(BF16))
        ks.append(_dot(cact_b[:, sls[h]], wk_ref[h]))
        v = _dot(mm_b[:, sls[h]], wv_ref[h]).astype(BF16)
        vs.append(jnp.concatenate([v, ones_v], axis=1))

    n_p = len(pairs)
    blk = lambda p: slice(p * CHUNK, (p + 1) * CHUNK)
    stack = lambda xs: jnp.concatenate(xs, axis=0)
    bc_all = stack([per_b[b][4][:, N_HEADS + h:N_HEADS + h + 1] for b, h in pairs])
    lic_all = stack([per_b[b][3][:, h:h + 1] for b, h in pairs])
    row_all = stack([jnp.broadcast_to(per_b[b][5][h:h + 1, :] - per_b[b][6][N_HEADS + h:N_HEADS + h + 1, :],
                                      (CHUNK, CHUNK)) for b, h in pairs])
    mprev_all = stack([jnp.broadcast_to(m_refs[p][0:1, 0:1], (CHUNK, 1)) for p in range(n_p)])
    glast_all = stack([jnp.broadcast_to(per_b[b][4][CHUNK - 1:CHUNK, N_HEADS + h:N_HEADS + h + 1], (CHUNK, 1))
                       for b, h in pairs])
    causal_all = stack([causal] * n_p)

    d_all = jnp.where(causal_all, bc_all + row_all, neg_inf)
    a_all = bc_all + mprev_all
    m_t_all = jnp.maximum(a_all, jnp.max(d_all, axis=-1, keepdims=True))
    w_intra_all = jnp.exp(d_all - m_t_all)
    w_inter_all = jnp.exp(a_all - m_t_all)

    qk_all = (stack([_dot_nt(qs[p], ks[p].astype(BF16)) for p in range(n_p)]) * w_intra_all).astype(BF16)
    s_augs = [s_refs[p][...] for p in range(n_p)]
    intra_all = stack([_dot(qk_all[blk(p)], vs[p]) for p in range(n_p)])
    inter_all = stack([_dot(qs[p], s_augs[p].astype(BF16)) for p in range(n_p)])
    numden_all = intra_all + w_inter_all * inter_all
    o_all = numden_all[:, :ML_DV] / jnp.maximum(jnp.abs(numden_all[:, ML_DV:]), jnp.exp(-m_t_all))
    ms_all = jnp.mean(o_all * o_all, axis=-1, keepdims=True)
    on_all = o_all * lax.rsqrt(ms_all + EPS)

    e_all = glast_all - bc_all + lic_all
    gm_all = glast_all + mprev_all
    e_max = jnp.max(e_all.reshape(n_p, CHUNK, 1), axis=1, keepdims=True)
    m_new_all = jnp.maximum(gm_all.reshape(n_p, CHUNK, 1), e_max).reshape(n_p * CHUNK, 1)
    w_s_all = jnp.exp(e_all - m_new_all)
    w_p_all = jnp.exp(gm_all - m_new_all)
    kw_all = stack(ks) * w_s_all
    for p in range(n_p):
        kw_t = kw_all[blk(p)].T.astype(BF16)
        s_refs[p][...] = w_p_all[p * CHUNK:p * CHUNK + 1] * s_augs[p] + _dot(kw_t, vs[p])
        m_refs[p][...] = jnp.broadcast_to(m_new_all[p * CHUNK:p * CHUNK + 1], m_refs[p].shape)

    for p, (b, h) in enumerate(pairs):
        cact, ogate = per_b[b][0], per_b[b][7]
        sl = sls[h]
        y = (on_all[blk(p)] * nw_ref[:, sl] + sk_ref[:, sl] * cact[:, sl]) * ogate[:, sl]
        y_ref[b, :, sl] = y.astype(y_ref.dtype)


def _mlstm(mm, mo, g_col, g_row, tri, conv_w, conv_b, wq, wk, wv, gb_col, gb_row, norm_w, skip, bb):
    bsz, t, _ = mm.shape
    nc = t // CHUNK
    blk = pl.BlockSpec((bb, CHUNK, ML_W), lambda i, c: (i, c, 0))
    gcb = pl.BlockSpec((bb, CHUNK, LANES), lambda i, c: (i, c, 0))
    grb = pl.BlockSpec((bb, 1, 8, CHUNK), lambda i, c: (i, c, 0, 0))

    def full(a):
        nd = a.ndim
        return pl.BlockSpec(a.shape, lambda i, c: (0,) * nd)

    params = (tri, conv_w, conv_b, wq, wk, wv, gb_col, gb_row, norm_w, skip)
    return pl.pallas_call(
        _mlstm_kernel,
        grid=(bsz // bb, nc),
        in_specs=[blk, blk, gcb, grb] + [full(p) for p in params],
        out_specs=blk,
        out_shape=jax.ShapeDtypeStruct((bsz, t, ML_W), BF16),
        scratch_shapes=([pltpu.VMEM((ML_DK, 2 * ML_DV), F32)] * (bb * N_HEADS)
                        + [pltpu.VMEM((8, LANES), F32)] * (bb * N_HEADS)
                        + [pltpu.VMEM((8, ML_W), F32)] * bb),
        compiler_params=_cparams(("parallel", "arbitrary")),
        name="mlstm",
    )(mm, mo, g_col, g_row, *params)


def _merge_route_kernel(x_ref, head_ref, yh_ref, ym_ref, gates_ref, wbh_ref, wbm_ref, wo_ref, nf_ref,
                        rw_ref, rb_ref, h2_ref, xn_ref, te_ref, tw_ref, rk_ref, cnt_ref, *, n_experts):
    d = x_ref.shape[1]
    tq = x_ref.shape[0]
    j = pl.program_id(1)

    @pl.when(jnp.logical_and(pl.program_id(0) == 0, j == 0))
    def _():
        cnt_ref[...] = jnp.zeros_like(cnt_ref)

    g0 = _sigmoid(gates_ref[:, :d].astype(F32))
    g1 = _sigmoid(gates_ref[:, d:].astype(F32))
    merged = g0 * _dot(yh_ref[...], wbh_ref[...]) + g1 * _dot(ym_ref[...], wbm_ref[...])
    h2 = _h_tile(x_ref, head_ref, j) + _dot(merged.astype(BF16), wo_ref[...])
    _store_token_tiles(h2_ref, h2)
    ms = jnp.mean(h2 * h2, axis=-1, keepdims=True)
    xn = (h2 * lax.rsqrt(ms + EPS)) * nf_ref[...]
    _store_token_tiles(xn_ref, xn)
    x1, x2, _ = _split3(xn)
    logits = (_dot(x1, rw_ref[0]) + _dot(x1, rw_ref[1]) + _dot(x2, rw_ref[0])) + rb_ref[...]
    lane = lax.broadcasted_iota(jnp.int32, logits.shape, 1)
    work = jnp.where(lane < n_experts, logits, -jnp.inf)
    vals, idxs = [], []
    for _ in range(TOP_K):
        vmax = jnp.max(work, axis=-1, keepdims=True)
        imax = jnp.min(jnp.where(work == vmax, lane, LANES), axis=-1, keepdims=True)
        vals.append(vmax)
        idxs.append(imax)
        work = jnp.where(lane == imax, -jnp.inf, work)
    exps = [jnp.exp(v - vals[0]) for v in vals]
    tot = exps[0] + exps[1] + exps[2] + exps[3]
    te = jnp.zeros(logits.shape, jnp.int32)
    tw = jnp.zeros(logits.shape, F32)
    for kk in range(TOP_K):
        te = jnp.where(lane == kk, idxs[kk], te)
        tw = jnp.where(lane == kk, exps[kk] / tot, tw)
    te_ref[...] = te[:, :TOP_K]
    tw_ref[...] = tw[:, :TOP_K]

    valid = (j * tq + lax.broadcasted_iota(jnp.int32, (tq, 1), 0)) >= N_PAD
    onehots = [jnp.where(jnp.logical_and(lane == idxs[kk], valid), 1.0, 0.0) for kk in range(TOP_K)]
    oh_all = onehots[0] + onehots[1] + onehots[2] + onehots[3]
    ri = lax.broadcasted_iota(jnp.int32, (tq, tq), 0)
    ci = lax.broadcasted_iota(jnp.int32, (tq, tq), 1)
    earlier = jnp.where(ci < ri, 1.0, 0.0).astype(BF16)
    before = _dot(earlier, oh_all.astype(BF16)) + cnt_ref[...]
    rk = jnp.zeros(logits.shape, F32)
    for kk in range(TOP_K):
        rank_k = jnp.sum(jnp.where(lane == idxs[kk], before, 0.0), axis=-1, keepdims=True)
        rk = jnp.where(lane == kk, rank_k, rk)
        before = before + onehots[kk]
    rk_ref[...] = rk[:, :TOP_K].astype(jnp.int32)
    cnt_ref[...] = cnt_ref[...] + jnp.sum(oh_all, axis=0, keepdims=True)


def _merge_route(x2d, head, y_hg, y_ml, gates, wbh, wbm, wo, norm_ffn, rw_split, rb, n_experts, bsz, tq):
    d = x2d.shape[1]
    seq = x2d.shape[0] // bsz
    nj = (seq + CHUNK) // tq
    m = bsz * nj * tq
    assert d == TOK_ROWS * LANES, "token-tile layout assumes one (8,128) tile per token"
    row = lambda n: pl.BlockSpec((tq, n), lambda b, j: (b * nj + j, 0))
    tiles = pl.BlockSpec((tq * TOK_ROWS, LANES), lambda b, j: (b * nj + j, 0))

    def full(a):
        nd = a.ndim
        return pl.BlockSpec(a.shape, lambda b, j: (0,) * nd)

    return pl.pallas_call(
        functools.partial(_merge_route_kernel, n_experts=n_experts),
        grid=(bsz, nj),
        in_specs=[_x_tile_spec(tq, d, seq), full(head), row(HG_W), row(ML_W), row(2 * d),
                  full(wbh), full(wbm), full(wo), full(norm_ffn), full(rw_split), full(rb)],
        out_specs=[tiles, tiles, row(TOP_K), row(TOP_K), row(TOP_K),
                   pl.BlockSpec((1, LANES), lambda b, j: (0, 0))],
        out_shape=[
            jax.ShapeDtypeStruct((m * TOK_ROWS, LANES), F32),
            jax.ShapeDtypeStruct((m * TOK_ROWS, LANES), F32),
            jax.ShapeDtypeStruct((m, TOP_K), jnp.int32),
            jax.ShapeDtypeStruct((m, TOP_K), F32),
            jax.ShapeDtypeStruct((m, TOP_K), jnp.int32),
            jax.ShapeDtypeStruct((1, LANES), F32),
        ],
        compiler_params=_cparams(("arbitrary", "arbitrary")),
        name="merge_route",
    )(x2d, head, y_hg, y_ml, gates, wbh, wbm, wo, norm_ffn, rw_split, rb)


def _toks(first, n=1):
    return pl.ds(pl.multiple_of(first * TOK_ROWS, TOK_ROWS), n * TOK_ROWS)


def _dispatch_kernel(ps_ref, pe_ref, pos_ref, x_ref, xs_hbm, zbuf, sem, *, n_experts, tm):
    b = pl.program_id(0)
    j = pl.program_id(1)
    tq = x_ref.shape[0] // TOK_ROWS

    @pl.when(jnp.logical_and(b == 0, j == 0))
    def _():
        zbuf[...] = jnp.zeros_like(zbuf)
        for e in range(n_experts):
            @pl.when(pe_ref[e] > ps_ref[e])
            def _():
                pltpu.make_async_copy(zbuf, xs_hbm.at[_toks(pe_ref[e] - tm, tm), :], sem).start()
        for e in range(n_experts):
            @pl.when(pe_ref[e] > ps_ref[e])
            def _():
                pltpu.make_async_copy(zbuf, xs_hbm.at[_toks(0, tm), :], sem).wait()

        def zero_tail(blk, carry):
            cp = pltpu.make_async_copy(zbuf, xs_hbm.at[_toks(blk * tm, tm), :], sem)
            cp.start()
            cp.wait()
            return carry
        lax.fori_loop(pe_ref[n_experts - 1] // tm, xs_hbm.shape[0] // (tm * TOK_ROWS), zero_tail, 0)

    def scatter_rows(lo):
        def body(r, carry):
            for kk in range(TOP_K):
                dst = pos_ref[0, 0, r * TOP_K + kk]
                pltpu.make_async_copy(
                    x_ref.at[_toks(r), :], xs_hbm.at[_toks(dst), :], sem).start(priority=kk % 2)
            return carry
        lax.fori_loop(lo, tq, body, 0, unroll=2)
        n = tq - lo
        for _ in range(TOP_K):
            pltpu.make_async_copy(x_ref.at[_toks(0, n), :], xs_hbm.at[_toks(0, n), :], sem).wait()

    @pl.when(j == 0)
    def _():
        scatter_rows(N_PAD)

    @pl.when(j != 0)
    def _():
        scatter_rows(0)


def _dispatch(pstart, pend, pos, xn_t, n_rows, bsz, tq, tm):
    m = xn_t.shape[0] // TOK_ROWS
    nj = m // (bsz * tq)
    n_experts = pstart.shape[0]
    sm = lambda a: a.reshape(bsz * nj, 1, tq * TOP_K)
    smem_blk = pl.BlockSpec((1, 1, tq * TOP_K), lambda b, j, ps, pe: (b * nj + j, 0, 0),
                            memory_space=pltpu.SMEM)
    grid_spec = pltpu.PrefetchScalarGridSpec(
        num_scalar_prefetch=2,
        grid=(bsz, nj),
        in_specs=[smem_blk,
                  pl.BlockSpec((tq * TOK_ROWS, LANES), lambda b, j, ps, pe: (b * nj + j, 0))],
        out_specs=pl.BlockSpec(memory_space=pl.ANY),
        scratch_shapes=[pltpu.VMEM((tm * TOK_ROWS, LANES), F32), pltpu.SemaphoreType.DMA(())],
    )
    return pl.pallas_call(
        functools.partial(_dispatch_kernel, n_experts=n_experts, tm=tm),
        grid_spec=grid_spec,
        out_shape=jax.ShapeDtypeStruct((n_rows * TOK_ROWS, LANES), F32),
        compiler_params=_cparams(("arbitrary", "arbitrary")),
        name="dispatch",
    )(pstart, pend, sm(pos), xn_t)


CAST_ROWS = 256


def _experts_kernel(be_ref, nu_ref, x_ref, wgu_ref, bg_ref, bu_ref, wd_ref, bd_ref, y_ref,
                    wg_s, wu_s, wd_s):
    i = pl.program_id(0)
    n_used = nu_ref[0]
    dff = wg_s.shape[0]
    new_expert = jnp.logical_or(i == 0, be_ref[i] != be_ref[jnp.maximum(i - 1, 0)])

    @pl.when(jnp.logical_and(new_expert, i < n_used))
    def _():
        for c in range(dff // CAST_ROWS):
            rows = pl.ds(c * CAST_ROWS, CAST_ROWS)
            wd_s[rows, :] = wd_ref[0, rows, :].astype(BF16)
            for s in range(wgu_ref.shape[1]):
                lanes = pl.ds(s * LANES, LANES)
                wg_s[rows, lanes] = wgu_ref[
                    0, s, pl.ds(2 * c * CAST_ROWS, CAST_ROWS, stride=2), :].astype(BF16)
                wu_s[rows, lanes] = wgu_ref[
                    0, s, pl.ds(2 * c * CAST_ROWS + 1, CAST_ROWS, stride=2), :].astype(BF16)

    @pl.when(i < n_used)
    def _():
        tm = x_ref.shape[0] // TOK_ROWS
        xb = _load_token_tiles(x_ref, tm).astype(BF16)
        g = _dot_nt(xb, wg_s[...]) + bg_ref[0]
        u = _dot_nt(xb, wu_s[...]) + bu_ref[0]
        gate = jnp.minimum(g, SWIGLU_LIMIT)
        up = jnp.clip(u, -SWIGLU_LIMIT, SWIGLU_LIMIT)
        act = (up + 1.0) * gate * _sigmoid(SWIGLU_ALPHA * gate)
        _store_token_tiles(y_ref, _dot(act.astype(BF16), wd_s[...]) + bd_ref[0])

    @pl.when(i >= n_used)
    def _():
        y_ref[...] = jnp.zeros_like(y_ref)


def _experts(block_e, n_used, xs, w_gu_t, b_g, b_u, w_d, b_d, tm):
    n_blocks = block_e.shape[0]
    dff, d = w_d.shape[1:]
    tile_blk = lambda f: pl.BlockSpec((tm * TOK_ROWS, LANES), f)
    assert dff % CAST_ROWS == 0
    wspec = lambda k, n: pl.BlockSpec((1, k, n), lambda i, be, nu: (be[i], 0, 0))
    grid_spec = pltpu.PrefetchScalarGridSpec(
        num_scalar_prefetch=2,
        grid=(n_blocks,),
        in_specs=[
            tile_blk(lambda i, be, nu: (jnp.minimum(i, nu[0] - 1), 0)),
            pl.BlockSpec((1, d // LANES, 2 * dff, LANES), lambda i, be, nu: (be[i], 0, 0, 0)),
            wspec(1, dff), wspec(1, dff), wspec(dff, d), wspec(1, d),
        ],
        out_specs=tile_blk(lambda i, be, nu: (i, 0)),
        scratch_shapes=[pltpu.VMEM((dff, d), BF16), pltpu.VMEM((dff, d), BF16), pltpu.VMEM((dff, d), BF16)],
    )
    return pl.pallas_call(
        _experts_kernel,
        grid_spec=grid_spec,
        out_shape=jax.ShapeDtypeStruct((n_blocks * tm * TOK_ROWS, LANES), F32),
        compiler_params=_cparams(("arbitrary",)),
        name="experts",
    )(block_e, n_used, xs, w_gu_t, b_g, b_u, w_d, b_d)


def _combine_kernel(pos_ref, posn_ref, hsrc_ref, hsrcn_ref, tw_ref, y_hbm, h_hbm,
                    nw_ref, o_ref, ybuf, hbuf, obuf, sem):
    i = pl.program_id(0)
    n = pl.num_programs(0)
    tt = o_ref.shape[0]
    d = o_ref.shape[1]
    slot = i % 2

    def start(src_pos_ref, src_h_ref, dst_slot):
        def body(r, carry):
            for kk in range(TOP_K):
                src = src_pos_ref[0, 0, r * TOP_K + kk]
                pltpu.make_async_copy(
                    y_hbm.at[_toks(src), :],
                    ybuf.at[dst_slot, _toks(kk * tt + r), :],
                    sem.at[dst_slot]).start(priority=kk % 2)
            return carry
        lax.fori_loop(0, tt, body, 0, unroll=2)
        pltpu.make_async_copy(
            h_hbm.at[_toks(src_h_ref[0, 0, 0], tt), :], hbuf.at[dst_slot], sem.at[dst_slot]).start()

    @pl.when(i == 0)
    def _():
        start(pos_ref, hsrc_ref, 0)

    @pl.when(i + 1 < n)
    def _():
        start(posn_ref, hsrcn_ref, 1 - slot)

    pltpu.make_async_copy(ybuf.at[slot], ybuf.at[slot], sem.at[slot]).wait()
    pltpu.make_async_copy(hbuf.at[slot], hbuf.at[slot], sem.at[slot]).wait()
    acc = hbuf[slot]
    tw = tw_ref[...]
    nt = tt * TOK_ROWS
    for kk in range(TOP_K):
        acc = acc + tw[:, kk:kk + 1] * ybuf[slot, kk * nt:(kk + 1) * nt, :]
    a3 = acc.reshape(tt, TOK_ROWS, LANES)
    ssq = jnp.sum(jnp.sum(a3 * a3, axis=2, keepdims=True), axis=1, keepdims=True)
    o3 = (a3 * lax.rsqrt(ssq * (1.0 / d) + EPS)) * nw_ref[...][None]
    obuf[...] = o3.reshape(nt, LANES)
    for s in range(TOK_ROWS):
        o_ref[:, s * LANES:(s + 1) * LANES] = obuf[pl.ds(s, tt, stride=TOK_ROWS), :]


def _combine(pos_seq, tw_rows, h_src, y_rows, h2_t, norm_w_t, d, tt):
    n_tiles = h_src.shape[0]
    last = n_tiles - 1
    nt = tt * TOK_ROWS
    cur = lambda i: (i, 0, 0)
    nxt = lambda i: (jnp.minimum(i + 1, last), 0, 0)
    sm = lambda a: a.reshape(n_tiles, 1, tt * TOP_K)
    idx_blk = lambda f: pl.BlockSpec((1, 1, TOP_K * tt), f, memory_space=pltpu.SMEM)
    one_blk = lambda f: pl.BlockSpec((1, 1, 1), f, memory_space=pltpu.SMEM)
    return pl.pallas_call(
        _combine_kernel,
        grid=(n_tiles,),
        in_specs=[
            idx_blk(cur), idx_blk(nxt), one_blk(cur), one_blk(nxt),
            pl.BlockSpec((nt, TOP_K), lambda i: (i, 0)),
            pl.BlockSpec(memory_space=pl.ANY),
            pl.BlockSpec(memory_space=pl.ANY),
            pl.BlockSpec(norm_w_t.shape, lambda i: (0, 0)),
        ],
        out_specs=pl.BlockSpec((tt, d), lambda i: (i, 0)),
        scratch_shapes=[pltpu.VMEM((2, TOP_K * nt, LANES), F32), pltpu.VMEM((2, nt, LANES), F32),
                        pltpu.VMEM((nt, LANES), F32), pltpu.SemaphoreType.DMA((2,))],
        out_shape=jax.ShapeDtypeStruct((n_tiles * tt, d), F32),
        compiler_params=_cparams(("arbitrary",)),
        name="combine",
    )(sm(pos_seq), sm(pos_seq), h_src, h_src, tw_rows, y_rows, h2_t, norm_w_t)


def _pick(n, prefs):
    for p in prefs:
        if n % p == 0:
            return p
    raise ValueError(f"no tile in {prefs} divides {n}")


def kernel(x, meta_tokens, hg_lb_logits, norm_mix, w_in, hg_norm, ml_conv_w, ml_conv_b, ml_wq, ml_wk, ml_wv,
           ml_gate_b, ml_norm, ml_skip, w_branch_hg, w_branch_ml, w_out, norm_ffn, router_w, router_b,
           exp_w_gu, exp_b_gu, exp_w_down, exp_b_down, norm_final):
    bsz, seq, d = x.shape
    assert norm_mix.shape[0] == 1, "single-layer block"
    assert seq % CHUNK == 0 and d % LANES == 0
    t = CHUNK + seq
    m_rows = bsz * t
    n_experts = router_w.shape[-1]
    dff = exp_w_down.shape[2]
    assert n_experts <= LANES

    head = jnp.concatenate([jnp.zeros((N_PAD, d), x.dtype), meta_tokens.astype(x.dtype)], axis=0)
    x2d = x.reshape(bsz * seq, d)
    lower_bounds = jnp.cumsum(jax.nn.softmax(hg_lb_logits.astype(F32), axis=0), axis=0)

    w = w_in[0]
    n_a = 4 * HG_W + 2 * ML_W
    w_a = w[:, :n_a].astype(BF16)
    w_m = jnp.pad(w[:, n_a:n_a + 2 * N_HEADS], ((0, 0), (0, LANES - 2 * N_HEADS))).astype(BF16)
    w_g = w[:, n_a + 2 * N_HEADS:].astype(BF16)
    tr1 = _pick(t, (704, 192, 64))
    hq, hf, hi, hg, mm, mo, gates, mif = _in_proj(x2d, head, norm_mix[0][None], w_a, w_g, w_m, bsz, tr1)

    bb = _pick(bsz, (4, 2, 1))
    r3 = lambda a: a.reshape(bsz, t, a.shape[-1])
    tri = jnp.asarray(np.tril(np.ones((CHUNK, CHUNK), np.float32)), BF16)
    y_hg = _hgrn2(r3(hq), r3(hf), r3(hi), r3(hg), tri, lower_bounds[0][None], hg_norm[0][None], bb)

    nc = t // CHUNK
    g_col = r3(mif)
    g_row = jnp.swapaxes(g_col[:, :, :2 * N_HEADS].reshape(bsz, nc, CHUNK, 2 * N_HEADS), 2, 3)
    gb = ml_gate_b[0].astype(F32)
    gb_col = jnp.pad(gb, (0, LANES - 2 * N_HEADS))[None]
    gb_row = jnp.broadcast_to(gb[:, None], (2 * N_HEADS, CHUNK))
    y_ml = _mlstm(r3(mm), r3(mo), g_col, g_row, tri, ml_conv_w[0], ml_conv_b[0][None],
                  ml_wq[0].astype(BF16), ml_wk[0].astype(BF16), ml_wv[0].astype(BF16),
                  gb_col, gb_row, ml_norm[0][None], ml_skip[0][None], bb)

    rw = jnp.pad(router_w[0].astype(F32), ((0, 0), (0, LANES - n_experts)))
    rw1 = rw.astype(BF16)
    rw2 = (rw - rw1.astype(F32)).astype(BF16)
    rw_split = jnp.stack([rw1, rw2])
    rb = jnp.pad(router_b[0].astype(F32), (0, LANES - n_experts))[None]
    tq = _pick(t, (352, 192, 64))
    h2, xn2, top_e, top_w, rank, cnt = _merge_route(
        x2d, head, y_hg.reshape(m_rows, HG_W), y_ml.reshape(m_rows, ML_W), gates,
        w_branch_hg[0].astype(BF16), w_branch_ml[0].astype(BF16), w_out[0].astype(BF16),
        norm_ffn[0][None], rw_split, rb, n_experts, bsz, tq)

    tm6 = 512
    n_assign = bsz * (t - N_PAD) * TOP_K
    counts = cnt[0, :n_experts].astype(jnp.int32)
    padded = ((counts + tm6 - 1) // tm6) * tm6
    pend = jnp.cumsum(padded).astype(jnp.int32)
    pstart = pend - padded
    n_blocks = -(-n_assign // tm6) + n_experts
    n_rows = n_blocks * tm6
    blk_start = jnp.arange(n_blocks, dtype=jnp.int32) * tm6
    block_e = jnp.minimum(jnp.sum((blk_start[:, None] >= pend[None, :]).astype(jnp.int32), axis=1),
                          n_experts - 1)
    n_used = (pend[-1] // tm6)[None]
    pos = rank + jnp.sum(jnp.where(top_e[:, :, None] == jnp.arange(n_experts, dtype=jnp.int32),
                                   pstart[None, None, :], 0), axis=-1)
    xs = _dispatch(pstart, pend, pos, xn2, n_rows, bsz, tq, tm6)

    w_gu_t = jnp.swapaxes(exp_w_gu[0].reshape(n_experts, d // LANES, LANES, 2 * dff), 2, 3)
    bgu = exp_b_gu[0]
    y_rows = _experts(block_e, n_used, xs, w_gu_t, bgu[:, None, 0::2], bgu[:, None, 1::2],
                      exp_w_down[0], exp_b_down[0][:, None, :], tm6)

    tt = _pick(seq, (128, 64))
    tiles_per_b = seq // tt
    n_tiles = bsz * tiles_per_b
    seq_part = lambda a: a.reshape(bsz, t, TOP_K)[:, CHUNK:].reshape(bsz * seq, TOP_K)
    h_src = (jnp.arange(bsz, dtype=jnp.int32)[:, None] * t + CHUNK
             + jnp.arange(tiles_per_b, dtype=jnp.int32)[None, :] * tt).reshape(n_tiles, 1, 1)
    tw_rows = jnp.repeat(seq_part(top_w), TOK_ROWS, axis=0)
    out = _combine(seq_part(pos), tw_rows, h_src, y_rows, h2,
                   norm_final.reshape(TOK_ROWS, LANES), d, tt)
    return out.reshape(bsz, seq, d)
```

```python
import functools

import numpy as np
import jax
import jax.numpy as jnp
from jax import lax
from jax.experimental import pallas as pl
from jax.experimental.pallas import tpu as pltpu

F32 = jnp.float32
BF16 = jnp.bfloat16

N_META = 16
CHUNK = 64
N_PAD = CHUNK - N_META
EPS = 1e-6

N_HEADS = 4
HG_DK = 128
HG_DV = 128
HG_W = N_HEADS * HG_DV
ML_DK = 64
ML_DV = 128
ML_W = N_HEADS * ML_DV
ML_CONV = 4
TOP_K = 4
SWIGLU_LIMIT = 7.0
SWIGLU_ALPHA = 1.702

LANES = 128
VMEM_LIMIT_BYTES = 56 * 1024 * 1024

HG_LEVELS = (32, 16, 8, 4, 2, 1)


def _cparams(sem):
    return pltpu.CompilerParams(dimension_semantics=sem, vmem_limit_bytes=VMEM_LIMIT_BYTES)


def _sigmoid(x):
    return 1.0 / (1.0 + jnp.exp(-x))


def _split3(x):
    x1 = x.astype(BF16)
    r1 = x - x1.astype(F32)
    x2 = r1.astype(BF16)
    x3 = (r1 - x2.astype(F32)).astype(BF16)
    return x1, x2, x3


def _dot(a, b):
    return jnp.dot(a, b, preferred_element_type=F32)


TOK_ROWS = 8


def _store_token_tiles(ref, x):
    n = x.shape[0]
    for s in range(TOK_ROWS):
        ref[pl.ds(s, n, stride=TOK_ROWS), :] = x[:, s * LANES:(s + 1) * LANES]


def _load_token_tiles(ref, n):
    return jnp.concatenate([ref[pl.ds(s, n, stride=TOK_ROWS), :] for s in range(TOK_ROWS)], axis=1)


def _dot_nt(a, b):
    return lax.dot_general(a, b, (((1,), (1,)), ((), ())), preferred_element_type=F32)


def _dot_exact_lhs(m_bf16, x):
    x1, x2, x3 = _split3(x)
    return _dot(m_bf16, x1) + _dot(m_bf16, x2) + _dot(m_bf16, x3)


def _h_tile(x_ref, head_ref, j):
    x = x_ref[...]
    first = jnp.concatenate([head_ref[...], x[:x.shape[0] - CHUNK]], axis=0)
    return jnp.where(j == 0, first, x)


def _x_tile_spec(tr, d, seq):
    assert seq % 8 == 0 and tr % 8 == 0 and CHUNK % 8 == 0
    return pl.BlockSpec(
        (pl.Element(tr), pl.Element(d)),
        lambda b, j: (pl.multiple_of(b * seq + jnp.maximum(j * tr - CHUNK, 0), 8), 0))


def _in_proj_kernel(x_ref, head_ref, nw_ref, wa_ref, wg_ref, wm_ref,
                    q_ref, f_ref, i_ref, g_ref, mm_ref, mo_ref, gates_ref, mif_ref):
    x = _h_tile(x_ref, head_ref, pl.program_id(1))
    ms = jnp.mean(x * x, axis=-1, keepdims=True)
    xb = ((x * lax.rsqrt(ms + EPS)) * nw_ref[...]).astype(BF16)
    outs = (q_ref, f_ref, i_ref, g_ref, mm_ref, mo_ref)
    for j, o_ref in enumerate(outs):
        o_ref[...] = _dot(xb, wa_ref[:, j * HG_W:(j + 1) * HG_W]).astype(o_ref.dtype)
    d_model = gates_ref.shape[1] // 2
    for j in range(2):
        gates_ref[:, j * d_model:(j + 1) * d_model] = _dot(
            xb, wg_ref[:, j * d_model:(j + 1) * d_model]).astype(gates_ref.dtype)
    mif_ref[...] = _dot(xb, wm_ref[...])


def _in_proj(x2d, head, norm_w, w_a, w_g, w_m, bsz, tr):
    d = x2d.shape[1]
    seq = x2d.shape[0] // bsz
    nj = (seq + CHUNK) // tr
    m = bsz * nj * tr
    row = lambda n: pl.BlockSpec((tr, n), lambda b, j: (b * nj + j, 0))
    full = lambda a: pl.BlockSpec(a.shape, lambda b, j: (0, 0))
    out_shape = [
        jax.ShapeDtypeStruct((m, HG_W), BF16),
        jax.ShapeDtypeStruct((m, HG_W), F32),
        jax.ShapeDtypeStruct((m, HG_W), BF16),
        jax.ShapeDtypeStruct((m, HG_W), BF16),
        jax.ShapeDtypeStruct((m, ML_W), BF16),
        jax.ShapeDtypeStruct((m, ML_W), BF16),
        jax.ShapeDtypeStruct((m, 2 * d), BF16),
        jax.ShapeDtypeStruct((m, LANES), F32),
    ]
    return pl.pallas_call(
        _in_proj_kernel,
        grid=(bsz, nj),
        in_specs=[_x_tile_spec(tr, d, seq), full(head), full(norm_w), full(w_a), full(w_g), full(w_m)],
        out_specs=[row(s.shape[1]) for s in out_shape],
        out_shape=out_shape,
        compiler_params=_cparams(("parallel", "parallel")),
        name="in_proj",
    )(x2d, head, norm_w, w_a, w_g, w_m)


def _hgrn2_kernel(q_ref, f_ref, v_ref, g_ref, tri_ref, lb_ref, nw_ref, y_ref, *st_refs):
    c = pl.program_id(1)
    bb = q_ref.shape[0]

    @pl.when(c == 0)
    def _():
        for st_ref in st_refs:
            st_ref[...] = jnp.zeros_like(st_ref)

    row = lax.broadcasted_iota(jnp.int32, (CHUNK, 1), 0)
    valid = (c * CHUNK + row) >= N_PAD
    ti = lax.broadcasted_iota(jnp.int32, (CHUNK, CHUNK), 0)
    si = lax.broadcasted_iota(jnp.int32, (CHUNK, CHUNK), 1)
    diag_mask = ti == si
    level_masks = {}
    for m in HG_LEVELS:
        same_pair = (ti & ~(2 * m - 1)) == (si & ~(2 * m - 1))
        level_masks[m] = same_pair & ((ti & m) != 0) & ((si & m) == 0)

    lb = lb_ref[...]
    tri = tri_ref[...]
    per_b = []
    for b in range(bb):
        hf = f_ref[b]
        f = lb + (1.0 - lb) * _sigmoid(hf)
        f = jnp.where(valid, f, 1.0)
        logf = jnp.log(f)
        k_all = 1.0 - f
        hq = q_ref[b].astype(F32)
        q_all = hq * _sigmoid(hq)
        b_cum = _dot_exact_lhs(tri, logf)
        e_b = jnp.exp(b_cum)
        e_bl = jnp.exp(b_cum[CHUNK - 1:CHUNK] - b_cum)

        q_fac, k_fac = {}, {}
        for m in HG_LEVELS:
            if m >= 4:
                grp = b_cum.reshape(CHUNK // (2 * m), 2 * m, HG_W)
                e = (grp - grp[:, m - 1:m, :]).reshape(CHUNK, HG_W)
                q_fac[m] = jnp.exp(jnp.minimum(e, 0.0))
                k_fac[m] = jnp.exp(jnp.minimum(-e, 0.0))
        f_prev = pltpu.roll(f, 1, 0)
        f_next = pltpu.roll(f, CHUNK - 1, 0)
        r4 = row & 3
        q_fac[2] = jnp.where(r4 == 2, f, jnp.where(r4 == 3, f * f_prev, 1.0))
        k_fac[2] = jnp.where(r4 == 0, f_next, 1.0)
        q_fac[1] = jnp.where((row & 1) == 1, f, 1.0)

        hg = g_ref[b].astype(F32)
        per_b.append((q_all, k_all, q_fac, k_fac, e_b, e_bl, v_ref[b], hg * _sigmoid(hg)))

    pairs = [(b, h) for b in range(bb) for h in range(N_HEADS)]
    sls = [slice(h * HG_DK, (h + 1) * HG_DK) for h in range(N_HEADS)]

    scores_all = []
    for b, h in pairs:
        q_all, k_all, q_fac, k_fac = per_b[b][:4]
        q = q_all[:, sls[h]]
        k = k_all[:, sls[h]]
        kb = k.astype(BF16)
        scores = jnp.where(diag_mask, _dot_nt(q.astype(BF16), kb), 0.0)
        for m in HG_LEVELS:
            qd = (q * q_fac[m][:, sls[h]]).astype(BF16)
            kd = (k * k_fac[m][:, sls[h]]).astype(BF16) if m in k_fac else kb
            scores = jnp.where(level_masks[m], _dot_nt(qd, kd), scores)
        scores_all.append(scores.astype(BF16))

    sts = [st_ref[...] for st_ref in st_refs]
    outs = []
    for p, (b, h) in enumerate(pairs):
        q_all, e_b, v_all = per_b[b][0], per_b[b][4], per_b[b][6]
        qe = (q_all[:, sls[h]] * e_b[:, sls[h]]).astype(BF16)
        outs.append(_dot(scores_all[p], v_all[:, sls[h]]) + _dot_nt(qe, sts[p].astype(BF16)))

    for p, (b, h) in enumerate(pairs):
        k_all, e_b, e_bl, v_all = per_b[b][1], per_b[b][4], per_b[b][5], per_b[b][6]
        kl = (k_all[:, sls[h]] * e_bl[:, sls[h]]).astype(BF16)
        vt = v_all[:, sls[h]].astype(F32).T.astype(BF16)
        st_refs[p][...] = e_b[CHUNK - 1:CHUNK, sls[h]] * sts[p] + _dot(vt, kl)

    for p, (b, h) in enumerate(pairs):
        o = outs[p]
        ms = jnp.mean(o * o, axis=-1, keepdims=True)
        y = (o * lax.rsqrt(ms + EPS)) * nw_ref[:, sls[h]] * per_b[b][7][:, sls[h]]
        y_ref[b, :, sls[h]] = y.astype(y_ref.dtype)


def _hgrn2(hq, hf, hi, hg, tri, lb, norm_w, bb):
    bsz, t, _ = hq.shape
    nc = t // CHUNK
    blk = pl.BlockSpec((bb, CHUNK, HG_W), lambda i, c: (i, c, 0))
    full = lambda a: pl.BlockSpec(a.shape, lambda i, c: (0, 0))
    return pl.pallas_call(
        _hgrn2_kernel,
        grid=(bsz // bb, nc),
        in_specs=[blk, blk, blk, blk, full(tri), full(lb), full(norm_w)],
        out_specs=blk,
        out_shape=jax.ShapeDtypeStruct((bsz, t, HG_W), BF16),
        scratch_shapes=[pltpu.VMEM((HG_DV, HG_DK), F32)] * (bb * N_HEADS),
        compiler_params=_cparams(("parallel", "arbitrary")),
        name="hgrn2",
    )(hq, hf, hi, hg, tri, lb, norm_w)


def _log_sigmoid(x):
    return jnp.minimum(x, 0.0) - jnp.log(1.0 + jnp.exp(-jnp.abs(x)))


def _mlstm_kernel(mm_ref, mo_ref, gc_ref, gr_ref, tri_ref, cw_ref, cb_ref, wq_ref, wk_ref, wv_ref,
                  gbc_ref, gbr_ref, nw_ref, sk_ref, y_ref, *scratch):
    c = pl.program_id(1)
    bb = mm_ref.shape[0]
    n_pairs = bb * N_HEADS
    s_refs = scratch[:n_pairs]
    m_refs = scratch[n_pairs:2 * n_pairs]
    tail_refs = scratch[2 * n_pairs:]

    @pl.when(c == 0)
    def _():
        for ref in scratch:
            ref[...] = jnp.zeros_like(ref)

    pos_c = c * CHUNK + lax.broadcasted_iota(jnp.int32, (CHUNK, 1), 0)
    valid_c = pos_c >= N_PAD
    pos_r = c * CHUNK + lax.broadcasted_iota(jnp.int32, (1, CHUNK), 1)
    valid_r = pos_r >= N_PAD
    ti = lax.broadcasted_iota(jnp.int32, (CHUNK, CHUNK), 0)
    si = lax.broadcasted_iota(jnp.int32, (CHUNK, CHUNK), 1)
    causal = si <= ti
    tri = tri_ref[...]
    ones_v = jnp.ones((CHUNK, ML_DV), BF16)
    neg_inf = -jnp.inf

    per_b = []
    for b in range(bb):
        mm = jnp.where(valid_c, mm_ref[b].astype(F32), 0.0)
        ext = jnp.concatenate([tail_refs[b][...], mm], axis=0)
        tail_refs[b][...] = mm[CHUNK - 8:CHUNK]
        conv = cb_ref[...]
        for j in range(ML_CONV):
            off = 8 - (ML_CONV - 1) + j
            conv = conv + cw_ref[j:j + 1, :] * ext[off:off + CHUNK]
        cact = conv * _sigmoid(conv)
        cact_b = cact.astype(BF16)
        mm_b = mm.astype(BF16)

        gcol = gc_ref[b] + gbc_ref[...]
        li_col = jnp.where(valid_c, gcol, neg_inf)
        lf_col = jnp.where(valid_c, _log_sigmoid(gcol), 0.0)
        b_col = _dot_exact_lhs(tri, lf_col)
        grow = gr_ref[b, 0] + gbr_ref[...]
        li_row = jnp.where(valid_r, grow, neg_inf)
        lf_row = jnp.where(valid_r, _log_sigmoid(grow), 0.0)
        r1, r2, r3 = _split3(lf_row)
        b_row = _dot_nt(r1, tri) + _dot_nt(r2, tri) + _dot_nt(r3, tri)

        ogate = _sigmoid(mo_ref[b].astype(F32))
        per_b.append((cact, cact_b, mm_b, li_col, b_col, li_row, b_row, ogate))

    pairs = [(b, h) for b in range(bb) for h in range(N_HEADS)]
    sls = [slice(h * ML_DV, (h + 1) * ML_DV) for h in range(N_HEADS)]

    qs, ks, vs = [], [], []
    for b, h in pairs:
        cact_b, mm_b = per_b[b][1], per_b[b][2]
        qs.append((_dot(cact_b[:, sls[h]], wq_ref[h]) * (ML_DK ** -0.5)).astype(BF16))
        ks.append(_dot(cact_b[:, sls[h]], wk_ref[h]))
        v = _dot(mm_b[:, sls[h]], wv_ref[h]).astype(BF16)
        vs.append(jnp.concatenate([v, ones_v], axis=1))

    n_p = len(pairs)
    blk = lambda p: slice(p * CHUNK, (p + 1) * CHUNK)
    stack = lambda xs: jnp.concatenate(xs, axis=0)
    bc_all = stack([per_b[b][4][:, N_HEADS + h:N_HEADS + h + 1] for b, h in pairs])
    lic_all = stack([per_b[b][3][:, h:h + 1] for b, h in pairs])
    row_all = stack([jnp.broadcast_to(per_b[b][5][h:h + 1, :] - per_b[b][6][N_HEADS + h:N_HEADS + h + 1, :],
                                      (CHUNK, CHUNK)) for b, h in pairs])
    mprev_all = stack([jnp.broadcast_to(m_refs[p][0:1, 0:1], (CHUNK, 1)) for p in range(n_p)])
    glast_all = stack([jnp.broadcast_to(per_b[b][4][CHUNK - 1:CHUNK, N_HEADS + h:N_HEADS + h + 1], (CHUNK, 1))
                       for b, h in pairs])
    causal_all = stack([causal] * n_p)

    d_all = jnp.where(causal_all, bc_all + row_all, neg_inf)
    a_all = bc_all + mprev_all
    m_t_all = jnp.maximum(a_all, jnp.max(d_all, axis=-1, keepdims=True))
    w_intra_all = jnp.exp(d_all - m_t_all)
    w_inter_all = jnp.exp(a_all - m_t_all)

    qk_all = (stack([_dot_nt(qs[p], ks[p].astype(BF16)) for p in range(n_p)]) * w_intra_all).astype(BF16)
    s_augs = [s_refs[p][...] for p in range(n_p)]
    intra_all = stack([_dot(qk_all[blk(p)], vs[p]) for p in range(n_p)])
    inter_all = stack([_dot(qs[p], s_augs[p].astype(BF16)) for p in range(n_p)])
    numden_all = intra_all + w_inter_all * inter_all
    o_all = numden_all[:, :ML_DV] / jnp.maximum(jnp.abs(numden_all[:, ML_DV:]), jnp.exp(-m_t_all))
    ms_all = jnp.mean(o_all * o_all, axis=-1, keepdims=True)
    on_all = o_all * lax.rsqrt(ms_all + EPS)

    e_all = glast_all - bc_all + lic_all
    gm_all = glast_all + mprev_all
    e_max = jnp.max(e_all.reshape(n_p, CHUNK, 1), axis=1, keepdims=True)
    m_new_all = jnp.maximum(gm_all.reshape(n_p, CHUNK, 1), e_max).reshape(n_p * CHUNK, 1)
    w_s_all = jnp.exp(e_all - m_new_all)
    w_p_all = jnp.exp(gm_all - m_new_all)
    kw_all = stack(ks) * w_s_all
    for p in range(n_p):
        kw_t = kw_all[blk(p)].T.astype(BF16)
        s_refs[p][...] = w_p_all[p * CHUNK:p * CHUNK + 1] * s_augs[p] + _dot(kw_t, vs[p])
        m_refs[p][...] = jnp.broadcast_to(m_new_all[p * CHUNK:p * CHUNK + 1], m_refs[p].shape)

    for p, (b, h) in enumerate(pairs):
        cact, ogate = per_b[b][0], per_b[b][7]
        sl = sls[h]
        y = (on_all[blk(p)] * nw_ref[:, sl] + sk_ref[:, sl] * cact[:, sl]) * ogate[:, sl]
        y_ref[b, :, sl] = y.astype(y_ref.dtype)


def _mlstm(mm, mo, g_col, g_row, tri, conv_w, conv_b, wq, wk, wv, gb_col, gb_row, norm_w, skip, bb):
    bsz, t, _ = mm.shape
    nc = t // CHUNK
    blk = pl.BlockSpec((bb, CHUNK, ML_W), lambda i, c: (i, c, 0))
    gcb = pl.BlockSpec((bb, CHUNK, LANES), lambda i, c: (i, c, 0))
    grb = pl.BlockSpec((bb, 1, 8, CHUNK), lambda i, c: (i, c, 0, 0))

    def full(a):
        nd = a.ndim
        return pl.BlockSpec(a.shape, lambda i, c: (0,) * nd)

    params = (tri, conv_w, conv_b, wq, wk, wv, gb_col, gb_row, norm_w, skip)
    return pl.pallas_call(
        _mlstm_kernel,
        grid=(bsz // bb, nc),
        in_specs=[blk, blk, gcb, grb] + [full(p) for p in params],
        out_specs=blk,
        out_shape=jax.ShapeDtypeStruct((bsz, t, ML_W), BF16),
        scratch_shapes=([pltpu.VMEM((ML_DK, 2 * ML_DV), F32)] * (bb * N_HEADS)
                        + [pltpu.VMEM((8, LANES), F32)] * (bb * N_HEADS)
                        + [pltpu.VMEM((8, ML_W), F32)] * bb),
        compiler_params=_cparams(("parallel", "arbitrary")),
        name="mlstm",
    )(mm, mo, g_col, g_row, *params)


def _merge_route_kernel(x_ref, head_ref, yh_ref, ym_ref, gates_ref, wbh_ref, wbm_ref, wo_ref, nf_ref,
                        rw_ref, rb_ref, h2_ref, xn_ref, te_ref, tw_ref, rk_ref, cnt_ref, *, n_experts):
    d = x_ref.shape[1]
    tq = x_ref.shape[0]
    j = pl.program_id(1)

    @pl.when(jnp.logical_and(pl.program_id(0) == 0, j == 0))
    def _():
        cnt_ref[...] = jnp.zeros_like(cnt_ref)

    g0 = _sigmoid(gates_ref[:, :d].astype(F32))
    g1 = _sigmoid(gates_ref[:, d:].astype(F32))
    merged = g0 * _dot(yh_ref[...], wbh_ref[...]) + g1 * _dot(ym_ref[...], wbm_ref[...])
    h2 = _h_tile(x_ref, head_ref, j) + _dot(merged.astype(BF16), wo_ref[...])
    h2_ref[...] = h2
    ms = jnp.mean(h2 * h2, axis=-1, keepdims=True)
    xn = (h2 * lax.rsqrt(ms + EPS)) * nf_ref[...]
    _store_token_tiles(xn_ref, xn)
    x1, x2, _ = _split3(xn)
    logits = (_dot(x1, rw_ref[0]) + _dot(x1, rw_ref[1]) + _dot(x2, rw_ref[0])) + rb_ref[...]
    lane = lax.broadcasted_iota(jnp.int32, logits.shape, 1)
    work = jnp.where(lane < n_experts, logits, -jnp.inf)
    vals, idxs = [], []
    for _ in range(TOP_K):
        vmax = jnp.max(work, axis=-1, keepdims=True)
        imax = jnp.min(jnp.where(work == vmax, lane, LANES), axis=-1, keepdims=True)
        vals.append(vmax)
        idxs.append(imax)
        work = jnp.where(lane == imax, -jnp.inf, work)
    exps = [jnp.exp(v - vals[0]) for v in vals]
    tot = exps[0] + exps[1] + exps[2] + exps[3]
    te = jnp.zeros(logits.shape, jnp.int32)
    tw = jnp.zeros(logits.shape, F32)
    for kk in range(TOP_K):
        te = jnp.where(lane == kk, idxs[kk], te)
        tw = jnp.where(lane == kk, exps[kk] / tot, tw)
    te_ref[...] = te[:, :TOP_K]
    tw_ref[...] = tw[:, :TOP_K]

    valid = (j * tq + lax.broadcasted_iota(jnp.int32, (tq, 1), 0)) >= N_PAD
    onehots = [jnp.where(jnp.logical_and(lane == idxs[kk], valid), 1.0, 0.0) for kk in range(TOP_K)]
    oh_all = onehots[0] + onehots[1] + onehots[2] + onehots[3]
    ri = lax.broadcasted_iota(jnp.int32, (tq, tq), 0)
    ci = lax.broadcasted_iota(jnp.int32, (tq, tq), 1)
    earlier = jnp.where(ci < ri, 1.0, 0.0).astype(BF16)
    before = _dot(earlier, oh_all.astype(BF16)) + cnt_ref[...]
    rk = jnp.zeros(logits.shape, F32)
    for kk in range(TOP_K):
        rank_k = jnp.sum(jnp.where(lane == idxs[kk], before, 0.0), axis=-1, keepdims=True)
        rk = jnp.where(lane == kk, rank_k, rk)
        before = before + onehots[kk]
    rk_ref[...] = rk[:, :TOP_K].astype(jnp.int32)
    cnt_ref[...] = cnt_ref[...] + jnp.sum(oh_all, axis=0, keepdims=True)


def _merge_route(x2d, head, y_hg, y_ml, gates, wbh, wbm, wo, norm_ffn, rw_split, rb, n_experts, bsz, tq):
    d = x2d.shape[1]
    seq = x2d.shape[0] // bsz
    nj = (seq + CHUNK) // tq
    m = bsz * nj * tq
    assert d == TOK_ROWS * LANES, "token-tile layout assumes one (8,128) tile per token"
    row = lambda n: pl.BlockSpec((tq, n), lambda b, j: (b * nj + j, 0))
    tiles = pl.BlockSpec((tq * TOK_ROWS, LANES), lambda b, j: (b * nj + j, 0))

    def full(a):
        nd = a.ndim
        return pl.BlockSpec(a.shape, lambda b, j: (0,) * nd)

    return pl.pallas_call(
        functools.partial(_merge_route_kernel, n_experts=n_experts),
        grid=(bsz, nj),
        in_specs=[_x_tile_spec(tq, d, seq), full(head), row(HG_W), row(ML_W), row(2 * d),
                  full(wbh), full(wbm), full(wo), full(norm_ffn), full(rw_split), full(rb)],
        out_specs=[row(d), tiles, row(TOP_K), row(TOP_K), row(TOP_K),
                   pl.BlockSpec((1, LANES), lambda b, j: (0, 0))],
        out_shape=[
            jax.ShapeDtypeStruct((m, d), F32),
            jax.ShapeDtypeStruct((m * TOK_ROWS, LANES), F32),
            jax.ShapeDtypeStruct((m, TOP_K), jnp.int32),
            jax.ShapeDtypeStruct((m, TOP_K), F32),
            jax.ShapeDtypeStruct((m, TOP_K), jnp.int32),
            jax.ShapeDtypeStruct((1, LANES), F32),
        ],
        compiler_params=_cparams(("arbitrary", "arbitrary")),
        name="merge_route",
    )(x2d, head, y_hg, y_ml, gates, wbh, wbm, wo, norm_ffn, rw_split, rb)


def _toks(first, n=1):
    return pl.ds(pl.multiple_of(first * TOK_ROWS, TOK_ROWS), n * TOK_ROWS)


def _dispatch_kernel(ps_ref, pe_ref, pos_ref, x_ref, xs_hbm, zbuf, sem, *, n_experts, tm):
    b = pl.program_id(0)
    j = pl.program_id(1)
    tq = x_ref.shape[0] // TOK_ROWS

    @pl.when(jnp.logical_and(b == 0, j == 0))
    def _():
        zbuf[...] = jnp.zeros_like(zbuf)
        for e in range(n_experts):
            @pl.when(pe_ref[e] > ps_ref[e])
            def _():
                pltpu.make_async_copy(zbuf, xs_hbm.at[_toks(pe_ref[e] - tm, tm), :], sem).start()
        for e in range(n_experts):
            @pl.when(pe_ref[e] > ps_ref[e])
            def _():
                pltpu.make_async_copy(zbuf, xs_hbm.at[_toks(0, tm), :], sem).wait()

        def zero_tail(blk, carry):
            cp = pltpu.make_async_copy(zbuf, xs_hbm.at[_toks(blk * tm, tm), :], sem)
            cp.start()
            cp.wait()
            return carry
        lax.fori_loop(pe_ref[n_experts - 1] // tm, xs_hbm.shape[0] // (tm * TOK_ROWS), zero_tail, 0)

    def scatter_rows(lo):
        def body(r, carry):
            for kk in range(TOP_K):
                dst = pos_ref[0, 0, r * TOP_K + kk]
                pltpu.make_async_copy(
                    x_ref.at[_toks(r), :], xs_hbm.at[_toks(dst), :], sem).start(priority=kk % 2)
            return carry
        lax.fori_loop(lo, tq, body, 0, unroll=2)
        n = tq - lo
        for _ in range(TOP_K):
            pltpu.make_async_copy(x_ref.at[_toks(0, n), :], xs_hbm.at[_toks(0, n), :], sem).wait()

    @pl.when(j == 0)
    def _():
        scatter_rows(N_PAD)

    @pl.when(j != 0)
    def _():
        scatter_rows(0)


def _dispatch(pstart, pend, pos, xn_t, n_rows, bsz, tq, tm):
    m = xn_t.shape[0] // TOK_ROWS
    nj = m // (bsz * tq)
    n_experts = pstart.shape[0]
    sm = lambda a: a.reshape(bsz * nj, 1, tq * TOP_K)
    smem_blk = pl.BlockSpec((1, 1, tq * TOP_K), lambda b, j, ps, pe: (b * nj + j, 0, 0),
                            memory_space=pltpu.SMEM)
    grid_spec = pltpu.PrefetchScalarGridSpec(
        num_scalar_prefetch=2,
        grid=(bsz, nj),
        in_specs=[smem_blk,
                  pl.BlockSpec((tq * TOK_ROWS, LANES), lambda b, j, ps, pe: (b * nj + j, 0))],
        out_specs=pl.BlockSpec(memory_space=pl.ANY),
        scratch_shapes=[pltpu.VMEM((tm * TOK_ROWS, LANES), F32), pltpu.SemaphoreType.DMA(())],
    )
    return pl.pallas_call(
        functools.partial(_dispatch_kernel, n_experts=n_experts, tm=tm),
        grid_spec=grid_spec,
        out_shape=jax.ShapeDtypeStruct((n_rows * TOK_ROWS, LANES), F32),
        compiler_params=_cparams(("arbitrary", "arbitrary")),
        name="dispatch",
    )(pstart, pend, sm(pos), xn_t)


CAST_ROWS = 256


def _experts_kernel(be_ref, nu_ref, x_ref, wgu_ref, bg_ref, bu_ref, wd_ref, bd_ref, y_ref,
                    wg_s, wu_s, wd_s):
    i = pl.program_id(0)
    n_used = nu_ref[0]
    dff = wg_s.shape[0]
    new_expert = jnp.logical_or(i == 0, be_ref[i] != be_ref[jnp.maximum(i - 1, 0)])

    @pl.when(jnp.logical_and(new_expert, i < n_used))
    def _():
        for c in range(dff // CAST_ROWS):
            rows = pl.ds(c * CAST_ROWS, CAST_ROWS)
            wd_s[rows, :] = wd_ref[0, rows, :].astype(BF16)
            for s in range(wgu_ref.shape[1]):
                lanes = pl.ds(s * LANES, LANES)
                wg_s[rows, lanes] = wgu_ref[
                    0, s, pl.ds(2 * c * CAST_ROWS, CAST_ROWS, stride=2), :].astype(BF16)
                wu_s[rows, lanes] = wgu_ref[
                    0, s, pl.ds(2 * c * CAST_ROWS + 1, CAST_ROWS, stride=2), :].astype(BF16)

    @pl.when(i < n_used)
    def _():
        tm = x_ref.shape[0] // TOK_ROWS
        xb = _load_token_tiles(x_ref, tm).astype(BF16)
        g = _dot_nt(xb, wg_s[...]) + bg_ref[0]
        u = _dot_nt(xb, wu_s[...]) + bu_ref[0]
        gate = jnp.minimum(g, SWIGLU_LIMIT)
        up = jnp.clip(u, -SWIGLU_LIMIT, SWIGLU_LIMIT)
        act = (up + 1.0) * gate * _sigmoid(SWIGLU_ALPHA * gate)
        _store_token_tiles(y_ref, _dot(act.astype(BF16), wd_s[...]) + bd_ref[0])

    @pl.when(i >= n_used)
    def _():
        y_ref[...] = jnp.zeros_like(y_ref)


def _experts(block_e, n_used, xs, w_gu_t, b_g, b_u, w_d, b_d, tm):
    n_blocks = block_e.shape[0]
    dff, d = w_d.shape[1:]
    tile_blk = lambda f: pl.BlockSpec((tm * TOK_ROWS, LANES), f)
    assert dff % CAST_ROWS == 0
    wspec = lambda k, n: pl.BlockSpec((1, k, n), lambda i, be, nu: (be[i], 0, 0))
    grid_spec = pltpu.PrefetchScalarGridSpec(
        num_scalar_prefetch=2,
        grid=(n_blocks,),
        in_specs=[
            tile_blk(lambda i, be, nu: (jnp.minimum(i, nu[0] - 1), 0)),
            pl.BlockSpec((1, d // LANES, 2 * dff, LANES), lambda i, be, nu: (be[i], 0, 0, 0)),
            wspec(1, dff), wspec(1, dff), wspec(dff, d), wspec(1, d),
        ],
        out_specs=tile_blk(lambda i, be, nu: (i, 0)),
        scratch_shapes=[pltpu.VMEM((dff, d), BF16), pltpu.VMEM((dff, d), BF16), pltpu.VMEM((dff, d), BF16)],
    )
    return pl.pallas_call(
        _experts_kernel,
        grid_spec=grid_spec,
        out_shape=jax.ShapeDtypeStruct((n_blocks * tm * TOK_ROWS, LANES), F32),
        compiler_params=_cparams(("arbitrary",)),
        name="experts",
    )(block_e, n_used, xs, w_gu_t, b_g, b_u, w_d, b_d)


def _combine_kernel(pos_ref, posn_ref, hsrc_ref, hsrcn_ref, tw_ref, y_hbm, h_hbm,
                    nw_ref, o_ref, ybuf, hbuf, sem):
    i = pl.program_id(0)
    n = pl.num_programs(0)
    tt = o_ref.shape[0]
    nt = tt * TOK_ROWS

    def copies(src_pos_ref, src_h_ref, dst_slot, r):
        return [pltpu.make_async_copy(
            y_hbm.at[_toks(src_pos_ref[0, 0, r * TOP_K + kk]), :],
            ybuf.at[dst_slot, _toks(kk * tt + r), :],
            sem.at[dst_slot]) for kk in range(TOP_K)]

    def start_h(src_h_ref, dst_slot):
        h_row = pl.multiple_of(src_h_ref[0, 0, 0], 8)
        pltpu.make_async_copy(h_hbm.at[pl.ds(h_row, tt), :], hbuf.at[dst_slot], sem.at[dst_slot]).start()

    @pl.when(i == 0)
    def _():
        def body(r, carry):
            for kk, cp in enumerate(copies(pos_ref, hsrc_ref, 0, r)):
                cp.start(priority=kk % 2)
            return carry
        lax.fori_loop(0, tt, body, 0, unroll=2)
        start_h(hsrc_ref, 0)

    def wait_slot(s):
        pltpu.make_async_copy(ybuf.at[s], ybuf.at[s], sem.at[s]).wait()
        pltpu.make_async_copy(hbuf.at[s], hbuf.at[s], sem.at[s]).wait()

    def step(slot):
        for r in range(tt):
            for kk, cp in enumerate(copies(posn_ref, hsrcn_ref, 1 - slot, r)):
                cp.start(priority=kk % 2)
        start_h(hsrcn_ref, 1 - slot)

        wait_slot(slot)
        acc = hbuf[slot]
        tw = tw_ref[...]
        for kk in range(TOP_K):
            yk = jnp.concatenate(
                [ybuf[slot, pl.ds(kk * nt + s, tt, stride=TOK_ROWS), :] for s in range(TOK_ROWS)], axis=1)
            acc = acc + tw[:, kk:kk + 1] * yk
        ms = jnp.mean(acc * acc, axis=-1, keepdims=True)
        o_ref[...] = (acc * lax.rsqrt(ms + EPS)) * nw_ref[...]

        @pl.when(i == n - 1)
        def _():
            wait_slot(1 - slot)

    for parity in range(2):
        pl.when(i % 2 == parity)(functools.partial(step, parity))


def _combine(pos_seq, tw_seq, h_src, y_rows, h2, norm_w, tt):
    n_tiles = h_src.shape[0]
    d = h2.shape[1]
    last = n_tiles - 1
    nt = tt * TOK_ROWS
    cur = lambda i: (i, 0, 0)
    nxt = lambda i: (jnp.minimum(i + 1, last), 0, 0)
    sm = lambda a: a.reshape(n_tiles, 1, tt * TOP_K)
    idx_blk = lambda f: pl.BlockSpec((1, 1, TOP_K * tt), f, memory_space=pltpu.SMEM)
    one_blk = lambda f: pl.BlockSpec((1, 1, 1), f, memory_space=pltpu.SMEM)
    return pl.pallas_call(
        _combine_kernel,
        grid=(n_tiles,),
        in_specs=[
            idx_blk(cur), idx_blk(nxt), one_blk(cur), one_blk(nxt),
            pl.BlockSpec((tt, TOP_K), lambda i: (i, 0)),
            pl.BlockSpec(memory_space=pl.ANY),
            pl.BlockSpec(memory_space=pl.ANY),
            pl.BlockSpec(norm_w.shape, lambda i: (0, 0)),
        ],
        out_specs=pl.BlockSpec((tt, d), lambda i: (i, 0)),
        scratch_shapes=[pltpu.VMEM((2, TOP_K * nt, LANES), F32), pltpu.VMEM((2, tt, d), F32),
                        pltpu.SemaphoreType.DMA((2,))],
        out_shape=jax.ShapeDtypeStruct((n_tiles * tt, d), F32),
        compiler_params=_cparams(("arbitrary",)),
        name="combine",
    )(sm(pos_seq), sm(pos_seq), h_src, h_src, tw_seq, y_rows, h2, norm_w)


def _pick(n, prefs):
    for p in prefs:
        if n % p == 0:
            return p
    raise ValueError(f"no tile in {prefs} divides {n}")


def kernel(x, meta_tokens, hg_lb_logits, norm_mix, w_in, hg_norm, ml_conv_w, ml_conv_b, ml_wq, ml_wk, ml_wv,
           ml_gate_b, ml_norm, ml_skip, w_branch_hg, w_branch_ml, w_out, norm_ffn, router_w, router_b,
           exp_w_gu, exp_b_gu, exp_w_down, exp_b_down, norm_final):
    bsz, seq, d = x.shape
    assert norm_mix.shape[0] == 1, "single-layer block"
    assert seq % CHUNK == 0 and d % LANES == 0
    t = CHUNK + seq
    m_rows = bsz * t
    n_experts = router_w.shape[-1]
    dff = exp_w_down.shape[2]
    assert n_experts <= LANES

    head = jnp.concatenate([jnp.zeros((N_PAD, d), x.dtype), meta_tokens.astype(x.dtype)], axis=0)
    x2d = x.reshape(bsz * seq, d)
    lower_bounds = jnp.cumsum(jax.nn.softmax(hg_lb_logits.astype(F32), axis=0), axis=0)

    w = w_in[0]
    n_a = 4 * HG_W + 2 * ML_W
    w_a = w[:, :n_a].astype(BF16)
    w_m = jnp.pad(w[:, n_a:n_a + 2 * N_HEADS], ((0, 0), (0, LANES - 2 * N_HEADS))).astype(BF16)
    w_g = w[:, n_a + 2 * N_HEADS:].astype(BF16)
    tr1 = _pick(t, (704, 192, 64))
    hq, hf, hi, hg, mm, mo, gates, mif = _in_proj(x2d, head, norm_mix[0][None], w_a, w_g, w_m, bsz, tr1)

    bb = _pick(bsz, (4, 2, 1))
    r3 = lambda a: a.reshape(bsz, t, a.shape[-1])
    tri = jnp.asarray(np.tril(np.ones((CHUNK, CHUNK), np.float32)), BF16)
    y_hg = _hgrn2(r3(hq), r3(hf), r3(hi), r3(hg), tri, lower_bounds[0][None], hg_norm[0][None], bb)

    nc = t // CHUNK
    g_col = r3(mif)
    g_row = jnp.swapaxes(g_col[:, :, :2 * N_HEADS].reshape(bsz, nc, CHUNK, 2 * N_HEADS), 2, 3)
    gb = ml_gate_b[0].astype(F32)
    gb_col = jnp.pad(gb, (0, LANES - 2 * N_HEADS))[None]
    gb_row = jnp.broadcast_to(gb[:, None], (2 * N_HEADS, CHUNK))
    y_ml = _mlstm(r3(mm), r3(mo), g_col, g_row, tri, ml_conv_w[0], ml_conv_b[0][None],
                  ml_wq[0].astype(BF16), ml_wk[0].astype(BF16), ml_wv[0].astype(BF16),
                  gb_col, gb_row, ml_norm[0][None], ml_skip[0][None], bb)

    rw = jnp.pad(router_w[0].astype(F32), ((0, 0), (0, LANES - n_experts)))
    rw1 = rw.astype(BF16)
    rw2 = (rw - rw1.astype(F32)).astype(BF16)
    rw_split = jnp.stack([rw1, rw2])
    rb = jnp.pad(router_b[0].astype(F32), (0, LANES - n_experts))[None]
    tq = _pick(t, (352, 192, 64))
    h2, xn2, top_e, top_w, rank, cnt = _merge_route(
        x2d, head, y_hg.reshape(m_rows, HG_W), y_ml.reshape(m_rows, ML_W), gates,
        w_branch_hg[0].astype(BF16), w_branch_ml[0].astype(BF16), w_out[0].astype(BF16),
        norm_ffn[0][None], rw_split, rb, n_experts, bsz, tq)

    tm6 = 512
    n_assign = bsz * (t - N_PAD) * TOP_K
    counts = cnt[0, :n_experts].astype(jnp.int32)
    padded = ((counts + tm6 - 1) // tm6) * tm6
    pend = jnp.cumsum(padded).astype(jnp.int32)
    pstart = pend - padded
    n_blocks = -(-n_assign // tm6) + n_experts
    n_rows = n_blocks * tm6
    blk_start = jnp.arange(n_blocks, dtype=jnp.int32) * tm6
    block_e = jnp.minimum(jnp.sum((blk_start[:, None] >= pend[None, :]).astype(jnp.int32), axis=1),
                          n_experts - 1)
    n_used = (pend[-1] // tm6)[None]
    pos = rank + jnp.sum(jnp.where(top_e[:, :, None] == jnp.arange(n_experts, dtype=jnp.int32),
                                   pstart[None, None, :], 0), axis=-1)
    xs = _dispatch(pstart, pend, pos, xn2, n_rows, bsz, tq, tm6)

    w_gu_t = jnp.swapaxes(exp_w_gu[0].reshape(n_experts, d // LANES, LANES, 2 * dff), 2, 3)
    bgu = exp_b_gu[0]
    y_rows = _experts(block_e, n_used, xs, w_gu_t, bgu[:, None, 0::2], bgu[:, None, 1::2],
                      exp_w_down[0], exp_b_down[0][:, None, :], tm6)

    tt = _pick(seq, (128, 64))
    tiles_per_b = seq // tt
    n_tiles = bsz * tiles_per_b
    seq_part = lambda a: a.reshape(bsz, t, TOP_K)[:, CHUNK:].reshape(bsz * seq, TOP_K)
    h_src = (jnp.arange(bsz, dtype=jnp.int32)[:, None] * t + CHUNK
             + jnp.arange(tiles_per_b, dtype=jnp.int32)[None, :] * tt).reshape(n_tiles, 1, 1)
    out = _combine(seq_part(pos), seq_part(top_w), h_src, y_rows, h2, norm_final[None], tt)
    return out.reshape(bsz, seq, d)
```

```python
import functools

import numpy as np
import jax
import jax.numpy as jnp
from jax import lax
from jax.experimental import pallas as pl
from jax.experimental.pallas import tpu as pltpu

F32 = jnp.float32
BF16 = jnp.bfloat16

N_META = 16
CHUNK = 64
N_PAD = CHUNK - N_META
EPS = 1e-6

N_HEADS = 4
HG_DK = 128
HG_DV = 128
HG_W = N_HEADS * HG_DV
ML_DK = 64
ML_DV = 128
ML_W = N_HEADS * ML_DV
ML_CONV = 4
TOP_K = 4
SWIGLU_LIMIT = 7.0
SWIGLU_ALPHA = 1.702

LANES = 128
VMEM_LIMIT_BYTES = 56 * 1024 * 1024

HG_LEVELS = (32, 16, 8, 4, 2, 1)


def _cparams(sem):
    return pltpu.CompilerParams(dimension_semantics=sem, vmem_limit_bytes=VMEM_LIMIT_BYTES)


def _sigmoid(x):
    return 1.0 / (1.0 + jnp.exp(-x))


def _split3(x):
    x1 = x.astype(BF16)
    r1 = x - x1.astype(F32)
    x2 = r1.astype(BF16)
    x3 = (r1 - x2.astype(F32)).astype(BF16)
    return x1, x2, x3


def _dot(a, b):
    return jnp.dot(a, b, preferred_element_type=F32)


TOK_ROWS = 8


def _store_token_tiles(ref, x):
    n = x.shape[0]
    for s in range(TOK_ROWS):
        ref[pl.ds(s, n, stride=TOK_ROWS), :] = x[:, s * LANES:(s + 1) * LANES]


def _load_token_tiles(ref, n):
    return jnp.concatenate([ref[pl.ds(s, n, stride=TOK_ROWS), :] for s in range(TOK_ROWS)], axis=1)


def _dot_nt(a, b):
    return lax.dot_general(a, b, (((1,), (1,)), ((), ())), preferred_element_type=F32)


def _dot_exact_lhs(m_bf16, x):
    x1, x2, x3 = _split3(x)
    return _dot(m_bf16, x1) + _dot(m_bf16, x2) + _dot(m_bf16, x3)


def _h_tile(x_ref, head_ref, j):
    x = x_ref[...]
    first = jnp.concatenate([head_ref[...], x[:x.shape[0] - CHUNK]], axis=0)
    return jnp.where(j == 0, first, x)


def _x_tile_spec(tr, d, seq):
    assert seq % 8 == 0 and tr % 8 == 0 and CHUNK % 8 == 0
    return pl.BlockSpec(
        (pl.Element(tr), pl.Element(d)),
        lambda b, j: (pl.multiple_of(b * seq + jnp.maximum(j * tr - CHUNK, 0), 8), 0))


def _rms_bf16(x, gain):
    ms = jnp.mean(x * x, axis=-1, keepdims=True)
    return ((x * lax.rsqrt(ms + EPS)) * gain).astype(BF16)


def _project_ahead(x_ref, gain_ref, w_ref, nxt_ref):
    x = x_ref[...].reshape(x_ref.shape[0] * CHUNK, x_ref.shape[2])
    nxt_ref[...] = _dot(_rms_bf16(x, gain_ref[...]), w_ref[...])


def _with_projection_pipeline(step, x_ref, head_ref, gain_ref, w_ref, proj_a, proj_b, c):
    @pl.when(c == 0)
    def _():
        p0 = _dot(_rms_bf16(head_ref[...], gain_ref[...]), w_ref[...])
        for b in range(x_ref.shape[0]):
            proj_a[b * CHUNK:(b + 1) * CHUNK, :] = p0

    for parity, (cur, nxt) in enumerate(((proj_a, proj_b), (proj_b, proj_a))):
        @pl.when(c % 2 == parity)
        def _():
            _project_ahead(x_ref, gain_ref, w_ref, nxt)
            step(cur)


def _chunk_ahead_spec(bb, d, nc):
    return pl.BlockSpec((bb, CHUNK, d), lambda i, c: (i, jnp.minimum(c, nc - 2), 0))


def _hgrn2_kernel(x_ref, head_ref, gain_ref, w_ref, tri_ref, lb_ref, nw_ref, y_ref, proj_a, proj_b, *st_refs):
    c = pl.program_id(1)

    @pl.when(c == 0)
    def _():
        for st_ref in st_refs:
            st_ref[...] = jnp.zeros_like(st_ref)

    step = functools.partial(_hgrn2_step, c, x_ref.shape[0], tri_ref, lb_ref, nw_ref, y_ref, st_refs)
    _with_projection_pipeline(step, x_ref, head_ref, gain_ref, w_ref, proj_a, proj_b, c)


def _hgrn2_step(c, bb, tri_ref, lb_ref, nw_ref, y_ref, st_refs, cur):
    row = lax.broadcasted_iota(jnp.int32, (CHUNK, 1), 0)
    valid = (c * CHUNK + row) >= N_PAD
    ti = lax.broadcasted_iota(jnp.int32, (CHUNK, CHUNK), 0)
    si = lax.broadcasted_iota(jnp.int32, (CHUNK, CHUNK), 1)
    diag_mask = ti == si
    level_masks = {}
    for m in HG_LEVELS:
        same_pair = (ti & ~(2 * m - 1)) == (si & ~(2 * m - 1))
        level_masks[m] = same_pair & ((ti & m) != 0) & ((si & m) == 0)

    lb = lb_ref[...]
    tri = tri_ref[...]
    per_b = []
    for b in range(bb):
        rows_b = slice(b * CHUNK, (b + 1) * CHUNK)
        hf = cur[rows_b, HG_W:2 * HG_W]
        f = lb + (1.0 - lb) * _sigmoid(hf)
        f = jnp.where(valid, f, 1.0)
        logf = jnp.log(f)
        k_all = 1.0 - f
        hq = cur[rows_b, 0:HG_W]
        q_all = hq * _sigmoid(hq)
        b_cum = _dot_exact_lhs(tri, logf)
        e_b = jnp.exp(b_cum)
        e_bl = jnp.exp(b_cum[CHUNK - 1:CHUNK] - b_cum)

        q_fac, k_fac = {}, {}
        for m in HG_LEVELS:
            if m >= 4:
                grp = b_cum.reshape(CHUNK // (2 * m), 2 * m, HG_W)
                e = (grp - grp[:, m - 1:m, :]).reshape(CHUNK, HG_W)
                q_fac[m] = jnp.exp(jnp.minimum(e, 0.0))
                k_fac[m] = jnp.exp(jnp.minimum(-e, 0.0))
        f_prev = pltpu.roll(f, 1, 0)
        f_next = pltpu.roll(f, CHUNK - 1, 0)
        r4 = row & 3
        q_fac[2] = jnp.where(r4 == 2, f, jnp.where(r4 == 3, f * f_prev, 1.0))
        k_fac[2] = jnp.where(r4 == 0, f_next, 1.0)
        q_fac[1] = jnp.where((row & 1) == 1, f, 1.0)

        hg = cur[rows_b, 3 * HG_W:4 * HG_W]
        v_all = cur[rows_b, 2 * HG_W:3 * HG_W]
        per_b.append((q_all, k_all, q_fac, k_fac, e_b, e_bl, v_all, hg * _sigmoid(hg)))

    pairs = [(b, h) for b in range(bb) for h in range(N_HEADS)]
    sls = [slice(h * HG_DK, (h + 1) * HG_DK) for h in range(N_HEADS)]

    scores_all = []
    for b, h in pairs:
        q_all, k_all, q_fac, k_fac = per_b[b][:4]
        q = q_all[:, sls[h]]
        k = k_all[:, sls[h]]
        kb = k.astype(BF16)
        scores = jnp.where(diag_mask, _dot_nt(q.astype(BF16), kb), 0.0)
        for m in HG_LEVELS:
            qd = (q * q_fac[m][:, sls[h]]).astype(BF16)
            kd = (k * k_fac[m][:, sls[h]]).astype(BF16) if m in k_fac else kb
            scores = jnp.where(level_masks[m], _dot_nt(qd, kd), scores)
        scores_all.append(scores.astype(BF16))

    sts = [st_ref[...] for st_ref in st_refs]
    outs = []
    for p, (b, h) in enumerate(pairs):
        q_all, e_b, v_all = per_b[b][0], per_b[b][4], per_b[b][6]
        qe = (q_all[:, sls[h]] * e_b[:, sls[h]]).astype(BF16)
        outs.append(_dot(scores_all[p], v_all[:, sls[h]].astype(BF16)) + _dot_nt(qe, sts[p].astype(BF16)))

    for p, (b, h) in enumerate(pairs):
        k_all, e_b, e_bl, v_all = per_b[b][1], per_b[b][4], per_b[b][5], per_b[b][6]
        kl = (k_all[:, sls[h]] * e_bl[:, sls[h]]).astype(BF16)
        vt = v_all[:, sls[h]].T.astype(BF16)
        st_refs[p][...] = e_b[CHUNK - 1:CHUNK, sls[h]] * sts[p] + _dot(vt, kl)

    for p, (b, h) in enumerate(pairs):
        o = outs[p]
        ms = jnp.mean(o * o, axis=-1, keepdims=True)
        y = (o * lax.rsqrt(ms + EPS)) * nw_ref[:, sls[h]] * per_b[b][7][:, sls[h]]
        y_ref[b, :, sls[h]] = y.astype(y_ref.dtype)


def _hgrn2(x3, head, gain, w_hg, tri, lb, norm_w, bb):
    bsz, seq, d = x3.shape
    nc = seq // CHUNK + 1
    blk = pl.BlockSpec((bb, CHUNK, HG_W), lambda i, c: (i, c, 0))
    full = lambda a: pl.BlockSpec(a.shape, lambda i, c: (0, 0))
    return pl.pallas_call(
        _hgrn2_kernel,
        grid=(bsz // bb, nc),
        in_specs=[_chunk_ahead_spec(bb, d, nc), full(head), full(gain), full(w_hg), full(tri), full(lb),
                  full(norm_w)],
        out_specs=blk,
        out_shape=jax.ShapeDtypeStruct((bsz, nc * CHUNK, HG_W), BF16),
        scratch_shapes=([pltpu.VMEM((bb * CHUNK, w_hg.shape[1]), F32)] * 2
                        + [pltpu.VMEM((HG_DV, HG_DK), F32)] * (bb * N_HEADS)),
        compiler_params=_cparams(("parallel", "arbitrary")),
        name="hgrn2",
    )(x3, head, gain, w_hg, tri, lb, norm_w)


def _log_sigmoid(x):
    return jnp.minimum(x, 0.0) - jnp.log(1.0 + jnp.exp(-jnp.abs(x)))


def _mlstm_kernel(x_ref, head_ref, gain_ref, w_ref, tri_ref, cw_ref, cb_ref, wq_ref, wk_ref, wv_ref,
                  gbc_ref, gbr_ref, nw_ref, sk_ref, y_ref, proj_a, proj_b, *scratch):
    c = pl.program_id(1)

    @pl.when(c == 0)
    def _():
        for ref in scratch:
            ref[...] = jnp.zeros_like(ref)

    step = functools.partial(_mlstm_step, c, x_ref.shape[0], tri_ref, cw_ref, cb_ref, wq_ref, wk_ref, wv_ref,
                             gbc_ref, gbr_ref, nw_ref, sk_ref, y_ref, scratch)
    _with_projection_pipeline(step, x_ref, head_ref, gain_ref, w_ref, proj_a, proj_b, c)


def _mlstm_step(c, bb, tri_ref, cw_ref, cb_ref, wq_ref, wk_ref, wv_ref, gbc_ref, gbr_ref, nw_ref, sk_ref, y_ref,
                scratch, cur):
    n_pairs = bb * N_HEADS
    s_refs = scratch[:n_pairs]
    m_refs = scratch[n_pairs:2 * n_pairs]
    tail_refs = scratch[2 * n_pairs:]

    pos_c = c * CHUNK + lax.broadcasted_iota(jnp.int32, (CHUNK, 1), 0)
    valid_c = pos_c >= N_PAD
    pos_r = c * CHUNK + lax.broadcasted_iota(jnp.int32, (1, CHUNK), 1)
    valid_r = pos_r >= N_PAD
    ti = lax.broadcasted_iota(jnp.int32, (CHUNK, CHUNK), 0)
    si = lax.broadcasted_iota(jnp.int32, (CHUNK, CHUNK), 1)
    causal = si <= ti
    tri = tri_ref[...]
    ones_v = jnp.ones((CHUNK, ML_DV), BF16)
    neg_inf = -jnp.inf

    per_b = []
    for b in range(bb):
        rows_b = slice(b * CHUNK, (b + 1) * CHUNK)
        mm = jnp.where(valid_c, cur[rows_b, 0:ML_W], 0.0)
        ext = jnp.concatenate([tail_refs[b][...], mm], axis=0)
        tail_refs[b][...] = mm[CHUNK - 8:CHUNK]
        conv = cb_ref[...]
        for j in range(ML_CONV):
            off = 8 - (ML_CONV - 1) + j
            conv = conv + cw_ref[j:j + 1, :] * ext[off:off + CHUNK]
        cact = conv * _sigmoid(conv)
        cact_b = cact.astype(BF16)
        mm_b = mm.astype(BF16)

        graw = cur[rows_b, 2 * ML_W:2 * ML_W + LANES]
        gcol = graw + gbc_ref[...]
        li_col = jnp.where(valid_c, gcol, neg_inf)
        lf_col = jnp.where(valid_c, _log_sigmoid(gcol), 0.0)
        b_col = _dot_exact_lhs(tri, lf_col)
        grow = graw.T[0:2 * N_HEADS] + gbr_ref[...]
        li_row = jnp.where(valid_r, grow, neg_inf)
        lf_row = jnp.where(valid_r, _log_sigmoid(grow), 0.0)
        r1, r2, r3 = _split3(lf_row)
        b_row = _dot_nt(r1, tri) + _dot_nt(r2, tri) + _dot_nt(r3, tri)

        ogate = _sigmoid(cur[rows_b, ML_W:2 * ML_W])
        per_b.append((cact, cact_b, mm_b, li_col, b_col, li_row, b_row, ogate))

    pairs = [(b, h) for b in range(bb) for h in range(N_HEADS)]
    sls = [slice(h * ML_DV, (h + 1) * ML_DV) for h in range(N_HEADS)]

    qs, ks, vs = [], [], []
    for b, h in pairs:
        cact_b, mm_b = per_b[b][1], per_b[b][2]
        qs.append((_dot(cact_b[:, sls[h]], wq_ref[h]) * (ML_DK ** -0.5)).astype(BF16))
        ks.append(_dot(cact_b[:, sls[h]], wk_ref[h]))
        v = _dot(mm_b[:, sls[h]], wv_ref[h]).astype(BF16)
        vs.append(jnp.concatenate([v, ones_v], axis=1))

    n_p = len(pairs)
    blk = lambda p: slice(p * CHUNK, (p + 1) * CHUNK)
    stack = lambda xs: jnp.concatenate(xs, axis=0)
    bc_all = stack([per_b[b][4][:, N_HEADS + h:N_HEADS + h + 1] for b, h in pairs])
    lic_all = stack([per_b[b][3][:, h:h + 1] for b, h in pairs])
    row_all = stack([jnp.broadcast_to(per_b[b][5][h:h + 1, :] - per_b[b][6][N_HEADS + h:N_HEADS + h + 1, :],
                                      (CHUNK, CHUNK)) for b, h in pairs])
    mprev_all = stack([jnp.broadcast_to(m_refs[p][0:1, 0:1], (CHUNK, 1)) for p in range(n_p)])
    glast_all = stack([jnp.broadcast_to(per_b[b][4][CHUNK - 1:CHUNK, N_HEADS + h:N_HEADS + h + 1], (CHUNK, 1))
                       for b, h in pairs])
    causal_all = stack([causal] * n_p)

    d_all = jnp.where(causal_all, bc_all + row_all, neg_inf)
    a_all = bc_all + mprev_all
    m_t_all = jnp.maximum(a_all, jnp.max(d_all, axis=-1, keepdims=True))
    w_intra_all = jnp.exp(d_all - m_t_all)
    w_inter_all = jnp.exp(a_all - m_t_all)

    qk_all = (stack([_dot_nt(qs[p], ks[p].astype(BF16)) for p in range(n_p)]) * w_intra_all).astype(BF16)
    s_augs = [s_refs[p][...] for p in range(n_p)]
    intra_all = stack([_dot(qk_all[blk(p)], vs[p]) for p in range(n_p)])
    inter_all = stack([_dot(qs[p], s_augs[p].astype(BF16)) for p in range(n_p)])
    numden_all = intra_all + w_inter_all * inter_all
    o_all = numden_all[:, :ML_DV] / jnp.maximum(jnp.abs(numden_all[:, ML_DV:]), jnp.exp(-m_t_all))
    ms_all = jnp.mean(o_all * o_all, axis=-1, keepdims=True)
    on_all = o_all * lax.rsqrt(ms_all + EPS)

    e_all = glast_all - bc_all + lic_all
    gm_all = glast_all + mprev_all
    e_max = jnp.max(e_all.reshape(n_p, CHUNK, 1), axis=1, keepdims=True)
    m_new_all = jnp.maximum(gm_all.reshape(n_p, CHUNK, 1), e_max).reshape(n_p * CHUNK, 1)
    w_s_all = jnp.exp(e_all - m_new_all)
    w_p_all = jnp.exp(gm_all - m_new_all)
    kw_all = stack(ks) * w_s_all
    for p in range(n_p):
        kw_t = kw_all[blk(p)].T.astype(BF16)
        s_refs[p][...] = w_p_all[p * CHUNK:p * CHUNK + 1] * s_augs[p] + _dot(kw_t, vs[p])
        m_refs[p][...] = jnp.broadcast_to(m_new_all[p * CHUNK:p * CHUNK + 1], m_refs[p].shape)

    for p, (b, h) in enumerate(pairs):
        cact, ogate = per_b[b][0], per_b[b][7]
        sl = sls[h]
        y = (on_all[blk(p)] * nw_ref[:, sl] + sk_ref[:, sl] * cact[:, sl]) * ogate[:, sl]
        y_ref[b, :, sl] = y.astype(y_ref.dtype)


def _mlstm(x3, head, gain, w_ml, tri, conv_w, conv_b, wq, wk, wv, gb_col, gb_row, norm_w, skip, bb):
    bsz, seq, d = x3.shape
    nc = seq // CHUNK + 1
    blk = pl.BlockSpec((bb, CHUNK, ML_W), lambda i, c: (i, c, 0))

    def full(a):
        nd = a.ndim
        return pl.BlockSpec(a.shape, lambda i, c: (0,) * nd)

    params = (head, gain, w_ml, tri, conv_w, conv_b, wq, wk, wv, gb_col, gb_row, norm_w, skip)
    return pl.pallas_call(
        _mlstm_kernel,
        grid=(bsz // bb, nc),
        in_specs=[_chunk_ahead_spec(bb, d, nc)] + [full(p) for p in params],
        out_specs=blk,
        out_shape=jax.ShapeDtypeStruct((bsz, nc * CHUNK, ML_W), BF16),
        scratch_shapes=([pltpu.VMEM((bb * CHUNK, w_ml.shape[1]), F32)] * 2
                        + [pltpu.VMEM((ML_DK, 2 * ML_DV), F32)] * (bb * N_HEADS)
                        + [pltpu.VMEM((8, LANES), F32)] * (bb * N_HEADS)
                        + [pltpu.VMEM((8, ML_W), F32)] * bb),
        compiler_params=_cparams(("parallel", "arbitrary")),
        name="mlstm",
    )(x3, *params)


def _merge_route_kernel(x_ref, head_ref, gain_ref, wg_ref, yh_ref, ym_ref, wbh_ref, wbm_ref, wo_ref, nf_ref,
                        rw_ref, rb_ref, h2_ref, xn_ref, te_ref, tw_ref, rk_ref, cnt_ref, *, n_experts):
    d = x_ref.shape[1]
    tq = x_ref.shape[0]
    j = pl.program_id(1)

    @pl.when(jnp.logical_and(pl.program_id(0) == 0, j == 0))
    def _():
        cnt_ref[...] = jnp.zeros_like(cnt_ref)

    h = _h_tile(x_ref, head_ref, j)
    hn = _rms_bf16(h, gain_ref[...])
    g0 = _sigmoid(_dot(hn, wg_ref[:, :d]))
    g1 = _sigmoid(_dot(hn, wg_ref[:, d:]))
    merged = g0 * _dot(yh_ref[...], wbh_ref[...]) + g1 * _dot(ym_ref[...], wbm_ref[...])
    h2 = h + _dot(merged.astype(BF16), wo_ref[...])
    h2_ref[...] = h2
    ms = jnp.mean(h2 * h2, axis=-1, keepdims=True)
    xn = (h2 * lax.rsqrt(ms + EPS)) * nf_ref[...]
    _store_token_tiles(xn_ref, xn)
    x1, x2, _ = _split3(xn)
    logits = (_dot(x1, rw_ref[0]) + _dot(x1, rw_ref[1]) + _dot(x2, rw_ref[0])) + rb_ref[...]
    lane = lax.broadcasted_iota(jnp.int32, logits.shape, 1)
    work = jnp.where(lane < n_experts, logits, -jnp.inf)
    vals, idxs = [], []
    for _ in range(TOP_K):
        vmax = jnp.max(work, axis=-1, keepdims=True)
        imax = jnp.min(jnp.where(work == vmax, lane, LANES), axis=-1, keepdims=True)
        vals.append(vmax)
        idxs.append(imax)
        work = jnp.where(lane == imax, -jnp.inf, work)
    exps = [jnp.exp(v - vals[0]) for v in vals]
    tot = exps[0] + exps[1] + exps[2] + exps[3]
    te = jnp.zeros(logits.shape, jnp.int32)
    tw = jnp.zeros(logits.shape, F32)
    for kk in range(TOP_K):
        te = jnp.where(lane == kk, idxs[kk], te)
        tw = jnp.where(lane == kk, exps[kk] / tot, tw)
    te_ref[...] = te[:, :TOP_K]
    tw_ref[...] = tw[:, :TOP_K]

    valid = (j * tq + lax.broadcasted_iota(jnp.int32, (tq, 1), 0)) >= N_PAD
    onehots = [jnp.where(jnp.logical_and(lane == idxs[kk], valid), 1.0, 0.0) for kk in range(TOP_K)]
    oh_all = onehots[0] + onehots[1] + onehots[2] + onehots[3]
    ri = lax.broadcasted_iota(jnp.int32, (tq, tq), 0)
    ci = lax.broadcasted_iota(jnp.int32, (tq, tq), 1)
    earlier = jnp.where(ci < ri, 1.0, 0.0).astype(BF16)
    before = _dot(earlier, oh_all.astype(BF16)) + cnt_ref[...]
    rk = jnp.zeros(logits.shape, F32)
    for kk in range(TOP_K):
        rank_k = jnp.sum(jnp.where(lane == idxs[kk], before, 0.0), axis=-1, keepdims=True)
        rk = jnp.where(lane == kk, rank_k, rk)
        before = before + onehots[kk]
    rk_ref[...] = rk[:, :TOP_K].astype(jnp.int32)
    cnt_ref[...] = cnt_ref[...] + jnp.sum(oh_all, axis=0, keepdims=True)


def _merge_route(x2d, head, gain, w_g, y_hg, y_ml, wbh, wbm, wo, norm_ffn, rw_split, rb, n_experts, bsz, tq):
    d = x2d.shape[1]
    seq = x2d.shape[0] // bsz
    nj = (seq + CHUNK) // tq
    m = bsz * nj * tq
    assert d == TOK_ROWS * LANES, "token-tile layout assumes one (8,128) tile per token"
    row = lambda n: pl.BlockSpec((tq, n), lambda b, j: (b * nj + j, 0))
    tiles = pl.BlockSpec((tq * TOK_ROWS, LANES), lambda b, j: (b * nj + j, 0))

    def full(a):
        nd = a.ndim
        return pl.BlockSpec(a.shape, lambda b, j: (0,) * nd)

    return pl.pallas_call(
        functools.partial(_merge_route_kernel, n_experts=n_experts),
        grid=(bsz, nj),
        in_specs=[_x_tile_spec(tq, d, seq), full(head), full(gain), full(w_g), row(HG_W), row(ML_W),
                  full(wbh), full(wbm), full(wo), full(norm_ffn), full(rw_split), full(rb)],
        out_specs=[row(d), tiles, row(TOP_K), row(TOP_K), row(TOP_K),
                   pl.BlockSpec((1, LANES), lambda b, j: (0, 0))],
        out_shape=[
            jax.ShapeDtypeStruct((m, d), F32),
            jax.ShapeDtypeStruct((m * TOK_ROWS, LANES), F32),
            jax.ShapeDtypeStruct((m, TOP_K), jnp.int32),
            jax.ShapeDtypeStruct((m, TOP_K), F32),
            jax.ShapeDtypeStruct((m, TOP_K), jnp.int32),
            jax.ShapeDtypeStruct((1, LANES), F32),
        ],
        compiler_params=_cparams(("arbitrary", "arbitrary")),
        name="merge_route",
    )(x2d, head, gain, w_g, y_hg, y_ml, wbh, wbm, wo, norm_ffn, rw_split, rb)


def _toks(first, n=1):
    return pl.ds(pl.multiple_of(first * TOK_ROWS, TOK_ROWS), n * TOK_ROWS)


def _dispatch_kernel(ps_ref, pe_ref, pos_ref, x_ref, xs_hbm, zbuf, sem, *, n_experts, tm):
    b = pl.program_id(0)
    j = pl.program_id(1)
    tq = x_ref.shape[0] // TOK_ROWS

    @pl.when(jnp.logical_and(b == 0, j == 0))
    def _():
        zbuf[...] = jnp.zeros_like(zbuf)
        for e in range(n_experts):
            @pl.when(pe_ref[e] > ps_ref[e])
            def _():
                pltpu.make_async_copy(zbuf, xs_hbm.at[_toks(pe_ref[e] - tm, tm), :], sem).start()
        for e in range(n_experts):
            @pl.when(pe_ref[e] > ps_ref[e])
            def _():
                pltpu.make_async_copy(zbuf, xs_hbm.at[_toks(0, tm), :], sem).wait()

        def zero_tail(blk, carry):
            cp = pltpu.make_async_copy(zbuf, xs_hbm.at[_toks(blk * tm, tm), :], sem)
            cp.start()
            cp.wait()
            return carry
        lax.fori_loop(pe_ref[n_experts - 1] // tm, xs_hbm.shape[0] // (tm * TOK_ROWS), zero_tail, 0)

    def scatter_rows(lo):
        def body(r, carry):
            for kk in range(TOP_K):
                dst = pos_ref[0, 0, r * TOP_K + kk]
                pltpu.make_async_copy(
                    x_ref.at[_toks(r), :], xs_hbm.at[_toks(dst), :], sem).start(priority=kk % 2)
            return carry
        lax.fori_loop(lo, tq, body, 0, unroll=2)
        n = tq - lo
        for _ in range(TOP_K):
            pltpu.make_async_copy(x_ref.at[_toks(0, n), :], xs_hbm.at[_toks(0, n), :], sem).wait()

    @pl.when(j == 0)
    def _():
        scatter_rows(N_PAD)

    @pl.when(j != 0)
    def _():
        scatter_rows(0)


def _dispatch(pstart, pend, pos, xn_t, n_rows, bsz, tq, tm):
    m = xn_t.shape[0] // TOK_ROWS
    nj = m // (bsz * tq)
    n_experts = pstart.shape[0]
    sm = lambda a: a.reshape(bsz * nj, 1, tq * TOP_K)
    smem_blk = pl.BlockSpec((1, 1, tq * TOP_K), lambda b, j, ps, pe: (b * nj + j, 0, 0),
                            memory_space=pltpu.SMEM)
    grid_spec = pltpu.PrefetchScalarGridSpec(
        num_scalar_prefetch=2,
        grid=(bsz, nj),
        in_specs=[smem_blk,
                  pl.BlockSpec((tq * TOK_ROWS, LANES), lambda b, j, ps, pe: (b * nj + j, 0))],
        out_specs=pl.BlockSpec(memory_space=pl.ANY),
        scratch_shapes=[pltpu.VMEM((tm * TOK_ROWS, LANES), F32), pltpu.SemaphoreType.DMA(())],
    )
    return pl.pallas_call(
        functools.partial(_dispatch_kernel, n_experts=n_experts, tm=tm),
        grid_spec=grid_spec,
        out_shape=jax.ShapeDtypeStruct((n_rows * TOK_ROWS, LANES), F32),
        compiler_params=_cparams(("arbitrary", "arbitrary")),
        name="dispatch",
    )(pstart, pend, sm(pos), xn_t)


CAST_ROWS = 256


def _experts_kernel(be_ref, nu_ref, x_ref, wgu_ref, bg_ref, bu_ref, wd_ref, bd_ref, y_ref,
                    wg_s, wu_s, wd_s):
    i = pl.program_id(0)
    n_used = nu_ref[0]
    dff = wg_s.shape[0]
    new_expert = jnp.logical_or(i == 0, be_ref[i] != be_ref[jnp.maximum(i - 1, 0)])

    @pl.when(jnp.logical_and(new_expert, i < n_used))
    def _():
        for c in range(dff // CAST_ROWS):
            rows = pl.ds(c * CAST_ROWS, CAST_ROWS)
            wd_s[rows, :] = wd_ref[0, rows, :].astype(BF16)
            for s in range(wgu_ref.shape[1]):
                lanes = pl.ds(s * LANES, LANES)
                wg_s[rows, lanes] = wgu_ref[
                    0, s, pl.ds(2 * c * CAST_ROWS, CAST_ROWS, stride=2), :].astype(BF16)
                wu_s[rows, lanes] = wgu_ref[
                    0, s, pl.ds(2 * c * CAST_ROWS + 1, CAST_ROWS, stride=2), :].astype(BF16)

    @pl.when(i < n_used)
    def _():
        tm = x_ref.shape[0] // TOK_ROWS
        xb = _load_token_tiles(x_ref, tm).astype(BF16)
        g = _dot_nt(xb, wg_s[...]) + bg_ref[0]
        u = _dot_nt(xb, wu_s[...]) + bu_ref[0]
        gate = jnp.minimum(g, SWIGLU_LIMIT)
        up = jnp.clip(u, -SWIGLU_LIMIT, SWIGLU_LIMIT)
        act = (up + 1.0) * gate * _sigmoid(SWIGLU_ALPHA * gate)
        _store_token_tiles(y_ref, _dot(act.astype(BF16), wd_s[...]) + bd_ref[0])

    @pl.when(i >= n_used)
    def _():
        y_ref[...] = jnp.zeros_like(y_ref)


def _experts(block_e, n_used, xs, w_gu_t, b_g, b_u, w_d, b_d, tm):
    n_blocks = block_e.shape[0]
    dff, d = w_d.shape[1:]
    tile_blk = lambda f: pl.BlockSpec((tm * TOK_ROWS, LANES), f)
    assert dff % CAST_ROWS == 0
    wspec = lambda k, n: pl.BlockSpec((1, k, n), lambda i, be, nu: (be[i], 0, 0))
    grid_spec = pltpu.PrefetchScalarGridSpec(
        num_scalar_prefetch=2,
        grid=(n_blocks,),
        in_specs=[
            tile_blk(lambda i, be, nu: (jnp.minimum(i, nu[0] - 1), 0)),
            pl.BlockSpec((1, d // LANES, 2 * dff, LANES), lambda i, be, nu: (be[i], 0, 0, 0)),
            wspec(1, dff), wspec(1, dff), wspec(dff, d), wspec(1, d),
        ],
        out_specs=tile_blk(lambda i, be, nu: (i, 0)),
        scratch_shapes=[pltpu.VMEM((dff, d), BF16), pltpu.VMEM((dff, d), BF16), pltpu.VMEM((dff, d), BF16)],
    )
    return pl.pallas_call(
        _experts_kernel,
        grid_spec=grid_spec,
        out_shape=jax.ShapeDtypeStruct((n_blocks * tm * TOK_ROWS, LANES), F32),
        compiler_params=_cparams(("arbitrary",)),
        name="experts",
    )(block_e, n_used, xs, w_gu_t, b_g, b_u, w_d, b_d)


def _combine_kernel(pos_ref, posn_ref, hsrc_ref, hsrcn_ref, tw_ref, y_hbm, h_hbm,
                    nw_ref, o_ref, ybuf, hbuf, sem):
    i = pl.program_id(0)
    n = pl.num_programs(0)
    tt = o_ref.shape[0]
    nt = tt * TOK_ROWS

    def copies(src_pos_ref, src_h_ref, dst_slot, r):
        return [pltpu.make_async_copy(
            y_hbm.at[_toks(src_pos_ref[0, 0, r * TOP_K + kk]), :],
            ybuf.at[dst_slot, _toks(kk * tt + r), :],
            sem.at[dst_slot]) for kk in range(TOP_K)]

    def start_h(src_h_ref, dst_slot):
        h_row = pl.multiple_of(src_h_ref[0, 0, 0], 8)
        pltpu.make_async_copy(h_hbm.at[pl.ds(h_row, tt), :], hbuf.at[dst_slot], sem.at[dst_slot]).start()

    @pl.when(i == 0)
    def _():
        def body(r, carry):
            for kk, cp in enumerate(copies(pos_ref, hsrc_ref, 0, r)):
                cp.start(priority=kk % 2)
            return carry
        lax.fori_loop(0, tt, body, 0, unroll=2)
        start_h(hsrc_ref, 0)

    def wait_slot(s):
        pltpu.make_async_copy(ybuf.at[s], ybuf.at[s], sem.at[s]).wait()
        pltpu.make_async_copy(hbuf.at[s], hbuf.at[s], sem.at[s]).wait()

    def step(slot):
        for r in range(tt):
            for kk, cp in enumerate(copies(posn_ref, hsrcn_ref, 1 - slot, r)):
                cp.start(priority=kk % 2)
        start_h(hsrcn_ref, 1 - slot)

        wait_slot(slot)
        acc = hbuf[slot]
        tw = tw_ref[...]
        for kk in range(TOP_K):
            yk = jnp.concatenate(
                [ybuf[slot, pl.ds(kk * nt + s, tt, stride=TOK_ROWS), :] for s in range(TOK_ROWS)], axis=1)
            acc = acc + tw[:, kk:kk + 1] * yk
        ms = jnp.mean(acc * acc, axis=-1, keepdims=True)
        o_ref[...] = (acc * lax.rsqrt(ms + EPS)) * nw_ref[...]

        @pl.when(i == n - 1)
        def _():
            wait_slot(1 - slot)

    for parity in range(2):
        pl.when(i % 2 == parity)(functools.partial(step, parity))


def _combine(pos_seq, tw_seq, h_src, y_rows, h2, norm_w, tt):
    n_tiles = h_src.shape[0]
    d = h2.shape[1]
    last = n_tiles - 1
    nt = tt * TOK_ROWS
    cur = lambda i: (i, 0, 0)
    nxt = lambda i: (jnp.minimum(i + 1, last), 0, 0)
    sm = lambda a: a.reshape(n_tiles, 1, tt * TOP_K)
    idx_blk = lambda f: pl.BlockSpec((1, 1, TOP_K * tt), f, memory_space=pltpu.SMEM)
    one_blk = lambda f: pl.BlockSpec((1, 1, 1), f, memory_space=pltpu.SMEM)
    return pl.pallas_call(
        _combine_kernel,
        grid=(n_tiles,),
        in_specs=[
            idx_blk(cur), idx_blk(nxt), one_blk(cur), one_blk(nxt),
            pl.BlockSpec((tt, TOP_K), lambda i: (i, 0)),
            pl.BlockSpec(memory_space=pl.ANY),
            pl.BlockSpec(memory_space=pl.ANY),
            pl.BlockSpec(norm_w.shape, lambda i: (0, 0)),
        ],
        out_specs=pl.BlockSpec((tt, d), lambda i: (i, 0)),
        scratch_shapes=[pltpu.VMEM((2, TOP_K * nt, LANES), F32), pltpu.VMEM((2, tt, d), F32),
                        pltpu.SemaphoreType.DMA((2,))],
        out_shape=jax.ShapeDtypeStruct((n_tiles * tt, d), F32),
        compiler_params=_cparams(("arbitrary",)),
        name="combine",
    )(sm(pos_seq), sm(pos_seq), h_src, h_src, tw_seq, y_rows, h2, norm_w)


def _pick(n, prefs):
    for p in prefs:
        if n % p == 0:
            return p
    raise ValueError(f"no tile in {prefs} divides {n}")


def kernel(x, meta_tokens, hg_lb_logits, norm_mix, w_in, hg_norm, ml_conv_w, ml_conv_b, ml_wq, ml_wk, ml_wv,
           ml_gate_b, ml_norm, ml_skip, w_branch_hg, w_branch_ml, w_out, norm_ffn, router_w, router_b,
           exp_w_gu, exp_b_gu, exp_w_down, exp_b_down, norm_final):
    bsz, seq, d = x.shape
    assert norm_mix.shape[0] == 1, "single-layer block"
    assert seq % CHUNK == 0 and d % LANES == 0
    t = CHUNK + seq
    m_rows = bsz * t
    n_experts = router_w.shape[-1]
    dff = exp_w_down.shape[2]
    assert n_experts <= LANES

    head = jnp.concatenate([jnp.zeros((N_PAD, d), x.dtype), meta_tokens.astype(x.dtype)], axis=0)
    x2d = x.reshape(bsz * seq, d)
    lower_bounds = jnp.cumsum(jax.nn.softmax(hg_lb_logits.astype(F32), axis=0), axis=0)

    w = w_in[0]
    n_hg = 4 * HG_W
    n_a = n_hg + 2 * ML_W
    w_hg = w[:, :n_hg].astype(BF16)
    w_m = jnp.pad(w[:, n_a:n_a + 2 * N_HEADS], ((0, 0), (0, LANES - 2 * N_HEADS)))
    w_ml = jnp.concatenate([w[:, n_hg:n_a], w_m], axis=1).astype(BF16)
    w_g = w[:, n_a + 2 * N_HEADS:].astype(BF16)
    gain = norm_mix[0][None]

    bb = _pick(bsz, (4, 2, 1))
    tri = jnp.asarray(np.tril(np.ones((CHUNK, CHUNK), np.float32)), BF16)
    y_hg = _hgrn2(x, head, gain, w_hg, tri, lower_bounds[0][None], hg_norm[0][None], bb)

    gb = ml_gate_b[0].astype(F32)
    gb_col = jnp.pad(gb, (0, LANES - 2 * N_HEADS))[None]
    gb_row = jnp.broadcast_to(gb[:, None], (2 * N_HEADS, CHUNK))
    y_ml = _mlstm(x, head, gain, w_ml, tri, ml_conv_w[0], ml_conv_b[0][None],
                  ml_wq[0].astype(BF16), ml_wk[0].astype(BF16), ml_wv[0].astype(BF16),
                  gb_col, gb_row, ml_norm[0][None], ml_skip[0][None], bb)

    rw = jnp.pad(router_w[0].astype(F32), ((0, 0), (0, LANES - n_experts)))
    rw1 = rw.astype(BF16)
    rw2 = (rw - rw1.astype(F32)).astype(BF16)
    rw_split = jnp.stack([rw1, rw2])
    rb = jnp.pad(router_b[0].astype(F32), (0, LANES - n_experts))[None]
    tq = _pick(t, (352, 192, 64))
    h2, xn2, top_e, top_w, rank, cnt = _merge_route(
        x2d, head, gain, w_g, y_hg.reshape(m_rows, HG_W), y_ml.reshape(m_rows, ML_W),
        w_branch_hg[0].astype(BF16), w_branch_ml[0].astype(BF16), w_out[0].astype(BF16),
        norm_ffn[0][None], rw_split, rb, n_experts, bsz, tq)

    tm6 = 512
    n_assign = bsz * (t - N_PAD) * TOP_K
    counts = cnt[0, :n_experts].astype(jnp.int32)
    padded = ((counts + tm6 - 1) // tm6) * tm6
    pend = jnp.cumsum(padded).astype(jnp.int32)
    pstart = pend - padded
    n_blocks = -(-n_assign // tm6) + n_experts
    n_rows = n_blocks * tm6
    blk_start = jnp.arange(n_blocks, dtype=jnp.int32) * tm6
    block_e = jnp.minimum(jnp.sum((blk_start[:, None] >= pend[None, :]).astype(jnp.int32), axis=1),
                          n_experts - 1)
    n_used = (pend[-1] // tm6)[None]
    pos = rank + jnp.sum(jnp.where(top_e[:, :, None] == jnp.arange(n_experts, dtype=jnp.int32),
                                   pstart[None, None, :], 0), axis=-1)
    xs = _dispatch(pstart, pend, pos, xn2, n_rows, bsz, tq, tm6)

    w_gu_t = jnp.swapaxes(exp_w_gu[0].reshape(n_experts, d // LANES, LANES, 2 * dff), 2, 3)
    bgu = exp_b_gu[0]
    y_rows = _experts(block_e, n_used, xs, w_gu_t, bgu[:, None, 0::2], bgu[:, None, 1::2],
                      exp_w_down[0], exp_b_down[0][:, None, :], tm6)

    tt = _pick(seq, (128, 64))
    tiles_per_b = seq // tt
    n_tiles = bsz * tiles_per_b
    seq_part = lambda a: a.reshape(bsz, t, TOP_K)[:, CHUNK:].reshape(bsz * seq, TOP_K)
    h_src = (jnp.arange(bsz, dtype=jnp.int32)[:, None] * t + CHUNK
             + jnp.arange(tiles_per_b, dtype=jnp.int32)[None, :] * tt).reshape(n_tiles, 1, 1)
    out = _combine(seq_part(pos), seq_part(top_w), h_src, y_rows, h2, norm_final[None], tt)
    return out.reshape(bsz, seq, d)
```

```python
import functools

import numpy as np
import jax
import jax.numpy as jnp
from jax import lax
from jax.experimental import pallas as pl
from jax.experimental.pallas import tpu as pltpu

F32 = jnp.float32
BF16 = jnp.bfloat16

N_META = 16
CHUNK = 64
N_PAD = CHUNK - N_META
EPS = 1e-6

N_HEADS = 4
HG_DK = 128
HG_DV = 128
HG_W = N_HEADS * HG_DV
ML_DK = 64
ML_DV = 128
ML_W = N_HEADS * ML_DV
ML_CONV = 4
TOP_K = 4
SWIGLU_LIMIT = 7.0
SWIGLU_ALPHA = 1.702

LANES = 128
VMEM_LIMIT_BYTES = 56 * 1024 * 1024

HG_LEVELS = (32, 16, 8, 4, 2, 1)


def _cparams(sem):
    return pltpu.CompilerParams(dimension_semantics=sem, vmem_limit_bytes=VMEM_LIMIT_BYTES)


def _sigmoid(x):
    return 1.0 / (1.0 + jnp.exp(-x))


def _split3(x):
    x1 = x.astype(BF16)
    r1 = x - x1.astype(F32)
    x2 = r1.astype(BF16)
    x3 = (r1 - x2.astype(F32)).astype(BF16)
    return x1, x2, x3


def _dot(a, b):
    return jnp.dot(a, b, preferred_element_type=F32)


TOK_ROWS = 8


def _store_token_tiles(ref, x):
    n = x.shape[0]
    for s in range(TOK_ROWS):
        ref[pl.ds(s, n, stride=TOK_ROWS), :] = x[:, s * LANES:(s + 1) * LANES]


def _load_token_tiles(ref, n):
    return jnp.concatenate([ref[pl.ds(s, n, stride=TOK_ROWS), :] for s in range(TOK_ROWS)], axis=1)


def _dot_nt(a, b):
    return lax.dot_general(a, b, (((1,), (1,)), ((), ())), preferred_element_type=F32)


def _dot_exact_lhs(m_bf16, x):
    x1, x2, x3 = _split3(x)
    return _dot(m_bf16, x1) + _dot(m_bf16, x2) + _dot(m_bf16, x3)


def _h_tile(x_ref, head_ref, j):
    x = x_ref[...]
    first = jnp.concatenate([head_ref[...], x[:x.shape[0] - CHUNK]], axis=0)
    return jnp.where(j == 0, first, x)


def _x_tile_spec(tr, d, seq):
    assert seq % 8 == 0 and tr % 8 == 0 and CHUNK % 8 == 0
    return pl.BlockSpec(
        (pl.Element(tr), pl.Element(d)),
        lambda b, j: (pl.multiple_of(b * seq + jnp.maximum(j * tr - CHUNK, 0), 8), 0))


def _rms_bf16(x, gain):
    ms = jnp.mean(x * x, axis=-1, keepdims=True)
    return ((x * lax.rsqrt(ms + EPS)) * gain).astype(BF16)


def _project_ahead(x_ref, gain_ref, w_ref, nxt_ref):
    x = x_ref[...].reshape(x_ref.shape[0] * CHUNK, x_ref.shape[2])
    nxt_ref[...] = _dot(_rms_bf16(x, gain_ref[...]), w_ref[...])


def _with_projection_pipeline(step, x_ref, head_ref, gain_ref, w_ref, proj_a, proj_b, c):
    @pl.when(c == 0)
    def _():
        p0 = _dot(_rms_bf16(head_ref[...], gain_ref[...]), w_ref[...])
        for b in range(x_ref.shape[0]):
            proj_a[b * CHUNK:(b + 1) * CHUNK, :] = p0

    for parity, (cur, nxt) in enumerate(((proj_a, proj_b), (proj_b, proj_a))):
        @pl.when(c % 2 == parity)
        def _():
            _project_ahead(x_ref, gain_ref, w_ref, nxt)
            step(cur)


def _chunk_ahead_spec(bb, d, nc):
    return pl.BlockSpec((bb, CHUNK, d), lambda i, c: (i, jnp.minimum(c, nc - 2), 0))


def _hgrn2_kernel(x_ref, head_ref, gain_ref, w_ref, tri_ref, lb_ref, nw_ref, y_ref, proj_a, proj_b, *st_refs):
    c = pl.program_id(1)

    @pl.when(c == 0)
    def _():
        for st_ref in st_refs:
            st_ref[...] = jnp.zeros_like(st_ref)

    step = functools.partial(_hgrn2_step, c, x_ref.shape[0], tri_ref, lb_ref, nw_ref, y_ref, st_refs)
    _with_projection_pipeline(step, x_ref, head_ref, gain_ref, w_ref, proj_a, proj_b, c)


def _hgrn2_step(c, bb, tri_ref, lb_ref, nw_ref, y_ref, st_refs, cur):
    row = lax.broadcasted_iota(jnp.int32, (CHUNK, 1), 0)
    valid = (c * CHUNK + row) >= N_PAD
    ti = lax.broadcasted_iota(jnp.int32, (CHUNK, CHUNK), 0)
    si = lax.broadcasted_iota(jnp.int32, (CHUNK, CHUNK), 1)
    diag_mask = ti == si
    level_masks = {}
    for m in HG_LEVELS:
        same_pair = (ti & ~(2 * m - 1)) == (si & ~(2 * m - 1))
        level_masks[m] = same_pair & ((ti & m) != 0) & ((si & m) == 0)

    lb = lb_ref[...]
    tri = tri_ref[...]
    per_b = []
    for b in range(bb):
        rows_b = slice(b * CHUNK, (b + 1) * CHUNK)
        hf = cur[rows_b, HG_W:2 * HG_W]
        f = lb + (1.0 - lb) * _sigmoid(hf)
        f = jnp.where(valid, f, 1.0)
        logf = jnp.log(f)
        k_all = 1.0 - f
        hq = cur[rows_b, 0:HG_W]
        q_all = hq * _sigmoid(hq)
        b_cum = _dot_exact_lhs(tri, logf)
        e_b = jnp.exp(b_cum)
        e_bl = jnp.exp(b_cum[CHUNK - 1:CHUNK] - b_cum)

        q_fac, k_fac = {}, {}
        for m in HG_LEVELS:
            if m >= 4:
                grp = b_cum.reshape(CHUNK // (2 * m), 2 * m, HG_W)
                e = (grp - grp[:, m - 1:m, :]).reshape(CHUNK, HG_W)
                q_fac[m] = jnp.exp(jnp.minimum(e, 0.0))
                k_fac[m] = jnp.exp(jnp.minimum(-e, 0.0))
        f_prev = pltpu.roll(f, 1, 0)
        f_next = pltpu.roll(f, CHUNK - 1, 0)
        r4 = row & 3
        q_fac[2] = jnp.where(r4 == 2, f, jnp.where(r4 == 3, f * f_prev, 1.0))
        k_fac[2] = jnp.where(r4 == 0, f_next, 1.0)
        q_fac[1] = jnp.where((row & 1) == 1, f, 1.0)

        hg = cur[rows_b, 3 * HG_W:4 * HG_W]
        v_all = cur[rows_b, 2 * HG_W:3 * HG_W]
        per_b.append((q_all, k_all, q_fac, k_fac, e_b, e_bl, v_all, hg * _sigmoid(hg)))

    pairs = [(b, h) for b in range(bb) for h in range(N_HEADS)]
    sls = [slice(h * HG_DK, (h + 1) * HG_DK) for h in range(N_HEADS)]

    scores_all = []
    for b, h in pairs:
        q_all, k_all, q_fac, k_fac = per_b[b][:4]
        q = q_all[:, sls[h]]
        k = k_all[:, sls[h]]
        kb = k.astype(BF16)
        scores = jnp.where(diag_mask, _dot_nt(q.astype(BF16), kb), 0.0)
        for m in HG_LEVELS:
            qd = (q * q_fac[m][:, sls[h]]).astype(BF16)
            kd = (k * k_fac[m][:, sls[h]]).astype(BF16) if m in k_fac else kb
            scores = jnp.where(level_masks[m], _dot_nt(qd, kd), scores)
        scores_all.append(scores.astype(BF16))

    sts = [st_ref[...] for st_ref in st_refs]
    outs = []
    for p, (b, h) in enumerate(pairs):
        q_all, e_b, v_all = per_b[b][0], per_b[b][4], per_b[b][6]
        qe = (q_all[:, sls[h]] * e_b[:, sls[h]]).astype(BF16)
        outs.append(_dot(scores_all[p], v_all[:, sls[h]].astype(BF16)) + _dot_nt(qe, sts[p].astype(BF16)))

    for p, (b, h) in enumerate(pairs):
        k_all, e_b, e_bl, v_all = per_b[b][1], per_b[b][4], per_b[b][5], per_b[b][6]
        kl = (k_all[:, sls[h]] * e_bl[:, sls[h]]).astype(BF16)
        vt = v_all[:, sls[h]].T.astype(BF16)
        st_refs[p][...] = e_b[CHUNK - 1:CHUNK, sls[h]] * sts[p] + _dot(vt, kl)

    for p, (b, h) in enumerate(pairs):
        o = outs[p]
        ms = jnp.mean(o * o, axis=-1, keepdims=True)
        y = (o * lax.rsqrt(ms + EPS)) * nw_ref[:, sls[h]] * per_b[b][7][:, sls[h]]
        y_ref[b, :, sls[h]] = y.astype(y_ref.dtype)


def _hgrn2(x3, head, gain, w_hg, tri, lb, norm_w, bb):
    bsz, seq, d = x3.shape
    nc = seq // CHUNK + 1
    blk = pl.BlockSpec((bb, CHUNK, HG_W), lambda i, c: (i, c, 0))
    full = lambda a: pl.BlockSpec(a.shape, lambda i, c: (0, 0))
    return pl.pallas_call(
        _hgrn2_kernel,
        grid=(bsz // bb, nc),
        in_specs=[_chunk_ahead_spec(bb, d, nc), full(head), full(gain), full(w_hg), full(tri), full(lb),
                  full(norm_w)],
        out_specs=blk,
        out_shape=jax.ShapeDtypeStruct((bsz, nc * CHUNK, HG_W), BF16),
        scratch_shapes=([pltpu.VMEM((bb * CHUNK, w_hg.shape[1]), F32)] * 2
                        + [pltpu.VMEM((HG_DV, HG_DK), F32)] * (bb * N_HEADS)),
        compiler_params=_cparams(("parallel", "arbitrary")),
        name="hgrn2",
    )(x3, head, gain, w_hg, tri, lb, norm_w)


def _log_sigmoid(x):
    return jnp.minimum(x, 0.0) - jnp.log(1.0 + jnp.exp(-jnp.abs(x)))


def _mlstm_kernel(x_ref, head_ref, gain_ref, w_ref, tri_ref, cw_ref, cb_ref, wq_ref, wk_ref, wv_ref,
                  gbc_ref, gbr_ref, nw_ref, sk_ref, y_ref, proj_a, proj_b, *scratch):
    c = pl.program_id(1)

    @pl.when(c == 0)
    def _():
        for ref in scratch:
            ref[...] = jnp.zeros_like(ref)

    step = functools.partial(_mlstm_step, c, x_ref.shape[0], tri_ref, cw_ref, cb_ref, wq_ref, wk_ref, wv_ref,
                             gbc_ref, gbr_ref, nw_ref, sk_ref, y_ref, scratch)
    _with_projection_pipeline(step, x_ref, head_ref, gain_ref, w_ref, proj_a, proj_b, c)


def _mlstm_step(c, bb, tri_ref, cw_ref, cb_ref, wq_ref, wk_ref, wv_ref, gbc_ref, gbr_ref, nw_ref, sk_ref, y_ref,
                scratch, cur):
    n_pairs = bb * N_HEADS
    s_refs = scratch[:n_pairs]
    m_refs = scratch[n_pairs:2 * n_pairs]
    tail_refs = scratch[2 * n_pairs:]

    pos_c = c * CHUNK + lax.broadcasted_iota(jnp.int32, (CHUNK, 1), 0)
    valid_c = pos_c >= N_PAD
    pos_r = c * CHUNK + lax.broadcasted_iota(jnp.int32, (1, CHUNK), 1)
    valid_r = pos_r >= N_PAD
    ti = lax.broadcasted_iota(jnp.int32, (CHUNK, CHUNK), 0)
    si = lax.broadcasted_iota(jnp.int32, (CHUNK, CHUNK), 1)
    causal = si <= ti
    tri = tri_ref[...]
    ones_v = jnp.ones((CHUNK, ML_DV), BF16)
    neg_inf = -jnp.inf

    per_b = []
    for b in range(bb):
        rows_b = slice(b * CHUNK, (b + 1) * CHUNK)
        mm = jnp.where(valid_c, cur[rows_b, 0:ML_W], 0.0)
        ext = jnp.concatenate([tail_refs[b][...], mm], axis=0)
        tail_refs[b][...] = mm[CHUNK - 8:CHUNK]
        conv = cb_ref[...]
        for j in range(ML_CONV):
            off = 8 - (ML_CONV - 1) + j
            conv = conv + cw_ref[j:j + 1, :] * ext[off:off + CHUNK]
        cact = conv * _sigmoid(conv)
        cact_b = cact.astype(BF16)
        mm_b = mm.astype(BF16)

        graw = cur[rows_b, 2 * ML_W:2 * ML_W + LANES]
        gcol = graw + gbc_ref[...]
        li_col = jnp.where(valid_c, gcol, neg_inf)
        lf_col = jnp.where(valid_c, _log_sigmoid(gcol), 0.0)
        b_col = _dot_exact_lhs(tri, lf_col)
        grow = graw.T[0:2 * N_HEADS] + gbr_ref[...]
        li_row = jnp.where(valid_r, grow, neg_inf)
        lf_row = jnp.where(valid_r, _log_sigmoid(grow), 0.0)
        r1, r2, r3 = _split3(lf_row)
        b_row = _dot_nt(r1, tri) + _dot_nt(r2, tri) + _dot_nt(r3, tri)

        ogate = _sigmoid(cur[rows_b, ML_W:2 * ML_W])
        per_b.append((cact, cact_b, mm_b, li_col, b_col, li_row, b_row, ogate))

    pairs = [(b, h) for b in range(bb) for h in range(N_HEADS)]
    sls = [slice(h * ML_DV, (h + 1) * ML_DV) for h in range(N_HEADS)]

    qs, ks, vs = [], [], []
    for b, h in pairs:
        cact_b, mm_b = per_b[b][1], per_b[b][2]
        qs.append((_dot(cact_b[:, sls[h]], wq_ref[h]) * (ML_DK ** -0.5)).astype(BF16))
        ks.append(_dot(cact_b[:, sls[h]], wk_ref[h]))
        v = _dot(mm_b[:, sls[h]], wv_ref[h]).astype(BF16)
        vs.append(jnp.concatenate([v, ones_v], axis=1))

    n_p = len(pairs)
    blk = lambda p: slice(p * CHUNK, (p + 1) * CHUNK)
    stack = lambda xs: jnp.concatenate(xs, axis=0)
    bc_all = stack([per_b[b][4][:, N_HEADS + h:N_HEADS + h + 1] for b, h in pairs])
    lic_all = stack([per_b[b][3][:, h:h + 1] for b, h in pairs])
    row_all = stack([jnp.broadcast_to(per_b[b][5][h:h + 1, :] - per_b[b][6][N_HEADS + h:N_HEADS + h + 1, :],
                                      (CHUNK, CHUNK)) for b, h in pairs])
    mprev_all = stack([jnp.broadcast_to(m_refs[p][0:1, 0:1], (CHUNK, 1)) for p in range(n_p)])
    glast_all = stack([jnp.broadcast_to(per_b[b][4][CHUNK - 1:CHUNK, N_HEADS + h:N_HEADS + h + 1], (CHUNK, 1))
                       for b, h in pairs])
    causal_all = stack([causal] * n_p)

    d_all = jnp.where(causal_all, bc_all + row_all, neg_inf)
    a_all = bc_all + mprev_all
    m_t_all = jnp.maximum(a_all, jnp.max(d_all, axis=-1, keepdims=True))
    w_intra_all = jnp.exp(d_all - m_t_all)
    w_inter_all = jnp.exp(a_all - m_t_all)

    qk_all = (stack([_dot_nt(qs[p], ks[p].astype(BF16)) for p in range(n_p)]) * w_intra_all).astype(BF16)
    s_augs = [s_refs[p][...] for p in range(n_p)]
    intra_all = stack([_dot(qk_all[blk(p)], vs[p]) for p in range(n_p)])
    inter_all = stack([_dot(qs[p], s_augs[p].astype(BF16)) for p in range(n_p)])
    numden_all = intra_all + w_inter_all * inter_all
    o_all = numden_all[:, :ML_DV] / jnp.maximum(jnp.abs(numden_all[:, ML_DV:]), jnp.exp(-m_t_all))
    ms_all = jnp.mean(o_all * o_all, axis=-1, keepdims=True)
    on_all = o_all * lax.rsqrt(ms_all + EPS)

    e_all = glast_all - bc_all + lic_all
    gm_all = glast_all + mprev_all
    e_max = jnp.max(e_all.reshape(n_p, CHUNK, 1), axis=1, keepdims=True)
    m_new_all = jnp.maximum(gm_all.reshape(n_p, CHUNK, 1), e_max).reshape(n_p * CHUNK, 1)
    w_s_all = jnp.exp(e_all - m_new_all)
    w_p_all = jnp.exp(gm_all - m_new_all)
    kw_all = stack(ks) * w_s_all
    for p in range(n_p):
        kw_t = kw_all[blk(p)].T.astype(BF16)
        s_refs[p][...] = w_p_all[p * CHUNK:p * CHUNK + 1] * s_augs[p] + _dot(kw_t, vs[p])
        m_refs[p][...] = jnp.broadcast_to(m_new_all[p * CHUNK:p * CHUNK + 1], m_refs[p].shape)

    for p, (b, h) in enumerate(pairs):
        cact, ogate = per_b[b][0], per_b[b][7]
        sl = sls[h]
        y = (on_all[blk(p)] * nw_ref[:, sl] + sk_ref[:, sl] * cact[:, sl]) * ogate[:, sl]
        y_ref[b, :, sl] = y.astype(y_ref.dtype)


def _mlstm(x3, head, gain, w_ml, tri, conv_w, conv_b, wq, wk, wv, gb_col, gb_row, norm_w, skip, bb):
    bsz, seq, d = x3.shape
    nc = seq // CHUNK + 1
    blk = pl.BlockSpec((bb, CHUNK, ML_W), lambda i, c: (i, c, 0))

    def full(a):
        nd = a.ndim
        return pl.BlockSpec(a.shape, lambda i, c: (0,) * nd)

    params = (head, gain, w_ml, tri, conv_w, conv_b, wq, wk, wv, gb_col, gb_row, norm_w, skip)
    return pl.pallas_call(
        _mlstm_kernel,
        grid=(bsz // bb, nc),
        in_specs=[_chunk_ahead_spec(bb, d, nc)] + [full(p) for p in params],
        out_specs=blk,
        out_shape=jax.ShapeDtypeStruct((bsz, nc * CHUNK, ML_W), BF16),
        scratch_shapes=([pltpu.VMEM((bb * CHUNK, w_ml.shape[1]), F32)] * 2
                        + [pltpu.VMEM((ML_DK, 2 * ML_DV), F32)] * (bb * N_HEADS)
                        + [pltpu.VMEM((8, LANES), F32)] * (bb * N_HEADS)
                        + [pltpu.VMEM((8, ML_W), F32)] * bb),
        compiler_params=_cparams(("parallel", "arbitrary")),
        name="mlstm",
    )(x3, *params)


def _merge_route_kernel(x_ref, head_ref, gain_ref, wg_ref, yh_ref, ym_ref, wbh_ref, wbm_ref, wo_ref, nf_ref,
                        rw_ref, rb_ref, h2_ref, xn_ref, te_ref, tw_ref, rk_ref, cnt_ref, *, n_experts):
    d = x_ref.shape[1]
    tq = x_ref.shape[0]
    j = pl.program_id(1)

    @pl.when(jnp.logical_and(pl.program_id(0) == 0, j == 0))
    def _():
        cnt_ref[...] = jnp.zeros_like(cnt_ref)

    h = _h_tile(x_ref, head_ref, j)
    hn = _rms_bf16(h, gain_ref[...])
    g0 = _sigmoid(_dot(hn, wg_ref[:, :d]))
    g1 = _sigmoid(_dot(hn, wg_ref[:, d:]))
    merged = g0 * _dot(yh_ref[...], wbh_ref[...]) + g1 * _dot(ym_ref[...], wbm_ref[...])
    h2 = h + _dot(merged.astype(BF16), wo_ref[...])
    h2_ref[...] = h2
    ms = jnp.mean(h2 * h2, axis=-1, keepdims=True)
    xn = (h2 * lax.rsqrt(ms + EPS)) * nf_ref[...]
    _store_token_tiles(xn_ref, xn)
    x1, x2, _ = _split3(xn)
    logits = (_dot_nt(rw_ref[0], x1) + _dot_nt(rw_ref[1], x1) + _dot_nt(rw_ref[0], x2)) + rb_ref[...]
    sub = lax.broadcasted_iota(jnp.int32, logits.shape, 0)
    work = logits
    vals, idxs = [], []
    for _ in range(TOP_K):
        vmax = jnp.max(work, axis=0, keepdims=True)
        imax = jnp.min(jnp.where(work == vmax, sub, n_experts), axis=0, keepdims=True)
        vals.append(vmax)
        idxs.append(imax)
        work = jnp.where(sub == imax, -jnp.inf, work)
    exps = [jnp.exp(v - vals[0]) for v in vals]
    tot = exps[0] + exps[1] + exps[2] + exps[3]

    krow = lax.broadcasted_iota(jnp.int32, (TOP_K, tq), 0)

    def rows(per_k):
        out = jnp.broadcast_to(per_k[0], (TOP_K, tq))
        for kk in range(1, TOP_K):
            out = jnp.where(krow == kk, per_k[kk], out)
        return out

    te_ref[0] = rows(idxs)
    tw_ref[0] = rows([e / tot for e in exps])

    valid = (j * tq + lax.broadcasted_iota(jnp.int32, (1, tq), 1)) >= N_PAD
    onehots = [jnp.where(jnp.logical_and(sub == idxs[kk], valid), 1.0, 0.0) for kk in range(TOP_K)]
    oh_all = onehots[0] + onehots[1] + onehots[2] + onehots[3]
    ri = lax.broadcasted_iota(jnp.int32, (tq, tq), 0)
    ci = lax.broadcasted_iota(jnp.int32, (tq, tq), 1)
    earlier = jnp.where(ri < ci, 1.0, 0.0).astype(BF16)
    cnt = cnt_ref[:, 0:1]
    before = _dot(oh_all.astype(BF16), earlier) + cnt
    ranks = []
    for kk in range(TOP_K):
        ranks.append(jnp.sum(jnp.where(sub == idxs[kk], before, 0.0), axis=0, keepdims=True))
        before = before + onehots[kk]
    rk_ref[0] = rows(ranks).astype(jnp.int32)
    cnt_ref[...] = jnp.broadcast_to(cnt + jnp.sum(oh_all, axis=1, keepdims=True), cnt_ref.shape)


def _merge_route(x2d, head, gain, w_g, y_hg, y_ml, wbh, wbm, wo, norm_ffn, rw_split, rb, n_experts, bsz, tq):
    d = x2d.shape[1]
    seq = x2d.shape[0] // bsz
    nj = (seq + CHUNK) // tq
    m = bsz * nj * tq
    assert d == TOK_ROWS * LANES, "token-tile layout assumes one (8,128) tile per token"
    row = lambda n: pl.BlockSpec((tq, n), lambda b, j: (b * nj + j, 0))
    tiles = pl.BlockSpec((tq * TOK_ROWS, LANES), lambda b, j: (b * nj + j, 0))
    per_tok = pl.BlockSpec((1, TOP_K, tq), lambda b, j: (b * nj + j, 0, 0))

    def full(a):
        nd = a.ndim
        return pl.BlockSpec(a.shape, lambda b, j: (0,) * nd)

    return pl.pallas_call(
        functools.partial(_merge_route_kernel, n_experts=n_experts),
        grid=(bsz, nj),
        in_specs=[_x_tile_spec(tq, d, seq), full(head), full(gain), full(w_g), row(HG_W), row(ML_W),
                  full(wbh), full(wbm), full(wo), full(norm_ffn), full(rw_split), full(rb)],
        out_specs=[row(d), tiles, per_tok, per_tok, per_tok,
                   pl.BlockSpec((n_experts, LANES), lambda b, j: (0, 0))],
        out_shape=[
            jax.ShapeDtypeStruct((m, d), F32),
            jax.ShapeDtypeStruct((m * TOK_ROWS, LANES), F32),
            jax.ShapeDtypeStruct((bsz * nj, TOP_K, tq), jnp.int32),
            jax.ShapeDtypeStruct((bsz * nj, TOP_K, tq), F32),
            jax.ShapeDtypeStruct((bsz * nj, TOP_K, tq), jnp.int32),
            jax.ShapeDtypeStruct((n_experts, LANES), F32),
        ],
        compiler_params=_cparams(("arbitrary", "arbitrary")),
        name="merge_route",
    )(x2d, head, gain, w_g, y_hg, y_ml, wbh, wbm, wo, norm_ffn, rw_split, rb)


def _toks(first, n=1):
    return pl.ds(pl.multiple_of(first * TOK_ROWS, TOK_ROWS), n * TOK_ROWS)


def _dispatch_kernel(ps_ref, pe_ref, pos_ref, x_ref, xs_hbm, zbuf, sem, *, n_experts, tm):
    b = pl.program_id(0)
    j = pl.program_id(1)
    tq = x_ref.shape[0] // TOK_ROWS

    @pl.when(jnp.logical_and(b == 0, j == 0))
    def _():
        zbuf[...] = jnp.zeros_like(zbuf)
        for e in range(n_experts):
            @pl.when(pe_ref[e] > ps_ref[e])
            def _():
                pltpu.make_async_copy(zbuf, xs_hbm.at[_toks(pe_ref[e] - tm, tm), :], sem).start()
        for e in range(n_experts):
            @pl.when(pe_ref[e] > ps_ref[e])
            def _():
                pltpu.make_async_copy(zbuf, xs_hbm.at[_toks(0, tm), :], sem).wait()

        def zero_tail(blk, carry):
            cp = pltpu.make_async_copy(zbuf, xs_hbm.at[_toks(blk * tm, tm), :], sem)
            cp.start()
            cp.wait()
            return carry
        lax.fori_loop(pe_ref[n_experts - 1] // tm, xs_hbm.shape[0] // (tm * TOK_ROWS), zero_tail, 0)

    def scatter_rows(lo):
        def body(r, carry):
            for kk in range(TOP_K):
                dst = pos_ref[0, kk, r]
                pltpu.make_async_copy(
                    x_ref.at[_toks(r), :], xs_hbm.at[_toks(dst), :], sem).start(priority=kk % 2)
            return carry
        lax.fori_loop(lo, tq, body, 0, unroll=2)
        n = tq - lo
        for _ in range(TOP_K):
            pltpu.make_async_copy(x_ref.at[_toks(0, n), :], xs_hbm.at[_toks(0, n), :], sem).wait()

    @pl.when(j == 0)
    def _():
        scatter_rows(N_PAD)

    @pl.when(j != 0)
    def _():
        scatter_rows(0)


def _dispatch(pstart, pend, pos, xn_t, n_rows, bsz, tq, tm):
    m = xn_t.shape[0] // TOK_ROWS
    nj = m // (bsz * tq)
    n_experts = pstart.shape[0]
    smem_blk = pl.BlockSpec((1, TOP_K, tq), lambda b, j, ps, pe: (b * nj + j, 0, 0),
                            memory_space=pltpu.SMEM)
    grid_spec = pltpu.PrefetchScalarGridSpec(
        num_scalar_prefetch=2,
        grid=(bsz, nj),
        in_specs=[smem_blk,
                  pl.BlockSpec((tq * TOK_ROWS, LANES), lambda b, j, ps, pe: (b * nj + j, 0))],
        out_specs=pl.BlockSpec(memory_space=pl.ANY),
        scratch_shapes=[pltpu.VMEM((tm * TOK_ROWS, LANES), F32), pltpu.SemaphoreType.DMA(())],
    )
    return pl.pallas_call(
        functools.partial(_dispatch_kernel, n_experts=n_experts, tm=tm),
        grid_spec=grid_spec,
        out_shape=jax.ShapeDtypeStruct((n_rows * TOK_ROWS, LANES), F32),
        compiler_params=_cparams(("arbitrary", "arbitrary")),
        name="dispatch",
    )(pstart, pend, pos, xn_t)


CAST_ROWS = 256


def _experts_kernel(be_ref, nu_ref, x_ref, wgu_ref, bg_ref, bu_ref, wd_ref, bd_ref, y_ref,
                    wg_s, wu_s, wd_s):
    i = pl.program_id(0)
    n_used = nu_ref[0]
    dff = wg_s.shape[0]
    new_expert = jnp.logical_or(i == 0, be_ref[i] != be_ref[jnp.maximum(i - 1, 0)])

    @pl.when(jnp.logical_and(new_expert, i < n_used))
    def _():
        for c in range(dff // CAST_ROWS):
            rows = pl.ds(c * CAST_ROWS, CAST_ROWS)
            wd_s[rows, :] = wd_ref[0, rows, :].astype(BF16)
            for s in range(wgu_ref.shape[1]):
                lanes = pl.ds(s * LANES, LANES)
                wg_s[rows, lanes] = wgu_ref[
                    0, s, pl.ds(2 * c * CAST_ROWS, CAST_ROWS, stride=2), :].astype(BF16)
                wu_s[rows, lanes] = wgu_ref[
                    0, s, pl.ds(2 * c * CAST_ROWS + 1, CAST_ROWS, stride=2), :].astype(BF16)

    @pl.when(i < n_used)
    def _():
        tm = x_ref.shape[0] // TOK_ROWS
        xb = _load_token_tiles(x_ref, tm).astype(BF16)
        g = _dot_nt(xb, wg_s[...]) + bg_ref[0]
        u = _dot_nt(xb, wu_s[...]) + bu_ref[0]
        gate = jnp.minimum(g, SWIGLU_LIMIT)
        up = jnp.clip(u, -SWIGLU_LIMIT, SWIGLU_LIMIT)
        act = (up + 1.0) * gate * _sigmoid(SWIGLU_ALPHA * gate)
        _store_token_tiles(y_ref, _dot(act.astype(BF16), wd_s[...]) + bd_ref[0])

    @pl.when(i >= n_used)
    def _():
        y_ref[...] = jnp.zeros_like(y_ref)


def _experts(block_e, n_used, xs, w_gu_t, b_g, b_u, w_d, b_d, tm):
    n_blocks = block_e.shape[0]
    dff, d = w_d.shape[1:]
    tile_blk = lambda f: pl.BlockSpec((tm * TOK_ROWS, LANES), f)
    assert dff % CAST_ROWS == 0
    wspec = lambda k, n: pl.BlockSpec((1, k, n), lambda i, be, nu: (be[i], 0, 0))
    grid_spec = pltpu.PrefetchScalarGridSpec(
        num_scalar_prefetch=2,
        grid=(n_blocks,),
        in_specs=[
            tile_blk(lambda i, be, nu: (jnp.minimum(i, nu[0] - 1), 0)),
            pl.BlockSpec((1, d // LANES, 2 * dff, LANES), lambda i, be, nu: (be[i], 0, 0, 0)),
            wspec(1, dff), wspec(1, dff), wspec(dff, d), wspec(1, d),
        ],
        out_specs=tile_blk(lambda i, be, nu: (i, 0)),
        scratch_shapes=[pltpu.VMEM((dff, d), BF16), pltpu.VMEM((dff, d), BF16), pltpu.VMEM((dff, d), BF16)],
    )
    return pl.pallas_call(
        _experts_kernel,
        grid_spec=grid_spec,
        out_shape=jax.ShapeDtypeStruct((n_blocks * tm * TOK_ROWS, LANES), F32),
        compiler_params=_cparams(("arbitrary",)),
        name="experts",
    )(block_e, n_used, xs, w_gu_t, b_g, b_u, w_d, b_d)


def _combine_kernel(pos_ref, posn_ref, hsrc_ref, hsrcn_ref, tw_ref, y_hbm, h_hbm,
                    nw_ref, o_ref, ybuf, hbuf, sem):
    i = pl.program_id(0)
    n = pl.num_programs(0)
    tt = o_ref.shape[0]
    nt = tt * TOK_ROWS

    def copies(src_pos_ref, src_h_ref, dst_slot, r):
        return [pltpu.make_async_copy(
            y_hbm.at[_toks(src_pos_ref[0, kk, r]), :],
            ybuf.at[dst_slot, _toks(kk * tt + r), :],
            sem.at[dst_slot]) for kk in range(TOP_K)]

    def start_h(src_h_ref, dst_slot):
        h_row = pl.multiple_of(src_h_ref[0, 0, 0], 8)
        pltpu.make_async_copy(h_hbm.at[pl.ds(h_row, tt), :], hbuf.at[dst_slot], sem.at[dst_slot]).start()

    @pl.when(i == 0)
    def _():
        def body(r, carry):
            for kk, cp in enumerate(copies(pos_ref, hsrc_ref, 0, r)):
                cp.start(priority=kk % 2)
            return carry
        lax.fori_loop(0, tt, body, 0, unroll=2)
        start_h(hsrc_ref, 0)

    def wait_slot(s):
        pltpu.make_async_copy(ybuf.at[s], ybuf.at[s], sem.at[s]).wait()
        pltpu.make_async_copy(hbuf.at[s], hbuf.at[s], sem.at[s]).wait()

    def step(slot):
        for r in range(tt):
            for kk, cp in enumerate(copies(posn_ref, hsrcn_ref, 1 - slot, r)):
                cp.start(priority=kk % 2)
        start_h(hsrcn_ref, 1 - slot)

        wait_slot(slot)
        acc = hbuf[slot]
        tw = tw_ref[...]
        for kk in range(TOP_K):
            yk = jnp.concatenate(
                [ybuf[slot, pl.ds(kk * nt + s, tt, stride=TOK_ROWS), :] for s in range(TOK_ROWS)], axis=1)
            acc = acc + tw[:, kk:kk + 1] * yk
        ms = jnp.mean(acc * acc, axis=-1, keepdims=True)
        o_ref[...] = (acc * lax.rsqrt(ms + EPS)) * nw_ref[...]

        @pl.when(i == n - 1)
        def _():
            wait_slot(1 - slot)

    for parity in range(2):
        pl.when(i % 2 == parity)(functools.partial(step, parity))


def _combine(pos_seq, tw_seq, h_src, y_rows, h2, norm_w, tt):
    n_tiles = h_src.shape[0]
    d = h2.shape[1]
    last = n_tiles - 1
    nt = tt * TOK_ROWS
    cur = lambda i: (i, 0, 0)
    nxt = lambda i: (jnp.minimum(i + 1, last), 0, 0)
    idx_blk = lambda f: pl.BlockSpec((1, TOP_K, tt), f, memory_space=pltpu.SMEM)
    one_blk = lambda f: pl.BlockSpec((1, 1, 1), f, memory_space=pltpu.SMEM)
    return pl.pallas_call(
        _combine_kernel,
        grid=(n_tiles,),
        in_specs=[
            idx_blk(cur), idx_blk(nxt), one_blk(cur), one_blk(nxt),
            pl.BlockSpec((tt, TOP_K), lambda i: (i, 0)),
            pl.BlockSpec(memory_space=pl.ANY),
            pl.BlockSpec(memory_space=pl.ANY),
            pl.BlockSpec(norm_w.shape, lambda i: (0, 0)),
        ],
        out_specs=pl.BlockSpec((tt, d), lambda i: (i, 0)),
        scratch_shapes=[pltpu.VMEM((2, TOP_K * nt, LANES), F32), pltpu.VMEM((2, tt, d), F32),
                        pltpu.SemaphoreType.DMA((2,))],
        out_shape=jax.ShapeDtypeStruct((n_tiles * tt, d), F32),
        compiler_params=_cparams(("arbitrary",)),
        name="combine",
    )(pos_seq, pos_seq, h_src, h_src, tw_seq, y_rows, h2, norm_w)


def _pick(n, prefs):
    for p in prefs:
        if n % p == 0:
            return p
    raise ValueError(f"no tile in {prefs} divides {n}")


def kernel(x, meta_tokens, hg_lb_logits, norm_mix, w_in, hg_norm, ml_conv_w, ml_conv_b, ml_wq, ml_wk, ml_wv,
           ml_gate_b, ml_norm, ml_skip, w_branch_hg, w_branch_ml, w_out, norm_ffn, router_w, router_b,
           exp_w_gu, exp_b_gu, exp_w_down, exp_b_down, norm_final):
    bsz, seq, d = x.shape
    assert norm_mix.shape[0] == 1, "single-layer block"
    assert seq % CHUNK == 0 and d % LANES == 0
    t = CHUNK + seq
    m_rows = bsz * t
    n_experts = router_w.shape[-1]
    dff = exp_w_down.shape[2]
    assert n_experts <= LANES

    head = jnp.concatenate([jnp.zeros((N_PAD, d), x.dtype), meta_tokens.astype(x.dtype)], axis=0)
    x2d = x.reshape(bsz * seq, d)
    lower_bounds = jnp.cumsum(jax.nn.softmax(hg_lb_logits.astype(F32), axis=0), axis=0)

    w = w_in[0]
    n_hg = 4 * HG_W
    n_a = n_hg + 2 * ML_W
    w_hg = w[:, :n_hg].astype(BF16)
    w_m = jnp.pad(w[:, n_a:n_a + 2 * N_HEADS], ((0, 0), (0, LANES - 2 * N_HEADS)))
    w_ml = jnp.concatenate([w[:, n_hg:n_a], w_m], axis=1).astype(BF16)
    w_g = w[:, n_a + 2 * N_HEADS:].astype(BF16)
    gain = norm_mix[0][None]

    bb = _pick(bsz, (4, 2, 1))
    tri = jnp.asarray(np.tril(np.ones((CHUNK, CHUNK), np.float32)), BF16)
    y_hg = _hgrn2(x, head, gain, w_hg, tri, lower_bounds[0][None], hg_norm[0][None], bb)

    gb = ml_gate_b[0].astype(F32)
    gb_col = jnp.pad(gb, (0, LANES - 2 * N_HEADS))[None]
    gb_row = jnp.broadcast_to(gb[:, None], (2 * N_HEADS, CHUNK))
    y_ml = _mlstm(x, head, gain, w_ml, tri, ml_conv_w[0], ml_conv_b[0][None],
                  ml_wq[0].astype(BF16), ml_wk[0].astype(BF16), ml_wv[0].astype(BF16),
                  gb_col, gb_row, ml_norm[0][None], ml_skip[0][None], bb)

    assert n_experts % 8 == 0
    tq = _pick(t, (352, 192, 64))
    rw = router_w[0].astype(F32).T
    rw1 = rw.astype(BF16)
    rw2 = (rw - rw1.astype(F32)).astype(BF16)
    rw_split = jnp.stack([rw1, rw2])
    rb = jnp.broadcast_to(router_b[0].astype(F32)[:, None], (n_experts, tq))
    h2, xn2, top_e, top_w, rank, cnt = _merge_route(
        x2d, head, gain, w_g, y_hg.reshape(m_rows, HG_W), y_ml.reshape(m_rows, ML_W),
        w_branch_hg[0].astype(BF16), w_branch_ml[0].astype(BF16), w_out[0].astype(BF16),
        norm_ffn[0][None], rw_split, rb, n_experts, bsz, tq)

    tm6 = 512
    n_assign = bsz * (t - N_PAD) * TOP_K
    counts = cnt[:, 0].astype(jnp.int32)
    padded = ((counts + tm6 - 1) // tm6) * tm6
    pend = jnp.cumsum(padded).astype(jnp.int32)
    pstart = pend - padded
    n_blocks = -(-n_assign // tm6) + n_experts
    n_rows = n_blocks * tm6
    blk_start = jnp.arange(n_blocks, dtype=jnp.int32) * tm6
    block_e = jnp.minimum(jnp.sum((blk_start[:, None] >= pend[None, :]).astype(jnp.int32), axis=1),
                          n_experts - 1)
    n_used = (pend[-1] // tm6)[None]
    pos = rank + jnp.sum(jnp.where(top_e[..., None] == jnp.arange(n_experts, dtype=jnp.int32), pstart, 0),
                         axis=-1)
    xs = _dispatch(pstart, pend, pos, xn2, n_rows, bsz, tq, tm6)

    w_gu_t = jnp.swapaxes(exp_w_gu[0].reshape(n_experts, d // LANES, LANES, 2 * dff), 2, 3)
    bgu = exp_b_gu[0]
    y_rows = _experts(block_e, n_used, xs, w_gu_t, bgu[:, None, 0::2], bgu[:, None, 1::2],
                      exp_w_down[0], exp_b_down[0][:, None, :], tm6)

    tt = _pick(seq, (128, 64))
    tiles_per_b = seq // tt
    n_tiles = bsz * tiles_per_b
    nj = t // tq

    def seq_tiles(a):
        a = a.reshape(bsz, nj, TOP_K, tq).transpose(0, 2, 1, 3).reshape(bsz, TOP_K, t)[:, :, CHUNK:]
        return a.reshape(bsz, TOP_K, tiles_per_b, tt).transpose(0, 2, 1, 3).reshape(n_tiles, TOP_K, tt)

    h_src = (jnp.arange(bsz, dtype=jnp.int32)[:, None] * t + CHUNK
             + jnp.arange(tiles_per_b, dtype=jnp.int32)[None, :] * tt).reshape(n_tiles, 1, 1)
    tw_cols = seq_tiles(top_w).transpose(0, 2, 1).reshape(n_tiles * tt, TOP_K)
    out = _combine(seq_tiles(pos), tw_cols, h_src, y_rows, h2, norm_final[None], tt)
    return out.reshape(bsz, seq, d)
```

```python
import functools

import numpy as np
import jax
import jax.numpy as jnp
from jax import lax
from jax.experimental import pallas as pl
from jax.experimental.pallas import tpu as pltpu

F32 = jnp.float32
BF16 = jnp.bfloat16

N_META = 16
CHUNK = 64
N_PAD = CHUNK - N_META
EPS = 1e-6

N_HEADS = 4
HG_DK = 128
HG_DV = 128
HG_W = N_HEADS * HG_DV
ML_DK = 64
ML_DV = 128
ML_W = N_HEADS * ML_DV
ML_CONV = 4
TOP_K = 4
SWIGLU_LIMIT = 7.0
SWIGLU_ALPHA = 1.702

LANES = 128
VMEM_LIMIT_BYTES = 56 * 1024 * 1024

HG_LEVELS = (32, 16, 8, 4, 2, 1)


def _cparams(sem):
    return pltpu.CompilerParams(dimension_semantics=sem, vmem_limit_bytes=VMEM_LIMIT_BYTES)


def _sigmoid(x):
    return 1.0 / (1.0 + jnp.exp(-x))


def _split3(x):
    x1 = x.astype(BF16)
    r1 = x - x1.astype(F32)
    x2 = r1.astype(BF16)
    x3 = (r1 - x2.astype(F32)).astype(BF16)
    return x1, x2, x3


def _dot(a, b):
    return jnp.dot(a, b, preferred_element_type=F32)


TOK_ROWS = 8


def _store_token_tiles(ref, x):
    n = x.shape[0]
    for s in range(TOK_ROWS):
        ref[pl.ds(s, n, stride=TOK_ROWS), :] = x[:, s * LANES:(s + 1) * LANES]


def _load_token_tiles(ref, n):
    return jnp.concatenate([ref[pl.ds(s, n, stride=TOK_ROWS), :] for s in range(TOK_ROWS)], axis=1)


def _dot_nt(a, b):
    return lax.dot_general(a, b, (((1,), (1,)), ((), ())), preferred_element_type=F32)


def _dot_exact_lhs(m_bf16, x):
    x1, x2, x3 = _split3(x)
    return _dot(m_bf16, x1) + _dot(m_bf16, x2) + _dot(m_bf16, x3)


def _h_tile(x_ref, head_ref, j):
    x = x_ref[...]
    first = jnp.concatenate([head_ref[...], x[:x.shape[0] - CHUNK]], axis=0)
    return jnp.where(j == 0, first, x)


def _x_tile_spec(tr, d, seq):
    assert seq % 8 == 0 and tr % 8 == 0 and CHUNK % 8 == 0
    return pl.BlockSpec(
        (pl.Element(tr), pl.Element(d)),
        lambda b, j: (pl.multiple_of(b * seq + jnp.maximum(j * tr - CHUNK, 0), 8), 0))


def _rms_bf16(x, gain):
    ms = jnp.mean(x * x, axis=-1, keepdims=True)
    return ((x * lax.rsqrt(ms + EPS)) * gain).astype(BF16)


def _project_ahead(x_ref, gain_ref, w_ref, nxt_ref):
    x = x_ref[...].reshape(x_ref.shape[0] * CHUNK, x_ref.shape[2])
    nxt_ref[...] = _dot(_rms_bf16(x, gain_ref[...]), w_ref[...])


def _with_projection_pipeline(step, x_ref, head_ref, gain_ref, w_ref, proj_a, proj_b, c):
    @pl.when(c == 0)
    def _():
        p0 = _dot(_rms_bf16(head_ref[...], gain_ref[...]), w_ref[...])
        for b in range(x_ref.shape[0]):
            proj_a[b * CHUNK:(b + 1) * CHUNK, :] = p0

    for parity, (cur, nxt) in enumerate(((proj_a, proj_b), (proj_b, proj_a))):
        @pl.when(c % 2 == parity)
        def _():
            _project_ahead(x_ref, gain_ref, w_ref, nxt)
            step(cur)


def _chunk_ahead_spec(bb, d, nc):
    return pl.BlockSpec((bb, CHUNK, d), lambda i, c: (i, jnp.minimum(c, nc - 2), 0))


def _hgrn2_kernel(x_ref, head_ref, gain_ref, w_ref, tri_ref, lb_ref, nw_ref, y_ref, proj_a, proj_b, *st_refs):
    c = pl.program_id(1)

    @pl.when(c == 0)
    def _():
        for st_ref in st_refs:
            st_ref[...] = jnp.zeros_like(st_ref)

    step = functools.partial(_hgrn2_step, c, x_ref.shape[0], tri_ref, lb_ref, nw_ref, y_ref, st_refs)
    _with_projection_pipeline(step, x_ref, head_ref, gain_ref, w_ref, proj_a, proj_b, c)


def _hgrn2_step(c, bb, tri_ref, lb_ref, nw_ref, y_ref, st_refs, cur):
    row = lax.broadcasted_iota(jnp.int32, (CHUNK, 1), 0)
    valid = (c * CHUNK + row) >= N_PAD
    ti = lax.broadcasted_iota(jnp.int32, (CHUNK, CHUNK), 0)
    si = lax.broadcasted_iota(jnp.int32, (CHUNK, CHUNK), 1)
    diag_mask = ti == si
    level_masks = {}
    for m in HG_LEVELS:
        same_pair = (ti & ~(2 * m - 1)) == (si & ~(2 * m - 1))
        level_masks[m] = same_pair & ((ti & m) != 0) & ((si & m) == 0)

    lb = lb_ref[...]
    tri = tri_ref[...]
    per_b = []
    for b in range(bb):
        rows_b = slice(b * CHUNK, (b + 1) * CHUNK)
        hf = cur[rows_b, HG_W:2 * HG_W]
        f = lb + (1.0 - lb) * _sigmoid(hf)
        f = jnp.where(valid, f, 1.0)
        logf = jnp.log(f)
        k_all = 1.0 - f
        hq = cur[rows_b, 0:HG_W]
        q_all = hq * _sigmoid(hq)
        b_cum = _dot_exact_lhs(tri, logf)
        e_b = jnp.exp(b_cum)
        e_bl = jnp.exp(b_cum[CHUNK - 1:CHUNK] - b_cum)

        q_fac, k_fac = {}, {}
        for m in HG_LEVELS:
            if m >= 4:
                grp = b_cum.reshape(CHUNK // (2 * m), 2 * m, HG_W)
                e = (grp - grp[:, m - 1:m, :]).reshape(CHUNK, HG_W)
                q_fac[m] = jnp.exp(jnp.minimum(e, 0.0))
                k_fac[m] = jnp.exp(jnp.minimum(-e, 0.0))
        f_prev = pltpu.roll(f, 1, 0)
        f_next = pltpu.roll(f, CHUNK - 1, 0)
        r4 = row & 3
        q_fac[2] = jnp.where(r4 == 2, f, jnp.where(r4 == 3, f * f_prev, 1.0))
        k_fac[2] = jnp.where(r4 == 0, f_next, 1.0)
        q_fac[1] = jnp.where((row & 1) == 1, f, 1.0)

        hg = cur[rows_b, 3 * HG_W:4 * HG_W]
        v_all = cur[rows_b, 2 * HG_W:3 * HG_W]
        per_b.append((q_all, k_all, q_fac, k_fac, e_b, e_bl, v_all, hg * _sigmoid(hg)))

    pairs = [(b, h) for b in range(bb) for h in range(N_HEADS)]
    sls = [slice(h * HG_DK, (h + 1) * HG_DK) for h in range(N_HEADS)]

    scores_all = []
    for b, h in pairs:
        q_all, k_all, q_fac, k_fac = per_b[b][:4]
        q = q_all[:, sls[h]]
        k = k_all[:, sls[h]]
        kb = k.astype(BF16)
        scores = jnp.where(diag_mask, _dot_nt(q.astype(BF16), kb), 0.0)
        for m in HG_LEVELS:
            qd = (q * q_fac[m][:, sls[h]]).astype(BF16)
            kd = (k * k_fac[m][:, sls[h]]).astype(BF16) if m in k_fac else kb
            scores = jnp.where(level_masks[m], _dot_nt(qd, kd), scores)
        scores_all.append(scores.astype(BF16))

    sts = [st_ref[...] for st_ref in st_refs]
    outs = []
    for p, (b, h) in enumerate(pairs):
        q_all, e_b, v_all = per_b[b][0], per_b[b][4], per_b[b][6]
        qe = (q_all[:, sls[h]] * e_b[:, sls[h]]).astype(BF16)
        outs.append(_dot(scores_all[p], v_all[:, sls[h]].astype(BF16)) + _dot_nt(qe, sts[p].astype(BF16)))

    for p, (b, h) in enumerate(pairs):
        k_all, e_b, e_bl, v_all = per_b[b][1], per_b[b][4], per_b[b][5], per_b[b][6]
        kl = (k_all[:, sls[h]] * e_bl[:, sls[h]]).astype(BF16)
        vt = v_all[:, sls[h]].T.astype(BF16)
        st_refs[p][...] = e_b[CHUNK - 1:CHUNK, sls[h]] * sts[p] + _dot(vt, kl)

    for p, (b, h) in enumerate(pairs):
        o = outs[p]
        ms = jnp.mean(o * o, axis=-1, keepdims=True)
        y = (o * lax.rsqrt(ms + EPS)) * nw_ref[:, sls[h]] * per_b[b][7][:, sls[h]]
        y_ref[b, :, sls[h]] = y.astype(y_ref.dtype)


def _hgrn2(x3, head, gain, w_hg, tri, lb, norm_w, bb):
    bsz, seq, d = x3.shape
    nc = seq // CHUNK + 1
    blk = pl.BlockSpec((bb, CHUNK, HG_W), lambda i, c: (i, c, 0))
    full = lambda a: pl.BlockSpec(a.shape, lambda i, c: (0, 0))
    return pl.pallas_call(
        _hgrn2_kernel,
        grid=(bsz // bb, nc),
        in_specs=[_chunk_ahead_spec(bb, d, nc), full(head), full(gain), full(w_hg), full(tri), full(lb),
                  full(norm_w)],
        out_specs=blk,
        out_shape=jax.ShapeDtypeStruct((bsz, nc * CHUNK, HG_W), BF16),
        scratch_shapes=([pltpu.VMEM((bb * CHUNK, w_hg.shape[1]), F32)] * 2
                        + [pltpu.VMEM((HG_DV, HG_DK), F32)] * (bb * N_HEADS)),
        compiler_params=_cparams(("parallel", "arbitrary")),
        name="hgrn2",
    )(x3, head, gain, w_hg, tri, lb, norm_w)


def _log_sigmoid(x):
    return jnp.minimum(x, 0.0) - jnp.log(1.0 + jnp.exp(-jnp.abs(x)))


def _mlstm_kernel(x_ref, head_ref, gain_ref, w_ref, tri_ref, cw_ref, cb_ref, wq_ref, wk_ref, wv_ref,
                  gbc_ref, gbr_ref, nw_ref, sk_ref, y_ref, proj_a, proj_b, *scratch):
    c = pl.program_id(1)

    @pl.when(c == 0)
    def _():
        for ref in scratch:
            ref[...] = jnp.zeros_like(ref)

    step = functools.partial(_mlstm_step, c, x_ref.shape[0], tri_ref, cw_ref, cb_ref, wq_ref, wk_ref, wv_ref,
                             gbc_ref, gbr_ref, nw_ref, sk_ref, y_ref, scratch)
    _with_projection_pipeline(step, x_ref, head_ref, gain_ref, w_ref, proj_a, proj_b, c)


def _mlstm_step(c, bb, tri_ref, cw_ref, cb_ref, wq_ref, wk_ref, wv_ref, gbc_ref, gbr_ref, nw_ref, sk_ref, y_ref,
                scratch, cur):
    n_pairs = bb * N_HEADS
    s_refs = scratch[:n_pairs]
    m_refs = scratch[n_pairs:2 * n_pairs]
    tail_refs = scratch[2 * n_pairs:]

    pos_c = c * CHUNK + lax.broadcasted_iota(jnp.int32, (CHUNK, 1), 0)
    valid_c = pos_c >= N_PAD
    pos_r = c * CHUNK + lax.broadcasted_iota(jnp.int32, (1, CHUNK), 1)
    valid_r = pos_r >= N_PAD
    ti = lax.broadcasted_iota(jnp.int32, (CHUNK, CHUNK), 0)
    si = lax.broadcasted_iota(jnp.int32, (CHUNK, CHUNK), 1)
    causal = si <= ti
    tri = tri_ref[...]
    ones_v = jnp.ones((CHUNK, ML_DV), BF16)
    neg_inf = -jnp.inf

    per_b = []
    for b in range(bb):
        rows_b = slice(b * CHUNK, (b + 1) * CHUNK)
        mm = jnp.where(valid_c, cur[rows_b, 0:ML_W], 0.0)
        ext = jnp.concatenate([tail_refs[b][...], mm], axis=0)
        tail_refs[b][...] = mm[CHUNK - 8:CHUNK]
        conv = cb_ref[...]
        for j in range(ML_CONV):
            off = 8 - (ML_CONV - 1) + j
            conv = conv + cw_ref[j:j + 1, :] * ext[off:off + CHUNK]
        cact = conv * _sigmoid(conv)
        cact_b = cact.astype(BF16)
        mm_b = mm.astype(BF16)

        graw = cur[rows_b, 2 * ML_W:2 * ML_W + LANES]
        gcol = graw + gbc_ref[...]
        li_col = jnp.where(valid_c, gcol, neg_inf)
        lf_col = jnp.where(valid_c, _log_sigmoid(gcol), 0.0)
        b_col = _dot_exact_lhs(tri, lf_col)
        grow = graw.T[0:2 * N_HEADS] + gbr_ref[...]
        li_row = jnp.where(valid_r, grow, neg_inf)
        lf_row = jnp.where(valid_r, _log_sigmoid(grow), 0.0)
        r1, r2, r3 = _split3(lf_row)
        b_row = _dot_nt(r1, tri) + _dot_nt(r2, tri) + _dot_nt(r3, tri)

        ogate = _sigmoid(cur[rows_b, ML_W:2 * ML_W])
        per_b.append((cact, cact_b, mm_b, li_col, b_col, li_row, b_row, ogate))

    pairs = [(b, h) for b in range(bb) for h in range(N_HEADS)]
    sls = [slice(h * ML_DV, (h + 1) * ML_DV) for h in range(N_HEADS)]

    qs, ks, vs = [], [], []
    for b, h in pairs:
        cact_b, mm_b = per_b[b][1], per_b[b][2]
        qs.append((_dot(cact_b[:, sls[h]], wq_ref[h]) * (ML_DK ** -0.5)).astype(BF16))
        ks.append(_dot(cact_b[:, sls[h]], wk_ref[h]))
        v = _dot(mm_b[:, sls[h]], wv_ref[h]).astype(BF16)
        vs.append(jnp.concatenate([v, ones_v], axis=1))

    n_p = len(pairs)
    blk = lambda p: slice(p * CHUNK, (p + 1) * CHUNK)
    stack = lambda xs: jnp.concatenate(xs, axis=0)
    bc_all = stack([per_b[b][4][:, N_HEADS + h:N_HEADS + h + 1] for b, h in pairs])
    lic_all = stack([per_b[b][3][:, h:h + 1] for b, h in pairs])
    row_all = stack([jnp.broadcast_to(per_b[b][5][h:h + 1, :] - per_b[b][6][N_HEADS + h:N_HEADS + h + 1, :],
                                      (CHUNK, CHUNK)) for b, h in pairs])
    mprev_all = stack([jnp.broadcast_to(m_refs[p][0:1, 0:1], (CHUNK, 1)) for p in range(n_p)])
    glast_all = stack([jnp.broadcast_to(per_b[b][4][CHUNK - 1:CHUNK, N_HEADS + h:N_HEADS + h + 1], (CHUNK, 1))
                       for b, h in pairs])
    causal_all = stack([causal] * n_p)

    d_all = jnp.where(causal_all, bc_all + row_all, neg_inf)
    a_all = bc_all + mprev_all
    m_t_all = jnp.maximum(a_all, jnp.max(d_all, axis=-1, keepdims=True))
    w_intra_all = jnp.exp(d_all - m_t_all)
    w_inter_all = jnp.exp(a_all - m_t_all)

    qk_all = (stack([_dot_nt(qs[p], ks[p].astype(BF16)) for p in range(n_p)]) * w_intra_all).astype(BF16)
    s_augs = [s_refs[p][...] for p in range(n_p)]
    intra_all = stack([_dot(qk_all[blk(p)], vs[p]) for p in range(n_p)])
    inter_all = stack([_dot(qs[p], s_augs[p].astype(BF16)) for p in range(n_p)])
    numden_all = intra_all + w_inter_all * inter_all
    o_all = numden_all[:, :ML_DV] / jnp.maximum(jnp.abs(numden_all[:, ML_DV:]), jnp.exp(-m_t_all))
    ms_all = jnp.mean(o_all * o_all, axis=-1, keepdims=True)
    on_all = o_all * lax.rsqrt(ms_all + EPS)

    e_all = glast_all - bc_all + lic_all
    gm_all = glast_all + mprev_all
    e_max = jnp.max(e_all.reshape(n_p, CHUNK, 1), axis=1, keepdims=True)
    m_new_all = jnp.maximum(gm_all.reshape(n_p, CHUNK, 1), e_max).reshape(n_p * CHUNK, 1)
    w_s_all = jnp.exp(e_all - m_new_all)
    w_p_all = jnp.exp(gm_all - m_new_all)
    kw_all = stack(ks) * w_s_all
    for p in range(n_p):
        kw_t = kw_all[blk(p)].T.astype(BF16)
        s_refs[p][...] = w_p_all[p * CHUNK:p * CHUNK + 1] * s_augs[p] + _dot(kw_t, vs[p])
        m_refs[p][...] = jnp.broadcast_to(m_new_all[p * CHUNK:p * CHUNK + 1], m_refs[p].shape)

    for p, (b, h) in enumerate(pairs):
        cact, ogate = per_b[b][0], per_b[b][7]
        sl = sls[h]
        y = (on_all[blk(p)] * nw_ref[:, sl] + sk_ref[:, sl] * cact[:, sl]) * ogate[:, sl]
        y_ref[b, :, sl] = y.astype(y_ref.dtype)


def _mlstm(x3, head, gain, w_ml, tri, conv_w, conv_b, wq, wk, wv, gb_col, gb_row, norm_w, skip, bb):
    bsz, seq, d = x3.shape
    nc = seq // CHUNK + 1
    blk = pl.BlockSpec((bb, CHUNK, ML_W), lambda i, c: (i, c, 0))

    def full(a):
        nd = a.ndim
        return pl.BlockSpec(a.shape, lambda i, c: (0,) * nd)

    params = (head, gain, w_ml, tri, conv_w, conv_b, wq, wk, wv, gb_col, gb_row, norm_w, skip)
    return pl.pallas_call(
        _mlstm_kernel,
        grid=(bsz // bb, nc),
        in_specs=[_chunk_ahead_spec(bb, d, nc)] + [full(p) for p in params],
        out_specs=blk,
        out_shape=jax.ShapeDtypeStruct((bsz, nc * CHUNK, ML_W), BF16),
        scratch_shapes=([pltpu.VMEM((bb * CHUNK, w_ml.shape[1]), F32)] * 2
                        + [pltpu.VMEM((ML_DK, 2 * ML_DV), F32)] * (bb * N_HEADS)
                        + [pltpu.VMEM((8, LANES), F32)] * (bb * N_HEADS)
                        + [pltpu.VMEM((8, ML_W), F32)] * bb),
        compiler_params=_cparams(("parallel", "arbitrary")),
        name="mlstm",
    )(x3, *params)


def _merge_route_kernel(x_ref, head_ref, gain_ref, wg_ref, yh_ref, ym_ref, wbh_ref, wbm_ref, wo_ref, nf_ref,
                        rw_ref, rb_ref, h2_ref, xn_ref, te_ref, tw_ref, rk_ref, cnt_ref, *, n_experts):
    d = x_ref.shape[1]
    tq = x_ref.shape[0]
    j = pl.program_id(1)

    @pl.when(jnp.logical_and(pl.program_id(0) == 0, j == 0))
    def _():
        cnt_ref[...] = jnp.zeros_like(cnt_ref)

    h = _h_tile(x_ref, head_ref, j)
    hn = _rms_bf16(h, gain_ref[...])
    g0 = _sigmoid(_dot(hn, wg_ref[:, :d]))
    g1 = _sigmoid(_dot(hn, wg_ref[:, d:]))
    merged = g0 * _dot(yh_ref[...], wbh_ref[...]) + g1 * _dot(ym_ref[...], wbm_ref[...])
    h2 = h + _dot(merged.astype(BF16), wo_ref[...])
    h2_ref[...] = h2
    ms = jnp.mean(h2 * h2, axis=-1, keepdims=True)
    xn = (h2 * lax.rsqrt(ms + EPS)) * nf_ref[...]
    _store_token_tiles(xn_ref, xn)
    x1, x2, _ = _split3(xn)
    logits = (_dot_nt(rw_ref[0], x1) + _dot_nt(rw_ref[1], x1) + _dot_nt(rw_ref[0], x2)) + rb_ref[...]
    sub = lax.broadcasted_iota(jnp.int32, logits.shape, 0)
    work = logits
    vals, idxs = [], []
    for _ in range(TOP_K):
        vmax = jnp.max(work, axis=0, keepdims=True)
        imax = jnp.min(jnp.where(work == vmax, sub, n_experts), axis=0, keepdims=True)
        vals.append(vmax)
        idxs.append(imax)
        work = jnp.where(sub == imax, -jnp.inf, work)
    exps = [jnp.exp(v - vals[0]) for v in vals]
    tot = exps[0] + exps[1] + exps[2] + exps[3]

    krow = lax.broadcasted_iota(jnp.int32, (TOP_K, tq), 0)

    def rows(per_k):
        out = jnp.broadcast_to(per_k[0], (TOP_K, tq))
        for kk in range(1, TOP_K):
            out = jnp.where(krow == kk, per_k[kk], out)
        return out

    te_ref[0] = rows(idxs)
    tw_ref[0] = rows([e / tot for e in exps])

    valid = (j * tq + lax.broadcasted_iota(jnp.int32, (1, tq), 1)) >= N_PAD
    onehots = [jnp.where(jnp.logical_and(sub == idxs[kk], valid), 1.0, 0.0) for kk in range(TOP_K)]
    oh_all = onehots[0] + onehots[1] + onehots[2] + onehots[3]
    ri = lax.broadcasted_iota(jnp.int32, (tq, tq), 0)
    ci = lax.broadcasted_iota(jnp.int32, (tq, tq), 1)
    earlier = jnp.where(ri < ci, 1.0, 0.0).astype(BF16)
    cnt = cnt_ref[:, 0:1]
    before = _dot(oh_all.astype(BF16), earlier) + cnt
    ranks = []
    for kk in range(TOP_K):
        ranks.append(jnp.sum(jnp.where(sub == idxs[kk], before, 0.0), axis=0, keepdims=True))
        before = before + onehots[kk]
    rk_ref[0] = rows(ranks).astype(jnp.int32)
    cnt_ref[...] = jnp.broadcast_to(cnt + jnp.sum(oh_all, axis=1, keepdims=True), cnt_ref.shape)


def _merge_route(x2d, head, gain, w_g, y_hg, y_ml, wbh, wbm, wo, norm_ffn, rw_split, rb, n_experts, bsz, tq):
    d = x2d.shape[1]
    seq = x2d.shape[0] // bsz
    nj = (seq + CHUNK) // tq
    m = bsz * nj * tq
    assert d == TOK_ROWS * LANES, "token-tile layout assumes one (8,128) tile per token"
    row = lambda n: pl.BlockSpec((tq, n), lambda b, j: (b * nj + j, 0))
    tiles = pl.BlockSpec((tq * TOK_ROWS, LANES), lambda b, j: (b * nj + j, 0))
    per_tok = pl.BlockSpec((1, TOP_K, tq), lambda b, j: (b * nj + j, 0, 0))

    def full(a):
        nd = a.ndim
        return pl.BlockSpec(a.shape, lambda b, j: (0,) * nd)

    return pl.pallas_call(
        functools.partial(_merge_route_kernel, n_experts=n_experts),
        grid=(bsz, nj),
        in_specs=[_x_tile_spec(tq, d, seq), full(head), full(gain), full(w_g), row(HG_W), row(ML_W),
                  full(wbh), full(wbm), full(wo), full(norm_ffn), full(rw_split), full(rb)],
        out_specs=[row(d), tiles, per_tok, per_tok, per_tok,
                   pl.BlockSpec((n_experts, LANES), lambda b, j: (0, 0))],
        out_shape=[
            jax.ShapeDtypeStruct((m, d), F32),
            jax.ShapeDtypeStruct((m * TOK_ROWS, LANES), F32),
            jax.ShapeDtypeStruct((bsz * nj, TOP_K, tq), jnp.int32),
            jax.ShapeDtypeStruct((bsz * nj, TOP_K, tq), F32),
            jax.ShapeDtypeStruct((bsz * nj, TOP_K, tq), jnp.int32),
            jax.ShapeDtypeStruct((n_experts, LANES), F32),
        ],
        compiler_params=_cparams(("arbitrary", "arbitrary")),
        name="merge_route",
    )(x2d, head, gain, w_g, y_hg, y_ml, wbh, wbm, wo, norm_ffn, rw_split, rb)


def _toks(first, n=1):
    return pl.ds(pl.multiple_of(first * TOK_ROWS, TOK_ROWS), n * TOK_ROWS)


def _dispatch_kernel(ps_ref, pe_ref, pos_ref, x_ref, xs_hbm, zbuf, sem, *, n_experts, tm):
    b = pl.program_id(0)
    j = pl.program_id(1)
    tq = x_ref.shape[0] // TOK_ROWS

    @pl.when(jnp.logical_and(b == 0, j == 0))
    def _():
        zbuf[...] = jnp.zeros_like(zbuf)
        for e in range(n_experts):
            @pl.when(pe_ref[e] > ps_ref[e])
            def _():
                pltpu.make_async_copy(zbuf, xs_hbm.at[_toks(pe_ref[e] - tm, tm), :], sem).start()
        for e in range(n_experts):
            @pl.when(pe_ref[e] > ps_ref[e])
            def _():
                pltpu.make_async_copy(zbuf, xs_hbm.at[_toks(0, tm), :], sem).wait()

        def zero_tail(blk, carry):
            cp = pltpu.make_async_copy(zbuf, xs_hbm.at[_toks(blk * tm, tm), :], sem)
            cp.start()
            cp.wait()
            return carry
        lax.fori_loop(pe_ref[n_experts - 1] // tm, xs_hbm.shape[0] // (tm * TOK_ROWS), zero_tail, 0)

    def scatter_rows(lo):
        def body(r, carry):
            for kk in range(TOP_K):
                dst = pos_ref[0, kk, r]
                pltpu.make_async_copy(
                    x_ref.at[_toks(r), :], xs_hbm.at[_toks(dst), :], sem).start(priority=kk % 2)
            return carry
        lax.fori_loop(lo, tq, body, 0, unroll=2)
        n = tq - lo
        for _ in range(TOP_K):
            pltpu.make_async_copy(x_ref.at[_toks(0, n), :], xs_hbm.at[_toks(0, n), :], sem).wait()

    @pl.when(j == 0)
    def _():
        scatter_rows(N_PAD)

    @pl.when(j != 0)
    def _():
        scatter_rows(0)


def _dispatch(pstart, pend, pos, xn_t, n_rows, bsz, tq, tm):
    m = xn_t.shape[0] // TOK_ROWS
    nj = m // (bsz * tq)
    n_experts = pstart.shape[0]
    smem_blk = pl.BlockSpec((1, TOP_K, tq), lambda b, j, ps, pe: (b * nj + j, 0, 0),
                            memory_space=pltpu.SMEM)
    grid_spec = pltpu.PrefetchScalarGridSpec(
        num_scalar_prefetch=2,
        grid=(bsz, nj),
        in_specs=[smem_blk,
                  pl.BlockSpec((tq * TOK_ROWS, LANES), lambda b, j, ps, pe: (b * nj + j, 0))],
        out_specs=pl.BlockSpec(memory_space=pl.ANY),
        scratch_shapes=[pltpu.VMEM((tm * TOK_ROWS, LANES), F32), pltpu.SemaphoreType.DMA(())],
    )
    return pl.pallas_call(
        functools.partial(_dispatch_kernel, n_experts=n_experts, tm=tm),
        grid_spec=grid_spec,
        out_shape=jax.ShapeDtypeStruct((n_rows * TOK_ROWS, LANES), F32),
        compiler_params=_cparams(("arbitrary", "arbitrary")),
        name="dispatch",
    )(pstart, pend, pos, xn_t)


CAST_ROWS = 256


def _experts_kernel(be_ref, nu_ref, x_ref, wgu_ref, bg_ref, bu_ref, wd_ref, bd_ref, y_ref,
                    wg_s, wu_s, wd_s):
    i = pl.program_id(0)
    n_used = nu_ref[0]
    dff = wg_s.shape[0]
    new_expert = jnp.logical_or(i == 0, be_ref[i] != be_ref[jnp.maximum(i - 1, 0)])

    @pl.when(jnp.logical_and(new_expert, i < n_used))
    def _():
        for c in range(dff // CAST_ROWS):
            rows = pl.ds(c * CAST_ROWS, CAST_ROWS)
            wd_s[rows, :] = wd_ref[0, rows, :].astype(BF16)
            for s in range(wgu_ref.shape[1]):
                lanes = pl.ds(s * LANES, LANES)
                wg_s[rows, lanes] = wgu_ref[
                    0, s, pl.ds(2 * c * CAST_ROWS, CAST_ROWS, stride=2), :].astype(BF16)
                wu_s[rows, lanes] = wgu_ref[
                    0, s, pl.ds(2 * c * CAST_ROWS + 1, CAST_ROWS, stride=2), :].astype(BF16)

    @pl.when(i < n_used)
    def _():
        tm = x_ref.shape[0] // TOK_ROWS
        xb = _load_token_tiles(x_ref, tm).astype(BF16)
        g = _dot_nt(xb, wg_s[...]) + bg_ref[0]
        u = _dot_nt(xb, wu_s[...]) + bu_ref[0]
        gate = jnp.minimum(g, SWIGLU_LIMIT)
        up = jnp.clip(u, -SWIGLU_LIMIT, SWIGLU_LIMIT)
        act = (up + 1.0) * gate * _sigmoid(SWIGLU_ALPHA * gate)
        _store_token_tiles(y_ref, _dot(act.astype(BF16), wd_s[...]) + bd_ref[0])

    @pl.when(i >= n_used)
    def _():
        y_ref[...] = jnp.zeros_like(y_ref)


def _experts(block_e, n_used, xs, w_gu_t, b_g, b_u, w_d, b_d, tm):
    n_blocks = block_e.shape[0]
    dff, d = w_d.shape[1:]
    tile_blk = lambda f: pl.BlockSpec((tm * TOK_ROWS, LANES), f)
    assert dff % CAST_ROWS == 0
    wspec = lambda k, n: pl.BlockSpec((1, k, n), lambda i, be, nu: (be[i], 0, 0))
    grid_spec = pltpu.PrefetchScalarGridSpec(
        num_scalar_prefetch=2,
        grid=(n_blocks,),
        in_specs=[
            tile_blk(lambda i, be, nu: (jnp.minimum(i, nu[0] - 1), 0)),
            pl.BlockSpec((1, d // LANES, 2 * dff, LANES), lambda i, be, nu: (be[i], 0, 0, 0)),
            wspec(1, dff), wspec(1, dff), wspec(dff, d), wspec(1, d),
        ],
        out_specs=tile_blk(lambda i, be, nu: (i, 0)),
        scratch_shapes=[pltpu.VMEM((dff, d), BF16), pltpu.VMEM((dff, d), BF16), pltpu.VMEM((dff, d), BF16)],
    )
    return pl.pallas_call(
        _experts_kernel,
        grid_spec=grid_spec,
        out_shape=jax.ShapeDtypeStruct((n_blocks * tm * TOK_ROWS, LANES), F32),
        compiler_params=_cparams(("arbitrary",)),
        name="experts",
    )(block_e, n_used, xs, w_gu_t, b_g, b_u, w_d, b_d)


def _combine_kernel(pos_ref, posn_ref, hsrc_ref, hsrcn_ref, tw_ref, y_hbm, h_hbm,
                    nw_ref, o_ref, ybuf, hbuf, sem):
    i = pl.program_id(0)
    n = pl.num_programs(0)
    tt = o_ref.shape[0]
    nt = tt * TOK_ROWS

    def copies(src_pos_ref, src_h_ref, dst_slot, r):
        return [pltpu.make_async_copy(
            y_hbm.at[_toks(src_pos_ref[0, kk, r]), :],
            ybuf.at[dst_slot, _toks(kk * tt + r), :],
            sem.at[dst_slot]) for kk in range(TOP_K)]

    def start_h(src_h_ref, dst_slot):
        h_row = pl.multiple_of(src_h_ref[0, 0, 0], 8)
        pltpu.make_async_copy(h_hbm.at[pl.ds(h_row, tt), :], hbuf.at[dst_slot], sem.at[dst_slot]).start()

    @pl.when(i == 0)
    def _():
        def body(r, carry):
            for kk, cp in enumerate(copies(pos_ref, hsrc_ref, 0, r)):
                cp.start(priority=kk % 2)
            return carry
        lax.fori_loop(0, tt, body, 0, unroll=2)
        start_h(hsrc_ref, 0)

    def wait_slot(s):
        pltpu.make_async_copy(ybuf.at[s], ybuf.at[s], sem.at[s]).wait()
        pltpu.make_async_copy(hbuf.at[s], hbuf.at[s], sem.at[s]).wait()

    def step(slot):
        for r in range(tt):
            for kk, cp in enumerate(copies(posn_ref, hsrcn_ref, 1 - slot, r)):
                cp.start(priority=kk % 2)
        start_h(hsrcn_ref, 1 - slot)

        wait_slot(slot)
        acc = hbuf[slot]
        tw = tw_ref[...]
        for kk in range(TOP_K):
            yk = jnp.concatenate(
                [ybuf[slot, pl.ds(kk * nt + s, tt, stride=TOK_ROWS), :] for s in range(TOK_ROWS)], axis=1)
            acc = acc + tw[:, kk:kk + 1] * yk
        ms = jnp.mean(acc * acc, axis=-1, keepdims=True)
        o_ref[...] = (acc * lax.rsqrt(ms + EPS)) * nw_ref[...]

        @pl.when(i == n - 1)
        def _():
            wait_slot(1 - slot)

    for parity in range(2):
        pl.when(i % 2 == parity)(functools.partial(step, parity))


def _combine(pos_seq, tw_seq, h_src, y_rows, h2, norm_w, tt):
    n_tiles = h_src.shape[0]
    d = h2.shape[1]
    last = n_tiles - 1
    nt = tt * TOK_ROWS
    cur = lambda i: (i, 0, 0)
    nxt = lambda i: (jnp.minimum(i + 1, last), 0, 0)
    idx_blk = lambda f: pl.BlockSpec((1, TOP_K, tt), f, memory_space=pltpu.SMEM)
    one_blk = lambda f: pl.BlockSpec((1, 1, 1), f, memory_space=pltpu.SMEM)
    return pl.pallas_call(
        _combine_kernel,
        grid=(n_tiles,),
        in_specs=[
            idx_blk(cur), idx_blk(nxt), one_blk(cur), one_blk(nxt),
            pl.BlockSpec((tt, TOP_K), lambda i: (i, 0)),
            pl.BlockSpec(memory_space=pl.ANY),
            pl.BlockSpec(memory_space=pl.ANY),
            pl.BlockSpec(norm_w.shape, lambda i: (0, 0)),
        ],
        out_specs=pl.BlockSpec((tt, d), lambda i: (i, 0)),
        scratch_shapes=[pltpu.VMEM((2, TOP_K * nt, LANES), F32), pltpu.VMEM((2, tt, d), F32),
                        pltpu.SemaphoreType.DMA((2,))],
        out_shape=jax.ShapeDtypeStruct((n_tiles * tt, d), F32),
        compiler_params=_cparams(("arbitrary",)),
        name="combine",
    )(pos_seq, pos_seq, h_src, h_src, tw_seq, y_rows, h2, norm_w)


def _pick(n, prefs):
    for p in prefs:
        if n % p == 0:
            return p
    raise ValueError(f"no tile in {prefs} divides {n}")


def kernel(x, meta_tokens, hg_lb_logits, norm_mix, w_in, hg_norm, ml_conv_w, ml_conv_b, ml_wq, ml_wk, ml_wv,
           ml_gate_b, ml_norm, ml_skip, w_branch_hg, w_branch_ml, w_out, norm_ffn, router_w, router_b,
           exp_w_gu, exp_b_gu, exp_w_down, exp_b_down, norm_final):
    bsz, seq, d = x.shape
    assert norm_mix.shape[0] == 1, "single-layer block"
    assert seq % CHUNK == 0 and d % LANES == 0
    t = CHUNK + seq
    m_rows = bsz * t
    n_experts = router_w.shape[-1]
    dff = exp_w_down.shape[2]
    assert n_experts <= LANES

    head = jnp.concatenate([jnp.zeros((N_PAD, d), x.dtype), meta_tokens.astype(x.dtype)], axis=0)
    x2d = x.reshape(bsz * seq, d)
    lower_bounds = jnp.cumsum(jax.nn.softmax(hg_lb_logits.astype(F32), axis=0), axis=0)

    w = w_in[0]
    n_hg = 4 * HG_W
    n_a = n_hg + 2 * ML_W
    w_hg = w[:, :n_hg].astype(BF16)
    w_m = jnp.pad(w[:, n_a:n_a + 2 * N_HEADS], ((0, 0), (0, LANES - 2 * N_HEADS)))
    w_ml = jnp.concatenate([w[:, n_hg:n_a], w_m], axis=1).astype(BF16)
    w_g = w[:, n_a + 2 * N_HEADS:].astype(BF16)
    gain = norm_mix[0][None]

    bb = _pick(bsz, (8, 4, 2, 1))
    tri = jnp.asarray(np.tril(np.ones((CHUNK, CHUNK), np.float32)), BF16)
    y_hg = _hgrn2(x, head, gain, w_hg, tri, lower_bounds[0][None], hg_norm[0][None], bb)

    gb = ml_gate_b[0].astype(F32)
    gb_col = jnp.pad(gb, (0, LANES - 2 * N_HEADS))[None]
    gb_row = jnp.broadcast_to(gb[:, None], (2 * N_HEADS, CHUNK))
    y_ml = _mlstm(x, head, gain, w_ml, tri, ml_conv_w[0], ml_conv_b[0][None],
                  ml_wq[0].astype(BF16), ml_wk[0].astype(BF16), ml_wv[0].astype(BF16),
                  gb_col, gb_row, ml_norm[0][None], ml_skip[0][None], bb)

    assert n_experts % 8 == 0
    tq = _pick(t, (704, 192, 64))
    rw = router_w[0].astype(F32).T
    rw1 = rw.astype(BF16)
    rw2 = (rw - rw1.astype(F32)).astype(BF16)
    rw_split = jnp.stack([rw1, rw2])
    rb = jnp.broadcast_to(router_b[0].astype(F32)[:, None], (n_experts, tq))
    h2, xn2, top_e, top_w, rank, cnt = _merge_route(
        x2d, head, gain, w_g, y_hg.reshape(m_rows, HG_W), y_ml.reshape(m_rows, ML_W),
        w_branch_hg[0].astype(BF16), w_branch_ml[0].astype(BF16), w_out[0].astype(BF16),
        norm_ffn[0][None], rw_split, rb, n_experts, bsz, tq)

    tm6 = 512
    n_assign = bsz * (t - N_PAD) * TOP_K
    counts = cnt[:, 0].astype(jnp.int32)
    padded = ((counts + tm6 - 1) // tm6) * tm6
    pend = jnp.cumsum(padded).astype(jnp.int32)
    pstart = pend - padded
    n_blocks = -(-n_assign // tm6) + n_experts
    n_rows = n_blocks * tm6
    blk_start = jnp.arange(n_blocks, dtype=jnp.int32) * tm6
    block_e = jnp.minimum(jnp.sum((blk_start[:, None] >= pend[None, :]).astype(jnp.int32), axis=1),
                          n_experts - 1)
    n_used = (pend[-1] // tm6)[None]
    pos = rank + jnp.sum(jnp.where(top_e[..., None] == jnp.arange(n_experts, dtype=jnp.int32), pstart, 0),
                         axis=-1)
    xs = _dispatch(pstart, pend, pos, xn2, n_rows, bsz, tq, tm6)

    w_gu_t = jnp.swapaxes(exp_w_gu[0].reshape(n_experts, d // LANES, LANES, 2 * dff), 2, 3)
    bgu = exp_b_gu[0]
    y_rows = _experts(block_e, n_used, xs, w_gu_t, bgu[:, None, 0::2], bgu[:, None, 1::2],
                      exp_w_down[0], exp_b_down[0][:, None, :], tm6)

    tt = _pick(seq, (128, 64))
    tiles_per_b = seq // tt
    n_tiles = bsz * tiles_per_b
    nj = t // tq

    def seq_tiles(a):
        a = a.reshape(bsz, nj, TOP_K, tq).transpose(0, 2, 1, 3).reshape(bsz, TOP_K, t)[:, :, CHUNK:]
        return a.reshape(bsz, TOP_K, tiles_per_b, tt).transpose(0, 2, 1, 3).reshape(n_tiles, TOP_K, tt)

    h_src = (jnp.arange(bsz, dtype=jnp.int32)[:, None] * t + CHUNK
             + jnp.arange(tiles_per_b, dtype=jnp.int32)[None, :] * tt).reshape(n_tiles, 1, 1)
    tw_cols = seq_tiles(top_w).transpose(0, 2, 1).reshape(n_tiles * tt, TOP_K)
    out = _combine(seq_tiles(pos), tw_cols, h_src, y_rows, h2, norm_final[None], tt)
    return out.reshape(bsz, seq, d)
```

```python
import functools

import numpy as np
import jax
import jax.numpy as jnp
from jax import lax
from jax.experimental import pallas as pl
from jax.experimental.pallas import tpu as pltpu

F32 = jnp.float32
BF16 = jnp.bfloat16

N_META = 16
CHUNK = 64
N_PAD = CHUNK - N_META
EPS = 1e-6

N_HEADS = 4
HG_DK = 128
HG_DV = 128
HG_W = N_HEADS * HG_DV
ML_DK = 64
ML_DV = 128
ML_W = N_HEADS * ML_DV
ML_CONV = 4
TOP_K = 4
SWIGLU_LIMIT = 7.0
SWIGLU_ALPHA = 1.702

LANES = 128
VMEM_LIMIT_BYTES = 56 * 1024 * 1024

HG_LEVELS = (32, 16, 8, 4, 2, 1)
GROUP_ROWS = 8


def _cparams(sem):
    return pltpu.CompilerParams(dimension_semantics=sem, vmem_limit_bytes=VMEM_LIMIT_BYTES)


def _sigmoid(x):
    return 0.5 + 0.5 * jnp.tanh(0.5 * x)


def _split3(x):
    x1 = x.astype(BF16)
    r1 = x - x1.astype(F32)
    x2 = r1.astype(BF16)
    x3 = (r1 - x2.astype(F32)).astype(BF16)
    return x1, x2, x3


def _dot(a, b):
    return jnp.dot(a, b, preferred_element_type=F32)


TOK_ROWS = 8


def _store_token_tiles(ref, x):
    n = x.shape[0]
    for s in range(TOK_ROWS):
        ref[pl.ds(s, n, stride=TOK_ROWS), :] = x[:, s * LANES:(s + 1) * LANES]


def _load_token_tiles(ref, n):
    return jnp.concatenate([ref[pl.ds(s, n, stride=TOK_ROWS), :] for s in range(TOK_ROWS)], axis=1)


def _dot_nt(a, b):
    return lax.dot_general(a, b, (((1,), (1,)), ((), ())), preferred_element_type=F32)


def _dot_exact_lhs(m_bf16, x):
    x1, x2, x3 = _split3(x)
    return _dot(m_bf16, x1) + _dot(m_bf16, x2) + _dot(m_bf16, x3)


def _h_tile(x_ref, head_ref, j):
    x = x_ref[...]
    first = jnp.concatenate([head_ref[...], x[:x.shape[0] - CHUNK]], axis=0)
    return jnp.where(j == 0, first, x)


def _x_tile_spec(tr, d, seq):
    assert seq % 8 == 0 and tr % 8 == 0 and CHUNK % 8 == 0
    return pl.BlockSpec(
        (pl.Element(tr), pl.Element(d)),
        lambda b, j: (pl.multiple_of(b * seq + jnp.maximum(j * tr - CHUNK, 0), 8), 0))


def _rms_bf16(x, gain):
    ms = jnp.mean(x * x, axis=-1, keepdims=True)
    return ((x * lax.rsqrt(ms + EPS)) * gain).astype(BF16)


def _project_ahead(x_ref, gain_ref, w_ref, nxt_ref):
    x = x_ref[...].reshape(x_ref.shape[0] * CHUNK, x_ref.shape[2])
    nxt_ref[...] = _dot(_rms_bf16(x, gain_ref[...]), w_ref[...])


def _with_projection_pipeline(step, x_ref, head_ref, gain_ref, w_ref, proj_a, proj_b, c):
    @pl.when(c == 0)
    def _():
        p0 = _dot(_rms_bf16(head_ref[...], gain_ref[...]), w_ref[...])
        for b in range(x_ref.shape[0]):
            proj_a[b * CHUNK:(b + 1) * CHUNK, :] = p0

    for parity, (cur, nxt) in enumerate(((proj_a, proj_b), (proj_b, proj_a))):
        @pl.when(c % 2 == parity)
        def _():
            _project_ahead(x_ref, gain_ref, w_ref, nxt)
            step(cur)


def _chunk_ahead_spec(bb, d, nc):
    return pl.BlockSpec((bb, CHUNK, d), lambda i, c: (i, jnp.minimum(c, nc - 2), 0))


def _hgrn2_kernel(x_ref, head_ref, gain_ref, w_ref, tri_ref, lb_ref, nw_ref, y_ref, proj_a, proj_b, *st_refs):
    c = pl.program_id(1)

    @pl.when(c == 0)
    def _():
        for st_ref in st_refs:
            st_ref[...] = jnp.zeros_like(st_ref)

    def step(cur):
        bb = x_ref.shape[0]
        for b0 in range(0, bb, GROUP_ROWS):
            _hgrn2_group(c, range(b0, min(b0 + GROUP_ROWS, bb)), tri_ref, lb_ref, nw_ref, y_ref, st_refs, cur)

    _with_projection_pipeline(step, x_ref, head_ref, gain_ref, w_ref, proj_a, proj_b, c)


def _hgrn2_group(c, bs, tri_ref, lb_ref, nw_ref, y_ref, st_refs, cur):
    row = lax.broadcasted_iota(jnp.int32, (CHUNK, 1), 0)
    valid = (c * CHUNK + row) >= N_PAD
    ti = lax.broadcasted_iota(jnp.int32, (CHUNK, CHUNK), 0)
    si = lax.broadcasted_iota(jnp.int32, (CHUNK, CHUNK), 1)
    diag_mask = ti == si
    level_masks = {}
    for m in HG_LEVELS:
        same_pair = (ti & ~(2 * m - 1)) == (si & ~(2 * m - 1))
        level_masks[m] = same_pair & ((ti & m) != 0) & ((si & m) == 0)

    lb = lb_ref[...]
    tri = tri_ref[...]
    per_b = {}
    for b in bs:
        rows_b = slice(b * CHUNK, (b + 1) * CHUNK)
        hf = cur[rows_b, HG_W:2 * HG_W]
        f = lb + (1.0 - lb) * _sigmoid(hf)
        f = jnp.where(valid, f, 1.0)
        logf = jnp.log(f)
        k_all = 1.0 - f
        hq = cur[rows_b, 0:HG_W]
        q_all = hq * _sigmoid(hq)
        b_cum = _dot_exact_lhs(tri, logf)
        e_b = jnp.exp(b_cum)
        e_bl = jnp.exp(b_cum[CHUNK - 1:CHUNK] - b_cum)

        q_fac, k_fac = {}, {}
        for m in HG_LEVELS:
            if m >= 4:
                grp = b_cum.reshape(CHUNK // (2 * m), 2 * m, HG_W)
                e = (grp - grp[:, m - 1:m, :]).reshape(CHUNK, HG_W)
                q_fac[m] = jnp.exp(jnp.minimum(e, 0.0))
                k_fac[m] = jnp.exp(jnp.minimum(-e, 0.0))
        f_prev = pltpu.roll(f, 1, 0)
        f_next = pltpu.roll(f, CHUNK - 1, 0)
        r4 = row & 3
        q_fac[2] = jnp.where(r4 == 2, f, jnp.where(r4 == 3, f * f_prev, 1.0))
        k_fac[2] = jnp.where(r4 == 0, f_next, 1.0)
        q_fac[1] = jnp.where((row & 1) == 1, f, 1.0)

        hg = cur[rows_b, 3 * HG_W:4 * HG_W]
        v_all = cur[rows_b, 2 * HG_W:3 * HG_W]
        q_fac = {m: v.astype(BF16) for m, v in q_fac.items()}
        k_fac = {m: v.astype(BF16) for m, v in k_fac.items()}
        per_b[b] = (q_all, k_all, q_fac, k_fac, e_b, e_bl, v_all, hg * _sigmoid(hg))

    pairs = [(b, h) for b in bs for h in range(N_HEADS)]
    sls = [slice(h * HG_DK, (h + 1) * HG_DK) for h in range(N_HEADS)]

    scores_all = []
    for b, h in pairs:
        q_all, k_all, q_fac, k_fac = per_b[b][:4]
        qb = q_all[:, sls[h]].astype(BF16)
        kb = k_all[:, sls[h]].astype(BF16)
        scores = jnp.where(diag_mask, _dot_nt(qb, kb), 0.0)
        for m in HG_LEVELS:
            qd = qb * q_fac[m][:, sls[h]]
            kd = kb * k_fac[m][:, sls[h]] if m in k_fac else kb
            scores = jnp.where(level_masks[m], _dot_nt(qd, kd), scores)
        scores_all.append(scores.astype(BF16))

    sts = [st_refs[b * N_HEADS + h][...] for b, h in pairs]
    outs = []
    for p, (b, h) in enumerate(pairs):
        q_all, e_b, v_all = per_b[b][0], per_b[b][4], per_b[b][6]
        qe = (q_all[:, sls[h]] * e_b[:, sls[h]]).astype(BF16)
        outs.append(_dot(scores_all[p], v_all[:, sls[h]].astype(BF16)) + _dot_nt(qe, sts[p].astype(BF16)))

    for p, (b, h) in enumerate(pairs):
        k_all, e_b, e_bl, v_all = per_b[b][1], per_b[b][4], per_b[b][5], per_b[b][6]
        kl = (k_all[:, sls[h]] * e_bl[:, sls[h]]).astype(BF16)
        vt = v_all[:, sls[h]].T.astype(BF16)
        st_refs[b * N_HEADS + h][...] = e_b[CHUNK - 1:CHUNK, sls[h]] * sts[p] + _dot(vt, kl)

    for p, (b, h) in enumerate(pairs):
        o = outs[p]
        ms = jnp.mean(o * o, axis=-1, keepdims=True)
        y = (o * lax.rsqrt(ms + EPS)) * nw_ref[:, sls[h]] * per_b[b][7][:, sls[h]]
        y_ref[b, :, sls[h]] = y.astype(y_ref.dtype)


def _hgrn2(x3, head, gain, w_hg, tri, lb, norm_w, bb):
    bsz, seq, d = x3.shape
    nc = seq // CHUNK + 1
    blk = pl.BlockSpec((bb, CHUNK, HG_W), lambda i, c: (i, c, 0))
    full = lambda a: pl.BlockSpec(a.shape, lambda i, c: (0, 0))
    return pl.pallas_call(
        _hgrn2_kernel,
        grid=(bsz // bb, nc),
        in_specs=[_chunk_ahead_spec(bb, d, nc), full(head), full(gain), full(w_hg), full(tri), full(lb),
                  full(norm_w)],
        out_specs=blk,
        out_shape=jax.ShapeDtypeStruct((bsz, nc * CHUNK, HG_W), BF16),
        scratch_shapes=([pltpu.VMEM((bb * CHUNK, w_hg.shape[1]), F32)] * 2
                        + [pltpu.VMEM((HG_DV, HG_DK), F32)] * (bb * N_HEADS)),
        compiler_params=_cparams(("parallel", "arbitrary")),
        name="hgrn2",
    )(x3, head, gain, w_hg, tri, lb, norm_w)


def _log_sigmoid(x):
    return jnp.minimum(x, 0.0) - jnp.log(1.0 + jnp.exp(-jnp.abs(x)))


def _mlstm_kernel(x_ref, head_ref, gain_ref, w_ref, tri_ref, cw_ref, cb_ref, wq_ref, wk_ref, wv_ref,
                  gbc_ref, gbr_ref, nw_ref, sk_ref, y_ref, proj_a, proj_b, *scratch):
    c = pl.program_id(1)

    @pl.when(c == 0)
    def _():
        for ref in scratch:
            ref[...] = jnp.zeros_like(ref)

    step = functools.partial(_mlstm_step, c, x_ref.shape[0], tri_ref, cw_ref, cb_ref, wq_ref, wk_ref, wv_ref,
                             gbc_ref, gbr_ref, nw_ref, sk_ref, y_ref, scratch)
    _with_projection_pipeline(step, x_ref, head_ref, gain_ref, w_ref, proj_a, proj_b, c)


def _mlstm_step(c, bb, tri_ref, cw_ref, cb_ref, wq_ref, wk_ref, wv_ref, gbc_ref, gbr_ref, nw_ref, sk_ref, y_ref,
                scratch, cur):
    n_pairs = bb * N_HEADS
    s_refs = scratch[:n_pairs]
    m_refs = scratch[n_pairs:2 * n_pairs]
    tail_refs = scratch[2 * n_pairs:]

    pos_c = c * CHUNK + lax.broadcasted_iota(jnp.int32, (CHUNK, 1), 0)
    valid_c = pos_c >= N_PAD
    pos_r = c * CHUNK + lax.broadcasted_iota(jnp.int32, (1, CHUNK), 1)
    valid_r = pos_r >= N_PAD
    ti = lax.broadcasted_iota(jnp.int32, (CHUNK, CHUNK), 0)
    si = lax.broadcasted_iota(jnp.int32, (CHUNK, CHUNK), 1)
    causal = si <= ti
    tri = tri_ref[...]
    ones_v = jnp.ones((CHUNK, ML_DV), BF16)
    neg_inf = -jnp.inf

    per_b = []
    for b in range(bb):
        rows_b = slice(b * CHUNK, (b + 1) * CHUNK)
        mm = jnp.where(valid_c, cur[rows_b, 0:ML_W], 0.0)
        ext = jnp.concatenate([tail_refs[b][...], mm], axis=0)
        tail_refs[b][...] = mm[CHUNK - 8:CHUNK]
        conv = cb_ref[...]
        for j in range(ML_CONV):
            off = 8 - (ML_CONV - 1) + j
            conv = conv + cw_ref[j:j + 1, :] * ext[off:off + CHUNK]
        cact = conv * _sigmoid(conv)
        cact_b = cact.astype(BF16)
        mm_b = mm.astype(BF16)

        graw = cur[rows_b, 2 * ML_W:2 * ML_W + LANES]
        gcol = graw + gbc_ref[...]
        li_col = jnp.where(valid_c, gcol, neg_inf)
        lf_col = jnp.where(valid_c, _log_sigmoid(gcol), 0.0)
        b_col = _dot_exact_lhs(tri, lf_col)
        grow = graw.T[0:2 * N_HEADS] + gbr_ref[...]
        li_row = jnp.where(valid_r, grow, neg_inf)
        lf_row = jnp.where(valid_r, _log_sigmoid(grow), 0.0)
        r1, r2, r3 = _split3(lf_row)
        b_row = _dot_nt(r1, tri) + _dot_nt(r2, tri) + _dot_nt(r3, tri)

        ogate = _sigmoid(cur[rows_b, ML_W:2 * ML_W])
        per_b.append((cact, cact_b, mm_b, li_col, b_col, li_row, b_row, ogate))

    pairs = [(b, h) for b in range(bb) for h in range(N_HEADS)]
    sls = [slice(h * ML_DV, (h + 1) * ML_DV) for h in range(N_HEADS)]

    qs, ks, vs = [], [], []
    for b, h in pairs:
        cact_b, mm_b = per_b[b][1], per_b[b][2]
        qs.append((_dot(cact_b[:, sls[h]], wq_ref[h]) * (ML_DK ** -0.5)).astype(BF16))
        ks.append(_dot(cact_b[:, sls[h]], wk_ref[h]))
        v = _dot(mm_b[:, sls[h]], wv_ref[h]).astype(BF16)
        vs.append(jnp.concatenate([v, ones_v], axis=1))

    n_p = len(pairs)
    blk = lambda p: slice(p * CHUNK, (p + 1) * CHUNK)
    stack = lambda xs: jnp.concatenate(xs, axis=0)
    bc_all = stack([per_b[b][4][:, N_HEADS + h:N_HEADS + h + 1] for b, h in pairs])
    lic_all = stack([per_b[b][3][:, h:h + 1] for b, h in pairs])
    row_all = stack([jnp.broadcast_to(per_b[b][5][h:h + 1, :] - per_b[b][6][N_HEADS + h:N_HEADS + h + 1, :],
                                      (CHUNK, CHUNK)) for b, h in pairs])
    mprev_all = stack([jnp.broadcast_to(m_refs[p][0:1, 0:1], (CHUNK, 1)) for p in range(n_p)])
    glast_all = stack([jnp.broadcast_to(per_b[b][4][CHUNK - 1:CHUNK, N_HEADS + h:N_HEADS + h + 1], (CHUNK, 1))
                       for b, h in pairs])
    causal_all = stack([causal] * n_p)

    d_all = jnp.where(causal_all, bc_all + row_all, neg_inf)
    a_all = bc_all + mprev_all
    m_t_all = jnp.maximum(a_all, jnp.max(d_all, axis=-1, keepdims=True))
    w_intra_all = jnp.exp(d_all - m_t_all)
    w_inter_all = jnp.exp(a_all - m_t_all)

    qk_all = (stack([_dot_nt(qs[p], ks[p].astype(BF16)) for p in range(n_p)]) * w_intra_all).astype(BF16)
    s_augs = [s_refs[p][...] for p in range(n_p)]
    intra_all = stack([_dot(qk_all[blk(p)], vs[p]) for p in range(n_p)])
    inter_all = stack([_dot(qs[p], s_augs[p].astype(BF16)) for p in range(n_p)])
    numden_all = intra_all + w_inter_all * inter_all
    o_all = numden_all[:, :ML_DV] / jnp.maximum(jnp.abs(numden_all[:, ML_DV:]), jnp.exp(-m_t_all))
    ms_all = jnp.mean(o_all * o_all, axis=-1, keepdims=True)
    on_all = o_all * lax.rsqrt(ms_all + EPS)

    e_all = glast_all - bc_all + lic_all
    gm_all = glast_all + mprev_all
    e_max = jnp.max(e_all.reshape(n_p, CHUNK, 1), axis=1, keepdims=True)
    m_new_all = jnp.maximum(gm_all.reshape(n_p, CHUNK, 1), e_max).reshape(n_p * CHUNK, 1)
    w_s_all = jnp.exp(e_all - m_new_all)
    w_p_all = jnp.exp(gm_all - m_new_all)
    kw_all = stack(ks) * w_s_all
    for p in range(n_p):
        kw_t = kw_all[blk(p)].T.astype(BF16)
        s_refs[p][...] = w_p_all[p * CHUNK:p * CHUNK + 1] * s_augs[p] + _dot(kw_t, vs[p])
        m_refs[p][...] = jnp.broadcast_to(m_new_all[p * CHUNK:p * CHUNK + 1], m_refs[p].shape)

    for p, (b, h) in enumerate(pairs):
        cact, ogate = per_b[b][0], per_b[b][7]
        sl = sls[h]
        y = (on_all[blk(p)] * nw_ref[:, sl] + sk_ref[:, sl] * cact[:, sl]) * ogate[:, sl]
        y_ref[b, :, sl] = y.astype(y_ref.dtype)


def _mlstm(x3, head, gain, w_ml, tri, conv_w, conv_b, wq, wk, wv, gb_col, gb_row, norm_w, skip, bb):
    bsz, seq, d = x3.shape
    nc = seq // CHUNK + 1
    blk = pl.BlockSpec((bb, CHUNK, ML_W), lambda i, c: (i, c, 0))

    def full(a):
        nd = a.ndim
        return pl.BlockSpec(a.shape, lambda i, c: (0,) * nd)

    params = (head, gain, w_ml, tri, conv_w, conv_b, wq, wk, wv, gb_col, gb_row, norm_w, skip)
    return pl.pallas_call(
        _mlstm_kernel,
        grid=(bsz // bb, nc),
        in_specs=[_chunk_ahead_spec(bb, d, nc)] + [full(p) for p in params],
        out_specs=blk,
        out_shape=jax.ShapeDtypeStruct((bsz, nc * CHUNK, ML_W), BF16),
        scratch_shapes=([pltpu.VMEM((bb * CHUNK, w_ml.shape[1]), F32)] * 2
                        + [pltpu.VMEM((ML_DK, 2 * ML_DV), F32)] * (bb * N_HEADS)
                        + [pltpu.VMEM((8, LANES), F32)] * (bb * N_HEADS)
                        + [pltpu.VMEM((8, ML_W), F32)] * bb),
        compiler_params=_cparams(("parallel", "arbitrary")),
        name="mlstm",
    )(x3, *params)


def _merge_route_kernel(x_ref, head_ref, gain_ref, wg_ref, yh_ref, ym_ref, wbh_ref, wbm_ref, wo_ref, nf_ref,
                        rw_ref, rb_ref, h2_ref, xn_ref, te_ref, tw_ref, rk_ref, cnt_ref, *, n_experts):
    d = x_ref.shape[1]
    tq = x_ref.shape[0]
    j = pl.program_id(1)

    @pl.when(jnp.logical_and(pl.program_id(0) == 0, j == 0))
    def _():
        cnt_ref[...] = jnp.zeros_like(cnt_ref)

    h = _h_tile(x_ref, head_ref, j)
    hn = _rms_bf16(h, gain_ref[...])
    g0 = _sigmoid(_dot(hn, wg_ref[:, :d]))
    g1 = _sigmoid(_dot(hn, wg_ref[:, d:]))
    merged = g0 * _dot(yh_ref[...], wbh_ref[...]) + g1 * _dot(ym_ref[...], wbm_ref[...])
    h2 = h + _dot(merged.astype(BF16), wo_ref[...])
    h2_ref[...] = h2
    ms = jnp.mean(h2 * h2, axis=-1, keepdims=True)
    xn = (h2 * lax.rsqrt(ms + EPS)) * nf_ref[...]
    _store_token_tiles(xn_ref, xn)
    x1, x2, _ = _split3(xn)
    logits = (_dot_nt(rw_ref[0], x1) + _dot_nt(rw_ref[1], x1) + _dot_nt(rw_ref[0], x2)) + rb_ref[...]
    sub = lax.broadcasted_iota(jnp.int32, logits.shape, 0)
    work = logits
    vals, idxs = [], []
    for _ in range(TOP_K):
        vmax = jnp.max(work, axis=0, keepdims=True)
        imax = jnp.min(jnp.where(work == vmax, sub, n_experts), axis=0, keepdims=True)
        vals.append(vmax)
        idxs.append(imax)
        work = jnp.where(sub == imax, -jnp.inf, work)
    exps = [jnp.exp(v - vals[0]) for v in vals]
    tot = exps[0] + exps[1] + exps[2] + exps[3]

    krow = lax.broadcasted_iota(jnp.int32, (TOP_K, tq), 0)

    def rows(per_k):
        out = jnp.broadcast_to(per_k[0], (TOP_K, tq))
        for kk in range(1, TOP_K):
            out = jnp.where(krow == kk, per_k[kk], out)
        return out

    te_ref[0] = rows(idxs)
    tw_ref[0] = rows([e / tot for e in exps])

    valid = (j * tq + lax.broadcasted_iota(jnp.int32, (1, tq), 1)) >= N_PAD
    onehots = [jnp.where(jnp.logical_and(sub == idxs[kk], valid), 1.0, 0.0) for kk in range(TOP_K)]
    oh_all = onehots[0] + onehots[1] + onehots[2] + onehots[3]
    ri = lax.broadcasted_iota(jnp.int32, (tq, tq), 0)
    ci = lax.broadcasted_iota(jnp.int32, (tq, tq), 1)
    earlier = jnp.where(ri < ci, 1.0, 0.0).astype(BF16)
    cnt = cnt_ref[:, 0:1]
    before = _dot(oh_all.astype(BF16), earlier) + cnt
    ranks = []
    for kk in range(TOP_K):
        ranks.append(jnp.sum(jnp.where(sub == idxs[kk], before, 0.0), axis=0, keepdims=True))
        before = before + onehots[kk]
    rk_ref[0] = rows(ranks).astype(jnp.int32)
    cnt_ref[...] = jnp.broadcast_to(cnt + jnp.sum(oh_all, axis=1, keepdims=True), cnt_ref.shape)


def _merge_route(x2d, head, gain, w_g, y_hg, y_ml, wbh, wbm, wo, norm_ffn, rw_split, rb, n_experts, bsz, tq):
    d = x2d.shape[1]
    seq = x2d.shape[0] // bsz
    nj = (seq + CHUNK) // tq
    m = bsz * nj * tq
    assert d == TOK_ROWS * LANES, "token-tile layout assumes one (8,128) tile per token"
    row = lambda n: pl.BlockSpec((tq, n), lambda b, j: (b * nj + j, 0))
    tiles = pl.BlockSpec((tq * TOK_ROWS, LANES), lambda b, j: (b * nj + j, 0))
    per_tok = pl.BlockSpec((1, TOP_K, tq), lambda b, j: (b * nj + j, 0, 0))

    def full(a):
        nd = a.ndim
        return pl.BlockSpec(a.shape, lambda b, j: (0,) * nd)

    return pl.pallas_call(
        functools.partial(_merge_route_kernel, n_experts=n_experts),
        grid=(bsz, nj),
        in_specs=[_x_tile_spec(tq, d, seq), full(head), full(gain), full(w_g), row(HG_W), row(ML_W),
                  full(wbh), full(wbm), full(wo), full(norm_ffn), full(rw_split), full(rb)],
        out_specs=[row(d), tiles, per_tok, per_tok, per_tok,
                   pl.BlockSpec((n_experts, LANES), lambda b, j: (0, 0))],
        out_shape=[
            jax.ShapeDtypeStruct((m, d), F32),
            jax.ShapeDtypeStruct((m * TOK_ROWS, LANES), F32),
            jax.ShapeDtypeStruct((bsz * nj, TOP_K, tq), jnp.int32),
            jax.ShapeDtypeStruct((bsz * nj, TOP_K, tq), F32),
            jax.ShapeDtypeStruct((bsz * nj, TOP_K, tq), jnp.int32),
            jax.ShapeDtypeStruct((n_experts, LANES), F32),
        ],
        compiler_params=_cparams(("arbitrary", "arbitrary")),
        name="merge_route",
    )(x2d, head, gain, w_g, y_hg, y_ml, wbh, wbm, wo, norm_ffn, rw_split, rb)


def _toks(first, n=1):
    return pl.ds(pl.multiple_of(first * TOK_ROWS, TOK_ROWS), n * TOK_ROWS)


def _dispatch_kernel(ps_ref, pe_ref, pos_ref, x_ref, xs_hbm, zbuf, sem, *, n_experts, tm):
    b = pl.program_id(0)
    j = pl.program_id(1)
    tq = x_ref.shape[0] // TOK_ROWS

    @pl.when(jnp.logical_and(b == 0, j == 0))
    def _():
        zbuf[...] = jnp.zeros_like(zbuf)
        for e in range(n_experts):
            @pl.when(pe_ref[e] > ps_ref[e])
            def _():
                pltpu.make_async_copy(zbuf, xs_hbm.at[_toks(pe_ref[e] - tm, tm), :], sem).start()
        for e in range(n_experts):
            @pl.when(pe_ref[e] > ps_ref[e])
            def _():
                pltpu.make_async_copy(zbuf, xs_hbm.at[_toks(0, tm), :], sem).wait()

        def zero_tail(blk, carry):
            cp = pltpu.make_async_copy(zbuf, xs_hbm.at[_toks(blk * tm, tm), :], sem)
            cp.start()
            cp.wait()
            return carry
        lax.fori_loop(pe_ref[n_experts - 1] // tm, xs_hbm.shape[0] // (tm * TOK_ROWS), zero_tail, 0)

    def scatter_rows(lo):
        def body(r, carry):
            for kk in range(TOP_K):
                dst = pos_ref[0, kk, r]
                pltpu.make_async_copy(
                    x_ref.at[_toks(r), :], xs_hbm.at[_toks(dst), :], sem).start(priority=kk % 2)
            return carry
        lax.fori_loop(lo, tq, body, 0, unroll=2)
        n = tq - lo
        for _ in range(TOP_K):
            pltpu.make_async_copy(x_ref.at[_toks(0, n), :], xs_hbm.at[_toks(0, n), :], sem).wait()

    @pl.when(j == 0)
    def _():
        scatter_rows(N_PAD)

    @pl.when(j != 0)
    def _():
        scatter_rows(0)


def _dispatch(pstart, pend, pos, xn_t, n_rows, bsz, tq, tm):
    m = xn_t.shape[0] // TOK_ROWS
    nj = m // (bsz * tq)
    n_experts = pstart.shape[0]
    smem_blk = pl.BlockSpec((1, TOP_K, tq), lambda b, j, ps, pe: (b * nj + j, 0, 0),
                            memory_space=pltpu.SMEM)
    grid_spec = pltpu.PrefetchScalarGridSpec(
        num_scalar_prefetch=2,
        grid=(bsz, nj),
        in_specs=[smem_blk,
                  pl.BlockSpec((tq * TOK_ROWS, LANES), lambda b, j, ps, pe: (b * nj + j, 0))],
        out_specs=pl.BlockSpec(memory_space=pl.ANY),
        scratch_shapes=[pltpu.VMEM((tm * TOK_ROWS, LANES), F32), pltpu.SemaphoreType.DMA(())],
    )
    return pl.pallas_call(
        functools.partial(_dispatch_kernel, n_experts=n_experts, tm=tm),
        grid_spec=grid_spec,
        out_shape=jax.ShapeDtypeStruct((n_rows * TOK_ROWS, LANES), F32),
        compiler_params=_cparams(("arbitrary", "arbitrary")),
        name="dispatch",
    )(pstart, pend, pos, xn_t)


CAST_ROWS = 256


def _experts_kernel(be_ref, nu_ref, slot_ref, nxt_ref, x_ref, wgu_hbm, bg_ref, bu_ref, wd_hbm, bd_ref, y_ref,
                    wg_s, wu_s, wd_s, wgu_buf, wd_buf, sem):
    i = pl.program_id(0)
    n_used = nu_ref[0]
    dff = wg_s.shape[0]
    new_expert = jnp.logical_or(i == 0, be_ref[i] != be_ref[jnp.maximum(i - 1, 0)])

    def weight_copies(e, slot):
        return (pltpu.make_async_copy(wgu_hbm.at[e], wgu_buf.at[slot], sem.at[slot]),
                pltpu.make_async_copy(wd_hbm.at[e], wd_buf.at[slot], sem.at[slot]))

    @pl.when(i == 0)
    def _():
        for cp in weight_copies(be_ref[0], 0):
            cp.start()

    for slot in range(2):
        @pl.when(jnp.logical_and(jnp.logical_and(new_expert, i < n_used), slot_ref[i] == slot))
        def _():
            for cp in weight_copies(be_ref[i], slot):
                cp.wait()

            @pl.when(nxt_ref[i] >= 0)
            def _():
                for cp in weight_copies(nxt_ref[i], 1 - slot):
                    cp.start()

            for c in range(dff // CAST_ROWS):
                rows = pl.ds(c * CAST_ROWS, CAST_ROWS)
                wd_s[rows, :] = wd_buf[slot, rows, :].astype(BF16)
                for sl in range(wgu_buf.shape[1]):
                    lanes = pl.ds(sl * LANES, LANES)
                    wg_s[rows, lanes] = wgu_buf[
                        slot, sl, pl.ds(2 * c * CAST_ROWS, CAST_ROWS, stride=2), :].astype(BF16)
                    wu_s[rows, lanes] = wgu_buf[
                        slot, sl, pl.ds(2 * c * CAST_ROWS + 1, CAST_ROWS, stride=2), :].astype(BF16)

    @pl.when(i < n_used)
    def _():
        tm = x_ref.shape[0] // TOK_ROWS
        xb = _load_token_tiles(x_ref, tm).astype(BF16)
        g = _dot_nt(xb, wg_s[...]) + bg_ref[0]
        u = _dot_nt(xb, wu_s[...]) + bu_ref[0]
        gate = jnp.minimum(g, SWIGLU_LIMIT)
        up = jnp.clip(u, -SWIGLU_LIMIT, SWIGLU_LIMIT)
        act = (up + 1.0) * gate * _sigmoid(SWIGLU_ALPHA * gate)
        _store_token_tiles(y_ref, _dot(act.astype(BF16), wd_s[...]) + bd_ref[0])

    @pl.when(i >= n_used)
    def _():
        y_ref[...] = jnp.zeros_like(y_ref)


def _experts(block_e, n_used, xs, w_gu_t, b_g, b_u, w_d, b_d, tm):
    n_blocks = block_e.shape[0]
    dff, d = w_d.shape[1:]
    tile_blk = lambda f: pl.BlockSpec((tm * TOK_ROWS, LANES), f)
    assert dff % CAST_ROWS == 0
    blk = jnp.arange(n_blocks, dtype=jnp.int32)
    changed = jnp.concatenate([jnp.zeros((1,), jnp.int32), (block_e[1:] != block_e[:-1]).astype(jnp.int32)])
    w_slot = jnp.cumsum(changed) % 2
    later = jnp.logical_and(jnp.logical_and(blk[None, :] > blk[:, None], blk[None, :] < n_used[0]),
                            block_e[None, :] != block_e[:, None])
    nxt_e = jnp.where(jnp.any(later, axis=1), block_e[jnp.argmax(later, axis=1)], -1).astype(jnp.int32)
    wspec = lambda k, n: pl.BlockSpec((1, k, n), lambda i, be, nu, ws, ne: (be[i], 0, 0))
    grid_spec = pltpu.PrefetchScalarGridSpec(
        num_scalar_prefetch=4,
        grid=(n_blocks,),
        in_specs=[
            tile_blk(lambda i, be, nu, ws, ne: (jnp.minimum(i, nu[0] - 1), 0)),
            pl.BlockSpec(memory_space=pl.ANY),
            wspec(1, dff), wspec(1, dff),
            pl.BlockSpec(memory_space=pl.ANY),
            wspec(1, d),
        ],
        out_specs=tile_blk(lambda i, be, nu, ws, ne: (i, 0)),
        scratch_shapes=[pltpu.VMEM((dff, d), BF16), pltpu.VMEM((dff, d), BF16), pltpu.VMEM((dff, d), BF16),
                        pltpu.VMEM((2,) + w_gu_t.shape[1:], F32), pltpu.VMEM((2, dff, d), F32),
                        pltpu.SemaphoreType.DMA((2,))],
    )
    return pl.pallas_call(
        _experts_kernel,
        grid_spec=grid_spec,
        out_shape=jax.ShapeDtypeStruct((n_blocks * tm * TOK_ROWS, LANES), F32),
        compiler_params=_cparams(("arbitrary",)),
        name="experts",
    )(block_e, n_used, w_slot, nxt_e, xs, w_gu_t, b_g, b_u, w_d, b_d)


def _combine_kernel(pos_ref, posn_ref, hsrc_ref, hsrcn_ref, tw_ref, y_hbm, h_hbm,
                    nw_ref, o_ref, ybuf, hbuf, sem):
    i = pl.program_id(0)
    n = pl.num_programs(0)
    tt = o_ref.shape[0]
    nt = tt * TOK_ROWS

    def copies(src_pos_ref, src_h_ref, dst_slot, r):
        return [pltpu.make_async_copy(
            y_hbm.at[_toks(src_pos_ref[0, kk, r]), :],
            ybuf.at[dst_slot, _toks(kk * tt + r), :],
            sem.at[dst_slot]) for kk in range(TOP_K)]

    def start_h(src_h_ref, dst_slot):
        h_row = pl.multiple_of(src_h_ref[0, 0, 0], 8)
        pltpu.make_async_copy(h_hbm.at[pl.ds(h_row, tt), :], hbuf.at[dst_slot], sem.at[dst_slot]).start()

    @pl.when(i == 0)
    def _():
        def body(r, carry):
            for kk, cp in enumerate(copies(pos_ref, hsrc_ref, 0, r)):
                cp.start(priority=kk % 2)
            return carry
        lax.fori_loop(0, tt, body, 0, unroll=2)
        start_h(hsrc_ref, 0)

    def wait_slot(s):
        pltpu.make_async_copy(ybuf.at[s], ybuf.at[s], sem.at[s]).wait()
        pltpu.make_async_copy(hbuf.at[s], hbuf.at[s], sem.at[s]).wait()

    def step(slot):
        for r in range(tt):
            for kk, cp in enumerate(copies(posn_ref, hsrcn_ref, 1 - slot, r)):
                cp.start(priority=kk % 2)
        start_h(hsrcn_ref, 1 - slot)

        wait_slot(slot)
        acc = hbuf[slot]
        tw = tw_ref[...]
        for kk in range(TOP_K):
            yk = jnp.concatenate(
                [ybuf[slot, pl.ds(kk * nt + s, tt, stride=TOK_ROWS), :] for s in range(TOK_ROWS)], axis=1)
            acc = acc + tw[:, kk:kk + 1] * yk
        ms = jnp.mean(acc * acc, axis=-1, keepdims=True)
        o_ref[...] = (acc * lax.rsqrt(ms + EPS)) * nw_ref[...]

        @pl.when(i == n - 1)
        def _():
            wait_slot(1 - slot)

    for parity in range(2):
        pl.when(i % 2 == parity)(functools.partial(step, parity))


def _combine(pos_seq, tw_seq, h_src, y_rows, h2, norm_w, tt):
    n_tiles = h_src.shape[0]
    d = h2.shape[1]
    last = n_tiles - 1
    nt = tt * TOK_ROWS
    cur = lambda i: (i, 0, 0)
    nxt = lambda i: (jnp.minimum(i + 1, last), 0, 0)
    idx_blk = lambda f: pl.BlockSpec((1, TOP_K, tt), f, memory_space=pltpu.SMEM)
    one_blk = lambda f: pl.BlockSpec((1, 1, 1), f, memory_space=pltpu.SMEM)
    return pl.pallas_call(
        _combine_kernel,
        grid=(n_tiles,),
        in_specs=[
            idx_blk(cur), idx_blk(nxt), one_blk(cur), one_blk(nxt),
            pl.BlockSpec((tt, TOP_K), lambda i: (i, 0)),
            pl.BlockSpec(memory_space=pl.ANY),
            pl.BlockSpec(memory_space=pl.ANY),
            pl.BlockSpec(norm_w.shape, lambda i: (0, 0)),
        ],
        out_specs=pl.BlockSpec((tt, d), lambda i: (i, 0)),
        scratch_shapes=[pltpu.VMEM((2, TOP_K * nt, LANES), F32), pltpu.VMEM((2, tt, d), F32),
                        pltpu.SemaphoreType.DMA((2,))],
        out_shape=jax.ShapeDtypeStruct((n_tiles * tt, d), F32),
        compiler_params=_cparams(("arbitrary",)),
        name="combine",
    )(pos_seq, pos_seq, h_src, h_src, tw_seq, y_rows, h2, norm_w)


def _pick(n, prefs):
    for p in prefs:
        if n % p == 0:
            return p
    raise ValueError(f"no tile in {prefs} divides {n}")


def kernel(x, meta_tokens, hg_lb_logits, norm_mix, w_in, hg_norm, ml_conv_w, ml_conv_b, ml_wq, ml_wk, ml_wv,
           ml_gate_b, ml_norm, ml_skip, w_branch_hg, w_branch_ml, w_out, norm_ffn, router_w, router_b,
           exp_w_gu, exp_b_gu, exp_w_down, exp_b_down, norm_final):
    bsz, seq, d = x.shape
    assert norm_mix.shape[0] == 1, "single-layer block"
    assert seq % CHUNK == 0 and d % LANES == 0
    t = CHUNK + seq
    m_rows = bsz * t
    n_experts = router_w.shape[-1]
    dff = exp_w_down.shape[2]
    assert n_experts <= LANES

    head = jnp.concatenate([jnp.zeros((N_PAD, d), x.dtype), meta_tokens.astype(x.dtype)], axis=0)
    x2d = x.reshape(bsz * seq, d)
    lower_bounds = jnp.cumsum(jax.nn.softmax(hg_lb_logits.astype(F32), axis=0), axis=0)

    w = w_in[0]
    n_hg = 4 * HG_W
    n_a = n_hg + 2 * ML_W
    w_hg = w[:, :n_hg].astype(BF16)
    w_m = jnp.pad(w[:, n_a:n_a + 2 * N_HEADS], ((0, 0), (0, LANES - 2 * N_HEADS)))
    w_ml = jnp.concatenate([w[:, n_hg:n_a], w_m], axis=1).astype(BF16)
    w_g = w[:, n_a + 2 * N_HEADS:].astype(BF16)
    gain = norm_mix[0][None]

    bb = _pick(bsz, (8, 4, 2, 1))
    tri = jnp.asarray(np.tril(np.ones((CHUNK, CHUNK), np.float32)), BF16)
    y_hg = _hgrn2(x, head, gain, w_hg, tri, lower_bounds[0][None], hg_norm[0][None], bb)

    gb = ml_gate_b[0].astype(F32)
    gb_col = jnp.pad(gb, (0, LANES - 2 * N_HEADS))[None]
    gb_row = jnp.broadcast_to(gb[:, None], (2 * N_HEADS, CHUNK))
    y_ml = _mlstm(x, head, gain, w_ml, tri, ml_conv_w[0], ml_conv_b[0][None],
                  ml_wq[0].astype(BF16), ml_wk[0].astype(BF16), ml_wv[0].astype(BF16),
                  gb_col, gb_row, ml_norm[0][None], ml_skip[0][None], bb)

    assert n_experts % 8 == 0
    tq = _pick(t, (704, 192, 64))
    rw = router_w[0].astype(F32).T
    rw1 = rw.astype(BF16)
    rw2 = (rw - rw1.astype(F32)).astype(BF16)
    rw_split = jnp.stack([rw1, rw2])
    rb = jnp.broadcast_to(router_b[0].astype(F32)[:, None], (n_experts, tq))
    h2, xn2, top_e, top_w, rank, cnt = _merge_route(
        x2d, head, gain, w_g, y_hg.reshape(m_rows, HG_W), y_ml.reshape(m_rows, ML_W),
        w_branch_hg[0].astype(BF16), w_branch_ml[0].astype(BF16), w_out[0].astype(BF16),
        norm_ffn[0][None], rw_split, rb, n_experts, bsz, tq)

    tm6 = 512
    n_assign = bsz * (t - N_PAD) * TOP_K
    counts = cnt[:, 0].astype(jnp.int32)
    padded = ((counts + tm6 - 1) // tm6) * tm6
    pend = jnp.cumsum(padded).astype(jnp.int32)
    pstart = pend - padded
    n_blocks = -(-n_assign // tm6) + n_experts
    n_rows = n_blocks * tm6
    blk_start = jnp.arange(n_blocks, dtype=jnp.int32) * tm6
    block_e = jnp.minimum(jnp.sum((blk_start[:, None] >= pend[None, :]).astype(jnp.int32), axis=1),
                          n_experts - 1)
    n_used = (pend[-1] // tm6)[None]
    pos = rank + jnp.sum(jnp.where(top_e[..., None] == jnp.arange(n_experts, dtype=jnp.int32), pstart, 0),
                         axis=-1)
    xs = _dispatch(pstart, pend, pos, xn2, n_rows, bsz, tq, tm6)

    w_gu_t = jnp.swapaxes(exp_w_gu[0].reshape(n_experts, d // LANES, LANES, 2 * dff), 2, 3)
    bgu = exp_b_gu[0]
    y_rows = _experts(block_e, n_used, xs, w_gu_t, bgu[:, None, 0::2], bgu[:, None, 1::2],
                      exp_w_down[0], exp_b_down[0][:, None, :], tm6)

    tt = _pick(seq, (128, 64))
    tiles_per_b = seq // tt
    n_tiles = bsz * tiles_per_b
    nj = t // tq

    def seq_tiles(a):
        a = a.reshape(bsz, nj, TOP_K, tq).transpose(0, 2, 1, 3).reshape(bsz, TOP_K, t)[:, :, CHUNK:]
        return a.reshape(bsz, TOP_K, tiles_per_b, tt).transpose(0, 2, 1, 3).reshape(n_tiles, TOP_K, tt)

    h_src = (jnp.arange(bsz, dtype=jnp.int32)[:, None] * t + CHUNK
             + jnp.arange(tiles_per_b, dtype=jnp.int32)[None, :] * tt).reshape(n_tiles, 1, 1)
    tw_cols = seq_tiles(top_w).transpose(0, 2, 1).reshape(n_tiles * tt, TOP_K)
    out = _combine(seq_tiles(pos), tw_cols, h_src, y_rows, h2, norm_final[None], tt)
    return out.reshape(bsz, seq, d)
```

```python
import functools

import numpy as np
import jax
import jax.numpy as jnp
from jax import lax
from jax.experimental import pallas as pl
from jax.experimental.pallas import tpu as pltpu

F32 = jnp.float32
BF16 = jnp.bfloat16

N_META = 16
CHUNK = 64
N_PAD = CHUNK - N_META
EPS = 1e-6

N_HEADS = 4
HG_DK = 128
HG_DV = 128
HG_W = N_HEADS * HG_DV
ML_DK = 64
ML_DV = 128
ML_W = N_HEADS * ML_DV
ML_CONV = 4
TOP_K = 4
SWIGLU_LIMIT = 7.0
SWIGLU_ALPHA = 1.702

LANES = 128
VMEM_LIMIT_BYTES = 56 * 1024 * 1024

HG_LEVELS = (32, 16, 8, 4, 2, 1)
GROUP_ROWS = 8


def _cparams(sem):
    return pltpu.CompilerParams(dimension_semantics=sem, vmem_limit_bytes=VMEM_LIMIT_BYTES)


def _sigmoid(x):
    return 0.5 + 0.5 * jnp.tanh(0.5 * x)


def _split3(x):
    x1 = x.astype(BF16)
    r1 = x - x1.astype(F32)
    x2 = r1.astype(BF16)
    x3 = (r1 - x2.astype(F32)).astype(BF16)
    return x1, x2, x3


def _dot(a, b):
    return jnp.dot(a, b, preferred_element_type=F32)


TOK_ROWS = 8


def _store_token_tiles(ref, x):
    n = x.shape[0]
    for s in range(TOK_ROWS):
        ref[pl.ds(s, n, stride=TOK_ROWS), :] = x[:, s * LANES:(s + 1) * LANES]


def _load_token_tiles(ref, n):
    return jnp.concatenate([ref[pl.ds(s, n, stride=TOK_ROWS), :] for s in range(TOK_ROWS)], axis=1)


def _dot_nt(a, b):
    return lax.dot_general(a, b, (((1,), (1,)), ((), ())), preferred_element_type=F32)


def _dot_exact_lhs(m_bf16, x):
    x1, x2, x3 = _split3(x)
    return _dot(m_bf16, x1) + _dot(m_bf16, x2) + _dot(m_bf16, x3)


def _h_tile(x_ref, head_ref, j):
    x = x_ref[...]
    first = jnp.concatenate([head_ref[...], x[:x.shape[0] - CHUNK]], axis=0)
    return jnp.where(j == 0, first, x)


def _x_tile_spec(tr, d, seq):
    assert seq % 8 == 0 and tr % 8 == 0 and CHUNK % 8 == 0
    return pl.BlockSpec(
        (pl.Element(tr), pl.Element(d)),
        lambda b, j: (pl.multiple_of(b * seq + jnp.maximum(j * tr - CHUNK, 0), 8), 0))


def _rms_bf16(x, gain):
    ms = jnp.mean(x * x, axis=-1, keepdims=True)
    return ((x * lax.rsqrt(ms + EPS)) * gain).astype(BF16)


def _project_ahead(x_ref, gain_ref, w_ref, nxt_ref):
    x = x_ref[...].reshape(x_ref.shape[0] * CHUNK, x_ref.shape[2])
    nxt_ref[...] = _dot(_rms_bf16(x, gain_ref[...]), w_ref[...])


def _with_projection_pipeline(step, x_ref, head_ref, gain_ref, w_ref, proj_a, proj_b, c):
    @pl.when(c == 0)
    def _():
        p0 = _dot(_rms_bf16(head_ref[...], gain_ref[...]), w_ref[...])
        for b in range(x_ref.shape[0]):
            proj_a[b * CHUNK:(b + 1) * CHUNK, :] = p0

    for parity, (cur, nxt) in enumerate(((proj_a, proj_b), (proj_b, proj_a))):
        @pl.when(c % 2 == parity)
        def _():
            _project_ahead(x_ref, gain_ref, w_ref, nxt)
            step(cur)


def _chunk_ahead_spec(bb, d, nc):
    return pl.BlockSpec((bb, CHUNK, d), lambda i, c: (i, jnp.minimum(c, nc - 2), 0))


def _hgrn2_kernel(x_ref, head_ref, gain_ref, w_ref, tri_ref, lb_ref, nw_ref, y_ref, proj_a, proj_b, *st_refs):
    c = pl.program_id(1)

    @pl.when(c == 0)
    def _():
        for st_ref in st_refs:
            st_ref[...] = jnp.zeros_like(st_ref)

    def step(cur):
        bb = x_ref.shape[0]
        for b0 in range(0, bb, GROUP_ROWS):
            _hgrn2_group(c, range(b0, min(b0 + GROUP_ROWS, bb)), tri_ref, lb_ref, nw_ref, y_ref, st_refs, cur)

    _with_projection_pipeline(step, x_ref, head_ref, gain_ref, w_ref, proj_a, proj_b, c)


def _hgrn2_group(c, bs, tri_ref, lb_ref, nw_ref, y_ref, st_refs, cur):
    row = lax.broadcasted_iota(jnp.int32, (CHUNK, 1), 0)
    valid = (c * CHUNK + row) >= N_PAD
    ti = lax.broadcasted_iota(jnp.int32, (CHUNK, CHUNK), 0)
    si = lax.broadcasted_iota(jnp.int32, (CHUNK, CHUNK), 1)
    diag_mask = ti == si
    level_masks = {}
    for m in HG_LEVELS:
        same_pair = (ti & ~(2 * m - 1)) == (si & ~(2 * m - 1))
        level_masks[m] = same_pair & ((ti & m) != 0) & ((si & m) == 0)

    lb = lb_ref[...]
    tri = tri_ref[...]
    per_b = {}
    for b in bs:
        rows_b = slice(b * CHUNK, (b + 1) * CHUNK)
        hf = cur[rows_b, HG_W:2 * HG_W]
        f = lb + (1.0 - lb) * _sigmoid(hf)
        f = jnp.where(valid, f, 1.0)
        logf = jnp.log(f)
        k_all = 1.0 - f
        hq = cur[rows_b, 0:HG_W]
        q_all = hq * _sigmoid(hq)
        b_cum = _dot_exact_lhs(tri, logf)
        e_b = jnp.exp(b_cum)
        e_bl = jnp.exp(b_cum[CHUNK - 1:CHUNK] - b_cum)

        q_fac, k_fac = {}, {}
        for m in HG_LEVELS:
            if m >= 4:
                grp = b_cum.reshape(CHUNK // (2 * m), 2 * m, HG_W)
                e = (grp - grp[:, m - 1:m, :]).reshape(CHUNK, HG_W)
                q_fac[m] = jnp.exp(jnp.minimum(e, 0.0))
                k_fac[m] = jnp.exp(jnp.minimum(-e, 0.0))
        f_prev = pltpu.roll(f, 1, 0)
        f_next = pltpu.roll(f, CHUNK - 1, 0)
        r4 = row & 3
        q_fac[2] = jnp.where(r4 == 2, f, jnp.where(r4 == 3, f * f_prev, 1.0))
        k_fac[2] = jnp.where(r4 == 0, f_next, 1.0)
        q_fac[1] = jnp.where((row & 1) == 1, f, 1.0)

        hg = cur[rows_b, 3 * HG_W:4 * HG_W]
        v_all = cur[rows_b, 2 * HG_W:3 * HG_W]
        q_fac = {m: v.astype(BF16) for m, v in q_fac.items()}
        k_fac = {m: v.astype(BF16) for m, v in k_fac.items()}
        per_b[b] = (q_all, k_all, q_fac, k_fac, e_b, e_bl, v_all, hg * _sigmoid(hg))

    pairs = [(b, h) for b in bs for h in range(N_HEADS)]
    sls = [slice(h * HG_DK, (h + 1) * HG_DK) for h in range(N_HEADS)]

    scores_all = []
    for b, h in pairs:
        q_all, k_all, q_fac, k_fac = per_b[b][:4]
        qb = q_all[:, sls[h]].astype(BF16)
        kb = k_all[:, sls[h]].astype(BF16)
        scores = jnp.where(diag_mask, _dot_nt(qb, kb), 0.0)
        for m in HG_LEVELS:
            qd = qb * q_fac[m][:, sls[h]]
            kd = kb * k_fac[m][:, sls[h]] if m in k_fac else kb
            scores = jnp.where(level_masks[m], _dot_nt(qd, kd), scores)
        scores_all.append(scores.astype(BF16))

    sts = [st_refs[b * N_HEADS + h][...] for b, h in pairs]
    outs = []
    for p, (b, h) in enumerate(pairs):
        q_all, e_b, v_all = per_b[b][0], per_b[b][4], per_b[b][6]
        qe = (q_all[:, sls[h]] * e_b[:, sls[h]]).astype(BF16)
        outs.append(_dot(scores_all[p], v_all[:, sls[h]].astype(BF16)) + _dot_nt(qe, sts[p].astype(BF16)))

    for p, (b, h) in enumerate(pairs):
        k_all, e_b, e_bl, v_all = per_b[b][1], per_b[b][4], per_b[b][5], per_b[b][6]
        kl = (k_all[:, sls[h]] * e_bl[:, sls[h]]).astype(BF16)
        vt = v_all[:, sls[h]].T.astype(BF16)
        st_refs[b * N_HEADS + h][...] = e_b[CHUNK - 1:CHUNK, sls[h]] * sts[p] + _dot(vt, kl)

    for p, (b, h) in enumerate(pairs):
        o = outs[p]
        ms = jnp.mean(o * o, axis=-1, keepdims=True)
        y = (o * lax.rsqrt(ms + EPS)) * nw_ref[:, sls[h]] * per_b[b][7][:, sls[h]]
        y_ref[b, :, sls[h]] = y.astype(y_ref.dtype)


def _hgrn2(x3, head, gain, w_hg, tri, lb, norm_w, bb):
    bsz, seq, d = x3.shape
    nc = seq // CHUNK + 1
    blk = pl.BlockSpec((bb, CHUNK, HG_W), lambda i, c: (i, c, 0))
    full = lambda a: pl.BlockSpec(a.shape, lambda i, c: (0, 0))
    return pl.pallas_call(
        _hgrn2_kernel,
        grid=(bsz // bb, nc),
        in_specs=[_chunk_ahead_spec(bb, d, nc), full(head), full(gain), full(w_hg), full(tri), full(lb),
                  full(norm_w)],
        out_specs=blk,
        out_shape=jax.ShapeDtypeStruct((bsz, nc * CHUNK, HG_W), BF16),
        scratch_shapes=([pltpu.VMEM((bb * CHUNK, w_hg.shape[1]), F32)] * 2
                        + [pltpu.VMEM((HG_DV, HG_DK), F32)] * (bb * N_HEADS)),
        compiler_params=_cparams(("parallel", "arbitrary")),
        name="hgrn2",
    )(x3, head, gain, w_hg, tri, lb, norm_w)


def _log_sigmoid(x):
    return jnp.minimum(x, 0.0) - jnp.log(1.0 + jnp.exp(-jnp.abs(x)))


def _mlstm_kernel(x_ref, head_ref, gain_ref, w_ref, tri_ref, cw_ref, cb_ref, wq_ref, wk_ref, wv_ref,
                  gbc_ref, gbr_ref, nw_ref, sk_ref, y_ref, proj_a, proj_b, *scratch):
    c = pl.program_id(1)

    @pl.when(c == 0)
    def _():
        for ref in scratch:
            ref[...] = jnp.zeros_like(ref)

    step = functools.partial(_mlstm_step, c, x_ref.shape[0], tri_ref, cw_ref, cb_ref, wq_ref, wk_ref, wv_ref,
                             gbc_ref, gbr_ref, nw_ref, sk_ref, y_ref, scratch)
    _with_projection_pipeline(step, x_ref, head_ref, gain_ref, w_ref, proj_a, proj_b, c)


def _mlstm_step(c, bb, tri_ref, cw_ref, cb_ref, wq_ref, wk_ref, wv_ref, gbc_ref, gbr_ref, nw_ref, sk_ref, y_ref,
                scratch, cur):
    n_pairs = bb * N_HEADS
    s_refs = scratch[:n_pairs]
    m_refs = scratch[n_pairs:2 * n_pairs]
    tail_refs = scratch[2 * n_pairs:]

    pos_c = c * CHUNK + lax.broadcasted_iota(jnp.int32, (CHUNK, 1), 0)
    valid_c = pos_c >= N_PAD
    pos_r = c * CHUNK + lax.broadcasted_iota(jnp.int32, (1, CHUNK), 1)
    valid_r = pos_r >= N_PAD
    ti = lax.broadcasted_iota(jnp.int32, (CHUNK, CHUNK), 0)
    si = lax.broadcasted_iota(jnp.int32, (CHUNK, CHUNK), 1)
    causal = si <= ti
    tri = tri_ref[...]
    ones_v = jnp.ones((CHUNK, ML_DV), BF16)
    neg_inf = -jnp.inf

    per_b = []
    for b in range(bb):
        rows_b = slice(b * CHUNK, (b + 1) * CHUNK)
        mm = jnp.where(valid_c, cur[rows_b, 0:ML_W], 0.0)
        ext = jnp.concatenate([tail_refs[b][...], mm], axis=0)
        tail_refs[b][...] = mm[CHUNK - 8:CHUNK]
        conv = cb_ref[...]
        for j in range(ML_CONV):
            off = 8 - (ML_CONV - 1) + j
            conv = conv + cw_ref[j:j + 1, :] * ext[off:off + CHUNK]
        cact = conv * _sigmoid(conv)
        cact_b = cact.astype(BF16)
        mm_b = mm.astype(BF16)

        graw = cur[rows_b, 2 * ML_W:2 * ML_W + LANES]
        gcol = graw + gbc_ref[...]
        li_col = jnp.where(valid_c, gcol, neg_inf)
        lf_col = jnp.where(valid_c, _log_sigmoid(gcol), 0.0)
        b_col = _dot_exact_lhs(tri, lf_col)
        grow = graw.T[0:2 * N_HEADS] + gbr_ref[...]
        li_row = jnp.where(valid_r, grow, neg_inf)
        lf_row = jnp.where(valid_r, _log_sigmoid(grow), 0.0)
        r1, r2, r3 = _split3(lf_row)
        b_row = _dot_nt(r1, tri) + _dot_nt(r2, tri) + _dot_nt(r3, tri)

        ogate = _sigmoid(cur[rows_b, ML_W:2 * ML_W])
        per_b.append((cact, cact_b, mm_b, li_col, b_col, li_row, b_row, ogate))

    pairs = [(b, h) for b in range(bb) for h in range(N_HEADS)]
    sls = [slice(h * ML_DV, (h + 1) * ML_DV) for h in range(N_HEADS)]

    qs, ks, vs = [], [], []
    for b, h in pairs:
        cact_b, mm_b = per_b[b][1], per_b[b][2]
        qs.append((_dot(cact_b[:, sls[h]], wq_ref[h]) * (ML_DK ** -0.5)).astype(BF16))
        ks.append(_dot(cact_b[:, sls[h]], wk_ref[h]))
        v = _dot(mm_b[:, sls[h]], wv_ref[h]).astype(BF16)
        vs.append(jnp.concatenate([v, ones_v], axis=1))

    n_p = len(pairs)
    blk = lambda p: slice(p * CHUNK, (p + 1) * CHUNK)
    stack = lambda xs: jnp.concatenate(xs, axis=0)
    bc_all = stack([per_b[b][4][:, N_HEADS + h:N_HEADS + h + 1] for b, h in pairs])
    lic_all = stack([per_b[b][3][:, h:h + 1] for b, h in pairs])
    row_all = stack([jnp.broadcast_to(per_b[b][5][h:h + 1, :] - per_b[b][6][N_HEADS + h:N_HEADS + h + 1, :],
                                      (CHUNK, CHUNK)) for b, h in pairs])
    mprev_all = stack([jnp.broadcast_to(m_refs[p][0:1, 0:1], (CHUNK, 1)) for p in range(n_p)])
    glast_all = stack([jnp.broadcast_to(per_b[b][4][CHUNK - 1:CHUNK, N_HEADS + h:N_HEADS + h + 1], (CHUNK, 1))
                       for b, h in pairs])
    causal_all = stack([causal] * n_p)

    d_all = jnp.where(causal_all, bc_all + row_all, neg_inf)
    a_all = bc_all + mprev_all
    m_t_all = jnp.maximum(a_all, jnp.max(d_all, axis=-1, keepdims=True))
    w_intra_all = jnp.exp(d_all - m_t_all)
    w_inter_all = jnp.exp(a_all - m_t_all)

    qk_all = (stack([_dot_nt(qs[p], ks[p].astype(BF16)) for p in range(n_p)]) * w_intra_all).astype(BF16)
    s_augs = [s_refs[p][...] for p in range(n_p)]
    intra_all = stack([_dot(qk_all[blk(p)], vs[p]) for p in range(n_p)])
    inter_all = stack([_dot(qs[p], s_augs[p].astype(BF16)) for p in range(n_p)])
    numden_all = intra_all + w_inter_all * inter_all
    o_all = numden_all[:, :ML_DV] / jnp.maximum(jnp.abs(numden_all[:, ML_DV:]), jnp.exp(-m_t_all))
    ms_all = jnp.mean(o_all * o_all, axis=-1, keepdims=True)
    on_all = o_all * lax.rsqrt(ms_all + EPS)

    e_all = glast_all - bc_all + lic_all
    gm_all = glast_all + mprev_all
    e_max = jnp.max(e_all.reshape(n_p, CHUNK, 1), axis=1, keepdims=True)
    m_new_all = jnp.maximum(gm_all.reshape(n_p, CHUNK, 1), e_max).reshape(n_p * CHUNK, 1)
    w_s_all = jnp.exp(e_all - m_new_all)
    w_p_all = jnp.exp(gm_all - m_new_all)
    kw_all = stack(ks) * w_s_all
    for p in range(n_p):
        kw_t = kw_all[blk(p)].T.astype(BF16)
        s_refs[p][...] = w_p_all[p * CHUNK:p * CHUNK + 1] * s_augs[p] + _dot(kw_t, vs[p])
        m_refs[p][...] = jnp.broadcast_to(m_new_all[p * CHUNK:p * CHUNK + 1], m_refs[p].shape)

    for p, (b, h) in enumerate(pairs):
        cact, ogate = per_b[b][0], per_b[b][7]
        sl = sls[h]
        y = (on_all[blk(p)] * nw_ref[:, sl] + sk_ref[:, sl] * cact[:, sl]) * ogate[:, sl]
        y_ref[b, :, sl] = y.astype(y_ref.dtype)


def _mlstm(x3, head, gain, w_ml, tri, conv_w, conv_b, wq, wk, wv, gb_col, gb_row, norm_w, skip, bb):
    bsz, seq, d = x3.shape
    nc = seq // CHUNK + 1
    blk = pl.BlockSpec((bb, CHUNK, ML_W), lambda i, c: (i, c, 0))

    def full(a):
        nd = a.ndim
        return pl.BlockSpec(a.shape, lambda i, c: (0,) * nd)

    params = (head, gain, w_ml, tri, conv_w, conv_b, wq, wk, wv, gb_col, gb_row, norm_w, skip)
    return pl.pallas_call(
        _mlstm_kernel,
        grid=(bsz // bb, nc),
        in_specs=[_chunk_ahead_spec(bb, d, nc)] + [full(p) for p in params],
        out_specs=blk,
        out_shape=jax.ShapeDtypeStruct((bsz, nc * CHUNK, ML_W), BF16),
        scratch_shapes=([pltpu.VMEM((bb * CHUNK, w_ml.shape[1]), F32)] * 2
                        + [pltpu.VMEM((ML_DK, 2 * ML_DV), F32)] * (bb * N_HEADS)
                        + [pltpu.VMEM((8, LANES), F32)] * (bb * N_HEADS)
                        + [pltpu.VMEM((8, ML_W), F32)] * bb),
        compiler_params=_cparams(("parallel", "arbitrary")),
        name="mlstm",
    )(x3, *params)


def _merge_route_kernel(x_ref, head_ref, gain_ref, wg_ref, yh_ref, ym_ref, wbh_ref, wbm_ref, wo_ref, nf_ref,
                        rw_ref, rb_ref, h2_ref, xn_ref, te_ref, tw_ref, rk_ref, cnt_ref, *, n_experts):
    d = x_ref.shape[1]
    tq = x_ref.shape[0]
    j = pl.program_id(1)

    @pl.when(jnp.logical_and(pl.program_id(0) == 0, j == 0))
    def _():
        cnt_ref[...] = jnp.zeros_like(cnt_ref)

    h = _h_tile(x_ref, head_ref, j)
    hn = _rms_bf16(h, gain_ref[...])
    g0 = _sigmoid(_dot(hn, wg_ref[:, :d]))
    g1 = _sigmoid(_dot(hn, wg_ref[:, d:]))
    merged = g0 * _dot(yh_ref[...], wbh_ref[...]) + g1 * _dot(ym_ref[...], wbm_ref[...])
    h2 = h + _dot(merged.astype(BF16), wo_ref[...])
    h2_ref[...] = h2
    ms = jnp.mean(h2 * h2, axis=-1, keepdims=True)
    xn = (h2 * lax.rsqrt(ms + EPS)) * nf_ref[...]
    _store_token_tiles(xn_ref, xn)
    x1, x2, _ = _split3(xn)
    logits = (_dot_nt(rw_ref[0], x1) + _dot_nt(rw_ref[1], x1) + _dot_nt(rw_ref[0], x2)) + rb_ref[...]
    sub = lax.broadcasted_iota(jnp.int32, logits.shape, 0)
    work = logits
    vals, idxs = [], []
    for _ in range(TOP_K):
        vmax = jnp.max(work, axis=0, keepdims=True)
        imax = jnp.min(jnp.where(work == vmax, sub, n_experts), axis=0, keepdims=True)
        vals.append(vmax)
        idxs.append(imax)
        work = jnp.where(sub == imax, -jnp.inf, work)
    exps = [jnp.exp(v - vals[0]) for v in vals]
    tot = exps[0] + exps[1] + exps[2] + exps[3]

    krow = lax.broadcasted_iota(jnp.int32, (TOP_K, tq), 0)

    def rows(per_k):
        out = jnp.broadcast_to(per_k[0], (TOP_K, tq))
        for kk in range(1, TOP_K):
            out = jnp.where(krow == kk, per_k[kk], out)
        return out

    te_ref[0] = rows(idxs)
    tw_ref[0] = rows([e / tot for e in exps])

    valid = (j * tq + lax.broadcasted_iota(jnp.int32, (1, tq), 1)) >= N_PAD
    onehots = [jnp.where(jnp.logical_and(sub == idxs[kk], valid), 1.0, 0.0) for kk in range(TOP_K)]
    oh_all = onehots[0] + onehots[1] + onehots[2] + onehots[3]
    ri = lax.broadcasted_iota(jnp.int32, (tq, tq), 0)
    ci = lax.broadcasted_iota(jnp.int32, (tq, tq), 1)
    earlier = jnp.where(ri < ci, 1.0, 0.0).astype(BF16)
    cnt = cnt_ref[:, 0:1]
    before = _dot(oh_all.astype(BF16), earlier) + cnt
    ranks = []
    for kk in range(TOP_K):
        ranks.append(jnp.sum(jnp.where(sub == idxs[kk], before, 0.0), axis=0, keepdims=True))
        before = before + onehots[kk]
    rk_ref[0] = rows(ranks).astype(jnp.int32)
    cnt_ref[...] = jnp.broadcast_to(cnt + jnp.sum(oh_all, axis=1, keepdims=True), cnt_ref.shape)


def _merge_route(x2d, head, gain, w_g, y_hg, y_ml, wbh, wbm, wo, norm_ffn, rw_split, rb, n_experts, bsz, tq):
    d = x2d.shape[1]
    seq = x2d.shape[0] // bsz
    nj = (seq + CHUNK) // tq
    m = bsz * nj * tq
    assert d == TOK_ROWS * LANES, "token-tile layout assumes one (8,128) tile per token"
    row = lambda n: pl.BlockSpec((tq, n), lambda b, j: (b * nj + j, 0))
    tiles = pl.BlockSpec((tq * TOK_ROWS, LANES), lambda b, j: (b * nj + j, 0))
    per_tok = pl.BlockSpec((1, TOP_K, tq), lambda b, j: (b * nj + j, 0, 0))

    def full(a):
        nd = a.ndim
        return pl.BlockSpec(a.shape, lambda b, j: (0,) * nd)

    return pl.pallas_call(
        functools.partial(_merge_route_kernel, n_experts=n_experts),
        grid=(bsz, nj),
        in_specs=[_x_tile_spec(tq, d, seq), full(head), full(gain), full(w_g), row(HG_W), row(ML_W),
                  full(wbh), full(wbm), full(wo), full(norm_ffn), full(rw_split), full(rb)],
        out_specs=[row(d), tiles, per_tok, per_tok, per_tok,
                   pl.BlockSpec((n_experts, LANES), lambda b, j: (0, 0))],
        out_shape=[
            jax.ShapeDtypeStruct((m, d), F32),
            jax.ShapeDtypeStruct((m * TOK_ROWS, LANES), F32),
            jax.ShapeDtypeStruct((bsz * nj, TOP_K, tq), jnp.int32),
            jax.ShapeDtypeStruct((bsz * nj, TOP_K, tq), F32),
            jax.ShapeDtypeStruct((bsz * nj, TOP_K, tq), jnp.int32),
            jax.ShapeDtypeStruct((n_experts, LANES), F32),
        ],
        compiler_params=_cparams(("arbitrary", "arbitrary")),
        name="merge_route",
    )(x2d, head, gain, w_g, y_hg, y_ml, wbh, wbm, wo, norm_ffn, rw_split, rb)


def _toks(first, n=1):
    return pl.ds(pl.multiple_of(first * TOK_ROWS, TOK_ROWS), n * TOK_ROWS)


def _dispatch_kernel(ps_ref, pe_ref, pos_ref, x_ref, x_hbm, xs_hbm, zbuf, sem, *, n_experts, tm):
    b = pl.program_id(0)
    j = pl.program_id(1)
    tq = x_ref.shape[0] // TOK_ROWS
    row0 = (b * pl.num_programs(1) + j) * tq

    @pl.when(jnp.logical_and(b == 0, j == 0))
    def _():
        zbuf[...] = jnp.zeros_like(zbuf)
        for e in range(n_experts):
            @pl.when(pe_ref[e] > ps_ref[e])
            def _():
                pltpu.make_async_copy(zbuf, xs_hbm.at[_toks(pe_ref[e] - tm, tm), :], sem).start()
        for e in range(n_experts):
            @pl.when(pe_ref[e] > ps_ref[e])
            def _():
                pltpu.make_async_copy(zbuf, xs_hbm.at[_toks(0, tm), :], sem).wait()

        def zero_tail(blk, carry):
            cp = pltpu.make_async_copy(zbuf, xs_hbm.at[_toks(blk * tm, tm), :], sem)
            cp.start()
            cp.wait()
            return carry
        lax.fori_loop(pe_ref[n_experts - 1] // tm, xs_hbm.shape[0] // (tm * TOK_ROWS), zero_tail, 0)

    def scatter_rows(lo):
        def body(r, carry):
            for kk in range(TOP_K):
                dst = pos_ref[0, kk, r]
                if kk < 2:
                    pltpu.make_async_copy(
                        x_ref.at[_toks(r), :], xs_hbm.at[_toks(dst), :], sem).start(priority=kk)
                else:
                    pltpu.make_async_copy(
                        x_hbm.at[_toks(row0 + r), :], xs_hbm.at[_toks(dst), :], sem).start()
            return carry
        lax.fori_loop(lo, tq, body, 0, unroll=2)
        n = tq - lo
        for _ in range(TOP_K):
            pltpu.make_async_copy(x_ref.at[_toks(0, n), :], xs_hbm.at[_toks(0, n), :], sem).wait()

    @pl.when(j == 0)
    def _():
        scatter_rows(N_PAD)

    @pl.when(j != 0)
    def _():
        scatter_rows(0)


def _dispatch(pstart, pend, pos, xn_t, n_rows, bsz, tq, tm):
    m = xn_t.shape[0] // TOK_ROWS
    nj = m // (bsz * tq)
    n_experts = pstart.shape[0]
    smem_blk = pl.BlockSpec((1, TOP_K, tq), lambda b, j, ps, pe: (b * nj + j, 0, 0),
                            memory_space=pltpu.SMEM)
    grid_spec = pltpu.PrefetchScalarGridSpec(
        num_scalar_prefetch=2,
        grid=(bsz, nj),
        in_specs=[smem_blk,
                  pl.BlockSpec((tq * TOK_ROWS, LANES), lambda b, j, ps, pe: (b * nj + j, 0)),
                  pl.BlockSpec(memory_space=pl.ANY)],
        out_specs=pl.BlockSpec(memory_space=pl.ANY),
        scratch_shapes=[pltpu.VMEM((tm * TOK_ROWS, LANES), F32), pltpu.SemaphoreType.DMA(())],
    )
    return pl.pallas_call(
        functools.partial(_dispatch_kernel, n_experts=n_experts, tm=tm),
        grid_spec=grid_spec,
        out_shape=jax.ShapeDtypeStruct((n_rows * TOK_ROWS, LANES), F32),
        compiler_params=_cparams(("arbitrary", "arbitrary")),
        name="dispatch",
    )(pstart, pend, pos, xn_t, xn_t)


CAST_ROWS = 256


def _experts_kernel(be_ref, nu_ref, slot_ref, nxt_ref, x_ref, wgu_hbm, bg_ref, bu_ref, wd_hbm, bd_ref, y_ref,
                    wg_s, wu_s, wd_s, wgu_buf, wd_buf, sem):
    i = pl.program_id(0)
    n_used = nu_ref[0]
    dff = wg_s.shape[0]
    new_expert = jnp.logical_or(i == 0, be_ref[i] != be_ref[jnp.maximum(i - 1, 0)])

    def weight_copies(e, slot):
        return (pltpu.make_async_copy(wgu_hbm.at[e], wgu_buf.at[slot], sem.at[slot]),
                pltpu.make_async_copy(wd_hbm.at[e], wd_buf.at[slot], sem.at[slot]))

    @pl.when(i == 0)
    def _():
        for cp in weight_copies(be_ref[0], 0):
            cp.start()

    for slot in range(2):
        @pl.when(jnp.logical_and(jnp.logical_and(new_expert, i < n_used), slot_ref[i] == slot))
        def _():
            for cp in weight_copies(be_ref[i], slot):
                cp.wait()

            @pl.when(nxt_ref[i] >= 0)
            def _():
                for cp in weight_copies(nxt_ref[i], 1 - slot):
                    cp.start()

            for c in range(dff // CAST_ROWS):
                rows = pl.ds(c * CAST_ROWS, CAST_ROWS)
                wd_s[rows, :] = wd_buf[slot, rows, :].astype(BF16)
                for sl in range(wgu_buf.shape[1]):
                    lanes = pl.ds(sl * LANES, LANES)
                    wg_s[rows, lanes] = wgu_buf[
                        slot, sl, pl.ds(2 * c * CAST_ROWS, CAST_ROWS, stride=2), :].astype(BF16)
                    wu_s[rows, lanes] = wgu_buf[
                        slot, sl, pl.ds(2 * c * CAST_ROWS + 1, CAST_ROWS, stride=2), :].astype(BF16)

    @pl.when(i < n_used)
    def _():
        tm = x_ref.shape[0] // TOK_ROWS
        xb = _load_token_tiles(x_ref, tm).astype(BF16)
        g = _dot_nt(xb, wg_s[...]) + bg_ref[0]
        u = _dot_nt(xb, wu_s[...]) + bu_ref[0]
        gate = jnp.minimum(g, SWIGLU_LIMIT)
        up = jnp.clip(u, -SWIGLU_LIMIT, SWIGLU_LIMIT)
        act = (up + 1.0) * gate * _sigmoid(SWIGLU_ALPHA * gate)
        _store_token_tiles(y_ref, _dot(act.astype(BF16), wd_s[...]) + bd_ref[0])

    @pl.when(i >= n_used)
    def _():
        y_ref[...] = jnp.zeros_like(y_ref)


def _experts(block_e, n_used, xs, w_gu_t, b_g, b_u, w_d, b_d, tm):
    n_blocks = block_e.shape[0]
    dff, d = w_d.shape[1:]
    tile_blk = lambda f: pl.BlockSpec((tm * TOK_ROWS, LANES), f)
    assert dff % CAST_ROWS == 0
    blk = jnp.arange(n_blocks, dtype=jnp.int32)
    changed = jnp.concatenate([jnp.zeros((1,), jnp.int32), (block_e[1:] != block_e[:-1]).astype(jnp.int32)])
    w_slot = jnp.cumsum(changed) % 2
    later = jnp.logical_and(jnp.logical_and(blk[None, :] > blk[:, None], blk[None, :] < n_used[0]),
                            block_e[None, :] != block_e[:, None])
    nxt_e = jnp.where(jnp.any(later, axis=1), block_e[jnp.argmax(later, axis=1)], -1).astype(jnp.int32)
    wspec = lambda k, n: pl.BlockSpec((1, k, n), lambda i, be, nu, ws, ne: (be[i], 0, 0))
    grid_spec = pltpu.PrefetchScalarGridSpec(
        num_scalar_prefetch=4,
        grid=(n_blocks,),
        in_specs=[
            tile_blk(lambda i, be, nu, ws, ne: (jnp.minimum(i, nu[0] - 1), 0)),
            pl.BlockSpec(memory_space=pl.ANY),
            wspec(1, dff), wspec(1, dff),
            pl.BlockSpec(memory_space=pl.ANY),
            wspec(1, d),
        ],
        out_specs=tile_blk(lambda i, be, nu, ws, ne: (i, 0)),
        scratch_shapes=[pltpu.VMEM((dff, d), BF16), pltpu.VMEM((dff, d), BF16), pltpu.VMEM((dff, d), BF16),
                        pltpu.VMEM((2,) + w_gu_t.shape[1:], F32), pltpu.VMEM((2, dff, d), F32),
                        pltpu.SemaphoreType.DMA((2,))],
    )
    return pl.pallas_call(
        _experts_kernel,
        grid_spec=grid_spec,
        out_shape=jax.ShapeDtypeStruct((n_blocks * tm * TOK_ROWS, LANES), F32),
        compiler_params=_cparams(("arbitrary",)),
        name="experts",
    )(block_e, n_used, w_slot, nxt_e, xs, w_gu_t, b_g, b_u, w_d, b_d)


def _combine_kernel(pos_ref, posn_ref, hsrc_ref, hsrcn_ref, tw_ref, y_hbm, h_hbm,
                    nw_ref, o_ref, ybuf, hbuf, sem):
    i = pl.program_id(0)
    n = pl.num_programs(0)
    tt = o_ref.shape[0]
    nt = tt * TOK_ROWS

    def copies(src_pos_ref, src_h_ref, dst_slot, r):
        return [pltpu.make_async_copy(
            y_hbm.at[_toks(src_pos_ref[0, kk, r]), :],
            ybuf.at[dst_slot, _toks(kk * tt + r), :],
            sem.at[dst_slot]) for kk in range(TOP_K)]

    def start_h(src_h_ref, dst_slot):
        h_row = pl.multiple_of(src_h_ref[0, 0, 0], 8)
        pltpu.make_async_copy(h_hbm.at[pl.ds(h_row, tt), :], hbuf.at[dst_slot], sem.at[dst_slot]).start()

    @pl.when(i == 0)
    def _():
        def body(r, carry):
            for kk, cp in enumerate(copies(pos_ref, hsrc_ref, 0, r)):
                cp.start(priority=kk % 2)
            return carry
        lax.fori_loop(0, tt, body, 0, unroll=2)
        start_h(hsrc_ref, 0)

    def wait_slot(s):
        pltpu.make_async_copy(ybuf.at[s], ybuf.at[s], sem.at[s]).wait()
        pltpu.make_async_copy(hbuf.at[s], hbuf.at[s], sem.at[s]).wait()

    def step(slot):
        for r in range(tt):
            for kk, cp in enumerate(copies(posn_ref, hsrcn_ref, 1 - slot, r)):
                cp.start(priority=kk % 2)
        start_h(hsrcn_ref, 1 - slot)

        wait_slot(slot)
        acc = hbuf[slot]
        tw = tw_ref[...]
        for kk in range(TOP_K):
            yk = jnp.concatenate(
                [ybuf[slot, pl.ds(kk * nt + s, tt, stride=TOK_ROWS), :] for s in range(TOK_ROWS)], axis=1)
            acc = acc + tw[:, kk:kk + 1] * yk
        ms = jnp.mean(acc * acc, axis=-1, keepdims=True)
        o_ref[...] = (acc * lax.rsqrt(ms + EPS)) * nw_ref[...]

        @pl.when(i == n - 1)
        def _():
            wait_slot(1 - slot)

    for parity in range(2):
        pl.when(i % 2 == parity)(functools.partial(step, parity))


def _combine(pos_seq, tw_seq, h_src, y_rows, h2, norm_w, tt):
    n_tiles = h_src.shape[0]
    d = h2.shape[1]
    last = n_tiles - 1
    nt = tt * TOK_ROWS
    cur = lambda i: (i, 0, 0)
    nxt = lambda i: (jnp.minimum(i + 1, last), 0, 0)
    idx_blk = lambda f: pl.BlockSpec((1, TOP_K, tt), f, memory_space=pltpu.SMEM)
    one_blk = lambda f: pl.BlockSpec((1, 1, 1), f, memory_space=pltpu.SMEM)
    return pl.pallas_call(
        _combine_kernel,
        grid=(n_tiles,),
        in_specs=[
            idx_blk(cur), idx_blk(nxt), one_blk(cur), one_blk(nxt),
            pl.BlockSpec((tt, TOP_K), lambda i: (i, 0)),
            pl.BlockSpec(memory_space=pl.ANY),
            pl.BlockSpec(memory_space=pl.ANY),
            pl.BlockSpec(norm_w.shape, lambda i: (0, 0)),
        ],
        out_specs=pl.BlockSpec((tt, d), lambda i: (i, 0)),
        scratch_shapes=[pltpu.VMEM((2, TOP_K * nt, LANES), F32), pltpu.VMEM((2, tt, d), F32),
                        pltpu.SemaphoreType.DMA((2,))],
        out_shape=jax.ShapeDtypeStruct((n_tiles * tt, d), F32),
        compiler_params=_cparams(("arbitrary",)),
        name="combine",
    )(pos_seq, pos_seq, h_src, h_src, tw_seq, y_rows, h2, norm_w)


def _pick(n, prefs):
    for p in prefs:
        if n % p == 0:
            return p
    raise ValueError(f"no tile in {prefs} divides {n}")


def kernel(x, meta_tokens, hg_lb_logits, norm_mix, w_in, hg_norm, ml_conv_w, ml_conv_b, ml_wq, ml_wk, ml_wv,
           ml_gate_b, ml_norm, ml_skip, w_branch_hg, w_branch_ml, w_out, norm_ffn, router_w, router_b,
           exp_w_gu, exp_b_gu, exp_w_down, exp_b_down, norm_final):
    bsz, seq, d = x.shape
    assert norm_mix.shape[0] == 1, "single-layer block"
    assert seq % CHUNK == 0 and d % LANES == 0
    t = CHUNK + seq
    m_rows = bsz * t
    n_experts = router_w.shape[-1]
    dff = exp_w_down.shape[2]
    assert n_experts <= LANES

    head = jnp.concatenate([jnp.zeros((N_PAD, d), x.dtype), meta_tokens.astype(x.dtype)], axis=0)
    x2d = x.reshape(bsz * seq, d)
    lower_bounds = jnp.cumsum(jax.nn.softmax(hg_lb_logits.astype(F32), axis=0), axis=0)

    w = w_in[0]
    n_hg = 4 * HG_W
    n_a = n_hg + 2 * ML_W
    w_hg = w[:, :n_hg].astype(BF16)
    w_m = jnp.pad(w[:, n_a:n_a + 2 * N_HEADS], ((0, 0), (0, LANES - 2 * N_HEADS)))
    w_ml = jnp.concatenate([w[:, n_hg:n_a], w_m], axis=1).astype(BF16)
    w_g = w[:, n_a + 2 * N_HEADS:].astype(BF16)
    gain = norm_mix[0][None]

    bb = _pick(bsz, (8, 4, 2, 1))
    tri = jnp.asarray(np.tril(np.ones((CHUNK, CHUNK), np.float32)), BF16)
    y_hg = _hgrn2(x, head, gain, w_hg, tri, lower_bounds[0][None], hg_norm[0][None], bb)

    gb = ml_gate_b[0].astype(F32)
    gb_col = jnp.pad(gb, (0, LANES - 2 * N_HEADS))[None]
    gb_row = jnp.broadcast_to(gb[:, None], (2 * N_HEADS, CHUNK))
    y_ml = _mlstm(x, head, gain, w_ml, tri, ml_conv_w[0], ml_conv_b[0][None],
                  ml_wq[0].astype(BF16), ml_wk[0].astype(BF16), ml_wv[0].astype(BF16),
                  gb_col, gb_row, ml_norm[0][None], ml_skip[0][None], bb)

    assert n_experts % 8 == 0
    tq = _pick(t, (704, 192, 64))
    rw = router_w[0].astype(F32).T
    rw1 = rw.astype(BF16)
    rw2 = (rw - rw1.astype(F32)).astype(BF16)
    rw_split = jnp.stack([rw1, rw2])
    rb = jnp.broadcast_to(router_b[0].astype(F32)[:, None], (n_experts, tq))
    h2, xn2, top_e, top_w, rank, cnt = _merge_route(
        x2d, head, gain, w_g, y_hg.reshape(m_rows, HG_W), y_ml.reshape(m_rows, ML_W),
        w_branch_hg[0].astype(BF16), w_branch_ml[0].astype(BF16), w_out[0].astype(BF16),
        norm_ffn[0][None], rw_split, rb, n_experts, bsz, tq)

    tm6 = 512
    n_assign = bsz * (t - N_PAD) * TOP_K
    counts = cnt[:, 0].astype(jnp.int32)
    padded = ((counts + tm6 - 1) // tm6) * tm6
    pend = jnp.cumsum(padded).astype(jnp.int32)
    pstart = pend - padded
    n_blocks = -(-n_assign // tm6) + n_experts
    n_rows = n_blocks * tm6
    blk_start = jnp.arange(n_blocks, dtype=jnp.int32) * tm6
    block_e = jnp.minimum(jnp.sum((blk_start[:, None] >= pend[None, :]).astype(jnp.int32), axis=1),
                          n_experts - 1)
    n_used = (pend[-1] // tm6)[None]
    pos = rank + jnp.sum(jnp.where(top_e[..., None] == jnp.arange(n_experts, dtype=jnp.int32), pstart, 0),
                         axis=-1)
    xs = _dispatch(pstart, pend, pos, xn2, n_rows, bsz, tq, tm6)

    w_gu_t = jnp.swapaxes(exp_w_gu[0].reshape(n_experts, d // LANES, LANES, 2 * dff), 2, 3)
    bgu = exp_b_gu[0]
    y_rows = _experts(block_e, n_used, xs, w_gu_t, bgu[:, None, 0::2], bgu[:, None, 1::2],
                      exp_w_down[0], exp_b_down[0][:, None, :], tm6)

    tt = _pick(seq, (128, 64))
    tiles_per_b = seq // tt
    n_tiles = bsz * tiles_per_b
    nj = t // tq

    def seq_tiles(a):
        a = a.reshape(bsz, nj, TOP_K, tq).transpose(0, 2, 1, 3).reshape(bsz, TOP_K, t)[:, :, CHUNK:]
        return a.reshape(bsz, TOP_K, tiles_per_b, tt).transpose(0, 2, 1, 3).reshape(n_tiles, TOP_K, tt)

    h_src = (jnp.arange(bsz, dtype=jnp.int32)[:, None] * t + CHUNK
             + jnp.arange(tiles_per_b, dtype=jnp.int32)[None, :] * tt).reshape(n_tiles, 1, 1)
    tw_cols = seq_tiles(top_w).transpose(0, 2, 1).reshape(n_tiles * tt, TOP_K)
    out = _combine(seq_tiles(pos), tw_cols, h_src, y_rows, h2, norm_final[None], tt)
    return out.reshape(bsz, seq, d)
```

```python
import functools

import numpy as np
import jax
import jax.numpy as jnp
from jax import lax
from jax.experimental import pallas as pl
from jax.experimental.pallas import tpu as pltpu

F32 = jnp.float32
BF16 = jnp.bfloat16

N_META = 16
CHUNK = 64
N_PAD = CHUNK - N_META
EPS = 1e-6

N_HEADS = 4
HG_DK = 128
HG_DV = 128
HG_W = N_HEADS * HG_DV
ML_DK = 64
ML_DV = 128
ML_W = N_HEADS * ML_DV
ML_CONV = 4
TOP_K = 4
SWIGLU_LIMIT = 7.0
SWIGLU_ALPHA = 1.702

LANES = 128
VMEM_LIMIT_BYTES = 56 * 1024 * 1024

HG_LEVELS = (32, 16, 8, 4, 2, 1)
GROUP_ROWS = 8


def _cparams(sem):
    return pltpu.CompilerParams(dimension_semantics=sem, vmem_limit_bytes=VMEM_LIMIT_BYTES)


def _sigmoid(x):
    return 0.5 + 0.5 * jnp.tanh(0.5 * x)


def _split3(x):
    x1 = x.astype(BF16)
    r1 = x - x1.astype(F32)
    x2 = r1.astype(BF16)
    x3 = (r1 - x2.astype(F32)).astype(BF16)
    return x1, x2, x3


def _dot(a, b):
    return jnp.dot(a, b, preferred_element_type=F32)


TOK_ROWS = 8


def _store_token_tiles(ref, x):
    n = x.shape[0]
    for s in range(TOK_ROWS):
        ref[pl.ds(s, n, stride=TOK_ROWS), :] = x[:, s * LANES:(s + 1) * LANES]


def _load_token_tiles(ref, n):
    return jnp.concatenate([ref[pl.ds(s, n, stride=TOK_ROWS), :] for s in range(TOK_ROWS)], axis=1)


def _dot_nt(a, b):
    return lax.dot_general(a, b, (((1,), (1,)), ((), ())), preferred_element_type=F32)


def _dot_exact_lhs(m_bf16, x):
    x1, x2, x3 = _split3(x)
    return _dot(m_bf16, x1) + _dot(m_bf16, x2) + _dot(m_bf16, x3)


def _h_tile(x_ref, head_ref, j):
    x = x_ref[...]
    first = jnp.concatenate([head_ref[...], x[:x.shape[0] - CHUNK]], axis=0)
    return jnp.where(j == 0, first, x)


def _x_tile_spec(tr, d, seq):
    assert seq % 8 == 0 and tr % 8 == 0 and CHUNK % 8 == 0
    return pl.BlockSpec(
        (pl.Element(tr), pl.Element(d)),
        lambda b, j: (pl.multiple_of(b * seq + jnp.maximum(j * tr - CHUNK, 0), 8), 0))


def _rms_bf16(x, gain):
    ms = jnp.mean(x * x, axis=-1, keepdims=True)
    return ((x * lax.rsqrt(ms + EPS)) * gain).astype(BF16)


def _project_ahead(x_ref, gain_ref, w_ref, nxt_ref):
    x = x_ref[...].reshape(x_ref.shape[0] * CHUNK, x_ref.shape[2])
    nxt_ref[...] = _dot(_rms_bf16(x, gain_ref[...]), w_ref[...])


def _with_projection_pipeline(step, x_ref, head_ref, gain_ref, w_ref, proj_a, proj_b, c):
    @pl.when(c == 0)
    def _():
        p0 = _dot(_rms_bf16(head_ref[...], gain_ref[...]), w_ref[...])
        for b in range(x_ref.shape[0]):
            proj_a[b * CHUNK:(b + 1) * CHUNK, :] = p0

    for parity, (cur, nxt) in enumerate(((proj_a, proj_b), (proj_b, proj_a))):
        @pl.when(c % 2 == parity)
        def _():
            _project_ahead(x_ref, gain_ref, w_ref, nxt)
            step(cur)


def _chunk_ahead_spec(bb, d, nc):
    return pl.BlockSpec((bb, CHUNK, d), lambda i, c: (i, jnp.minimum(c, nc - 2), 0))


def _hgrn2_kernel(x_ref, head_ref, gain_ref, w_ref, tri_ref, lb_ref, nw_ref, y_ref, proj_a, proj_b, *st_refs):
    c = pl.program_id(1)

    @pl.when(c == 0)
    def _():
        for st_ref in st_refs:
            st_ref[...] = jnp.zeros_like(st_ref)

    def step(cur):
        bb = x_ref.shape[0]
        for b0 in range(0, bb, GROUP_ROWS):
            _hgrn2_group(c, range(b0, min(b0 + GROUP_ROWS, bb)), tri_ref, lb_ref, nw_ref, y_ref, st_refs, cur)

    _with_projection_pipeline(step, x_ref, head_ref, gain_ref, w_ref, proj_a, proj_b, c)


def _hgrn2_group(c, bs, tri_ref, lb_ref, nw_ref, y_ref, st_refs, cur):
    row = lax.broadcasted_iota(jnp.int32, (CHUNK, 1), 0)
    valid = (c * CHUNK + row) >= N_PAD
    ti = lax.broadcasted_iota(jnp.int32, (CHUNK, CHUNK), 0)
    si = lax.broadcasted_iota(jnp.int32, (CHUNK, CHUNK), 1)
    diag_mask = ti == si
    level_masks = {}
    for m in HG_LEVELS:
        same_pair = (ti & ~(2 * m - 1)) == (si & ~(2 * m - 1))
        level_masks[m] = same_pair & ((ti & m) != 0) & ((si & m) == 0)

    lb = lb_ref[...]
    tri = tri_ref[...]
    per_b = {}
    for b in bs:
        rows_b = slice(b * CHUNK, (b + 1) * CHUNK)
        hf = cur[rows_b, HG_W:2 * HG_W]
        f = lb + (1.0 - lb) * _sigmoid(hf)
        f = jnp.where(valid, f, 1.0)
        logf = jnp.log(f)
        k_all = 1.0 - f
        hq = cur[rows_b, 0:HG_W]
        q_all = hq * _sigmoid(hq)
        b_cum = _dot_exact_lhs(tri, logf)
        e_b = jnp.exp(b_cum)
        e_bl = jnp.exp(b_cum[CHUNK - 1:CHUNK] - b_cum)

        q_fac, k_fac = {}, {}
        for m in HG_LEVELS:
            if m >= 4:
                grp = b_cum.reshape(CHUNK // (2 * m), 2 * m, HG_W)
                e = (grp - grp[:, m - 1:m, :]).reshape(CHUNK, HG_W)
                q_fac[m] = jnp.exp(jnp.minimum(e, 0.0))
                k_fac[m] = jnp.exp(jnp.minimum(-e, 0.0))
        f_prev = pltpu.roll(f, 1, 0)
        f_next = pltpu.roll(f, CHUNK - 1, 0)
        r4 = row & 3
        q_fac[2] = jnp.where(r4 == 2, f, jnp.where(r4 == 3, f * f_prev, 1.0))
        k_fac[2] = jnp.where(r4 == 0, f_next, 1.0)
        q_fac[1] = jnp.where((row & 1) == 1, f, 1.0)

        hg = cur[rows_b, 3 * HG_W:4 * HG_W]
        v_all = cur[rows_b, 2 * HG_W:3 * HG_W]
        q_fac = {m: v.astype(BF16) for m, v in q_fac.items()}
        k_fac = {m: v.astype(BF16) for m, v in k_fac.items()}
        per_b[b] = (q_all, k_all, q_fac, k_fac, e_b, e_bl, v_all, hg * _sigmoid(hg))

    pairs = [(b, h) for b in bs for h in range(N_HEADS)]
    sls = [slice(h * HG_DK, (h + 1) * HG_DK) for h in range(N_HEADS)]

    scores_all = []
    for b, h in pairs:
        q_all, k_all, q_fac, k_fac = per_b[b][:4]
        qb = q_all[:, sls[h]].astype(BF16)
        kb = k_all[:, sls[h]].astype(BF16)
        scores = jnp.where(diag_mask, _dot_nt(qb, kb), 0.0)
        for m in HG_LEVELS:
            qd = qb * q_fac[m][:, sls[h]]
            kd = kb * k_fac[m][:, sls[h]] if m in k_fac else kb
            scores = jnp.where(level_masks[m], _dot_nt(qd, kd), scores)
        scores_all.append(scores.astype(BF16))

    sts = [st_refs[b * N_HEADS + h][...] for b, h in pairs]
    outs = []
    for p, (b, h) in enumerate(pairs):
        q_all, e_b, v_all = per_b[b][0], per_b[b][4], per_b[b][6]
        qe = (q_all[:, sls[h]] * e_b[:, sls[h]]).astype(BF16)
        outs.append(_dot(scores_all[p], v_all[:, sls[h]].astype(BF16)) + _dot_nt(qe, sts[p].astype(BF16)))

    for p, (b, h) in enumerate(pairs):
        k_all, e_b, e_bl, v_all = per_b[b][1], per_b[b][4], per_b[b][5], per_b[b][6]
        kl = (k_all[:, sls[h]] * e_bl[:, sls[h]]).astype(BF16)
        vt = v_all[:, sls[h]].T.astype(BF16)
        st_refs[b * N_HEADS + h][...] = e_b[CHUNK - 1:CHUNK, sls[h]] * sts[p] + _dot(vt, kl)

    for p, (b, h) in enumerate(pairs):
        o = outs[p]
        ms = jnp.mean(o * o, axis=-1, keepdims=True)
        y = (o * lax.rsqrt(ms + EPS)) * nw_ref[:, sls[h]] * per_b[b][7][:, sls[h]]
        y_ref[b, :, sls[h]] = y.astype(y_ref.dtype)


def _hgrn2(x3, head, gain, w_hg, tri, lb, norm_w, bb):
    bsz, seq, d = x3.shape
    nc = seq // CHUNK + 1
    blk = pl.BlockSpec((bb, CHUNK, HG_W), lambda i, c: (i, c, 0))
    full = lambda a: pl.BlockSpec(a.shape, lambda i, c: (0, 0))
    return pl.pallas_call(
        _hgrn2_kernel,
        grid=(bsz // bb, nc),
        in_specs=[_chunk_ahead_spec(bb, d, nc), full(head), full(gain), full(w_hg), full(tri), full(lb),
                  full(norm_w)],
        out_specs=blk,
        out_shape=jax.ShapeDtypeStruct((bsz, nc * CHUNK, HG_W), BF16),
        scratch_shapes=([pltpu.VMEM((bb * CHUNK, w_hg.shape[1]), F32)] * 2
                        + [pltpu.VMEM((HG_DV, HG_DK), F32)] * (bb * N_HEADS)),
        compiler_params=_cparams(("parallel", "arbitrary")),
        name="hgrn2",
    )(x3, head, gain, w_hg, tri, lb, norm_w)


def _log_sigmoid(x):
    return jnp.minimum(x, 0.0) - jnp.log(1.0 + jnp.exp(-jnp.abs(x)))


def _mlstm_kernel(x_ref, head_ref, gain_ref, w_ref, tri_ref, cw_ref, cb_ref, wq_ref, wk_ref, wv_ref,
                  gbc_ref, gbr_ref, nw_ref, sk_ref, y_ref, proj_a, proj_b, *scratch):
    c = pl.program_id(1)

    @pl.when(c == 0)
    def _():
        for ref in scratch:
            ref[...] = jnp.zeros_like(ref)

    step = functools.partial(_mlstm_step, c, x_ref.shape[0], tri_ref, cw_ref, cb_ref, wq_ref, wk_ref, wv_ref,
                             gbc_ref, gbr_ref, nw_ref, sk_ref, y_ref, scratch)
    _with_projection_pipeline(step, x_ref, head_ref, gain_ref, w_ref, proj_a, proj_b, c)


def _mlstm_step(c, bb, tri_ref, cw_ref, cb_ref, wq_ref, wk_ref, wv_ref, gbc_ref, gbr_ref, nw_ref, sk_ref, y_ref,
                scratch, cur):
    n_pairs = bb * N_HEADS
    s_refs = scratch[:n_pairs]
    m_refs = scratch[n_pairs:2 * n_pairs]
    tail_refs = scratch[2 * n_pairs:]

    pos_c = c * CHUNK + lax.broadcasted_iota(jnp.int32, (CHUNK, 1), 0)
    valid_c = pos_c >= N_PAD
    pos_r = c * CHUNK + lax.broadcasted_iota(jnp.int32, (1, CHUNK), 1)
    valid_r = pos_r >= N_PAD
    ti = lax.broadcasted_iota(jnp.int32, (CHUNK, CHUNK), 0)
    si = lax.broadcasted_iota(jnp.int32, (CHUNK, CHUNK), 1)
    causal = si <= ti
    tri = tri_ref[...]
    ones_v = jnp.ones((CHUNK, ML_DV), BF16)
    neg_inf = -jnp.inf

    per_b = []
    for b in range(bb):
        rows_b = slice(b * CHUNK, (b + 1) * CHUNK)
        mm = jnp.where(valid_c, cur[rows_b, 0:ML_W], 0.0)
        ext = jnp.concatenate([tail_refs[b][...], mm], axis=0)
        tail_refs[b][...] = mm[CHUNK - 8:CHUNK]
        conv = cb_ref[...]
        for j in range(ML_CONV):
            off = 8 - (ML_CONV - 1) + j
            conv = conv + cw_ref[j:j + 1, :] * ext[off:off + CHUNK]
        cact = conv * _sigmoid(conv)
        cact_b = cact.astype(BF16)
        mm_b = mm.astype(BF16)

        graw = cur[rows_b, 2 * ML_W:2 * ML_W + LANES]
        gcol = graw + gbc_ref[...]
        li_col = jnp.where(valid_c, gcol, neg_inf)
        lf_col = jnp.where(valid_c, _log_sigmoid(gcol), 0.0)
        b_col = _dot_exact_lhs(tri, lf_col)
        grow = graw.T[0:2 * N_HEADS] + gbr_ref[...]
        li_row = jnp.where(valid_r, grow, neg_inf)
        lf_row = jnp.where(valid_r, _log_sigmoid(grow), 0.0)
        r1, r2, r3 = _split3(lf_row)
        b_row = _dot_nt(r1, tri) + _dot_nt(r2, tri) + _dot_nt(r3, tri)

        ogate = _sigmoid(cur[rows_b, ML_W:2 * ML_W])
        per_b.append((cact, cact_b, mm_b, li_col, b_col, li_row, b_row, ogate))

    pairs = [(b, h) for b in range(bb) for h in range(N_HEADS)]
    sls = [slice(h * ML_DV, (h + 1) * ML_DV) for h in range(N_HEADS)]

    qs, ks, vs = [], [], []
    for b, h in pairs:
        cact_b, mm_b = per_b[b][1], per_b[b][2]
        qs.append((_dot(cact_b[:, sls[h]], wq_ref[h]) * (ML_DK ** -0.5)).astype(BF16))
        ks.append(_dot(cact_b[:, sls[h]], wk_ref[h]))
        v = _dot(mm_b[:, sls[h]], wv_ref[h]).astype(BF16)
        vs.append(jnp.concatenate([v, ones_v], axis=1))

    n_p = len(pairs)
    blk = lambda p: slice(p * CHUNK, (p + 1) * CHUNK)
    stack = lambda xs: jnp.concatenate(xs, axis=0)
    bc_all = stack([per_b[b][4][:, N_HEADS + h:N_HEADS + h + 1] for b, h in pairs])
    lic_all = stack([per_b[b][3][:, h:h + 1] for b, h in pairs])
    row_all = stack([jnp.broadcast_to(per_b[b][5][h:h + 1, :] - per_b[b][6][N_HEADS + h:N_HEADS + h + 1, :],
                                      (CHUNK, CHUNK)) for b, h in pairs])
    mprev_all = stack([jnp.broadcast_to(m_refs[p][0:1, 0:1], (CHUNK, 1)) for p in range(n_p)])
    glast_all = stack([jnp.broadcast_to(per_b[b][4][CHUNK - 1:CHUNK, N_HEADS + h:N_HEADS + h + 1], (CHUNK, 1))
                       for b, h in pairs])
    causal_all = stack([causal] * n_p)

    d_all = jnp.where(causal_all, bc_all + row_all, neg_inf)
    a_all = bc_all + mprev_all
    m_t_all = jnp.maximum(a_all, jnp.max(d_all, axis=-1, keepdims=True))
    w_intra_all = jnp.exp(d_all - m_t_all)
    w_inter_all = jnp.exp(a_all - m_t_all)

    qk_all = (stack([_dot_nt(qs[p], ks[p].astype(BF16)) for p in range(n_p)]) * w_intra_all).astype(BF16)
    s_augs = [s_refs[p][...] for p in range(n_p)]
    intra_all = stack([_dot(qk_all[blk(p)], vs[p]) for p in range(n_p)])
    inter_all = stack([_dot(qs[p], s_augs[p].astype(BF16)) for p in range(n_p)])
    numden_all = intra_all + w_inter_all * inter_all
    o_all = numden_all[:, :ML_DV] / jnp.maximum(jnp.abs(numden_all[:, ML_DV:]), jnp.exp(-m_t_all))
    ms_all = jnp.mean(o_all * o_all, axis=-1, keepdims=True)
    on_all = o_all * lax.rsqrt(ms_all + EPS)

    e_all = glast_all - bc_all + lic_all
    gm_all = glast_all + mprev_all
    e_max = jnp.max(e_all.reshape(n_p, CHUNK, 1), axis=1, keepdims=True)
    m_new_all = jnp.maximum(gm_all.reshape(n_p, CHUNK, 1), e_max).reshape(n_p * CHUNK, 1)
    w_s_all = jnp.exp(e_all - m_new_all)
    w_p_all = jnp.exp(gm_all - m_new_all)
    kw_all = stack(ks) * w_s_all
    for p in range(n_p):
        kw_t = kw_all[blk(p)].T.astype(BF16)
        s_refs[p][...] = w_p_all[p * CHUNK:p * CHUNK + 1] * s_augs[p] + _dot(kw_t, vs[p])
        m_refs[p][...] = jnp.broadcast_to(m_new_all[p * CHUNK:p * CHUNK + 1], m_refs[p].shape)

    for p, (b, h) in enumerate(pairs):
        cact, ogate = per_b[b][0], per_b[b][7]
        sl = sls[h]
        y = (on_all[blk(p)] * nw_ref[:, sl] + sk_ref[:, sl] * cact[:, sl]) * ogate[:, sl]
        y_ref[b, :, sl] = y.astype(y_ref.dtype)


def _mlstm(x3, head, gain, w_ml, tri, conv_w, conv_b, wq, wk, wv, gb_col, gb_row, norm_w, skip, bb):
    bsz, seq, d = x3.shape
    nc = seq // CHUNK + 1
    blk = pl.BlockSpec((bb, CHUNK, ML_W), lambda i, c: (i, c, 0))

    def full(a):
        nd = a.ndim
        return pl.BlockSpec(a.shape, lambda i, c: (0,) * nd)

    params = (head, gain, w_ml, tri, conv_w, conv_b, wq, wk, wv, gb_col, gb_row, norm_w, skip)
    return pl.pallas_call(
        _mlstm_kernel,
        grid=(bsz // bb, nc),
        in_specs=[_chunk_ahead_spec(bb, d, nc)] + [full(p) for p in params],
        out_specs=blk,
        out_shape=jax.ShapeDtypeStruct((bsz, nc * CHUNK, ML_W), BF16),
        scratch_shapes=([pltpu.VMEM((bb * CHUNK, w_ml.shape[1]), F32)] * 2
                        + [pltpu.VMEM((ML_DK, 2 * ML_DV), F32)] * (bb * N_HEADS)
                        + [pltpu.VMEM((8, LANES), F32)] * (bb * N_HEADS)
                        + [pltpu.VMEM((8, ML_W), F32)] * bb),
        compiler_params=_cparams(("parallel", "arbitrary")),
        name="mlstm",
    )(x3, *params)


def _merge_route_kernel(x_ref, head_ref, gain_ref, wg_ref, yh_ref, ym_ref, wbh_ref, wbm_ref, wo_ref, nf_ref,
                        rw_ref, rb_ref, h2_ref, xn_ref, te_ref, tw_ref, rk_ref, cnt_ref, *, n_experts):
    d = x_ref.shape[1]
    tq = x_ref.shape[0]
    j = pl.program_id(1)

    @pl.when(jnp.logical_and(pl.program_id(0) == 0, j == 0))
    def _():
        cnt_ref[...] = jnp.zeros_like(cnt_ref)

    h = _h_tile(x_ref, head_ref, j)
    hn = _rms_bf16(h, gain_ref[...])
    g0 = _sigmoid(_dot(hn, wg_ref[:, :d]))
    g1 = _sigmoid(_dot(hn, wg_ref[:, d:]))
    merged = g0 * _dot(yh_ref[...], wbh_ref[...]) + g1 * _dot(ym_ref[...], wbm_ref[...])
    h2 = h + _dot(merged.astype(BF16), wo_ref[...])
    h2_ref[...] = h2
    ms = jnp.mean(h2 * h2, axis=-1, keepdims=True)
    xn = (h2 * lax.rsqrt(ms + EPS)) * nf_ref[...]
    _store_token_tiles(xn_ref, xn)
    x1, x2, _ = _split3(xn)
    logits = (_dot_nt(rw_ref[0], x1) + _dot_nt(rw_ref[1], x1) + _dot_nt(rw_ref[0], x2)) + rb_ref[...]
    sub = lax.broadcasted_iota(jnp.int32, logits.shape, 0)
    work = logits
    vals, idxs = [], []
    for _ in range(TOP_K):
        vmax = jnp.max(work, axis=0, keepdims=True)
        imax = jnp.min(jnp.where(work == vmax, sub, n_experts), axis=0, keepdims=True)
        vals.append(vmax)
        idxs.append(imax)
        work = jnp.where(sub == imax, -jnp.inf, work)
    exps = [jnp.exp(v - vals[0]) for v in vals]
    tot = exps[0] + exps[1] + exps[2] + exps[3]

    krow = lax.broadcasted_iota(jnp.int32, (TOP_K, tq), 0)

    def rows(per_k):
        out = jnp.broadcast_to(per_k[0], (TOP_K, tq))
        for kk in range(1, TOP_K):
            out = jnp.where(krow == kk, per_k[kk], out)
        return out

    te_ref[0] = rows(idxs)
    tw_ref[0] = rows([e / tot for e in exps])

    valid = (j * tq + lax.broadcasted_iota(jnp.int32, (1, tq), 1)) >= N_PAD
    onehots = [jnp.where(jnp.logical_and(sub == idxs[kk], valid), 1.0, 0.0) for kk in range(TOP_K)]
    oh_all = onehots[0] + onehots[1] + onehots[2] + onehots[3]
    ri = lax.broadcasted_iota(jnp.int32, (tq, tq), 0)
    ci = lax.broadcasted_iota(jnp.int32, (tq, tq), 1)
    earlier = jnp.where(ri < ci, 1.0, 0.0).astype(BF16)
    cnt = cnt_ref[:, 0:1]
    before = _dot(oh_all.astype(BF16), earlier) + cnt
    ranks = []
    for kk in range(TOP_K):
        ranks.append(jnp.sum(jnp.where(sub == idxs[kk], before, 0.0), axis=0, keepdims=True))
        before = before + onehots[kk]
    rk_ref[0] = rows(ranks).astype(jnp.int32)
    cnt_ref[...] = jnp.broadcast_to(cnt + jnp.sum(oh_all, axis=1, keepdims=True), cnt_ref.shape)


def _merge_route(x2d, head, gain, w_g, y_hg, y_ml, wbh, wbm, wo, norm_ffn, rw_split, rb, n_experts, bsz, tq):
    d = x2d.shape[1]
    seq = x2d.shape[0] // bsz
    nj = (seq + CHUNK) // tq
    m = bsz * nj * tq
    assert d == TOK_ROWS * LANES, "token-tile layout assumes one (8,128) tile per token"
    row = lambda n: pl.BlockSpec((tq, n), lambda b, j: (b * nj + j, 0))
    tiles = pl.BlockSpec((tq * TOK_ROWS, LANES), lambda b, j: (b * nj + j, 0))
    per_tok = pl.BlockSpec((1, TOP_K, tq), lambda b, j: (b * nj + j, 0, 0))

    def full(a):
        nd = a.ndim
        return pl.BlockSpec(a.shape, lambda b, j: (0,) * nd)

    return pl.pallas_call(
        functools.partial(_merge_route_kernel, n_experts=n_experts),
        grid=(bsz, nj),
        in_specs=[_x_tile_spec(tq, d, seq), full(head), full(gain), full(w_g), row(HG_W), row(ML_W),
                  full(wbh), full(wbm), full(wo), full(norm_ffn), full(rw_split), full(rb)],
        out_specs=[row(d), tiles, per_tok, per_tok, per_tok,
                   pl.BlockSpec((n_experts, LANES), lambda b, j: (0, 0))],
        out_shape=[
            jax.ShapeDtypeStruct((m, d), F32),
            jax.ShapeDtypeStruct((m * TOK_ROWS, LANES), F32),
            jax.ShapeDtypeStruct((bsz * nj, TOP_K, tq), jnp.int32),
            jax.ShapeDtypeStruct((bsz * nj, TOP_K, tq), F32),
            jax.ShapeDtypeStruct((bsz * nj, TOP_K, tq), jnp.int32),
            jax.ShapeDtypeStruct((n_experts, LANES), F32),
        ],
        compiler_params=_cparams(("arbitrary", "arbitrary")),
        name="merge_route",
    )(x2d, head, gain, w_g, y_hg, y_ml, wbh, wbm, wo, norm_ffn, rw_split, rb)


def _toks(first, n=1):
    return pl.ds(pl.multiple_of(first * TOK_ROWS, TOK_ROWS), n * TOK_ROWS)


def _dispatch_kernel(ps_ref, pe_ref, pos_ref, x_ref, xs_hbm, zbuf, sem, *, n_experts, tm):
    b = pl.program_id(0)
    j = pl.program_id(1)
    tq = x_ref.shape[0] // TOK_ROWS

    @pl.when(jnp.logical_and(b == 0, j == 0))
    def _():
        zbuf[...] = jnp.zeros_like(zbuf)
        for e in range(n_experts):
            @pl.when(pe_ref[e] > ps_ref[e])
            def _():
                pltpu.make_async_copy(zbuf, xs_hbm.at[_toks(pe_ref[e] - tm, tm), :], sem).start()
        for e in range(n_experts):
            @pl.when(pe_ref[e] > ps_ref[e])
            def _():
                pltpu.make_async_copy(zbuf, xs_hbm.at[_toks(0, tm), :], sem).wait()

        def zero_tail(blk, carry):
            cp = pltpu.make_async_copy(zbuf, xs_hbm.at[_toks(blk * tm, tm), :], sem)
            cp.start()
            cp.wait()
            return carry
        lax.fori_loop(pe_ref[n_experts - 1] // tm, xs_hbm.shape[0] // (tm * TOK_ROWS), zero_tail, 0)

    def scatter_rows(lo):
        def body(r, carry):
            for kk in range(TOP_K):
                dst = pos_ref[0, kk, r]
                pltpu.make_async_copy(
                    x_ref.at[_toks(r), :], xs_hbm.at[_toks(dst), :], sem).start(priority=kk % 2)
            return carry
        lax.fori_loop(lo, tq, body, 0, unroll=4)
        n = tq - lo
        for _ in range(TOP_K):
            pltpu.make_async_copy(x_ref.at[_toks(0, n), :], xs_hbm.at[_toks(0, n), :], sem).wait()

    @pl.when(j == 0)
    def _():
        scatter_rows(N_PAD)

    @pl.when(j != 0)
    def _():
        scatter_rows(0)


def _dispatch(pstart, pend, pos, xn_t, n_rows, bsz, tq, tm):
    m = xn_t.shape[0] // TOK_ROWS
    nj = m // (bsz * tq)
    n_experts = pstart.shape[0]
    smem_blk = pl.BlockSpec((1, TOP_K, tq), lambda b, j, ps, pe: (b * nj + j, 0, 0),
                            memory_space=pltpu.SMEM)
    grid_spec = pltpu.PrefetchScalarGridSpec(
        num_scalar_prefetch=2,
        grid=(bsz, nj),
        in_specs=[smem_blk,
                  pl.BlockSpec((tq * TOK_ROWS, LANES), lambda b, j, ps, pe: (b * nj + j, 0))],
        out_specs=pl.BlockSpec(memory_space=pl.ANY),
        scratch_shapes=[pltpu.VMEM((tm * TOK_ROWS, LANES), F32), pltpu.SemaphoreType.DMA(())],
    )
    return pl.pallas_call(
        functools.partial(_dispatch_kernel, n_experts=n_experts, tm=tm),
        grid_spec=grid_spec,
        out_shape=jax.ShapeDtypeStruct((n_rows * TOK_ROWS, LANES), F32),
        compiler_params=_cparams(("arbitrary", "arbitrary")),
        name="dispatch",
    )(pstart, pend, pos, xn_t)


CAST_ROWS = 256


def _experts_kernel(be_ref, nu_ref, slot_ref, nxt_ref, x_ref, wgu_hbm, bg_ref, bu_ref, wd_hbm, bd_ref, y_ref,
                    wg_s, wu_s, wd_s, wgu_buf, wd_buf, sem):
    i = pl.program_id(0)
    n_used = nu_ref[0]
    dff = wg_s.shape[0]
    new_expert = jnp.logical_or(i == 0, be_ref[i] != be_ref[jnp.maximum(i - 1, 0)])

    def weight_copies(e, slot):
        return (pltpu.make_async_copy(wgu_hbm.at[e], wgu_buf.at[slot], sem.at[slot]),
                pltpu.make_async_copy(wd_hbm.at[e], wd_buf.at[slot], sem.at[slot]))

    @pl.when(i == 0)
    def _():
        for cp in weight_copies(be_ref[0], 0):
            cp.start()

    for slot in range(2):
        @pl.when(jnp.logical_and(jnp.logical_and(new_expert, i < n_used), slot_ref[i] == slot))
        def _():
            for cp in weight_copies(be_ref[i], slot):
                cp.wait()

            @pl.when(nxt_ref[i] >= 0)
            def _():
                for cp in weight_copies(nxt_ref[i], 1 - slot):
                    cp.start()

            for c in range(dff // CAST_ROWS):
                rows = pl.ds(c * CAST_ROWS, CAST_ROWS)
                wd_s[rows, :] = wd_buf[slot, rows, :].astype(BF16)
                for sl in range(wgu_buf.shape[1]):
                    lanes = pl.ds(sl * LANES, LANES)
                    wg_s[rows, lanes] = wgu_buf[
                        slot, sl, pl.ds(2 * c * CAST_ROWS, CAST_ROWS, stride=2), :].astype(BF16)
                    wu_s[rows, lanes] = wgu_buf[
                        slot, sl, pl.ds(2 * c * CAST_ROWS + 1, CAST_ROWS, stride=2), :].astype(BF16)

    @pl.when(i < n_used)
    def _():
        tm = x_ref.shape[0] // TOK_ROWS
        xb = _load_token_tiles(x_ref, tm).astype(BF16)
        g = _dot_nt(xb, wg_s[...]) + bg_ref[0]
        u = _dot_nt(xb, wu_s[...]) + bu_ref[0]
        gate = jnp.minimum(g, SWIGLU_LIMIT)
        up = jnp.clip(u, -SWIGLU_LIMIT, SWIGLU_LIMIT)
        act = (up + 1.0) * gate * _sigmoid(SWIGLU_ALPHA * gate)
        _store_token_tiles(y_ref, _dot(act.astype(BF16), wd_s[...]) + bd_ref[0])

    @pl.when(i >= n_used)
    def _():
        y_ref[...] = jnp.zeros_like(y_ref)


def _experts(block_e, n_used, xs, w_gu_t, b_g, b_u, w_d, b_d, tm):
    n_blocks = block_e.shape[0]
    dff, d = w_d.shape[1:]
    tile_blk = lambda f: pl.BlockSpec((tm * TOK_ROWS, LANES), f)
    assert dff % CAST_ROWS == 0
    blk = jnp.arange(n_blocks, dtype=jnp.int32)
    changed = jnp.concatenate([jnp.zeros((1,), jnp.int32), (block_e[1:] != block_e[:-1]).astype(jnp.int32)])
    w_slot = jnp.cumsum(changed) % 2
    later = jnp.logical_and(jnp.logical_and(blk[None, :] > blk[:, None], blk[None, :] < n_used[0]),
                            block_e[None, :] != block_e[:, None])
    nxt_e = jnp.where(jnp.any(later, axis=1), block_e[jnp.argmax(later, axis=1)], -1).astype(jnp.int32)
    wspec = lambda k, n: pl.BlockSpec((1, k, n), lambda i, be, nu, ws, ne: (be[i], 0, 0))
    grid_spec = pltpu.PrefetchScalarGridSpec(
        num_scalar_prefetch=4,
        grid=(n_blocks,),
        in_specs=[
            tile_blk(lambda i, be, nu, ws, ne: (jnp.minimum(i, nu[0] - 1), 0)),
            pl.BlockSpec(memory_space=pl.ANY),
            wspec(1, dff), wspec(1, dff),
            pl.BlockSpec(memory_space=pl.ANY),
            wspec(1, d),
        ],
        out_specs=tile_blk(lambda i, be, nu, ws, ne: (i, 0)),
        scratch_shapes=[pltpu.VMEM((dff, d), BF16), pltpu.VMEM((dff, d), BF16), pltpu.VMEM((dff, d), BF16),
                        pltpu.VMEM((2,) + w_gu_t.shape[1:], F32), pltpu.VMEM((2, dff, d), F32),
                        pltpu.SemaphoreType.DMA((2,))],
    )
    return pl.pallas_call(
        _experts_kernel,
        grid_spec=grid_spec,
        out_shape=jax.ShapeDtypeStruct((n_blocks * tm * TOK_ROWS, LANES), F32),
        compiler_params=_cparams(("arbitrary",)),
        name="experts",
    )(block_e, n_used, w_slot, nxt_e, xs, w_gu_t, b_g, b_u, w_d, b_d)


def _combine_kernel(pos_ref, posn_ref, hsrc_ref, hsrcn_ref, tw_ref, y_hbm, h_hbm,
                    nw_ref, o_ref, ybuf, hbuf, sem):
    i = pl.program_id(0)
    n = pl.num_programs(0)
    tt = o_ref.shape[0]
    nt = tt * TOK_ROWS

    def copies(src_pos_ref, src_h_ref, dst_slot, r):
        return [pltpu.make_async_copy(
            y_hbm.at[_toks(src_pos_ref[0, kk, r]), :],
            ybuf.at[dst_slot, _toks(kk * tt + r), :],
            sem.at[dst_slot]) for kk in range(TOP_K)]

    def start_h(src_h_ref, dst_slot):
        h_row = pl.multiple_of(src_h_ref[0, 0, 0], 8)
        pltpu.make_async_copy(h_hbm.at[pl.ds(h_row, tt), :], hbuf.at[dst_slot], sem.at[dst_slot]).start()

    @pl.when(i == 0)
    def _():
        def body(r, carry):
            for kk, cp in enumerate(copies(pos_ref, hsrc_ref, 0, r)):
                cp.start(priority=kk % 2)
            return carry
        lax.fori_loop(0, tt, body, 0, unroll=2)
        start_h(hsrc_ref, 0)

    def wait_slot(s):
        pltpu.make_async_copy(ybuf.at[s], ybuf.at[s], sem.at[s]).wait()
        pltpu.make_async_copy(hbuf.at[s], hbuf.at[s], sem.at[s]).wait()

    def step(slot):
        for r in range(tt):
            for kk, cp in enumerate(copies(posn_ref, hsrcn_ref, 1 - slot, r)):
                cp.start(priority=kk % 2)
        start_h(hsrcn_ref, 1 - slot)

        wait_slot(slot)
        acc = hbuf[slot]
        tw = tw_ref[...]
        for kk in range(TOP_K):
            yk = jnp.concatenate(
                [ybuf[slot, pl.ds(kk * nt + s, tt, stride=TOK_ROWS), :] for s in range(TOK_ROWS)], axis=1)
            acc = acc + tw[:, kk:kk + 1] * yk
        ms = jnp.mean(acc * acc, axis=-1, keepdims=True)
        o_ref[...] = (acc * lax.rsqrt(ms + EPS)) * nw_ref[...]

        @pl.when(i == n - 1)
        def _():
            wait_slot(1 - slot)

    for parity in range(2):
        pl.when(i % 2 == parity)(functools.partial(step, parity))


def _combine(pos_seq, tw_seq, h_src, y_rows, h2, norm_w, tt):
    n_tiles = h_src.shape[0]
    d = h2.shape[1]
    last = n_tiles - 1
    nt = tt * TOK_ROWS
    cur = lambda i: (i, 0, 0)
    nxt = lambda i: (jnp.minimum(i + 1, last), 0, 0)
    idx_blk = lambda f: pl.BlockSpec((1, TOP_K, tt), f, memory_space=pltpu.SMEM)
    one_blk = lambda f: pl.BlockSpec((1, 1, 1), f, memory_space=pltpu.SMEM)
    return pl.pallas_call(
        _combine_kernel,
        grid=(n_tiles,),
        in_specs=[
            idx_blk(cur), idx_blk(nxt), one_blk(cur), one_blk(nxt),
            pl.BlockSpec((tt, TOP_K), lambda i: (i, 0)),
            pl.BlockSpec(memory_space=pl.ANY),
            pl.BlockSpec(memory_space=pl.ANY),
            pl.BlockSpec(norm_w.shape, lambda i: (0, 0)),
        ],
        out_specs=pl.BlockSpec((tt, d), lambda i: (i, 0)),
        scratch_shapes=[pltpu.VMEM((2, TOP_K * nt, LANES), F32), pltpu.VMEM((2, tt, d), F32),
                        pltpu.SemaphoreType.DMA((2,))],
        out_shape=jax.ShapeDtypeStruct((n_tiles * tt, d), F32),
        compiler_params=_cparams(("arbitrary",)),
        name="combine",
    )(pos_seq, pos_seq, h_src, h_src, tw_seq, y_rows, h2, norm_w)


def _pick(n, prefs):
    for p in prefs:
        if n % p == 0:
            return p
    raise ValueError(f"no tile in {prefs} divides {n}")


def kernel(x, meta_tokens, hg_lb_logits, norm_mix, w_in, hg_norm, ml_conv_w, ml_conv_b, ml_wq, ml_wk, ml_wv,
           ml_gate_b, ml_norm, ml_skip, w_branch_hg, w_branch_ml, w_out, norm_ffn, router_w, router_b,
           exp_w_gu, exp_b_gu, exp_w_down, exp_b_down, norm_final):
    bsz, seq, d = x.shape
    assert norm_mix.shape[0] == 1, "single-layer block"
    assert seq % CHUNK == 0 and d % LANES == 0
    t = CHUNK + seq
    m_rows = bsz * t
    n_experts = router_w.shape[-1]
    dff = exp_w_down.shape[2]
    assert n_experts <= LANES

    head = jnp.concatenate([jnp.zeros((N_PAD, d), x.dtype), meta_tokens.astype(x.dtype)], axis=0)
    x2d = x.reshape(bsz * seq, d)
    lower_bounds = jnp.cumsum(jax.nn.softmax(hg_lb_logits.astype(F32), axis=0), axis=0)

    w = w_in[0]
    n_hg = 4 * HG_W
    n_a = n_hg + 2 * ML_W
    w_hg = w[:, :n_hg].astype(BF16)
    w_m = jnp.pad(w[:, n_a:n_a + 2 * N_HEADS], ((0, 0), (0, LANES - 2 * N_HEADS)))
    w_ml = jnp.concatenate([w[:, n_hg:n_a], w_m], axis=1).astype(BF16)
    w_g = w[:, n_a + 2 * N_HEADS:].astype(BF16)
    gain = norm_mix[0][None]

    bb = _pick(bsz, (8, 4, 2, 1))
    tri = jnp.asarray(np.tril(np.ones((CHUNK, CHUNK), np.float32)), BF16)
    y_hg = _hgrn2(x, head, gain, w_hg, tri, lower_bounds[0][None], hg_norm[0][None], bb)

    gb = ml_gate_b[0].astype(F32)
    gb_col = jnp.pad(gb, (0, LANES - 2 * N_HEADS))[None]
    gb_row = jnp.broadcast_to(gb[:, None], (2 * N_HEADS, CHUNK))
    y_ml = _mlstm(x, head, gain, w_ml, tri, ml_conv_w[0], ml_conv_b[0][None],
                  ml_wq[0].astype(BF16), ml_wk[0].astype(BF16), ml_wv[0].astype(BF16),
                  gb_col, gb_row, ml_norm[0][None], ml_skip[0][None], bb)

    assert n_experts % 8 == 0
    tq = _pick(t, (704, 192, 64))
    rw = router_w[0].astype(F32).T
    rw1 = rw.astype(BF16)
    rw2 = (rw - rw1.astype(F32)).astype(BF16)
    rw_split = jnp.stack([rw1, rw2])
    rb = jnp.broadcast_to(router_b[0].astype(F32)[:, None], (n_experts, tq))
    h2, xn2, top_e, top_w, rank, cnt = _merge_route(
        x2d, head, gain, w_g, y_hg.reshape(m_rows, HG_W), y_ml.reshape(m_rows, ML_W),
        w_branch_hg[0].astype(BF16), w_branch_ml[0].astype(BF16), w_out[0].astype(BF16),
        norm_ffn[0][None], rw_split, rb, n_experts, bsz, tq)

    tm6 = 512
    n_assign = bsz * (t - N_PAD) * TOP_K
    counts = cnt[:, 0].astype(jnp.int32)
    padded = ((counts + tm6 - 1) // tm6) * tm6
    pend = jnp.cumsum(padded).astype(jnp.int32)
    pstart = pend - padded
    n_blocks = -(-n_assign // tm6) + n_experts
    n_rows = n_blocks * tm6
    blk_start = jnp.arange(n_blocks, dtype=jnp.int32) * tm6
    block_e = jnp.minimum(jnp.sum((blk_start[:, None] >= pend[None, :]).astype(jnp.int32), axis=1),
                          n_experts - 1)
    n_used = (pend[-1] // tm6)[None]
    pos = rank + jnp.sum(jnp.where(top_e[..., None] == jnp.arange(n_experts, dtype=jnp.int32), pstart, 0),
                         axis=-1)
    xs = _dispatch(pstart, pend, pos, xn2, n_rows, bsz, tq, tm6)

    w_gu_t = jnp.swapaxes(exp_w_gu[0].reshape(n_experts, d // LANES, LANES, 2 * dff), 2, 3)
    bgu = exp_b_gu[0]
    y_rows = _experts(block_e, n_used, xs, w_gu_t, bgu[:, None, 0::2], bgu[:, None, 1::2],
                      exp_w_down[0], exp_b_down[0][:, None, :], tm6)

    tt = _pick(seq, (256, 128, 64))
    tiles_per_b = seq // tt
    n_tiles = bsz * tiles_per_b
    nj = t // tq

    def seq_tiles(a):
        a = a.reshape(bsz, nj, TOP_K, tq).transpose(0, 2, 1, 3).reshape(bsz, TOP_K, t)[:, :, CHUNK:]
        return a.reshape(bsz, TOP_K, tiles_per_b, tt).transpose(0, 2, 1, 3).reshape(n_tiles, TOP_K, tt)

    h_src = (jnp.arange(bsz, dtype=jnp.int32)[:, None] * t + CHUNK
             + jnp.arange(tiles_per_b, dtype=jnp.int32)[None, :] * tt).reshape(n_tiles, 1, 1)
    tw_cols = seq_tiles(top_w).transpose(0, 2, 1).reshape(n_tiles * tt, TOP_K)
    out = _combine(seq_tiles(pos), tw_cols, h_src, y_rows, h2, norm_final[None], tt)
    return out.reshape(bsz, seq, d)
```

```python
import functools

import numpy as np
import jax
import jax.numpy as jnp
from jax import lax
from jax.experimental import pallas as pl
from jax.experimental.pallas import tpu as pltpu

F32 = jnp.float32
BF16 = jnp.bfloat16

N_META = 16
CHUNK = 64
N_PAD = CHUNK - N_META
EPS = 1e-6

N_HEADS = 4
HG_DK = 128
HG_DV = 128
HG_W = N_HEADS * HG_DV
ML_DK = 64
ML_DV = 128
ML_W = N_HEADS * ML_DV
ML_CONV = 4
TOP_K = 4
SWIGLU_LIMIT = 7.0
SWIGLU_ALPHA = 1.702

LANES = 128
VMEM_LIMIT_BYTES = 56 * 1024 * 1024

HG_LEVELS = (32, 16, 8, 4, 2, 1)
GROUP_ROWS = 8


def _cparams(sem):
    return pltpu.CompilerParams(dimension_semantics=sem, vmem_limit_bytes=VMEM_LIMIT_BYTES)


def _sigmoid(x):
    return 0.5 + 0.5 * jnp.tanh(0.5 * x)


def _split3(x):
    x1 = x.astype(BF16)
    r1 = x - x1.astype(F32)
    x2 = r1.astype(BF16)
    x3 = (r1 - x2.astype(F32)).astype(BF16)
    return x1, x2, x3


def _dot(a, b):
    return jnp.dot(a, b, preferred_element_type=F32)


TOK_ROWS = 8


def _store_token_tiles(ref, x):
    n = x.shape[0]
    for s in range(TOK_ROWS):
        ref[pl.ds(s, n, stride=TOK_ROWS), :] = x[:, s * LANES:(s + 1) * LANES]


def _load_token_tiles(ref, n):
    return jnp.concatenate([ref[pl.ds(s, n, stride=TOK_ROWS), :] for s in range(TOK_ROWS)], axis=1)


def _dot_nt(a, b):
    return lax.dot_general(a, b, (((1,), (1,)), ((), ())), preferred_element_type=F32)


def _dot_exact_lhs(m_bf16, x):
    x1, x2, x3 = _split3(x)
    return _dot(m_bf16, x1) + _dot(m_bf16, x2) + _dot(m_bf16, x3)


def _h_tile(x_ref, head_ref, j):
    x = x_ref[...]
    first = jnp.concatenate([head_ref[...], x[:x.shape[0] - CHUNK]], axis=0)
    return jnp.where(j == 0, first, x)


def _x_tile_spec(tr, d, seq):
    assert seq % 8 == 0 and tr % 8 == 0 and CHUNK % 8 == 0
    return pl.BlockSpec(
        (pl.Element(tr), pl.Element(d)),
        lambda b, j: (pl.multiple_of(b * seq + jnp.maximum(j * tr - CHUNK, 0), 8), 0))


def _rms_bf16(x, gain):
    ms = jnp.mean(x * x, axis=-1, keepdims=True)
    return ((x * lax.rsqrt(ms + EPS)) * gain).astype(BF16)


def _project_ahead(x_ref, gain_ref, w_ref, nxt_ref):
    x = x_ref[...].reshape(x_ref.shape[0] * CHUNK, x_ref.shape[2])
    nxt_ref[...] = _dot(_rms_bf16(x, gain_ref[...]), w_ref[...])


def _with_projection_pipeline(step, x_ref, head_ref, gain_ref, w_ref, proj_a, proj_b, c):
    @pl.when(c == 0)
    def _():
        p0 = _dot(_rms_bf16(head_ref[...], gain_ref[...]), w_ref[...])
        for b in range(x_ref.shape[0]):
            proj_a[b * CHUNK:(b + 1) * CHUNK, :] = p0

    for parity, (cur, nxt) in enumerate(((proj_a, proj_b), (proj_b, proj_a))):
        @pl.when(c % 2 == parity)
        def _():
            step(cur, functools.partial(_project_ahead, x_ref, gain_ref, w_ref, nxt))


def _chunk_ahead_spec(bb, d, nc):
    return pl.BlockSpec((bb, CHUNK, d), lambda i, c: (i, jnp.minimum(c, nc - 2), 0))


def _hgrn2_kernel(x_ref, head_ref, gain_ref, w_ref, tri_ref, lb_ref, nw_ref, y_ref, proj_a, proj_b, *st_refs):
    c = pl.program_id(1)

    @pl.when(c == 0)
    def _():
        for st_ref in st_refs:
            st_ref[...] = jnp.zeros_like(st_ref)

    def step(cur, project):
        bb = x_ref.shape[0]
        for b0 in range(0, bb, GROUP_ROWS):
            _hgrn2_group(c, range(b0, min(b0 + GROUP_ROWS, bb)), tri_ref, lb_ref, nw_ref, y_ref, st_refs, cur,
                         project if b0 == 0 else None)

    _with_projection_pipeline(step, x_ref, head_ref, gain_ref, w_ref, proj_a, proj_b, c)


def _hgrn2_group(c, bs, tri_ref, lb_ref, nw_ref, y_ref, st_refs, cur, project):
    row = lax.broadcasted_iota(jnp.int32, (CHUNK, 1), 0)
    valid = (c * CHUNK + row) >= N_PAD
    ti = lax.broadcasted_iota(jnp.int32, (CHUNK, CHUNK), 0)
    si = lax.broadcasted_iota(jnp.int32, (CHUNK, CHUNK), 1)
    diag_mask = ti == si
    level_masks = {}
    for m in HG_LEVELS:
        same_pair = (ti & ~(2 * m - 1)) == (si & ~(2 * m - 1))
        level_masks[m] = same_pair & ((ti & m) != 0) & ((si & m) == 0)

    lb = lb_ref[...]
    tri = tri_ref[...]
    per_b = {}
    for b in bs:
        rows_b = slice(b * CHUNK, (b + 1) * CHUNK)
        hf = cur[rows_b, HG_W:2 * HG_W]
        f = lb + (1.0 - lb) * _sigmoid(hf)
        f = jnp.where(valid, f, 1.0)
        logf = jnp.log(f)
        k_all = 1.0 - f
        hq = cur[rows_b, 0:HG_W]
        q_all = hq * _sigmoid(hq)
        b_cum = _dot_exact_lhs(tri, logf)
        e_b = jnp.exp(b_cum)
        e_bl = jnp.exp(b_cum[CHUNK - 1:CHUNK] - b_cum)

        q_fac, k_fac = {}, {}
        for m in HG_LEVELS:
            if m >= 4:
                grp = b_cum.reshape(CHUNK // (2 * m), 2 * m, HG_W)
                e = (grp - grp[:, m - 1:m, :]).reshape(CHUNK, HG_W)
                q_fac[m] = jnp.exp(jnp.minimum(e, 0.0))
                k_fac[m] = jnp.exp(jnp.minimum(-e, 0.0))
        f_prev = pltpu.roll(f, 1, 0)
        f_next = pltpu.roll(f, CHUNK - 1, 0)
        r4 = row & 3
        q_fac[2] = jnp.where(r4 == 2, f, jnp.where(r4 == 3, f * f_prev, 1.0))
        k_fac[2] = jnp.where(r4 == 0, f_next, 1.0)
        q_fac[1] = jnp.where((row & 1) == 1, f, 1.0)

        hg = cur[rows_b, 3 * HG_W:4 * HG_W]
        v_all = cur[rows_b, 2 * HG_W:3 * HG_W]
        q_fac = {m: v.astype(BF16) for m, v in q_fac.items()}
        k_fac = {m: v.astype(BF16) for m, v in k_fac.items()}
        per_b[b] = (q_all, k_all, q_fac, k_fac, e_b, e_bl, v_all, hg * _sigmoid(hg))

    if project is not None:
        project()

    pairs = [(b, h) for b in bs for h in range(N_HEADS)]
    sls = [slice(h * HG_DK, (h + 1) * HG_DK) for h in range(N_HEADS)]

    scores_all = []
    for b, h in pairs:
        q_all, k_all, q_fac, k_fac = per_b[b][:4]
        qb = q_all[:, sls[h]].astype(BF16)
        kb = k_all[:, sls[h]].astype(BF16)
        scores = jnp.where(diag_mask, _dot_nt(qb, kb), 0.0)
        for m in HG_LEVELS:
            qd = qb * q_fac[m][:, sls[h]]
            kd = kb * k_fac[m][:, sls[h]] if m in k_fac else kb
            scores = jnp.where(level_masks[m], _dot_nt(qd, kd), scores)
        scores_all.append(scores.astype(BF16))

    sts = [st_refs[b * N_HEADS + h][...] for b, h in pairs]
    outs = []
    for p, (b, h) in enumerate(pairs):
        q_all, e_b, v_all = per_b[b][0], per_b[b][4], per_b[b][6]
        qe = (q_all[:, sls[h]] * e_b[:, sls[h]]).astype(BF16)
        outs.append(_dot(scores_all[p], v_all[:, sls[h]].astype(BF16)) + _dot_nt(qe, sts[p].astype(BF16)))

    for p, (b, h) in enumerate(pairs):
        k_all, e_b, e_bl, v_all = per_b[b][1], per_b[b][4], per_b[b][5], per_b[b][6]
        kl = (k_all[:, sls[h]] * e_bl[:, sls[h]]).astype(BF16)
        vt = v_all[:, sls[h]].T.astype(BF16)
        st_refs[b * N_HEADS + h][...] = e_b[CHUNK - 1:CHUNK, sls[h]] * sts[p] + _dot(vt, kl)

    for p, (b, h) in enumerate(pairs):
        o = outs[p]
        ms = jnp.mean(o * o, axis=-1, keepdims=True)
        y = (o * lax.rsqrt(ms + EPS)) * nw_ref[:, sls[h]] * per_b[b][7][:, sls[h]]
        y_ref[b, :, sls[h]] = y.astype(y_ref.dtype)


def _hgrn2(x3, head, gain, w_hg, tri, lb, norm_w, bb):
    bsz, seq, d = x3.shape
    nc = seq // CHUNK + 1
    blk = pl.BlockSpec((bb, CHUNK, HG_W), lambda i, c: (i, c, 0))
    full = lambda a: pl.BlockSpec(a.shape, lambda i, c: (0, 0))
    return pl.pallas_call(
        _hgrn2_kernel,
        grid=(bsz // bb, nc),
        in_specs=[_chunk_ahead_spec(bb, d, nc), full(head), full(gain), full(w_hg), full(tri), full(lb),
                  full(norm_w)],
        out_specs=blk,
        out_shape=jax.ShapeDtypeStruct((bsz, nc * CHUNK, HG_W), BF16),
        scratch_shapes=([pltpu.VMEM((bb * CHUNK, w_hg.shape[1]), F32)] * 2
                        + [pltpu.VMEM((HG_DV, HG_DK), F32)] * (bb * N_HEADS)),
        compiler_params=_cparams(("parallel", "arbitrary")),
        name="hgrn2",
    )(x3, head, gain, w_hg, tri, lb, norm_w)


def _log_sigmoid(x):
    return jnp.minimum(x, 0.0) - jnp.log(1.0 + jnp.exp(-jnp.abs(x)))


def _mlstm_kernel(x_ref, head_ref, gain_ref, w_ref, tri_ref, cw_ref, cb_ref, wq_ref, wk_ref, wv_ref,
                  gbc_ref, gbr_ref, nw_ref, sk_ref, y_ref, proj_a, proj_b, *scratch):
    c = pl.program_id(1)

    @pl.when(c == 0)
    def _():
        for ref in scratch:
            ref[...] = jnp.zeros_like(ref)

    step = functools.partial(_mlstm_step, c, x_ref.shape[0], tri_ref, cw_ref, cb_ref, wq_ref, wk_ref, wv_ref,
                             gbc_ref, gbr_ref, nw_ref, sk_ref, y_ref, scratch)
    _with_projection_pipeline(step, x_ref, head_ref, gain_ref, w_ref, proj_a, proj_b, c)


def _mlstm_step(c, bb, tri_ref, cw_ref, cb_ref, wq_ref, wk_ref, wv_ref, gbc_ref, gbr_ref, nw_ref, sk_ref, y_ref,
                scratch, cur, project):
    n_pairs = bb * N_HEADS
    s_refs = scratch[:n_pairs]
    m_refs = scratch[n_pairs:2 * n_pairs]
    tail_refs = scratch[2 * n_pairs:]

    pos_c = c * CHUNK + lax.broadcasted_iota(jnp.int32, (CHUNK, 1), 0)
    valid_c = pos_c >= N_PAD
    pos_r = c * CHUNK + lax.broadcasted_iota(jnp.int32, (1, CHUNK), 1)
    valid_r = pos_r >= N_PAD
    ti = lax.broadcasted_iota(jnp.int32, (CHUNK, CHUNK), 0)
    si = lax.broadcasted_iota(jnp.int32, (CHUNK, CHUNK), 1)
    causal = si <= ti
    tri = tri_ref[...]
    ones_v = jnp.ones((CHUNK, ML_DV), BF16)
    neg_inf = -jnp.inf

    per_b = []
    for b in range(bb):
        rows_b = slice(b * CHUNK, (b + 1) * CHUNK)
        mm = jnp.where(valid_c, cur[rows_b, 0:ML_W], 0.0)
        ext = jnp.concatenate([tail_refs[b][...], mm], axis=0)
        tail_refs[b][...] = mm[CHUNK - 8:CHUNK]
        conv = cb_ref[...]
        for j in range(ML_CONV):
            off = 8 - (ML_CONV - 1) + j
            conv = conv + cw_ref[j:j + 1, :] * ext[off:off + CHUNK]
        cact = conv * _sigmoid(conv)
        cact_b = cact.astype(BF16)
        mm_b = mm.astype(BF16)

        graw = cur[rows_b, 2 * ML_W:2 * ML_W + LANES]
        gcol = graw + gbc_ref[...]
        li_col = jnp.where(valid_c, gcol, neg_inf)
        lf_col = jnp.where(valid_c, _log_sigmoid(gcol), 0.0)
        b_col = _dot_exact_lhs(tri, lf_col)
        grow = graw.T[0:2 * N_HEADS] + gbr_ref[...]
        li_row = jnp.where(valid_r, grow, neg_inf)
        lf_row = jnp.where(valid_r, _log_sigmoid(grow), 0.0)
        r1, r2, r3 = _split3(lf_row)
        b_row = _dot_nt(r1, tri) + _dot_nt(r2, tri) + _dot_nt(r3, tri)

        ogate = _sigmoid(cur[rows_b, ML_W:2 * ML_W])
        per_b.append((cact, cact_b, mm_b, li_col, b_col, li_row, b_row, ogate))

    project()

    pairs = [(b, h) for b in range(bb) for h in range(N_HEADS)]
    sls = [slice(h * ML_DV, (h + 1) * ML_DV) for h in range(N_HEADS)]

    qs, ks, vs = [], [], []
    for b, h in pairs:
        cact_b, mm_b = per_b[b][1], per_b[b][2]
        qs.append((_dot(cact_b[:, sls[h]], wq_ref[h]) * (ML_DK ** -0.5)).astype(BF16))
        ks.append(_dot(cact_b[:, sls[h]], wk_ref[h]))
        v = _dot(mm_b[:, sls[h]], wv_ref[h]).astype(BF16)
        vs.append(jnp.concatenate([v, ones_v], axis=1))

    n_p = len(pairs)
    blk = lambda p: slice(p * CHUNK, (p + 1) * CHUNK)
    stack = lambda xs: jnp.concatenate(xs, axis=0)
    bc_all = stack([per_b[b][4][:, N_HEADS + h:N_HEADS + h + 1] for b, h in pairs])
    lic_all = stack([per_b[b][3][:, h:h + 1] for b, h in pairs])
    row_all = stack([jnp.broadcast_to(per_b[b][5][h:h + 1, :] - per_b[b][6][N_HEADS + h:N_HEADS + h + 1, :],
                                      (CHUNK, CHUNK)) for b, h in pairs])
    mprev_all = stack([jnp.broadcast_to(m_refs[p][0:1, 0:1], (CHUNK, 1)) for p in range(n_p)])
    glast_all = stack([jnp.broadcast_to(per_b[b][4][CHUNK - 1:CHUNK, N_HEADS + h:N_HEADS + h + 1], (CHUNK, 1))
                       for b, h in pairs])
    causal_all = stack([causal] * n_p)

    d_all = jnp.where(causal_all, bc_all + row_all, neg_inf)
    a_all = bc_all + mprev_all
    m_t_all = jnp.maximum(a_all, jnp.max(d_all, axis=-1, keepdims=True))
    w_intra_all = jnp.exp(d_all - m_t_all)
    w_inter_all = jnp.exp(a_all - m_t_all)

    qk_all = (stack([_dot_nt(qs[p], ks[p].astype(BF16)) for p in range(n_p)]) * w_intra_all).astype(BF16)
    s_augs = [s_refs[p][...] for p in range(n_p)]
    intra_all = stack([_dot(qk_all[blk(p)], vs[p]) for p in range(n_p)])
    inter_all = stack([_dot(qs[p], s_augs[p].astype(BF16)) for p in range(n_p)])
    numden_all = intra_all + w_inter_all * inter_all
    o_all = numden_all[:, :ML_DV] / jnp.maximum(jnp.abs(numden_all[:, ML_DV:]), jnp.exp(-m_t_all))
    ms_all = jnp.mean(o_all * o_all, axis=-1, keepdims=True)
    on_all = o_all * lax.rsqrt(ms_all + EPS)

    e_all = glast_all - bc_all + lic_all
    gm_all = glast_all + mprev_all
    e_max = jnp.max(e_all.reshape(n_p, CHUNK, 1), axis=1, keepdims=True)
    m_new_all = jnp.maximum(gm_all.reshape(n_p, CHUNK, 1), e_max).reshape(n_p * CHUNK, 1)
    w_s_all = jnp.exp(e_all - m_new_all)
    w_p_all = jnp.exp(gm_all - m_new_all)
    kw_all = stack(ks) * w_s_all
    for p in range(n_p):
        kw_t = kw_all[blk(p)].T.astype(BF16)
        s_refs[p][...] = w_p_all[p * CHUNK:p * CHUNK + 1] * s_augs[p] + _dot(kw_t, vs[p])
        m_refs[p][...] = jnp.broadcast_to(m_new_all[p * CHUNK:p * CHUNK + 1], m_refs[p].shape)

    for p, (b, h) in enumerate(pairs):
        cact, ogate = per_b[b][0], per_b[b][7]
        sl = sls[h]
        y = (on_all[blk(p)] * nw_ref[:, sl] + sk_ref[:, sl] * cact[:, sl]) * ogate[:, sl]
        y_ref[b, :, sl] = y.astype(y_ref.dtype)


def _mlstm(x3, head, gain, w_ml, tri, conv_w, conv_b, wq, wk, wv, gb_col, gb_row, norm_w, skip, bb):
    bsz, seq, d = x3.shape
    nc = seq // CHUNK + 1
    blk = pl.BlockSpec((bb, CHUNK, ML_W), lambda i, c: (i, c, 0))

    def full(a):
        nd = a.ndim
        return pl.BlockSpec(a.shape, lambda i, c: (0,) * nd)

    params = (head, gain, w_ml, tri, conv_w, conv_b, wq, wk, wv, gb_col, gb_row, norm_w, skip)
    return pl.pallas_call(
        _mlstm_kernel,
        grid=(bsz // bb, nc),
        in_specs=[_chunk_ahead_spec(bb, d, nc)] + [full(p) for p in params],
        out_specs=blk,
        out_shape=jax.ShapeDtypeStruct((bsz, nc * CHUNK, ML_W), BF16),
        scratch_shapes=([pltpu.VMEM((bb * CHUNK, w_ml.shape[1]), F32)] * 2
                        + [pltpu.VMEM((ML_DK, 2 * ML_DV), F32)] * (bb * N_HEADS)
                        + [pltpu.VMEM((8, LANES), F32)] * (bb * N_HEADS)
                        + [pltpu.VMEM((8, ML_W), F32)] * bb),
        compiler_params=_cparams(("parallel", "arbitrary")),
        name="mlstm",
    )(x3, *params)


def _merge_route_kernel(x_ref, head_ref, gain_ref, wg_ref, yh_ref, ym_ref, wbh_ref, wbm_ref, wo_ref, nf_ref,
                        rw_ref, rb_ref, h2_ref, xn_ref, te_ref, tw_ref, rk_ref, cnt_ref, *, n_experts):
    d = x_ref.shape[1]
    tq = x_ref.shape[0]
    j = pl.program_id(1)

    @pl.when(jnp.logical_and(pl.program_id(0) == 0, j == 0))
    def _():
        cnt_ref[...] = jnp.zeros_like(cnt_ref)

    h = _h_tile(x_ref, head_ref, j)
    hn = _rms_bf16(h, gain_ref[...])
    g0 = _sigmoid(_dot(hn, wg_ref[:, :d]))
    g1 = _sigmoid(_dot(hn, wg_ref[:, d:]))
    merged = g0 * _dot(yh_ref[...], wbh_ref[...]) + g1 * _dot(ym_ref[...], wbm_ref[...])
    h2 = h + _dot(merged.astype(BF16), wo_ref[...])
    h2_ref[...] = h2
    ms = jnp.mean(h2 * h2, axis=-1, keepdims=True)
    xn = (h2 * lax.rsqrt(ms + EPS)) * nf_ref[...]
    _store_token_tiles(xn_ref, xn)
    x1, x2, _ = _split3(xn)
    logits = (_dot_nt(rw_ref[0], x1) + _dot_nt(rw_ref[1], x1) + _dot_nt(rw_ref[0], x2)) + rb_ref[...]
    sub = lax.broadcasted_iota(jnp.int32, logits.shape, 0)
    work = logits
    vals, idxs = [], []
    for _ in range(TOP_K):
        vmax = jnp.max(work, axis=0, keepdims=True)
        imax = jnp.min(jnp.where(work == vmax, sub, n_experts), axis=0, keepdims=True)
        vals.append(vmax)
        idxs.append(imax)
        work = jnp.where(sub == imax, -jnp.inf, work)
    exps = [jnp.exp(v - vals[0]) for v in vals]
    tot = exps[0] + exps[1] + exps[2] + exps[3]

    krow = lax.broadcasted_iota(jnp.int32, (TOP_K, tq), 0)

    def rows(per_k):
        out = jnp.broadcast_to(per_k[0], (TOP_K, tq))
        for kk in range(1, TOP_K):
            out = jnp.where(krow == kk, per_k[kk], out)
        return out

    te_ref[0] = rows(idxs)
    tw_ref[0] = rows([e / tot for e in exps])

    valid = (j * tq + lax.broadcasted_iota(jnp.int32, (1, tq), 1)) >= N_PAD
    onehots = [jnp.where(jnp.logical_and(sub == idxs[kk], valid), 1.0, 0.0) for kk in range(TOP_K)]
    oh_all = onehots[0] + onehots[1] + onehots[2] + onehots[3]
    ri = lax.broadcasted_iota(jnp.int32, (tq, tq), 0)
    ci = lax.broadcasted_iota(jnp.int32, (tq, tq), 1)
    earlier = jnp.where(ri < ci, 1.0, 0.0).astype(BF16)
    cnt = cnt_ref[:, 0:1]
    before = _dot(oh_all.astype(BF16), earlier) + cnt
    ranks = []
    for kk in range(TOP_K):
        ranks.append(jnp.sum(jnp.where(sub == idxs[kk], before, 0.0), axis=0, keepdims=True))
        before = before + onehots[kk]
    rk_ref[0] = rows(ranks).astype(jnp.int32)
    cnt_ref[...] = jnp.broadcast_to(cnt + jnp.sum(oh_all, axis=1, keepdims=True), cnt_ref.shape)


def _merge_route(x2d, head, gain, w_g, y_hg, y_ml, wbh, wbm, wo, norm_ffn, rw_split, rb, n_experts, bsz, tq):
    d = x2d.shape[1]
    seq = x2d.shape[0] // bsz
    nj = (seq + CHUNK) // tq
    m = bsz * nj * tq
    assert d == TOK_ROWS * LANES, "token-tile layout assumes one (8,128) tile per token"
    row = lambda n: pl.BlockSpec((tq, n), lambda b, j: (b * nj + j, 0))
    tiles = pl.BlockSpec((tq * TOK_ROWS, LANES), lambda b, j: (b * nj + j, 0))
    per_tok = pl.BlockSpec((1, TOP_K, tq), lambda b, j: (b * nj + j, 0, 0))

    def full(a):
        nd = a.ndim
        return pl.BlockSpec(a.shape, lambda b, j: (0,) * nd)

    return pl.pallas_call(
        functools.partial(_merge_route_kernel, n_experts=n_experts),
        grid=(bsz, nj),
        in_specs=[_x_tile_spec(tq, d, seq), full(head), full(gain), full(w_g), row(HG_W), row(ML_W),
                  full(wbh), full(wbm), full(wo), full(norm_ffn), full(rw_split), full(rb)],
        out_specs=[row(d), tiles, per_tok, per_tok, per_tok,
                   pl.BlockSpec((n_experts, LANES), lambda b, j: (0, 0))],
        out_shape=[
            jax.ShapeDtypeStruct((m, d), F32),
            jax.ShapeDtypeStruct((m * TOK_ROWS, LANES), F32),
            jax.ShapeDtypeStruct((bsz * nj, TOP_K, tq), jnp.int32),
            jax.ShapeDtypeStruct((bsz * nj, TOP_K, tq), F32),
            jax.ShapeDtypeStruct((bsz * nj, TOP_K, tq), jnp.int32),
            jax.ShapeDtypeStruct((n_experts, LANES), F32),
        ],
        compiler_params=_cparams(("arbitrary", "arbitrary")),
        name="merge_route",
    )(x2d, head, gain, w_g, y_hg, y_ml, wbh, wbm, wo, norm_ffn, rw_split, rb)


def _toks(first, n=1):
    return pl.ds(pl.multiple_of(first * TOK_ROWS, TOK_ROWS), n * TOK_ROWS)


def _dispatch_kernel(ps_ref, pe_ref, pos_ref, x_ref, xs_hbm, zbuf, sem, *, n_experts, tm):
    b = pl.program_id(0)
    j = pl.program_id(1)
    tq = x_ref.shape[0] // TOK_ROWS

    @pl.when(jnp.logical_and(b == 0, j == 0))
    def _():
        zbuf[...] = jnp.zeros_like(zbuf)
        for e in range(n_experts):
            @pl.when(pe_ref[e] > ps_ref[e])
            def _():
                pltpu.make_async_copy(zbuf, xs_hbm.at[_toks(pe_ref[e] - tm, tm), :], sem).start()
        for e in range(n_experts):
            @pl.when(pe_ref[e] > ps_ref[e])
            def _():
                pltpu.make_async_copy(zbuf, xs_hbm.at[_toks(0, tm), :], sem).wait()

        def zero_tail(blk, carry):
            cp = pltpu.make_async_copy(zbuf, xs_hbm.at[_toks(blk * tm, tm), :], sem)
            cp.start()
            cp.wait()
            return carry
        lax.fori_loop(pe_ref[n_experts - 1] // tm, xs_hbm.shape[0] // (tm * TOK_ROWS), zero_tail, 0)

    def scatter_rows(lo):
        def body(r, carry):
            for kk in range(TOP_K):
                dst = pos_ref[0, kk, r]
                pltpu.make_async_copy(
                    x_ref.at[_toks(r), :], xs_hbm.at[_toks(dst), :], sem).start(priority=kk % 2)
            return carry
        lax.fori_loop(lo, tq, body, 0, unroll=4)
        n = tq - lo
        for _ in range(TOP_K):
            pltpu.make_async_copy(x_ref.at[_toks(0, n), :], xs_hbm.at[_toks(0, n), :], sem).wait()

    @pl.when(j == 0)
    def _():
        scatter_rows(N_PAD)

    @pl.when(j != 0)
    def _():
        scatter_rows(0)


def _dispatch(pstart, pend, pos, xn_t, n_rows, bsz, tq, tm):
    m = xn_t.shape[0] // TOK_ROWS
    nj = m // (bsz * tq)
    n_experts = pstart.shape[0]
    smem_blk = pl.BlockSpec((1, TOP_K, tq), lambda b, j, ps, pe: (b * nj + j, 0, 0),
                            memory_space=pltpu.SMEM)
    grid_spec = pltpu.PrefetchScalarGridSpec(
        num_scalar_prefetch=2,
        grid=(bsz, nj),
        in_specs=[smem_blk,
                  pl.BlockSpec((tq * TOK_ROWS, LANES), lambda b, j, ps, pe: (b * nj + j, 0))],
        out_specs=pl.BlockSpec(memory_space=pl.ANY),
        scratch_shapes=[pltpu.VMEM((tm * TOK_ROWS, LANES), F32), pltpu.SemaphoreType.DMA(())],
    )
    return pl.pallas_call(
        functools.partial(_dispatch_kernel, n_experts=n_experts, tm=tm),
        grid_spec=grid_spec,
        out_shape=jax.ShapeDtypeStruct((n_rows * TOK_ROWS, LANES), F32),
        compiler_params=_cparams(("arbitrary", "arbitrary")),
        name="dispatch",
    )(pstart, pend, pos, xn_t)


CAST_ROWS = 256


def _experts_kernel(be_ref, nu_ref, slot_ref, nxt_ref, x_ref, wgu_hbm, bg_ref, bu_ref, wd_hbm, bd_ref, y_ref,
                    wg_s, wu_s, wd_s, wgu_buf, wd_buf, sem):
    i = pl.program_id(0)
    n_used = nu_ref[0]
    dff = wg_s.shape[0]
    new_expert = jnp.logical_or(i == 0, be_ref[i] != be_ref[jnp.maximum(i - 1, 0)])

    def weight_copies(e, slot):
        return (pltpu.make_async_copy(wgu_hbm.at[e], wgu_buf.at[slot], sem.at[slot]),
                pltpu.make_async_copy(wd_hbm.at[e], wd_buf.at[slot], sem.at[slot]))

    @pl.when(i == 0)
    def _():
        for cp in weight_copies(be_ref[0], 0):
            cp.start()

    for slot in range(2):
        @pl.when(jnp.logical_and(jnp.logical_and(new_expert, i < n_used), slot_ref[i] == slot))
        def _():
            for cp in weight_copies(be_ref[i], slot):
                cp.wait()

            @pl.when(nxt_ref[i] >= 0)
            def _():
                for cp in weight_copies(nxt_ref[i], 1 - slot):
                    cp.start()

            for c in range(dff // CAST_ROWS):
                rows = pl.ds(c * CAST_ROWS, CAST_ROWS)
                wd_s[rows, :] = wd_buf[slot, rows, :].astype(BF16)
                for sl in range(wgu_buf.shape[1]):
                    lanes = pl.ds(sl * LANES, LANES)
                    wg_s[rows, lanes] = wgu_buf[
                        slot, sl, pl.ds(2 * c * CAST_ROWS, CAST_ROWS, stride=2), :].astype(BF16)
                    wu_s[rows, lanes] = wgu_buf[
                        slot, sl, pl.ds(2 * c * CAST_ROWS + 1, CAST_ROWS, stride=2), :].astype(BF16)

    @pl.when(i < n_used)
    def _():
        tm = x_ref.shape[0] // TOK_ROWS
        xb = _load_token_tiles(x_ref, tm).astype(BF16)
        g = _dot_nt(xb, wg_s[...]) + bg_ref[0]
        u = _dot_nt(xb, wu_s[...]) + bu_ref[0]
        gate = jnp.minimum(g, SWIGLU_LIMIT)
        up = jnp.clip(u, -SWIGLU_LIMIT, SWIGLU_LIMIT)
        act = (up + 1.0) * gate * _sigmoid(SWIGLU_ALPHA * gate)
        _store_token_tiles(y_ref, _dot(act.astype(BF16), wd_s[...]) + bd_ref[0])

    @pl.when(i >= n_used)
    def _():
        y_ref[...] = jnp.zeros_like(y_ref)


def _experts(block_e, n_used, xs, w_gu_t, b_g, b_u, w_d, b_d, tm):
    n_blocks = block_e.shape[0]
    dff, d = w_d.shape[1:]
    tile_blk = lambda f: pl.BlockSpec((tm * TOK_ROWS, LANES), f)
    assert dff % CAST_ROWS == 0
    blk = jnp.arange(n_blocks, dtype=jnp.int32)
    changed = jnp.concatenate([jnp.zeros((1,), jnp.int32), (block_e[1:] != block_e[:-1]).astype(jnp.int32)])
    w_slot = jnp.cumsum(changed) % 2
    later = jnp.logical_and(jnp.logical_and(blk[None, :] > blk[:, None], blk[None, :] < n_used[0]),
                            block_e[None, :] != block_e[:, None])
    nxt_e = jnp.where(jnp.any(later, axis=1), block_e[jnp.argmax(later, axis=1)], -1).astype(jnp.int32)
    wspec = lambda k, n: pl.BlockSpec((1, k, n), lambda i, be, nu, ws, ne: (be[i], 0, 0))
    grid_spec = pltpu.PrefetchScalarGridSpec(
        num_scalar_prefetch=4,
        grid=(n_blocks,),
        in_specs=[
            tile_blk(lambda i, be, nu, ws, ne: (jnp.minimum(i, nu[0] - 1), 0)),
            pl.BlockSpec(memory_space=pl.ANY),
            wspec(1, dff), wspec(1, dff),
            pl.BlockSpec(memory_space=pl.ANY),
            wspec(1, d),
        ],
        out_specs=tile_blk(lambda i, be, nu, ws, ne: (i, 0)),
        scratch_shapes=[pltpu.VMEM((dff, d), BF16), pltpu.VMEM((dff, d), BF16), pltpu.VMEM((dff, d), BF16),
                        pltpu.VMEM((2,) + w_gu_t.shape[1:], F32), pltpu.VMEM((2, dff, d), F32),
                        pltpu.SemaphoreType.DMA((2,))],
    )
    return pl.pallas_call(
        _experts_kernel,
        grid_spec=grid_spec,
        out_shape=jax.ShapeDtypeStruct((n_blocks * tm * TOK_ROWS, LANES), F32),
        compiler_params=_cparams(("arbitrary",)),
        name="experts",
    )(block_e, n_used, w_slot, nxt_e, xs, w_gu_t, b_g, b_u, w_d, b_d)


def _combine_kernel(pos_ref, posn_ref, hsrc_ref, hsrcn_ref, tw_ref, y_hbm, h_hbm,
                    nw_ref, o_ref, ybuf, hbuf, sem):
    i = pl.program_id(0)
    n = pl.num_programs(0)
    tt = o_ref.shape[0]
    nt = tt * TOK_ROWS

    def copies(src_pos_ref, src_h_ref, dst_slot, r):
        return [pltpu.make_async_copy(
            y_hbm.at[_toks(src_pos_ref[0, kk, r]), :],
            ybuf.at[dst_slot, _toks(kk * tt + r), :],
            sem.at[dst_slot]) for kk in range(TOP_K)]

    def start_h(src_h_ref, dst_slot):
        h_row = pl.multiple_of(src_h_ref[0, 0, 0], 8)
        pltpu.make_async_copy(h_hbm.at[pl.ds(h_row, tt), :], hbuf.at[dst_slot], sem.at[dst_slot]).start()

    @pl.when(i == 0)
    def _():
        def body(r, carry):
            for kk, cp in enumerate(copies(pos_ref, hsrc_ref, 0, r)):
                cp.start(priority=kk % 2)
            return carry
        lax.fori_loop(0, tt, body, 0, unroll=2)
        start_h(hsrc_ref, 0)

    def wait_slot(s):
        pltpu.make_async_copy(ybuf.at[s], ybuf.at[s], sem.at[s]).wait()
        pltpu.make_async_copy(hbuf.at[s], hbuf.at[s], sem.at[s]).wait()

    def step(slot):
        for r in range(tt):
            for kk, cp in enumerate(copies(posn_ref, hsrcn_ref, 1 - slot, r)):
                cp.start(priority=kk % 2)
        start_h(hsrcn_ref, 1 - slot)

        wait_slot(slot)
        acc = hbuf[slot]
        tw = tw_ref[...]
        for kk in range(TOP_K):
            yk = jnp.concatenate(
                [ybuf[slot, pl.ds(kk * nt + s, tt, stride=TOK_ROWS), :] for s in range(TOK_ROWS)], axis=1)
            acc = acc + tw[:, kk:kk + 1] * yk
        ms = jnp.mean(acc * acc, axis=-1, keepdims=True)
        o_ref[...] = (acc * lax.rsqrt(ms + EPS)) * nw_ref[...]

        @pl.when(i == n - 1)
        def _():
            wait_slot(1 - slot)

    for parity in range(2):
        pl.when(i % 2 == parity)(functools.partial(step, parity))


def _combine(pos_seq, tw_seq, h_src, y_rows, h2, norm_w, tt):
    n_tiles = h_src.shape[0]
    d = h2.shape[1]
    last = n_tiles - 1
    nt = tt * TOK_ROWS
    cur = lambda i: (i, 0, 0)
    nxt = lambda i: (jnp.minimum(i + 1, last), 0, 0)
    idx_blk = lambda f: pl.BlockSpec((1, TOP_K, tt), f, memory_space=pltpu.SMEM)
    one_blk = lambda f: pl.BlockSpec((1, 1, 1), f, memory_space=pltpu.SMEM)
    return pl.pallas_call(
        _combine_kernel,
        grid=(n_tiles,),
        in_specs=[
            idx_blk(cur), idx_blk(nxt), one_blk(cur), one_blk(nxt),
            pl.BlockSpec((tt, TOP_K), lambda i: (i, 0)),
            pl.BlockSpec(memory_space=pl.ANY),
            pl.BlockSpec(memory_space=pl.ANY),
            pl.BlockSpec(norm_w.shape, lambda i: (0, 0)),
        ],
        out_specs=pl.BlockSpec((tt, d), lambda i: (i, 0)),
        scratch_shapes=[pltpu.VMEM((2, TOP_K * nt, LANES), F32), pltpu.VMEM((2, tt, d), F32),
                        pltpu.SemaphoreType.DMA((2,))],
        out_shape=jax.ShapeDtypeStruct((n_tiles * tt, d), F32),
        compiler_params=_cparams(("arbitrary",)),
        name="combine",
    )(pos_seq, pos_seq, h_src, h_src, tw_seq, y_rows, h2, norm_w)


def _pick(n, prefs):
    for p in prefs:
        if n % p == 0:
            return p
    raise ValueError(f"no tile in {prefs} divides {n}")


def kernel(x, meta_tokens, hg_lb_logits, norm_mix, w_in, hg_norm, ml_conv_w, ml_conv_b, ml_wq, ml_wk, ml_wv,
           ml_gate_b, ml_norm, ml_skip, w_branch_hg, w_branch_ml, w_out, norm_ffn, router_w, router_b,
           exp_w_gu, exp_b_gu, exp_w_down, exp_b_down, norm_final):
    bsz, seq, d = x.shape
    assert norm_mix.shape[0] == 1, "single-layer block"
    assert seq % CHUNK == 0 and d % LANES == 0
    t = CHUNK + seq
    m_rows = bsz * t
    n_experts = router_w.shape[-1]
    dff = exp_w_down.shape[2]
    assert n_experts <= LANES

    head = jnp.concatenate([jnp.zeros((N_PAD, d), x.dtype), meta_tokens.astype(x.dtype)], axis=0)
    x2d = x.reshape(bsz * seq, d)
    lower_bounds = jnp.cumsum(jax.nn.softmax(hg_lb_logits.astype(F32), axis=0), axis=0)

    w = w_in[0]
    n_hg = 4 * HG_W
    n_a = n_hg + 2 * ML_W
    w_hg = w[:, :n_hg].astype(BF16)
    w_m = jnp.pad(w[:, n_a:n_a + 2 * N_HEADS], ((0, 0), (0, LANES - 2 * N_HEADS)))
    w_ml = jnp.concatenate([w[:, n_hg:n_a], w_m], axis=1).astype(BF16)
    w_g = w[:, n_a + 2 * N_HEADS:].astype(BF16)
    gain = norm_mix[0][None]

    bb = _pick(bsz, (8, 4, 2, 1))
    tri = jnp.asarray(np.tril(np.ones((CHUNK, CHUNK), np.float32)), BF16)
    y_hg = _hgrn2(x, head, gain, w_hg, tri, lower_bounds[0][None], hg_norm[0][None], bb)

    gb = ml_gate_b[0].astype(F32)
    gb_col = jnp.pad(gb, (0, LANES - 2 * N_HEADS))[None]
    gb_row = jnp.broadcast_to(gb[:, None], (2 * N_HEADS, CHUNK))
    y_ml = _mlstm(x, head, gain, w_ml, tri, ml_conv_w[0], ml_conv_b[0][None],
                  ml_wq[0].astype(BF16), ml_wk[0].astype(BF16), ml_wv[0].astype(BF16),
                  gb_col, gb_row, ml_norm[0][None], ml_skip[0][None], bb)

    assert n_experts % 8 == 0
    tq = _pick(t, (704, 192, 64))
    rw = router_w[0].astype(F32).T
    rw1 = rw.astype(BF16)
    rw2 = (rw - rw1.astype(F32)).astype(BF16)
    rw_split = jnp.stack([rw1, rw2])
    rb = jnp.broadcast_to(router_b[0].astype(F32)[:, None], (n_experts, tq))
    h2, xn2, top_e, top_w, rank, cnt = _merge_route(
        x2d, head, gain, w_g, y_hg.reshape(m_rows, HG_W), y_ml.reshape(m_rows, ML_W),
        w_branch_hg[0].astype(BF16), w_branch_ml[0].astype(BF16), w_out[0].astype(BF16),
        norm_ffn[0][None], rw_split, rb, n_experts, bsz, tq)

    tm6 = 512
    n_assign = bsz * (t - N_PAD) * TOP_K
    counts = cnt[:, 0].astype(jnp.int32)
    padded = ((counts + tm6 - 1) // tm6) * tm6
    pend = jnp.cumsum(padded).astype(jnp.int32)
    pstart = pend - padded
    n_blocks = -(-n_assign // tm6) + n_experts
    n_rows = n_blocks * tm6
    blk_start = jnp.arange(n_blocks, dtype=jnp.int32) * tm6
    block_e = jnp.minimum(jnp.sum((blk_start[:, None] >= pend[None, :]).astype(jnp.int32), axis=1),
                          n_experts - 1)
    n_used = (pend[-1] // tm6)[None]
    pos = rank + jnp.sum(jnp.where(top_e[..., None] == jnp.arange(n_experts, dtype=jnp.int32), pstart, 0),
                         axis=-1)
    xs = _dispatch(pstart, pend, pos, xn2, n_rows, bsz, tq, tm6)

    w_gu_t = jnp.swapaxes(exp_w_gu[0].reshape(n_experts, d // LANES, LANES, 2 * dff), 2, 3)
    bgu = exp_b_gu[0]
    y_rows = _experts(block_e, n_used, xs, w_gu_t, bgu[:, None, 0::2], bgu[:, None, 1::2],
                      exp_w_down[0], exp_b_down[0][:, None, :], tm6)

    tt = _pick(seq, (256, 128, 64))
    tiles_per_b = seq // tt
    n_tiles = bsz * tiles_per_b
    nj = t // tq

    def seq_tiles(a):
        a = a.reshape(bsz, nj, TOP_K, tq).transpose(0, 2, 1, 3).reshape(bsz, TOP_K, t)[:, :, CHUNK:]
        return a.reshape(bsz, TOP_K, tiles_per_b, tt).transpose(0, 2, 1, 3).reshape(n_tiles, TOP_K, tt)

    h_src = (jnp.arange(bsz, dtype=jnp.int32)[:, None] * t + CHUNK
             + jnp.arange(tiles_per_b, dtype=jnp.int32)[None, :] * tt).reshape(n_tiles, 1, 1)
    tw_cols = seq_tiles(top_w).transpose(0, 2, 1).reshape(n_tiles * tt, TOP_K)
    out = _combine(seq_tiles(pos), tw_cols, h_src, y_rows, h2, norm_final[None], tt)
    return out.reshape(bsz, seq, d)
```

```python
import functools

import numpy as np
import jax
import jax.numpy as jnp
from jax import lax
from jax.experimental import pallas as pl
from jax.experimental.pallas import tpu as pltpu

F32 = jnp.float32
BF16 = jnp.bfloat16

N_META = 16
CHUNK = 64
N_PAD = CHUNK - N_META
EPS = 1e-6

N_HEADS = 4
HG_DK = 128
HG_DV = 128
HG_W = N_HEADS * HG_DV
ML_DK = 64
ML_DV = 128
ML_W = N_HEADS * ML_DV
ML_CONV = 4
TOP_K = 4
SWIGLU_LIMIT = 7.0
SWIGLU_ALPHA = 1.702

LANES = 128
VMEM_LIMIT_BYTES = 56 * 1024 * 1024

HG_LEVELS = (32, 16, 8, 4, 2, 1)
GROUP_ROWS = 8


def _cparams(sem):
    return pltpu.CompilerParams(dimension_semantics=sem, vmem_limit_bytes=VMEM_LIMIT_BYTES)


def _sigmoid(x):
    return 0.5 + 0.5 * jnp.tanh(0.5 * x)


def _split3(x):
    x1 = x.astype(BF16)
    r1 = x - x1.astype(F32)
    x2 = r1.astype(BF16)
    x3 = (r1 - x2.astype(F32)).astype(BF16)
    return x1, x2, x3


def _dot(a, b):
    return jnp.dot(a, b, preferred_element_type=F32)


TOK_ROWS = 8


def _store_token_tiles(ref, x):
    n = x.shape[0]
    for s in range(TOK_ROWS):
        ref[pl.ds(s, n, stride=TOK_ROWS), :] = x[:, s * LANES:(s + 1) * LANES]


def _load_token_tiles(ref, n):
    return jnp.concatenate([ref[pl.ds(s, n, stride=TOK_ROWS), :] for s in range(TOK_ROWS)], axis=1)


def _dot_nt(a, b):
    return lax.dot_general(a, b, (((1,), (1,)), ((), ())), preferred_element_type=F32)


def _dot_exact_lhs(m_bf16, x):
    x1, x2, x3 = _split3(x)
    return _dot(m_bf16, x1) + _dot(m_bf16, x2) + _dot(m_bf16, x3)


def _h_tile(x_ref, head_ref, j):
    x = x_ref[...]
    first = jnp.concatenate([head_ref[...], x[:x.shape[0] - CHUNK]], axis=0)
    return jnp.where(j == 0, first, x)


def _x_tile_spec(tr, d, seq):
    assert seq % 8 == 0 and tr % 8 == 0 and CHUNK % 8 == 0
    return pl.BlockSpec(
        (pl.Element(tr), pl.Element(d)),
        lambda b, j: (pl.multiple_of(b * seq + jnp.maximum(j * tr - CHUNK, 0), 8), 0))


def _rms_bf16(x, gain):
    ms = jnp.mean(x * x, axis=-1, keepdims=True)
    return ((x * lax.rsqrt(ms + EPS)) * gain).astype(BF16)


def _project_ahead(x_ref, gain_ref, w_ref, nxt_ref):
    x = x_ref[...].reshape(x_ref.shape[0] * CHUNK, x_ref.shape[2])
    nxt_ref[...] = _dot(_rms_bf16(x, gain_ref[...]), w_ref[...])


def _with_projection_pipeline(step, x_ref, head_ref, gain_ref, w_ref, proj_a, proj_b, c):
    @pl.when(c == 0)
    def _():
        p0 = _dot(_rms_bf16(head_ref[...], gain_ref[...]), w_ref[...])
        for b in range(x_ref.shape[0]):
            proj_a[b * CHUNK:(b + 1) * CHUNK, :] = p0

    for parity, (cur, nxt) in enumerate(((proj_a, proj_b), (proj_b, proj_a))):
        @pl.when(c % 2 == parity)
        def _():
            step(cur, functools.partial(_project_ahead, x_ref, gain_ref, w_ref, nxt))


def _chunk_ahead_spec(bb, d, nc):
    return pl.BlockSpec((bb, CHUNK, d), lambda i, c: (i, jnp.minimum(c, nc - 2), 0))


def _hgrn2_kernel(x_ref, head_ref, gain_ref, w_ref, tri_ref, lb_ref, nw_ref, y_ref, proj_a, proj_b, *st_refs):
    c = pl.program_id(1)

    @pl.when(c == 0)
    def _():
        for st_ref in st_refs:
            st_ref[...] = jnp.zeros_like(st_ref)

    def step(cur, project):
        bb = x_ref.shape[0]
        for b0 in range(0, bb, GROUP_ROWS):
            _hgrn2_group(c, range(b0, min(b0 + GROUP_ROWS, bb)), tri_ref, lb_ref, nw_ref, y_ref, st_refs, cur,
                         project if b0 == 0 else None)

    _with_projection_pipeline(step, x_ref, head_ref, gain_ref, w_ref, proj_a, proj_b, c)


def _hgrn2_group(c, bs, tri_ref, lb_ref, nw_ref, y_ref, st_refs, cur, project):
    row = lax.broadcasted_iota(jnp.int32, (CHUNK, 1), 0)
    valid = (c * CHUNK + row) >= N_PAD
    ti = lax.broadcasted_iota(jnp.int32, (CHUNK, CHUNK), 0)
    si = lax.broadcasted_iota(jnp.int32, (CHUNK, CHUNK), 1)
    diag_mask = ti == si
    level_masks = {}
    for m in HG_LEVELS:
        same_pair = (ti & ~(2 * m - 1)) == (si & ~(2 * m - 1))
        level_masks[m] = same_pair & ((ti & m) != 0) & ((si & m) == 0)

    lb = lb_ref[...]
    tri = tri_ref[...]
    per_b = {}
    for b in bs:
        rows_b = slice(b * CHUNK, (b + 1) * CHUNK)
        hf = cur[rows_b, HG_W:2 * HG_W]
        f = lb + (1.0 - lb) * _sigmoid(hf)
        f = jnp.where(valid, f, 1.0)
        logf = jnp.log(f)
        k_all = 1.0 - f
        hq = cur[rows_b, 0:HG_W]
        q_all = hq * _sigmoid(hq)
        b_cum = _dot_exact_lhs(tri, logf)
        e_b = jnp.exp(b_cum)
        e_bl = jnp.exp(b_cum[CHUNK - 1:CHUNK] - b_cum)

        q_fac, k_fac = {}, {}
        for m in HG_LEVELS:
            if m >= 4:
                grp = b_cum.reshape(CHUNK // (2 * m), 2 * m, HG_W)
                e = (grp - grp[:, m - 1:m, :]).reshape(CHUNK, HG_W)
                q_fac[m] = jnp.exp(jnp.minimum(e, 0.0))
                k_fac[m] = jnp.exp(jnp.minimum(-e, 0.0))
        f_prev = pltpu.roll(f, 1, 0)
        f_next = pltpu.roll(f, CHUNK - 1, 0)
        r4 = row & 3
        q_fac[2] = jnp.where(r4 == 2, f, jnp.where(r4 == 3, f * f_prev, 1.0))
        k_fac[2] = jnp.where(r4 == 0, f_next, 1.0)
        q_fac[1] = jnp.where((row & 1) == 1, f, 1.0)

        hg = cur[rows_b, 3 * HG_W:4 * HG_W]
        v_all = cur[rows_b, 2 * HG_W:3 * HG_W]
        q_fac = {m: v.astype(BF16) for m, v in q_fac.items()}
        k_fac = {m: v.astype(BF16) for m, v in k_fac.items()}
        per_b[b] = (q_all, k_all, q_fac, k_fac, e_b, e_bl, v_all, hg * _sigmoid(hg))

    if project is not None:
        project()

    pairs = [(b, h) for b in bs for h in range(N_HEADS)]
    sls = [slice(h * HG_DK, (h + 1) * HG_DK) for h in range(N_HEADS)]

    scores_all = []
    for b, h in pairs:
        q_all, k_all, q_fac, k_fac = per_b[b][:4]
        qb = q_all[:, sls[h]].astype(BF16)
        kb = k_all[:, sls[h]].astype(BF16)
        scores = jnp.where(diag_mask, _dot_nt(qb, kb), 0.0)
        for m in HG_LEVELS:
            qd = qb * q_fac[m][:, sls[h]]
            kd = kb * k_fac[m][:, sls[h]] if m in k_fac else kb
            scores = jnp.where(level_masks[m], _dot_nt(qd, kd), scores)
        scores_all.append(scores.astype(BF16))

    sts = [st_refs[b * N_HEADS + h][...] for b, h in pairs]
    outs = []
    for p, (b, h) in enumerate(pairs):
        q_all, e_b, v_all = per_b[b][0], per_b[b][4], per_b[b][6]
        qe = (q_all[:, sls[h]] * e_b[:, sls[h]]).astype(BF16)
        outs.append(_dot(scores_all[p], v_all[:, sls[h]].astype(BF16)) + _dot_nt(qe, sts[p].astype(BF16)))

    for p, (b, h) in enumerate(pairs):
        k_all, e_b, e_bl, v_all = per_b[b][1], per_b[b][4], per_b[b][5], per_b[b][6]
        kl = (k_all[:, sls[h]] * e_bl[:, sls[h]]).astype(BF16)
        vt = v_all[:, sls[h]].T.astype(BF16)
        st_refs[b * N_HEADS + h][...] = e_b[CHUNK - 1:CHUNK, sls[h]] * sts[p] + _dot(vt, kl)

    for p, (b, h) in enumerate(pairs):
        o = outs[p]
        ms = jnp.mean(o * o, axis=-1, keepdims=True)
        y = (o * lax.rsqrt(ms + EPS)) * nw_ref[:, sls[h]] * per_b[b][7][:, sls[h]]
        y_ref[b, :, sls[h]] = y.astype(y_ref.dtype)


def _hgrn2(x3, head, gain, w_hg, tri, lb, norm_w, bb):
    bsz, seq, d = x3.shape
    nc = seq // CHUNK + 1
    blk = pl.BlockSpec((bb, CHUNK, HG_W), lambda i, c: (i, c, 0))
    full = lambda a: pl.BlockSpec(a.shape, lambda i, c: (0, 0))
    return pl.pallas_call(
        _hgrn2_kernel,
        grid=(bsz // bb, nc),
        in_specs=[_chunk_ahead_spec(bb, d, nc), full(head), full(gain), full(w_hg), full(tri), full(lb),
                  full(norm_w)],
        out_specs=blk,
        out_shape=jax.ShapeDtypeStruct((bsz, nc * CHUNK, HG_W), BF16),
        scratch_shapes=([pltpu.VMEM((bb * CHUNK, w_hg.shape[1]), F32)] * 2
                        + [pltpu.VMEM((HG_DV, HG_DK), F32)] * (bb * N_HEADS)),
        compiler_params=_cparams(("parallel", "arbitrary")),
        name="hgrn2",
    )(x3, head, gain, w_hg, tri, lb, norm_w)


def _log_sigmoid(x):
    return jnp.minimum(x, 0.0) - jnp.log(1.0 + jnp.exp(-jnp.abs(x)))


def _mlstm_kernel(x_ref, head_ref, gain_ref, w_ref, tri_ref, cw_ref, cb_ref, wq_ref, wk_ref, wv_ref,
                  gbc_ref, gbr_ref, nw_ref, sk_ref, y_ref, proj_a, proj_b, *scratch):
    c = pl.program_id(1)

    @pl.when(c == 0)
    def _():
        for ref in scratch:
            ref[...] = jnp.zeros_like(ref)

    step = functools.partial(_mlstm_step, c, x_ref.shape[0], tri_ref, cw_ref, cb_ref, wq_ref, wk_ref, wv_ref,
                             gbc_ref, gbr_ref, nw_ref, sk_ref, y_ref, scratch)
    _with_projection_pipeline(step, x_ref, head_ref, gain_ref, w_ref, proj_a, proj_b, c)


def _mlstm_step(c, bb, tri_ref, cw_ref, cb_ref, wq_ref, wk_ref, wv_ref, gbc_ref, gbr_ref, nw_ref, sk_ref, y_ref,
                scratch, cur, project):
    n_pairs = bb * N_HEADS
    s_refs = scratch[:n_pairs]
    m_refs = scratch[n_pairs:2 * n_pairs]
    tail_refs = scratch[2 * n_pairs:]

    pos_c = c * CHUNK + lax.broadcasted_iota(jnp.int32, (CHUNK, 1), 0)
    valid_c = pos_c >= N_PAD
    pos_r = c * CHUNK + lax.broadcasted_iota(jnp.int32, (1, CHUNK), 1)
    valid_r = pos_r >= N_PAD
    ti = lax.broadcasted_iota(jnp.int32, (CHUNK, CHUNK), 0)
    si = lax.broadcasted_iota(jnp.int32, (CHUNK, CHUNK), 1)
    causal = si <= ti
    tri = tri_ref[...]
    ones_v = jnp.ones((CHUNK, ML_DV), BF16)
    neg_inf = -jnp.inf

    per_b = []
    for b in range(bb):
        rows_b = slice(b * CHUNK, (b + 1) * CHUNK)
        mm = jnp.where(valid_c, cur[rows_b, 0:ML_W], 0.0)
        ext = jnp.concatenate([tail_refs[b][...], mm], axis=0)
        tail_refs[b][...] = mm[CHUNK - 8:CHUNK]
        conv = cb_ref[...]
        for j in range(ML_CONV):
            off = 8 - (ML_CONV - 1) + j
            conv = conv + cw_ref[j:j + 1, :] * ext[off:off + CHUNK]
        cact = conv * _sigmoid(conv)
        cact_b = cact.astype(BF16)
        mm_b = mm.astype(BF16)

        graw = cur[rows_b, 2 * ML_W:2 * ML_W + LANES]
        gcol = graw + gbc_ref[...]
        li_col = jnp.where(valid_c, gcol, neg_inf)
        lf_col = jnp.where(valid_c, _log_sigmoid(gcol), 0.0)
        b_col = _dot_exact_lhs(tri, lf_col)
        grow = graw.T[0:2 * N_HEADS] + gbr_ref[...]
        li_row = jnp.where(valid_r, grow, neg_inf)
        lf_row = jnp.where(valid_r, _log_sigmoid(grow), 0.0)
        r1, r2, r3 = _split3(lf_row)
        b_row = _dot_nt(r1, tri) + _dot_nt(r2, tri) + _dot_nt(r3, tri)

        ogate = _sigmoid(cur[rows_b, ML_W:2 * ML_W])
        per_b.append((cact, cact_b, mm_b, li_col, b_col, li_row, b_row, ogate))

    project()

    pairs = [(b, h) for b in range(bb) for h in range(N_HEADS)]
    sls = [slice(h * ML_DV, (h + 1) * ML_DV) for h in range(N_HEADS)]

    qs, ks, vs = [], [], []
    for b, h in pairs:
        cact_b, mm_b = per_b[b][1], per_b[b][2]
        qs.append((_dot(cact_b[:, sls[h]], wq_ref[h]) * (ML_DK ** -0.5)).astype(BF16))
        ks.append(_dot(cact_b[:, sls[h]], wk_ref[h]))
        v = _dot(mm_b[:, sls[h]], wv_ref[h]).astype(BF16)
        vs.append(jnp.concatenate([v, ones_v], axis=1))

    n_p = len(pairs)
    blk = lambda p: slice(p * CHUNK, (p + 1) * CHUNK)
    stack = lambda xs: jnp.concatenate(xs, axis=0)
    bc_all = stack([per_b[b][4][:, N_HEADS + h:N_HEADS + h + 1] for b, h in pairs])
    lic_all = stack([per_b[b][3][:, h:h + 1] for b, h in pairs])
    row_all = stack([jnp.broadcast_to(per_b[b][5][h:h + 1, :] - per_b[b][6][N_HEADS + h:N_HEADS + h + 1, :],
                                      (CHUNK, CHUNK)) for b, h in pairs])
    mprev_all = stack([jnp.broadcast_to(m_refs[p][0:1, 0:1], (CHUNK, 1)) for p in range(n_p)])
    glast_all = stack([jnp.broadcast_to(per_b[b][4][CHUNK - 1:CHUNK, N_HEADS + h:N_HEADS + h + 1], (CHUNK, 1))
                       for b, h in pairs])
    causal_all = stack([causal] * n_p)

    d_all = jnp.where(causal_all, bc_all + row_all, neg_inf)
    a_all = bc_all + mprev_all
    m_t_all = jnp.maximum(a_all, jnp.max(d_all, axis=-1, keepdims=True))
    w_intra_all = jnp.exp(d_all - m_t_all)
    w_inter_all = jnp.exp(a_all - m_t_all)

    qk_all = (stack([_dot_nt(qs[p], ks[p].astype(BF16)) for p in range(n_p)]) * w_intra_all).astype(BF16)
    s_augs = [s_refs[p][...] for p in range(n_p)]
    intra_all = stack([_dot(qk_all[blk(p)], vs[p]) for p in range(n_p)])
    inter_all = stack([_dot(qs[p], s_augs[p].astype(BF16)) for p in range(n_p)])
    numden_all = intra_all + w_inter_all * inter_all
    o_all = numden_all[:, :ML_DV] / jnp.maximum(jnp.abs(numden_all[:, ML_DV:]), jnp.exp(-m_t_all))
    ms_all = jnp.mean(o_all * o_all, axis=-1, keepdims=True)
    on_all = o_all * lax.rsqrt(ms_all + EPS)

    e_all = glast_all - bc_all + lic_all
    gm_all = glast_all + mprev_all
    e_max = jnp.max(e_all.reshape(n_p, CHUNK, 1), axis=1, keepdims=True)
    m_new_all = jnp.maximum(gm_all.reshape(n_p, CHUNK, 1), e_max).reshape(n_p * CHUNK, 1)
    w_s_all = jnp.exp(e_all - m_new_all)
    w_p_all = jnp.exp(gm_all - m_new_all)
    kw_all = stack(ks) * w_s_all
    for p in range(n_p):
        kw_t = kw_all[blk(p)].T.astype(BF16)
        s_refs[p][...] = w_p_all[p * CHUNK:p * CHUNK + 1] * s_augs[p] + _dot(kw_t, vs[p])
        m_refs[p][...] = jnp.broadcast_to(m_new_all[p * CHUNK:p * CHUNK + 1], m_refs[p].shape)

    for p, (b, h) in enumerate(pairs):
        cact, ogate = per_b[b][0], per_b[b][7]
        sl = sls[h]
        y = (on_all[blk(p)] * nw_ref[:, sl] + sk_ref[:, sl] * cact[:, sl]) * ogate[:, sl]
        y_ref[b, :, sl] = y.astype(y_ref.dtype)


def _mlstm(x3, head, gain, w_ml, tri, conv_w, conv_b, wq, wk, wv, gb_col, gb_row, norm_w, skip, bb):
    bsz, seq, d = x3.shape
    nc = seq // CHUNK + 1
    blk = pl.BlockSpec((bb, CHUNK, ML_W), lambda i, c: (i, c, 0))

    def full(a):
        nd = a.ndim
        return pl.BlockSpec(a.shape, lambda i, c: (0,) * nd)

    params = (head, gain, w_ml, tri, conv_w, conv_b, wq, wk, wv, gb_col, gb_row, norm_w, skip)
    return pl.pallas_call(
        _mlstm_kernel,
        grid=(bsz // bb, nc),
        in_specs=[_chunk_ahead_spec(bb, d, nc)] + [full(p) for p in params],
        out_specs=blk,
        out_shape=jax.ShapeDtypeStruct((bsz, nc * CHUNK, ML_W), BF16),
        scratch_shapes=([pltpu.VMEM((bb * CHUNK, w_ml.shape[1]), F32)] * 2
                        + [pltpu.VMEM((ML_DK, 2 * ML_DV), F32)] * (bb * N_HEADS)
                        + [pltpu.VMEM((8, LANES), F32)] * (bb * N_HEADS)
                        + [pltpu.VMEM((8, ML_W), F32)] * bb),
        compiler_params=_cparams(("parallel", "arbitrary")),
        name="mlstm",
    )(x3, *params)


def _merge_route_kernel(x_ref, head_ref, gain_ref, wg_ref, yh_ref, ym_ref, wbh_ref, wbm_ref, wo_ref, nf_ref,
                        rw_ref, rb_ref, h2_ref, xn_ref, te_ref, tw_ref, rk_ref, cnt_ref, *, n_experts):
    d = x_ref.shape[1]
    tq = x_ref.shape[0]
    j = pl.program_id(1)

    @pl.when(jnp.logical_and(pl.program_id(0) == 0, j == 0))
    def _():
        cnt_ref[...] = jnp.zeros_like(cnt_ref)

    p_hg = _dot(yh_ref[...], wbh_ref[...])
    p_ml = _dot(ym_ref[...], wbm_ref[...])
    h = _h_tile(x_ref, head_ref, j)
    hn = _rms_bf16(h, gain_ref[...])
    g0 = _sigmoid(_dot(hn, wg_ref[:, :d]))
    g1 = _sigmoid(_dot(hn, wg_ref[:, d:]))
    merged = g0 * p_hg + g1 * p_ml
    h2 = h + _dot(merged.astype(BF16), wo_ref[...])
    h2_ref[...] = h2
    ms = jnp.mean(h2 * h2, axis=-1, keepdims=True)
    xn = (h2 * lax.rsqrt(ms + EPS)) * nf_ref[...]
    _store_token_tiles(xn_ref, xn)
    x1, x2, _ = _split3(xn)
    logits = (_dot_nt(rw_ref[0], x1) + _dot_nt(rw_ref[1], x1) + _dot_nt(rw_ref[0], x2)) + rb_ref[...]
    sub = lax.broadcasted_iota(jnp.int32, logits.shape, 0)
    work = logits
    vals, idxs = [], []
    for _ in range(TOP_K):
        vmax = jnp.max(work, axis=0, keepdims=True)
        imax = jnp.min(jnp.where(work == vmax, sub, n_experts), axis=0, keepdims=True)
        vals.append(vmax)
        idxs.append(imax)
        work = jnp.where(sub == imax, -jnp.inf, work)
    exps = [jnp.exp(v - vals[0]) for v in vals]
    tot = exps[0] + exps[1] + exps[2] + exps[3]

    krow = lax.broadcasted_iota(jnp.int32, (TOP_K, tq), 0)

    def rows(per_k):
        out = jnp.broadcast_to(per_k[0], (TOP_K, tq))
        for kk in range(1, TOP_K):
            out = jnp.where(krow == kk, per_k[kk], out)
        return out

    te_ref[0] = rows(idxs)
    tw_ref[0] = rows([e / tot for e in exps])

    valid = (j * tq + lax.broadcasted_iota(jnp.int32, (1, tq), 1)) >= N_PAD
    onehots = [jnp.where(jnp.logical_and(sub == idxs[kk], valid), 1.0, 0.0) for kk in range(TOP_K)]
    oh_all = onehots[0] + onehots[1] + onehots[2] + onehots[3]
    ri = lax.broadcasted_iota(jnp.int32, (tq, tq), 0)
    ci = lax.broadcasted_iota(jnp.int32, (tq, tq), 1)
    earlier = jnp.where(ri < ci, 1.0, 0.0).astype(BF16)
    cnt = cnt_ref[:, 0:1]
    before = _dot(oh_all.astype(BF16), earlier) + cnt
    ranks = []
    for kk in range(TOP_K):
        ranks.append(jnp.sum(jnp.where(sub == idxs[kk], before, 0.0), axis=0, keepdims=True))
        before = before + onehots[kk]
    rk_ref[0] = rows(ranks).astype(jnp.int32)
    cnt_ref[...] = jnp.broadcast_to(cnt + jnp.sum(oh_all, axis=1, keepdims=True), cnt_ref.shape)


def _merge_route(x2d, head, gain, w_g, y_hg, y_ml, wbh, wbm, wo, norm_ffn, rw_split, rb, n_experts, bsz, tq):
    d = x2d.shape[1]
    seq = x2d.shape[0] // bsz
    nj = (seq + CHUNK) // tq
    m = bsz * nj * tq
    assert d == TOK_ROWS * LANES, "token-tile layout assumes one (8,128) tile per token"
    row = lambda n: pl.BlockSpec((tq, n), lambda b, j: (b * nj + j, 0))
    tiles = pl.BlockSpec((tq * TOK_ROWS, LANES), lambda b, j: (b * nj + j, 0))
    per_tok = pl.BlockSpec((1, TOP_K, tq), lambda b, j: (b * nj + j, 0, 0))

    def full(a):
        nd = a.ndim
        return pl.BlockSpec(a.shape, lambda b, j: (0,) * nd)

    return pl.pallas_call(
        functools.partial(_merge_route_kernel, n_experts=n_experts),
        grid=(bsz, nj),
        in_specs=[_x_tile_spec(tq, d, seq), full(head), full(gain), full(w_g), row(HG_W), row(ML_W),
                  full(wbh), full(wbm), full(wo), full(norm_ffn), full(rw_split), full(rb)],
        out_specs=[row(d), tiles, per_tok, per_tok, per_tok,
                   pl.BlockSpec((n_experts, LANES), lambda b, j: (0, 0))],
        out_shape=[
            jax.ShapeDtypeStruct((m, d), F32),
            jax.ShapeDtypeStruct((m * TOK_ROWS, LANES), F32),
            jax.ShapeDtypeStruct((bsz * nj, TOP_K, tq), jnp.int32),
            jax.ShapeDtypeStruct((bsz * nj, TOP_K, tq), F32),
            jax.ShapeDtypeStruct((bsz * nj, TOP_K, tq), jnp.int32),
            jax.ShapeDtypeStruct((n_experts, LANES), F32),
        ],
        compiler_params=_cparams(("arbitrary", "arbitrary")),
        name="merge_route",
    )(x2d, head, gain, w_g, y_hg, y_ml, wbh, wbm, wo, norm_ffn, rw_split, rb)


def _toks(first, n=1):
    return pl.ds(pl.multiple_of(first * TOK_ROWS, TOK_ROWS), n * TOK_ROWS)


def _dispatch_kernel(ps_ref, pe_ref, pos_ref, x_hbm, xs_hbm, zbuf, stage, sem_in, sem_out, *, n_experts, tm, nj):
    g = pl.program_id(0)
    n_tiles = pl.num_programs(0)
    tq = stage.shape[1] // TOK_ROWS

    def tile_in(tile, slot):
        return pltpu.make_async_copy(x_hbm.at[_toks(tile * tq, tq), :], stage.at[slot], sem_in.at[slot])

    def wait_out(tile):
        for lo in (N_PAD, 0):
            @pl.when((tile % nj == 0) == (lo == N_PAD))
            def _():
                for _ in range(TOP_K):
                    pltpu.make_async_copy(stage.at[0, _toks(0, tq - lo), :], xs_hbm.at[_toks(0, tq - lo), :],
                                          sem_out.at[tile % 2]).wait()

    @pl.when(g == 0)
    def _():
        zbuf[...] = jnp.zeros_like(zbuf)
        for e in range(n_experts):
            @pl.when(pe_ref[e] > ps_ref[e])
            def _():
                pltpu.make_async_copy(zbuf, xs_hbm.at[_toks(pe_ref[e] - tm, tm), :], sem_out.at[0]).start()
        for e in range(n_experts):
            @pl.when(pe_ref[e] > ps_ref[e])
            def _():
                pltpu.make_async_copy(zbuf, xs_hbm.at[_toks(0, tm), :], sem_out.at[0]).wait()

        def zero_tail(blk, carry):
            cp = pltpu.make_async_copy(zbuf, xs_hbm.at[_toks(blk * tm, tm), :], sem_out.at[0])
            cp.start()
            cp.wait()
            return carry
        lax.fori_loop(pe_ref[n_experts - 1] // tm, xs_hbm.shape[0] // (tm * TOK_ROWS), zero_tail, 0)
        tile_in(0, 0).start()

    slot = g % 3

    @pl.when(g >= 2)
    def _():
        wait_out(g - 2)

    @pl.when(g + 1 < n_tiles)
    def _():
        tile_in(g + 1, (g + 1) % 3).start()

    tile_in(g, slot).wait()

    def scatter_rows(lo):
        def body(r, carry):
            for kk in range(TOP_K):
                dst = pos_ref[0, kk, r]
                pltpu.make_async_copy(stage.at[slot, _toks(r), :], xs_hbm.at[_toks(dst), :],
                                      sem_out.at[g % 2]).start(priority=kk % 2)
            return carry
        lax.fori_loop(lo, tq, body, 0, unroll=4)

    @pl.when(g % nj == 0)
    def _():
        scatter_rows(N_PAD)

    @pl.when(g % nj != 0)
    def _():
        scatter_rows(0)

    @pl.when(g == n_tiles - 1)
    def _():
        @pl.when(g >= 1)
        def _():
            wait_out(g - 1)
        wait_out(g)


def _dispatch(pstart, pend, pos, xn_t, n_rows, bsz, tq, tm):
    m = xn_t.shape[0] // TOK_ROWS
    nj = m // (bsz * tq)
    n_experts = pstart.shape[0]
    grid_spec = pltpu.PrefetchScalarGridSpec(
        num_scalar_prefetch=2,
        grid=(bsz * nj,),
        in_specs=[pl.BlockSpec((1, TOP_K, tq), lambda g, ps, pe: (g, 0, 0), memory_space=pltpu.SMEM),
                  pl.BlockSpec(memory_space=pl.ANY)],
        out_specs=pl.BlockSpec(memory_space=pl.ANY),
        scratch_shapes=[pltpu.VMEM((tm * TOK_ROWS, LANES), F32),
                        pltpu.VMEM((3, tq * TOK_ROWS, LANES), F32),
                        pltpu.SemaphoreType.DMA((3,)), pltpu.SemaphoreType.DMA((2,))],
    )
    return pl.pallas_call(
        functools.partial(_dispatch_kernel, n_experts=n_experts, tm=tm, nj=nj),
        grid_spec=grid_spec,
        out_shape=jax.ShapeDtypeStruct((n_rows * TOK_ROWS, LANES), F32),
        compiler_params=_cparams(("arbitrary",)),
        name="dispatch",
    )(pstart, pend, pos, xn_t)


CAST_ROWS = 256


def _experts_kernel(be_ref, nu_ref, slot_ref, nxt_ref, x_ref, wgu_hbm, bg_ref, bu_ref, wd_hbm, bd_ref, y_ref,
                    wg_s, wu_s, wd_s, wgu_buf, wd_buf, sem):
    i = pl.program_id(0)
    n_used = nu_ref[0]
    dff = wg_s.shape[0]
    new_expert = jnp.logical_or(i == 0, be_ref[i] != be_ref[jnp.maximum(i - 1, 0)])

    def weight_copies(e, slot):
        return (pltpu.make_async_copy(wgu_hbm.at[e], wgu_buf.at[slot], sem.at[slot]),
                pltpu.make_async_copy(wd_hbm.at[e], wd_buf.at[slot], sem.at[slot]))

    @pl.when(i == 0)
    def _():
        for cp in weight_copies(be_ref[0], 0):
            cp.start()

    for slot in range(2):
        @pl.when(jnp.logical_and(jnp.logical_and(new_expert, i < n_used), slot_ref[i] == slot))
        def _():
            for cp in weight_copies(be_ref[i], slot):
                cp.wait()

            @pl.when(nxt_ref[i] >= 0)
            def _():
                for cp in weight_copies(nxt_ref[i], 1 - slot):
                    cp.start()

            for c in range(dff // CAST_ROWS):
                rows = pl.ds(c * CAST_ROWS, CAST_ROWS)
                wd_s[rows, :] = wd_buf[slot, rows, :].astype(BF16)
                for sl in range(wgu_buf.shape[1]):
                    lanes = pl.ds(sl * LANES, LANES)
                    wg_s[rows, lanes] = wgu_buf[
                        slot, sl, pl.ds(2 * c * CAST_ROWS, CAST_ROWS, stride=2), :].astype(BF16)
                    wu_s[rows, lanes] = wgu_buf[
                        slot, sl, pl.ds(2 * c * CAST_ROWS + 1, CAST_ROWS, stride=2), :].astype(BF16)

    @pl.when(i < n_used)
    def _():
        tm = x_ref.shape[0] // TOK_ROWS
        xb = _load_token_tiles(x_ref, tm).astype(BF16)
        g = _dot_nt(xb, wg_s[...]) + bg_ref[0]
        u = _dot_nt(xb, wu_s[...]) + bu_ref[0]
        gate = jnp.minimum(g, SWIGLU_LIMIT)
        up = jnp.clip(u, -SWIGLU_LIMIT, SWIGLU_LIMIT)
        act = (up + 1.0) * gate * _sigmoid(SWIGLU_ALPHA * gate)
        _store_token_tiles(y_ref, _dot(act.astype(BF16), wd_s[...]) + bd_ref[0])

    @pl.when(i >= n_used)
    def _():
        y_ref[...] = jnp.zeros_like(y_ref)


def _experts(block_e, n_used, xs, w_gu_t, b_g, b_u, w_d, b_d, tm):
    n_blocks = block_e.shape[0]
    dff, d = w_d.shape[1:]
    tile_blk = lambda f: pl.BlockSpec((tm * TOK_ROWS, LANES), f)
    assert dff % CAST_ROWS == 0
    blk = jnp.arange(n_blocks, dtype=jnp.int32)
    changed = jnp.concatenate([jnp.zeros((1,), jnp.int32), (block_e[1:] != block_e[:-1]).astype(jnp.int32)])
    w_slot = jnp.cumsum(changed) % 2
    later = jnp.logical_and(jnp.logical_and(blk[None, :] > blk[:, None], blk[None, :] < n_used[0]),
                            block_e[None, :] != block_e[:, None])
    nxt_e = jnp.where(jnp.any(later, axis=1), block_e[jnp.argmax(later, axis=1)], -1).astype(jnp.int32)
    wspec = lambda k, n: pl.BlockSpec((1, k, n), lambda i, be, nu, ws, ne: (be[i], 0, 0))
    grid_spec = pltpu.PrefetchScalarGridSpec(
        num_scalar_prefetch=4,
        grid=(n_blocks,),
        in_specs=[
            tile_blk(lambda i, be, nu, ws, ne: (jnp.minimum(i, nu[0] - 1), 0)),
            pl.BlockSpec(memory_space=pl.ANY),
            wspec(1, dff), wspec(1, dff),
            pl.BlockSpec(memory_space=pl.ANY),
            wspec(1, d),
        ],
        out_specs=tile_blk(lambda i, be, nu, ws, ne: (i, 0)),
        scratch_shapes=[pltpu.VMEM((dff, d), BF16), pltpu.VMEM((dff, d), BF16), pltpu.VMEM((dff, d), BF16),
                        pltpu.VMEM((2,) + w_gu_t.shape[1:], F32), pltpu.VMEM((2, dff, d), F32),
                        pltpu.SemaphoreType.DMA((2,))],
    )
    return pl.pallas_call(
        _experts_kernel,
        grid_spec=grid_spec,
        out_shape=jax.ShapeDtypeStruct((n_blocks * tm * TOK_ROWS, LANES), F32),
        compiler_params=_cparams(("arbitrary",)),
        name="experts",
    )(block_e, n_used, w_slot, nxt_e, xs, w_gu_t, b_g, b_u, w_d, b_d)


def _combine_kernel(pos_ref, posn_ref, hsrc_ref, hsrcn_ref, tw_ref, y_hbm, h_hbm,
                    nw_ref, o_ref, ybuf, hbuf, sem):
    i = pl.program_id(0)
    n = pl.num_programs(0)
    tt = o_ref.shape[0]
    nt = tt * TOK_ROWS

    def copies(src_pos_ref, src_h_ref, dst_slot, r):
        return [pltpu.make_async_copy(
            y_hbm.at[_toks(src_pos_ref[0, kk, r]), :],
            ybuf.at[dst_slot, _toks(kk * tt + r), :],
            sem.at[dst_slot]) for kk in range(TOP_K)]

    def start_h(src_h_ref, dst_slot):
        h_row = pl.multiple_of(src_h_ref[0, 0, 0], 8)
        pltpu.make_async_copy(h_hbm.at[pl.ds(h_row, tt), :], hbuf.at[dst_slot], sem.at[dst_slot]).start()

    @pl.when(i == 0)
    def _():
        def body(r, carry):
            for kk, cp in enumerate(copies(pos_ref, hsrc_ref, 0, r)):
                cp.start(priority=kk % 2)
            return carry
        lax.fori_loop(0, tt, body, 0, unroll=2)
        start_h(hsrc_ref, 0)

    def wait_slot(s):
        pltpu.make_async_copy(ybuf.at[s], ybuf.at[s], sem.at[s]).wait()
        pltpu.make_async_copy(hbuf.at[s], hbuf.at[s], sem.at[s]).wait()

    def step(slot):
        for r in range(tt):
            for kk, cp in enumerate(copies(posn_ref, hsrcn_ref, 1 - slot, r)):
                cp.start(priority=kk % 2)
        start_h(hsrcn_ref, 1 - slot)

        wait_slot(slot)
        acc = hbuf[slot]
        tw = tw_ref[...]
        for kk in range(TOP_K):
            yk = jnp.concatenate(
                [ybuf[slot, pl.ds(kk * nt + s, tt, stride=TOK_ROWS), :] for s in range(TOK_ROWS)], axis=1)
            acc = acc + tw[:, kk:kk + 1] * yk
        ms = jnp.mean(acc * acc, axis=-1, keepdims=True)
        o_ref[...] = (acc * lax.rsqrt(ms + EPS)) * nw_ref[...]

        @pl.when(i == n - 1)
        def _():
            wait_slot(1 - slot)

    for parity in range(2):
        pl.when(i % 2 == parity)(functools.partial(step, parity))


def _combine(pos_seq, tw_seq, h_src, y_rows, h2, norm_w, tt):
    n_tiles = h_src.shape[0]
    d = h2.shape[1]
    last = n_tiles - 1
    nt = tt * TOK_ROWS
    cur = lambda i: (i, 0, 0)
    nxt = lambda i: (jnp.minimum(i + 1, last), 0, 0)
    idx_blk = lambda f: pl.BlockSpec((1, TOP_K, tt), f, memory_space=pltpu.SMEM)
    one_blk = lambda f: pl.BlockSpec((1, 1, 1), f, memory_space=pltpu.SMEM)
    return pl.pallas_call(
        _combine_kernel,
        grid=(n_tiles,),
        in_specs=[
            idx_blk(cur), idx_blk(nxt), one_blk(cur), one_blk(nxt),
            pl.BlockSpec((tt, TOP_K), lambda i: (i, 0)),
            pl.BlockSpec(memory_space=pl.ANY),
            pl.BlockSpec(memory_space=pl.ANY),
            pl.BlockSpec(norm_w.shape, lambda i: (0, 0)),
        ],
        out_specs=pl.BlockSpec((tt, d), lambda i: (i, 0)),
        scratch_shapes=[pltpu.VMEM((2, TOP_K * nt, LANES), F32), pltpu.VMEM((2, tt, d), F32),
                        pltpu.SemaphoreType.DMA((2,))],
        out_shape=jax.ShapeDtypeStruct((n_tiles * tt, d), F32),
        compiler_params=_cparams(("arbitrary",)),
        name="combine",
    )(pos_seq, pos_seq, h_src, h_src, tw_seq, y_rows, h2, norm_w)


def _pick(n, prefs):
    for p in prefs:
        if n % p == 0:
            return p
    raise ValueError(f"no tile in {prefs} divides {n}")


def kernel(x, meta_tokens, hg_lb_logits, norm_mix, w_in, hg_norm, ml_conv_w, ml_conv_b, ml_wq, ml_wk, ml_wv,
           ml_gate_b, ml_norm, ml_skip, w_branch_hg, w_branch_ml, w_out, norm_ffn, router_w, router_b,
           exp_w_gu, exp_b_gu, exp_w_down, exp_b_down, norm_final):
    bsz, seq, d = x.shape
    assert norm_mix.shape[0] == 1, "single-layer block"
    assert seq % CHUNK == 0 and d % LANES == 0
    t = CHUNK + seq
    m_rows = bsz * t
    n_experts = router_w.shape[-1]
    dff = exp_w_down.shape[2]
    assert n_experts <= LANES

    head = jnp.concatenate([jnp.zeros((N_PAD, d), x.dtype), meta_tokens.astype(x.dtype)], axis=0)
    x2d = x.reshape(bsz * seq, d)
    lower_bounds = jnp.cumsum(jax.nn.softmax(hg_lb_logits.astype(F32), axis=0), axis=0)

    w = w_in[0]
    n_hg = 4 * HG_W
    n_a = n_hg + 2 * ML_W
    w_hg = w[:, :n_hg].astype(BF16)
    w_m = jnp.pad(w[:, n_a:n_a + 2 * N_HEADS], ((0, 0), (0, LANES - 2 * N_HEADS)))
    w_ml = jnp.concatenate([w[:, n_hg:n_a], w_m], axis=1).astype(BF16)
    w_g = w[:, n_a + 2 * N_HEADS:].astype(BF16)
    gain = norm_mix[0][None]

    bb = _pick(bsz, (8, 4, 2, 1))
    tri = jnp.asarray(np.tril(np.ones((CHUNK, CHUNK), np.float32)), BF16)
    y_hg = _hgrn2(x, head, gain, w_hg, tri, lower_bounds[0][None], hg_norm[0][None], bb)

    gb = ml_gate_b[0].astype(F32)
    gb_col = jnp.pad(gb, (0, LANES - 2 * N_HEADS))[None]
    gb_row = jnp.broadcast_to(gb[:, None], (2 * N_HEADS, CHUNK))
    y_ml = _mlstm(x, head, gain, w_ml, tri, ml_conv_w[0], ml_conv_b[0][None],
                  ml_wq[0].astype(BF16), ml_wk[0].astype(BF16), ml_wv[0].astype(BF16),
                  gb_col, gb_row, ml_norm[0][None], ml_skip[0][None], bb)

    assert n_experts % 8 == 0
    tq = _pick(t, (704, 192, 64))
    rw = router_w[0].astype(F32).T
    rw1 = rw.astype(BF16)
    rw2 = (rw - rw1.astype(F32)).astype(BF16)
    rw_split = jnp.stack([rw1, rw2])
    rb = jnp.broadcast_to(router_b[0].astype(F32)[:, None], (n_experts, tq))
    h2, xn2, top_e, top_w, rank, cnt = _merge_route(
        x2d, head, gain, w_g, y_hg.reshape(m_rows, HG_W), y_ml.reshape(m_rows, ML_W),
        w_branch_hg[0].astype(BF16), w_branch_ml[0].astype(BF16), w_out[0].astype(BF16),
        norm_ffn[0][None], rw_split, rb, n_experts, bsz, tq)

    tm6 = 512
    n_assign = bsz * (t - N_PAD) * TOP_K
    counts = cnt[:, 0].astype(jnp.int32)
    padded = ((counts + tm6 - 1) // tm6) * tm6
    pend = jnp.cumsum(padded).astype(jnp.int32)
    pstart = pend - padded
    n_blocks = -(-n_assign // tm6) + n_experts
    n_rows = n_blocks * tm6
    blk_start = jnp.arange(n_blocks, dtype=jnp.int32) * tm6
    block_e = jnp.minimum(jnp.sum((blk_start[:, None] >= pend[None, :]).astype(jnp.int32), axis=1),
                          n_experts - 1)
    n_used = (pend[-1] // tm6)[None]
    pos = rank + jnp.sum(jnp.where(top_e[..., None] == jnp.arange(n_experts, dtype=jnp.int32), pstart, 0),
                         axis=-1)
    xs = _dispatch(pstart, pend, pos, xn2, n_rows, bsz, tq, tm6)

    w_gu_t = jnp.swapaxes(exp_w_gu[0].reshape(n_experts, d // LANES, LANES, 2 * dff), 2, 3)
    bgu = exp_b_gu[0]
    y_rows = _experts(block_e, n_used, xs, w_gu_t, bgu[:, None, 0::2], bgu[:, None, 1::2],
                      exp_w_down[0], exp_b_down[0][:, None, :], tm6)

    tt = _pick(seq, (256, 128, 64))
    tiles_per_b = seq // tt
    n_tiles = bsz * tiles_per_b
    nj = t // tq

    def seq_tiles(a):
        a = a.reshape(bsz, nj, TOP_K, tq).transpose(0, 2, 1, 3).reshape(bsz, TOP_K, t)[:, :, CHUNK:]
        return a.reshape(bsz, TOP_K, tiles_per_b, tt).transpose(0, 2, 1, 3).reshape(n_tiles, TOP_K, tt)

    h_src = (jnp.arange(bsz, dtype=jnp.int32)[:, None] * t + CHUNK
             + jnp.arange(tiles_per_b, dtype=jnp.int32)[None, :] * tt).reshape(n_tiles, 1, 1)
    tw_cols = seq_tiles(top_w).transpose(0, 2, 1).reshape(n_tiles * tt, TOP_K)
    out = _combine(seq_tiles(pos), tw_cols, h_src, y_rows, h2, norm_final[None], tt)
    return out.reshape(bsz, seq, d)
```

```python
import functools

import numpy as np
import jax
import jax.numpy as jnp
from jax import lax
from jax.experimental import pallas as pl
from jax.experimental.pallas import tpu as pltpu

F32 = jnp.float32
BF16 = jnp.bfloat16

N_META = 16
CHUNK = 64
N_PAD = CHUNK - N_META
EPS = 1e-6

N_HEADS = 4
HG_DK = 128
HG_DV = 128
HG_W = N_HEADS * HG_DV
ML_DK = 64
ML_DV = 128
ML_W = N_HEADS * ML_DV
ML_CONV = 4
TOP_K = 4
SWIGLU_LIMIT = 7.0
SWIGLU_ALPHA = 1.702

LANES = 128
VMEM_LIMIT_BYTES = 56 * 1024 * 1024

HG_LEVELS = (32, 16, 8, 4, 2, 1)
GROUP_ROWS = 8


def _cparams(sem):
    return pltpu.CompilerParams(dimension_semantics=sem, vmem_limit_bytes=VMEM_LIMIT_BYTES)


def _sigmoid(x):
    return 0.5 + 0.5 * jnp.tanh(0.5 * x)


def _split3(x):
    x1 = x.astype(BF16)
    r1 = x - x1.astype(F32)
    x2 = r1.astype(BF16)
    x3 = (r1 - x2.astype(F32)).astype(BF16)
    return x1, x2, x3


def _dot(a, b):
    return jnp.dot(a, b, preferred_element_type=F32)


TOK_ROWS = 8


def _store_token_tiles(ref, x):
    n = x.shape[0]
    for s in range(TOK_ROWS):
        ref[pl.ds(s, n, stride=TOK_ROWS), :] = x[:, s * LANES:(s + 1) * LANES]


def _load_token_tiles(ref, n):
    return jnp.concatenate([ref[pl.ds(s, n, stride=TOK_ROWS), :] for s in range(TOK_ROWS)], axis=1)


def _dot_nt(a, b):
    return lax.dot_general(a, b, (((1,), (1,)), ((), ())), preferred_element_type=F32)


def _dot_exact_lhs(m_bf16, x):
    x1, x2, x3 = _split3(x)
    return _dot(m_bf16, x1) + _dot(m_bf16, x2) + _dot(m_bf16, x3)


def _h_tile(x_ref, head_ref, j):
    x = x_ref[...]
    first = jnp.concatenate([head_ref[...], x[:x.shape[0] - CHUNK]], axis=0)
    return jnp.where(j == 0, first, x)


def _x_tile_spec(tr, d, seq):
    assert seq % 8 == 0 and tr % 8 == 0 and CHUNK % 8 == 0
    return pl.BlockSpec(
        (pl.Element(tr), pl.Element(d)),
        lambda b, j: (pl.multiple_of(b * seq + jnp.maximum(j * tr - CHUNK, 0), 8), 0))


def _rms_bf16(x, gain):
    ms = jnp.mean(x * x, axis=-1, keepdims=True)
    return ((x * lax.rsqrt(ms + EPS)) * gain).astype(BF16)


def _project_ahead(x_ref, gain_ref, w_ref, nxt_ref):
    x = x_ref[...].reshape(x_ref.shape[0] * CHUNK, x_ref.shape[2])
    nxt_ref[...] = _dot(_rms_bf16(x, gain_ref[...]), w_ref[...])


def _with_projection_pipeline(step, x_ref, head_ref, gain_ref, w_ref, proj_a, proj_b, c):
    @pl.when(c == 0)
    def _():
        p0 = _dot(_rms_bf16(head_ref[...], gain_ref[...]), w_ref[...])
        for b in range(x_ref.shape[0]):
            proj_a[b * CHUNK:(b + 1) * CHUNK, :] = p0

    for parity, (cur, nxt) in enumerate(((proj_a, proj_b), (proj_b, proj_a))):
        @pl.when(c % 2 == parity)
        def _():
            step(cur, functools.partial(_project_ahead, x_ref, gain_ref, w_ref, nxt))


def _chunk_ahead_spec(bb, d, nc):
    return pl.BlockSpec((bb, CHUNK, d), lambda i, c: (i, jnp.minimum(c, nc - 2), 0))


def _hgrn2_kernel(x_ref, head_ref, gain_ref, w_ref, tri_ref, lb_ref, nw_ref, y_ref, proj_a, proj_b, *st_refs):
    c = pl.program_id(1)

    @pl.when(c == 0)
    def _():
        for st_ref in st_refs:
            st_ref[...] = jnp.zeros_like(st_ref)

    def step(cur, project):
        bb = x_ref.shape[0]
        for b0 in range(0, bb, GROUP_ROWS):
            _hgrn2_group(c, range(b0, min(b0 + GROUP_ROWS, bb)), tri_ref, lb_ref, nw_ref, y_ref, st_refs, cur,
                         project if b0 == 0 else None)

    _with_projection_pipeline(step, x_ref, head_ref, gain_ref, w_ref, proj_a, proj_b, c)


def _hgrn2_group(c, bs, tri_ref, lb_ref, nw_ref, y_ref, st_refs, cur, project):
    row = lax.broadcasted_iota(jnp.int32, (CHUNK, 1), 0)
    valid = (c * CHUNK + row) >= N_PAD
    ti = lax.broadcasted_iota(jnp.int32, (CHUNK, CHUNK), 0)
    si = lax.broadcasted_iota(jnp.int32, (CHUNK, CHUNK), 1)
    diag_mask = ti == si
    level_masks = {}
    for m in HG_LEVELS:
        same_pair = (ti & ~(2 * m - 1)) == (si & ~(2 * m - 1))
        level_masks[m] = same_pair & ((ti & m) != 0) & ((si & m) == 0)

    lb = lb_ref[...]
    tri = tri_ref[...]
    per_b = {}
    for b in bs:
        rows_b = slice(b * CHUNK, (b + 1) * CHUNK)
        hf = cur[rows_b, HG_W:2 * HG_W]
        f = lb + (1.0 - lb) * _sigmoid(hf)
        f = jnp.where(valid, f, 1.0)
        logf = jnp.log(f)
        k_all = 1.0 - f
        hq = cur[rows_b, 0:HG_W]
        q_all = hq * _sigmoid(hq)
        b_cum = _dot_exact_lhs(tri, logf)
        e_b = jnp.exp(b_cum)
        e_bl = jnp.exp(b_cum[CHUNK - 1:CHUNK] - b_cum)

        q_fac, k_fac = {}, {}
        for m in HG_LEVELS:
            if m >= 4:
                grp = b_cum.reshape(CHUNK // (2 * m), 2 * m, HG_W)
                e = (grp - grp[:, m - 1:m, :]).reshape(CHUNK, HG_W)
                q_fac[m] = jnp.exp(jnp.minimum(e, 0.0))
                k_fac[m] = jnp.exp(jnp.minimum(-e, 0.0))
        f_prev = pltpu.roll(f, 1, 0)
        f_next = pltpu.roll(f, CHUNK - 1, 0)
        r4 = row & 3
        q_fac[2] = jnp.where(r4 == 2, f, jnp.where(r4 == 3, f * f_prev, 1.0))
        k_fac[2] = jnp.where(r4 == 0, f_next, 1.0)
        q_fac[1] = jnp.where((row & 1) == 1, f, 1.0)

        hg = cur[rows_b, 3 * HG_W:4 * HG_W]
        v_all = cur[rows_b, 2 * HG_W:3 * HG_W]
        q_fac = {m: v.astype(BF16) for m, v in q_fac.items()}
        k_fac = {m: v.astype(BF16) for m, v in k_fac.items()}
        per_b[b] = (q_all, k_all, q_fac, k_fac, e_b, e_bl, v_all, hg * _sigmoid(hg))

    if project is not None:
        project()

    pairs = [(b, h) for b in bs for h in range(N_HEADS)]
    sls = [slice(h * HG_DK, (h + 1) * HG_DK) for h in range(N_HEADS)]

    scores_all = []
    for b, h in pairs:
        q_all, k_all, q_fac, k_fac = per_b[b][:4]
        qb = q_all[:, sls[h]].astype(BF16)
        kb = k_all[:, sls[h]].astype(BF16)
        scores = jnp.where(diag_mask, _dot_nt(qb, kb), 0.0)
        for m in HG_LEVELS:
            qd = qb * q_fac[m][:, sls[h]]
            kd = kb * k_fac[m][:, sls[h]] if m in k_fac else kb
            scores = jnp.where(level_masks[m], _dot_nt(qd, kd), scores)
        scores_all.append(scores.astype(BF16))

    sts = [st_refs[b * N_HEADS + h][...] for b, h in pairs]
    outs = []
    for p, (b, h) in enumerate(pairs):
        q_all, e_b, v_all = per_b[b][0], per_b[b][4], per_b[b][6]
        qe = (q_all[:, sls[h]] * e_b[:, sls[h]]).astype(BF16)
        outs.append(_dot(scores_all[p], v_all[:, sls[h]].astype(BF16)) + _dot_nt(qe, sts[p].astype(BF16)))

    for p, (b, h) in enumerate(pairs):
        k_all, e_b, e_bl, v_all = per_b[b][1], per_b[b][4], per_b[b][5], per_b[b][6]
        kl = (k_all[:, sls[h]] * e_bl[:, sls[h]]).astype(BF16)
        vt = v_all[:, sls[h]].T.astype(BF16)
        st_refs[b * N_HEADS + h][...] = e_b[CHUNK - 1:CHUNK, sls[h]] * sts[p] + _dot(vt, kl)

    for p, (b, h) in enumerate(pairs):
        o = outs[p]
        ms = jnp.mean(o * o, axis=-1, keepdims=True)
        y = (o * lax.rsqrt(ms + EPS)) * nw_ref[:, sls[h]] * per_b[b][7][:, sls[h]]
        y_ref[b, :, sls[h]] = y.astype(y_ref.dtype)


def _hgrn2(x3, head, gain, w_hg, tri, lb, norm_w, bb):
    bsz, seq, d = x3.shape
    nc = seq // CHUNK + 1
    blk = pl.BlockSpec((bb, CHUNK, HG_W), lambda i, c: (i, c, 0))
    full = lambda a: pl.BlockSpec(a.shape, lambda i, c: (0, 0))
    return pl.pallas_call(
        _hgrn2_kernel,
        grid=(bsz // bb, nc),
        in_specs=[_chunk_ahead_spec(bb, d, nc), full(head), full(gain), full(w_hg), full(tri), full(lb),
                  full(norm_w)],
        out_specs=blk,
        out_shape=jax.ShapeDtypeStruct((bsz, nc * CHUNK, HG_W), BF16),
        scratch_shapes=([pltpu.VMEM((bb * CHUNK, w_hg.shape[1]), F32)] * 2
                        + [pltpu.VMEM((HG_DV, HG_DK), F32)] * (bb * N_HEADS)),
        compiler_params=_cparams(("parallel", "arbitrary")),
        name="hgrn2",
    )(x3, head, gain, w_hg, tri, lb, norm_w)


def _log_sigmoid(x):
    return jnp.minimum(x, 0.0) - jnp.log(1.0 + jnp.exp(-jnp.abs(x)))


def _mlstm_kernel(x_ref, head_ref, gain_ref, w_ref, tri_ref, cw_ref, cb_ref, wq_ref, wk_ref, wv_ref,
                  gbc_ref, gbr_ref, nw_ref, sk_ref, y_ref, proj_a, proj_b, *scratch):
    c = pl.program_id(1)

    @pl.when(c == 0)
    def _():
        for ref in scratch:
            ref[...] = jnp.zeros_like(ref)

    step = functools.partial(_mlstm_step, c, x_ref.shape[0], tri_ref, cw_ref, cb_ref, wq_ref, wk_ref, wv_ref,
                             gbc_ref, gbr_ref, nw_ref, sk_ref, y_ref, scratch)
    _with_projection_pipeline(step, x_ref, head_ref, gain_ref, w_ref, proj_a, proj_b, c)


def _mlstm_step(c, bb, tri_ref, cw_ref, cb_ref, wq_ref, wk_ref, wv_ref, gbc_ref, gbr_ref, nw_ref, sk_ref, y_ref,
                scratch, cur, project):
    n_pairs = bb * N_HEADS
    s_refs = scratch[:n_pairs]
    m_refs = scratch[n_pairs:2 * n_pairs]
    tail_refs = scratch[2 * n_pairs:]

    pos_c = c * CHUNK + lax.broadcasted_iota(jnp.int32, (CHUNK, 1), 0)
    valid_c = pos_c >= N_PAD
    pos_r = c * CHUNK + lax.broadcasted_iota(jnp.int32, (1, CHUNK), 1)
    valid_r = pos_r >= N_PAD
    ti = lax.broadcasted_iota(jnp.int32, (CHUNK, CHUNK), 0)
    si = lax.broadcasted_iota(jnp.int32, (CHUNK, CHUNK), 1)
    causal = si <= ti
    tri = tri_ref[...]
    ones_v = jnp.ones((CHUNK, ML_DV), BF16)
    neg_inf = -jnp.inf

    per_b = []
    for b in range(bb):
        rows_b = slice(b * CHUNK, (b + 1) * CHUNK)
        mm = jnp.where(valid_c, cur[rows_b, 0:ML_W], 0.0)
        ext = jnp.concatenate([tail_refs[b][...], mm], axis=0)
        tail_refs[b][...] = mm[CHUNK - 8:CHUNK]
        conv = cb_ref[...]
        for j in range(ML_CONV):
            off = 8 - (ML_CONV - 1) + j
            conv = conv + cw_ref[j:j + 1, :] * ext[off:off + CHUNK]
        cact = conv * _sigmoid(conv)
        cact_b = cact.astype(BF16)
        mm_b = mm.astype(BF16)

        graw = cur[rows_b, 2 * ML_W:2 * ML_W + LANES]
        gcol = graw + gbc_ref[...]
        li_col = jnp.where(valid_c, gcol, neg_inf)
        lf_col = jnp.where(valid_c, _log_sigmoid(gcol), 0.0)
        b_col = _dot_exact_lhs(tri, lf_col)
        grow = graw.T[0:2 * N_HEADS] + gbr_ref[...]
        li_row = jnp.where(valid_r, grow, neg_inf)
        lf_row = jnp.where(valid_r, _log_sigmoid(grow), 0.0)
        r1, r2, r3 = _split3(lf_row)
        b_row = _dot_nt(r1, tri) + _dot_nt(r2, tri) + _dot_nt(r3, tri)

        ogate = _sigmoid(cur[rows_b, ML_W:2 * ML_W])
        per_b.append((cact, cact_b, mm_b, li_col, b_col, li_row, b_row, ogate))

    project()

    pairs = [(b, h) for b in range(bb) for h in range(N_HEADS)]
    sls = [slice(h * ML_DV, (h + 1) * ML_DV) for h in range(N_HEADS)]

    qs, ks, vs = [], [], []
    for b, h in pairs:
        cact_b, mm_b = per_b[b][1], per_b[b][2]
        qs.append((_dot(cact_b[:, sls[h]], wq_ref[h]) * (ML_DK ** -0.5)).astype(BF16))
        ks.append(_dot(cact_b[:, sls[h]], wk_ref[h]))
        v = _dot(mm_b[:, sls[h]], wv_ref[h]).astype(BF16)
        vs.append(jnp.concatenate([v, ones_v], axis=1))

    n_p = len(pairs)
    blk = lambda p: slice(p * CHUNK, (p + 1) * CHUNK)
    stack = lambda xs: jnp.concatenate(xs, axis=0)
    bc_all = stack([per_b[b][4][:, N_HEADS + h:N_HEADS + h + 1] for b, h in pairs])
    lic_all = stack([per_b[b][3][:, h:h + 1] for b, h in pairs])
    row_all = stack([jnp.broadcast_to(per_b[b][5][h:h + 1, :] - per_b[b][6][N_HEADS + h:N_HEADS + h + 1, :],
                                      (CHUNK, CHUNK)) for b, h in pairs])
    mprev_all = stack([jnp.broadcast_to(m_refs[p][0:1, 0:1], (CHUNK, 1)) for p in range(n_p)])
    glast_all = stack([jnp.broadcast_to(per_b[b][4][CHUNK - 1:CHUNK, N_HEADS + h:N_HEADS + h + 1], (CHUNK, 1))
                       for b, h in pairs])
    causal_all = stack([causal] * n_p)

    d_all = jnp.where(causal_all, bc_all + row_all, neg_inf)
    a_all = bc_all + mprev_all
    m_t_all = jnp.maximum(a_all, jnp.max(d_all, axis=-1, keepdims=True))
    w_intra_all = jnp.exp(d_all - m_t_all)
    w_inter_all = jnp.exp(a_all - m_t_all)

    qk_all = (stack([_dot_nt(qs[p], ks[p].astype(BF16)) for p in range(n_p)]) * w_intra_all).astype(BF16)
    s_augs = [s_refs[p][...] for p in range(n_p)]
    intra_all = stack([_dot(qk_all[blk(p)], vs[p]) for p in range(n_p)])
    inter_all = stack([_dot(qs[p], s_augs[p].astype(BF16)) for p in range(n_p)])
    numden_all = intra_all + w_inter_all * inter_all
    o_all = numden_all[:, :ML_DV] / jnp.maximum(jnp.abs(numden_all[:, ML_DV:]), jnp.exp(-m_t_all))
    ms_all = jnp.mean(o_all * o_all, axis=-1, keepdims=True)
    on_all = o_all * lax.rsqrt(ms_all + EPS)

    e_all = glast_all - bc_all + lic_all
    gm_all = glast_all + mprev_all
    e_max = jnp.max(e_all.reshape(n_p, CHUNK, 1), axis=1, keepdims=True)
    m_new_all = jnp.maximum(gm_all.reshape(n_p, CHUNK, 1), e_max).reshape(n_p * CHUNK, 1)
    w_s_all = jnp.exp(e_all - m_new_all)
    w_p_all = jnp.exp(gm_all - m_new_all)
    kw_all = stack(ks) * w_s_all
    for p in range(n_p):
        kw_t = kw_all[blk(p)].T.astype(BF16)
        s_refs[p][...] = w_p_all[p * CHUNK:p * CHUNK + 1] * s_augs[p] + _dot(kw_t, vs[p])
        m_refs[p][...] = jnp.broadcast_to(m_new_all[p * CHUNK:p * CHUNK + 1], m_refs[p].shape)

    for p, (b, h) in enumerate(pairs):
        cact, ogate = per_b[b][0], per_b[b][7]
        sl = sls[h]
        y = (on_all[blk(p)] * nw_ref[:, sl] + sk_ref[:, sl] * cact[:, sl]) * ogate[:, sl]
        y_ref[b, :, sl] = y.astype(y_ref.dtype)


def _mlstm(x3, head, gain, w_ml, tri, conv_w, conv_b, wq, wk, wv, gb_col, gb_row, norm_w, skip, bb):
    bsz, seq, d = x3.shape
    nc = seq // CHUNK + 1
    blk = pl.BlockSpec((bb, CHUNK, ML_W), lambda i, c: (i, c, 0))

    def full(a):
        nd = a.ndim
        return pl.BlockSpec(a.shape, lambda i, c: (0,) * nd)

    params = (head, gain, w_ml, tri, conv_w, conv_b, wq, wk, wv, gb_col, gb_row, norm_w, skip)
    return pl.pallas_call(
        _mlstm_kernel,
        grid=(bsz // bb, nc),
        in_specs=[_chunk_ahead_spec(bb, d, nc)] + [full(p) for p in params],
        out_specs=blk,
        out_shape=jax.ShapeDtypeStruct((bsz, nc * CHUNK, ML_W), BF16),
        scratch_shapes=([pltpu.VMEM((bb * CHUNK, w_ml.shape[1]), F32)] * 2
                        + [pltpu.VMEM((ML_DK, 2 * ML_DV), F32)] * (bb * N_HEADS)
                        + [pltpu.VMEM((8, LANES), F32)] * (bb * N_HEADS)
                        + [pltpu.VMEM((8, ML_W), F32)] * bb),
        compiler_params=_cparams(("parallel", "arbitrary")),
        name="mlstm",
    )(x3, *params)


def _merge_route_kernel(x_ref, head_ref, gain_ref, wg_ref, yh_ref, ym_ref, wbh_ref, wbm_ref, wo_ref, nf_ref,
                        rw_ref, rb_ref, h2_ref, xn_ref, te_ref, tw_ref, rk_ref, cnt_ref, *, n_experts):
    d = x_ref.shape[1]
    tq = x_ref.shape[0]
    j = pl.program_id(1)

    @pl.when(jnp.logical_and(pl.program_id(0) == 0, j == 0))
    def _():
        cnt_ref[...] = jnp.zeros_like(cnt_ref)

    p_hg = _dot(yh_ref[...], wbh_ref[...])
    p_ml = _dot(ym_ref[...], wbm_ref[...])
    h = _h_tile(x_ref, head_ref, j)
    hn = _rms_bf16(h, gain_ref[...])
    g0 = _sigmoid(_dot(hn, wg_ref[:, :d]))
    g1 = _sigmoid(_dot(hn, wg_ref[:, d:]))
    merged = g0 * p_hg + g1 * p_ml
    h2 = h + _dot(merged.astype(BF16), wo_ref[...])
    h2_ref[...] = h2
    ms = jnp.mean(h2 * h2, axis=-1, keepdims=True)
    xn = (h2 * lax.rsqrt(ms + EPS)) * nf_ref[...]
    _store_token_tiles(xn_ref, xn)
    x1, x2, _ = _split3(xn)
    logits = (_dot_nt(rw_ref[0], x1) + _dot_nt(rw_ref[1], x1) + _dot_nt(rw_ref[0], x2)) + rb_ref[...]
    sub = lax.broadcasted_iota(jnp.int32, logits.shape, 0)
    work = logits
    vals, idxs = [], []
    for _ in range(TOP_K):
        vmax = jnp.max(work, axis=0, keepdims=True)
        imax = jnp.min(jnp.where(work == vmax, sub, n_experts), axis=0, keepdims=True)
        vals.append(vmax)
        idxs.append(imax)
        work = jnp.where(sub == imax, -jnp.inf, work)
    exps = [jnp.exp(v - vals[0]) for v in vals]
    tot = exps[0] + exps[1] + exps[2] + exps[3]

    krow = lax.broadcasted_iota(jnp.int32, (TOP_K, tq), 0)

    def rows(per_k):
        out = jnp.broadcast_to(per_k[0], (TOP_K, tq))
        for kk in range(1, TOP_K):
            out = jnp.where(krow == kk, per_k[kk], out)
        return out

    te_ref[0] = rows(idxs)
    tw_ref[0] = rows([e / tot for e in exps])

    valid = (j * tq + lax.broadcasted_iota(jnp.int32, (1, tq), 1)) >= N_PAD
    onehots = [jnp.where(jnp.logical_and(sub == idxs[kk], valid), 1.0, 0.0) for kk in range(TOP_K)]
    oh_all = onehots[0] + onehots[1] + onehots[2] + onehots[3]
    ri = lax.broadcasted_iota(jnp.int32, (tq, tq), 0)
    ci = lax.broadcasted_iota(jnp.int32, (tq, tq), 1)
    earlier = jnp.where(ri < ci, 1.0, 0.0).astype(BF16)
    cnt = cnt_ref[:, 0:1]
    before = _dot(oh_all.astype(BF16), earlier) + cnt
    ranks = []
    for kk in range(TOP_K):
        ranks.append(jnp.sum(jnp.where(sub == idxs[kk], before, 0.0), axis=0, keepdims=True))
        before = before + onehots[kk]
    rk_ref[0] = rows(ranks).astype(jnp.int32)
    cnt_ref[...] = jnp.broadcast_to(cnt + jnp.sum(oh_all, axis=1, keepdims=True), cnt_ref.shape)


def _merge_route(x2d, head, gain, w_g, y_hg, y_ml, wbh, wbm, wo, norm_ffn, rw_split, rb, n_experts, bsz, tq):
    d = x2d.shape[1]
    seq = x2d.shape[0] // bsz
    nj = (seq + CHUNK) // tq
    m = bsz * nj * tq
    assert d == TOK_ROWS * LANES, "token-tile layout assumes one (8,128) tile per token"
    row = lambda n: pl.BlockSpec((tq, n), lambda b, j: (b * nj + j, 0))
    tiles = pl.BlockSpec((tq * TOK_ROWS, LANES), lambda b, j: (b * nj + j, 0))
    per_tok = pl.BlockSpec((1, TOP_K, tq), lambda b, j: (b * nj + j, 0, 0))

    def full(a):
        nd = a.ndim
        return pl.BlockSpec(a.shape, lambda b, j: (0,) * nd)

    return pl.pallas_call(
        functools.partial(_merge_route_kernel, n_experts=n_experts),
        grid=(bsz, nj),
        in_specs=[_x_tile_spec(tq, d, seq), full(head), full(gain), full(w_g), row(HG_W), row(ML_W),
                  full(wbh), full(wbm), full(wo), full(norm_ffn), full(rw_split), full(rb)],
        out_specs=[row(d), tiles, per_tok, per_tok, per_tok,
                   pl.BlockSpec((n_experts, LANES), lambda b, j: (0, 0))],
        out_shape=[
            jax.ShapeDtypeStruct((m, d), F32),
            jax.ShapeDtypeStruct((m * TOK_ROWS, LANES), F32),
            jax.ShapeDtypeStruct((bsz * nj, TOP_K, tq), jnp.int32),
            jax.ShapeDtypeStruct((bsz * nj, TOP_K, tq), F32),
            jax.ShapeDtypeStruct((bsz * nj, TOP_K, tq), jnp.int32),
            jax.ShapeDtypeStruct((n_experts, LANES), F32),
        ],
        compiler_params=_cparams(("arbitrary", "arbitrary")),
        name="merge_route",
    )(x2d, head, gain, w_g, y_hg, y_ml, wbh, wbm, wo, norm_ffn, rw_split, rb)


def _toks(first, n=1):
    return pl.ds(pl.multiple_of(first * TOK_ROWS, TOK_ROWS), n * TOK_ROWS)


def _dispatch_kernel(ps_ref, pe_ref, pos_ref, x_hbm, xs_hbm, zbuf, stage, sem_in, sem_out, *, n_experts, tm, nj):
    g = pl.program_id(0)
    n_tiles = pl.num_programs(0)
    tq = stage.shape[1] // TOK_ROWS

    def tile_in(tile, slot):
        return pltpu.make_async_copy(x_hbm.at[_toks(tile * tq, tq), :], stage.at[slot], sem_in.at[slot])

    def wait_out(tile):
        for lo in (N_PAD, 0):
            @pl.when((tile % nj == 0) == (lo == N_PAD))
            def _():
                for _ in range(TOP_K):
                    pltpu.make_async_copy(stage.at[0, _toks(0, tq - lo), :], xs_hbm.at[_toks(0, tq - lo), :],
                                          sem_out.at[tile % 2]).wait()

    @pl.when(g == 0)
    def _():
        zbuf[...] = jnp.zeros_like(zbuf)
        for e in range(n_experts):
            @pl.when(pe_ref[e] > ps_ref[e])
            def _():
                pltpu.make_async_copy(zbuf, xs_hbm.at[_toks(pe_ref[e] - tm, tm), :], sem_out.at[0]).start()
        for e in range(n_experts):
            @pl.when(pe_ref[e] > ps_ref[e])
            def _():
                pltpu.make_async_copy(zbuf, xs_hbm.at[_toks(0, tm), :], sem_out.at[0]).wait()

        def zero_tail(blk, carry):
            cp = pltpu.make_async_copy(zbuf, xs_hbm.at[_toks(blk * tm, tm), :], sem_out.at[0])
            cp.start()
            cp.wait()
            return carry
        lax.fori_loop(pe_ref[n_experts - 1] // tm, xs_hbm.shape[0] // (tm * TOK_ROWS), zero_tail, 0)
        tile_in(0, 0).start()

    slot = g % 3

    @pl.when(g >= 2)
    def _():
        wait_out(g - 2)

    @pl.when(g + 1 < n_tiles)
    def _():
        tile_in(g + 1, (g + 1) % 3).start()

    tile_in(g, slot).wait()

    def scatter_rows(lo):
        def body(r, carry):
            for kk in range(TOP_K):
                dst = pos_ref[0, kk, r]
                pltpu.make_async_copy(stage.at[slot, _toks(r), :], xs_hbm.at[_toks(dst), :],
                                      sem_out.at[g % 2]).start(priority=kk % 2)
            return carry
        lax.fori_loop(lo, tq, body, 0, unroll=4)

    @pl.when(g % nj == 0)
    def _():
        scatter_rows(N_PAD)

    @pl.when(g % nj != 0)
    def _():
        scatter_rows(0)

    @pl.when(g == n_tiles - 1)
    def _():
        @pl.when(g >= 1)
        def _():
            wait_out(g - 1)
        wait_out(g)


def _dispatch(pstart, pend, pos, xn_t, n_rows, bsz, tq, tm):
    m = xn_t.shape[0] // TOK_ROWS
    nj = m // (bsz * tq)
    n_experts = pstart.shape[0]
    grid_spec = pltpu.PrefetchScalarGridSpec(
        num_scalar_prefetch=2,
        grid=(bsz * nj,),
        in_specs=[pl.BlockSpec((1, TOP_K, tq), lambda g, ps, pe: (g, 0, 0), memory_space=pltpu.SMEM),
                  pl.BlockSpec(memory_space=pl.ANY)],
        out_specs=pl.BlockSpec(memory_space=pl.ANY),
        scratch_shapes=[pltpu.VMEM((tm * TOK_ROWS, LANES), F32),
                        pltpu.VMEM((3, tq * TOK_ROWS, LANES), F32),
                        pltpu.SemaphoreType.DMA((3,)), pltpu.SemaphoreType.DMA((2,))],
    )
    return pl.pallas_call(
        functools.partial(_dispatch_kernel, n_experts=n_experts, tm=tm, nj=nj),
        grid_spec=grid_spec,
        out_shape=jax.ShapeDtypeStruct((n_rows * TOK_ROWS, LANES), F32),
        compiler_params=_cparams(("arbitrary",)),
        name="dispatch",
    )(pstart, pend, pos, xn_t)


CAST_ROWS = 256


def _experts_kernel(be_ref, nu_ref, slot_ref, nxt_ref, x_ref, wgu_hbm, bg_ref, bu_ref, wd_hbm, bd_ref, y_ref,
                    wg_s, wu_s, wd_s, wgu_buf, wd_buf, sem):
    i = pl.program_id(0)
    n_used = nu_ref[0]
    dff = wg_s.shape[0]
    new_expert = jnp.logical_or(i == 0, be_ref[i] != be_ref[jnp.maximum(i - 1, 0)])

    def weight_copies(e, slot):
        return (pltpu.make_async_copy(wgu_hbm.at[e], wgu_buf.at[slot], sem.at[slot]),
                pltpu.make_async_copy(wd_hbm.at[e], wd_buf.at[slot], sem.at[slot]))

    @pl.when(i == 0)
    def _():
        for cp in weight_copies(be_ref[0], 0):
            cp.start()

    for slot in range(2):
        @pl.when(jnp.logical_and(jnp.logical_and(new_expert, i < n_used), slot_ref[i] == slot))
        def _():
            for cp in weight_copies(be_ref[i], slot):
                cp.wait()

            @pl.when(nxt_ref[i] >= 0)
            def _():
                for cp in weight_copies(nxt_ref[i], 1 - slot):
                    cp.start()

            for c in range(dff // CAST_ROWS):
                rows = pl.ds(c * CAST_ROWS, CAST_ROWS)
                wd_s[rows, :] = wd_buf[slot, rows, :].astype(BF16)
                for sl in range(wgu_buf.shape[1]):
                    lanes = pl.ds(sl * LANES, LANES)
                    wg_s[rows, lanes] = wgu_buf[
                        slot, sl, pl.ds(2 * c * CAST_ROWS, CAST_ROWS, stride=2), :].astype(BF16)
                    wu_s[rows, lanes] = wgu_buf[
                        slot, sl, pl.ds(2 * c * CAST_ROWS + 1, CAST_ROWS, stride=2), :].astype(BF16)

    @pl.when(i < n_used)
    def _():
        tm = x_ref.shape[0] // TOK_ROWS
        xb = _load_token_tiles(x_ref, tm).astype(BF16)
        g = _dot_nt(xb, wg_s[...]) + bg_ref[0]
        u = _dot_nt(xb, wu_s[...]) + bu_ref[0]
        gate = jnp.minimum(g, SWIGLU_LIMIT)
        up = jnp.clip(u, -SWIGLU_LIMIT, SWIGLU_LIMIT)
        act = (up + 1.0) * gate * _sigmoid(SWIGLU_ALPHA * gate)
        _store_token_tiles(y_ref, _dot(act.astype(BF16), wd_s[...]) + bd_ref[0])

    @pl.when(i >= n_used)
    def _():
        y_ref[...] = jnp.zeros_like(y_ref)


def _experts(block_e, n_used, xs, w_gu_t, b_g, b_u, w_d, b_d, tm):
    n_blocks = block_e.shape[0]
    dff, d = w_d.shape[1:]
    tile_blk = lambda f: pl.BlockSpec((tm * TOK_ROWS, LANES), f)
    assert dff % CAST_ROWS == 0
    blk = jnp.arange(n_blocks, dtype=jnp.int32)
    changed = jnp.concatenate([jnp.zeros((1,), jnp.int32), (block_e[1:] != block_e[:-1]).astype(jnp.int32)])
    w_slot = jnp.cumsum(changed) % 2
    later = jnp.logical_and(jnp.logical_and(blk[None, :] > blk[:, None], blk[None, :] < n_used[0]),
                            block_e[None, :] != block_e[:, None])
    nxt_e = jnp.where(jnp.any(later, axis=1), block_e[jnp.argmax(later, axis=1)], -1).astype(jnp.int32)
    wspec = lambda k, n: pl.BlockSpec((1, k, n), lambda i, be, nu, ws, ne: (be[i], 0, 0))
    grid_spec = pltpu.PrefetchScalarGridSpec(
        num_scalar_prefetch=4,
        grid=(n_blocks,),
        in_specs=[
            tile_blk(lambda i, be, nu, ws, ne: (jnp.minimum(i, nu[0] - 1), 0)),
            pl.BlockSpec(memory_space=pl.ANY),
            wspec(1, dff), wspec(1, dff),
            pl.BlockSpec(memory_space=pl.ANY),
            wspec(1, d),
        ],
        out_specs=tile_blk(lambda i, be, nu, ws, ne: (i, 0)),
        scratch_shapes=[pltpu.VMEM((dff, d), BF16), pltpu.VMEM((dff, d), BF16), pltpu.VMEM((dff, d), BF16),
                        pltpu.VMEM((2,) + w_gu_t.shape[1:], F32), pltpu.VMEM((2, dff, d), F32),
                        pltpu.SemaphoreType.DMA((2,))],
    )
    return pl.pallas_call(
        _experts_kernel,
        grid_spec=grid_spec,
        out_shape=jax.ShapeDtypeStruct((n_blocks * tm * TOK_ROWS, LANES), F32),
        compiler_params=_cparams(("arbitrary",)),
        name="experts",
    )(block_e, n_used, w_slot, nxt_e, xs, w_gu_t, b_g, b_u, w_d, b_d)


def _combine_kernel(pos_ref, posn_ref, hsrc_ref, hsrcn_ref, tw_ref, y_hbm, h_hbm,
                    nw_ref, o_ref, ybuf, hbuf, sem):
    i = pl.program_id(0)
    n = pl.num_programs(0)
    tt = o_ref.shape[0]
    nt = tt * TOK_ROWS

    def copies(src_pos_ref, src_h_ref, dst_slot, r):
        return [pltpu.make_async_copy(
            y_hbm.at[_toks(src_pos_ref[0, kk, r]), :],
            ybuf.at[dst_slot, _toks(kk * tt + r), :],
            sem.at[dst_slot]) for kk in range(TOP_K)]

    def start_h(src_h_ref, dst_slot):
        h_row = pl.multiple_of(src_h_ref[0, 0, 0], 8)
        pltpu.make_async_copy(h_hbm.at[pl.ds(h_row, tt), :], hbuf.at[dst_slot], sem.at[dst_slot]).start()

    @pl.when(i == 0)
    def _():
        def body(r, carry):
            for kk, cp in enumerate(copies(pos_ref, hsrc_ref, 0, r)):
                cp.start(priority=kk % 2)
            return carry
        lax.fori_loop(0, tt, body, 0, unroll=2)
        start_h(hsrc_ref, 0)

    def wait_slot(s):
        pltpu.make_async_copy(ybuf.at[s], ybuf.at[s], sem.at[s]).wait()
        pltpu.make_async_copy(hbuf.at[s], hbuf.at[s], sem.at[s]).wait()

    def step(slot):
        for r in range(tt):
            for kk, cp in enumerate(copies(posn_ref, hsrcn_ref, 1 - slot, r)):
                cp.start(priority=kk % 2)
        start_h(hsrcn_ref, 1 - slot)

        wait_slot(slot)
        acc = hbuf[slot]
        tw = tw_ref[...]
        for kk in range(TOP_K):
            yk = jnp.concatenate(
                [ybuf[slot, pl.ds(kk * nt + s, tt, stride=TOK_ROWS), :] for s in range(TOK_ROWS)], axis=1)
            acc = acc + tw[:, kk:kk + 1] * yk
        ms = jnp.mean(acc * acc, axis=-1, keepdims=True)
        o_ref[...] = (acc * lax.rsqrt(ms + EPS)) * nw_ref[...]

        @pl.when(i == n - 1)
        def _():
            wait_slot(1 - slot)

    for parity in range(2):
        pl.when(i % 2 == parity)(functools.partial(step, parity))


def _combine(pos_seq, tw_seq, h_src, y_rows, h2, norm_w, tt):
    n_tiles = h_src.shape[0]
    d = h2.shape[1]
    last = n_tiles - 1
    nt = tt * TOK_ROWS
    cur = lambda i: (i, 0, 0)
    nxt = lambda i: (jnp.minimum(i + 1, last), 0, 0)
    idx_blk = lambda f: pl.BlockSpec((1, TOP_K, tt), f, memory_space=pltpu.SMEM)
    one_blk = lambda f: pl.BlockSpec((1, 1, 1), f, memory_space=pltpu.SMEM)
    return pl.pallas_call(
        _combine_kernel,
        grid=(n_tiles,),
        in_specs=[
            idx_blk(cur), idx_blk(nxt), one_blk(cur), one_blk(nxt),
            pl.BlockSpec((tt, TOP_K), lambda i: (i, 0)),
            pl.BlockSpec(memory_space=pl.ANY),
            pl.BlockSpec(memory_space=pl.ANY),
            pl.BlockSpec(norm_w.shape, lambda i: (0, 0)),
        ],
        out_specs=pl.BlockSpec((tt, d), lambda i: (i, 0)),
        scratch_shapes=[pltpu.VMEM((2, TOP_K * nt, LANES), F32), pltpu.VMEM((2, tt, d), F32),
                        pltpu.SemaphoreType.DMA((2,))],
        out_shape=jax.ShapeDtypeStruct((n_tiles * tt, d), F32),
        compiler_params=_cparams(("arbitrary",)),
        name="combine",
    )(pos_seq, pos_seq, h_src, h_src, tw_seq, y_rows, h2, norm_w)


def _pick(n, prefs):
    for p in prefs:
        if n % p == 0:
            return p
    raise ValueError(f"no tile in {prefs} divides {n}")


def kernel(x, meta_tokens, hg_lb_logits, norm_mix, w_in, hg_norm, ml_conv_w, ml_conv_b, ml_wq, ml_wk, ml_wv,
           ml_gate_b, ml_norm, ml_skip, w_branch_hg, w_branch_ml, w_out, norm_ffn, router_w, router_b,
           exp_w_gu, exp_b_gu, exp_w_down, exp_b_down, norm_final):
    bsz, seq, d = x.shape
    assert norm_mix.shape[0] == 1, "single-layer block"
    assert seq % CHUNK == 0 and d % LANES == 0
    t = CHUNK + seq
    m_rows = bsz * t
    n_experts = router_w.shape[-1]
    dff = exp_w_down.shape[2]
    assert n_experts <= LANES

    head = jnp.concatenate([jnp.zeros((N_PAD, d), x.dtype), meta_tokens.astype(x.dtype)], axis=0)
    x2d = x.reshape(bsz * seq, d)
    lower_bounds = jnp.cumsum(jax.nn.softmax(hg_lb_logits.astype(F32), axis=0), axis=0)

    w = w_in[0]
    n_hg = 4 * HG_W
    n_a = n_hg + 2 * ML_W
    w_hg = w[:, :n_hg].astype(BF16)
    w_m = jnp.pad(w[:, n_a:n_a + 2 * N_HEADS], ((0, 0), (0, LANES - 2 * N_HEADS)))
    w_ml = jnp.concatenate([w[:, n_hg:n_a], w_m], axis=1).astype(BF16)
    w_g = w[:, n_a + 2 * N_HEADS:].astype(BF16)
    gain = norm_mix[0][None]

    bb = _pick(bsz, (8, 4, 2, 1))
    tri = jnp.asarray(np.tril(np.ones((CHUNK, CHUNK), np.float32)), BF16)
    y_hg = _hgrn2(x, head, gain, w_hg, tri, lower_bounds[0][None], hg_norm[0][None], bb)

    gb = ml_gate_b[0].astype(F32)
    gb_col = jnp.pad(gb, (0, LANES - 2 * N_HEADS))[None]
    gb_row = jnp.broadcast_to(gb[:, None], (2 * N_HEADS, CHUNK))
    y_ml = _mlstm(x, head, gain, w_ml, tri, ml_conv_w[0], ml_conv_b[0][None],
                  ml_wq[0].astype(BF16), ml_wk[0].astype(BF16), ml_wv[0].astype(BF16),
                  gb_col, gb_row, ml_norm[0][None], ml_skip[0][None], bb)

    assert n_experts % 8 == 0
    tq = _pick(t, (1056, 704, 192, 64))
    rw = router_w[0].astype(F32).T
    rw1 = rw.astype(BF16)
    rw2 = (rw - rw1.astype(F32)).astype(BF16)
    rw_split = jnp.stack([rw1, rw2])
    rb = jnp.broadcast_to(router_b[0].astype(F32)[:, None], (n_experts, tq))
    h2, xn2, top_e, top_w, rank, cnt = _merge_route(
        x2d, head, gain, w_g, y_hg.reshape(m_rows, HG_W), y_ml.reshape(m_rows, ML_W),
        w_branch_hg[0].astype(BF16), w_branch_ml[0].astype(BF16), w_out[0].astype(BF16),
        norm_ffn[0][None], rw_split, rb, n_experts, bsz, tq)

    tm6 = 512
    n_assign = bsz * (t - N_PAD) * TOP_K
    counts = cnt[:, 0].astype(jnp.int32)
    padded = ((counts + tm6 - 1) // tm6) * tm6
    pend = jnp.cumsum(padded).astype(jnp.int32)
    pstart = pend - padded
    n_blocks = -(-n_assign // tm6) + n_experts
    n_rows = n_blocks * tm6
    blk_start = jnp.arange(n_blocks, dtype=jnp.int32) * tm6
    block_e = jnp.minimum(jnp.sum((blk_start[:, None] >= pend[None, :]).astype(jnp.int32), axis=1),
                          n_experts - 1)
    n_used = (pend[-1] // tm6)[None]
    pos = rank + jnp.sum(jnp.where(top_e[..., None] == jnp.arange(n_experts, dtype=jnp.int32), pstart, 0),
                         axis=-1)
    xs = _dispatch(pstart, pend, pos, xn2, n_rows, bsz, tq, tm6)

    w_gu_t = jnp.swapaxes(exp_w_gu[0].reshape(n_experts, d // LANES, LANES, 2 * dff), 2, 3)
    bgu = exp_b_gu[0]
    y_rows = _experts(block_e, n_used, xs, w_gu_t, bgu[:, None, 0::2], bgu[:, None, 1::2],
                      exp_w_down[0], exp_b_down[0][:, None, :], tm6)

    tt = _pick(seq, (256, 128, 64))
    tiles_per_b = seq // tt
    n_tiles = bsz * tiles_per_b
    nj = t // tq

    def seq_tiles(a):
        a = a.reshape(bsz, nj, TOP_K, tq).transpose(0, 2, 1, 3).reshape(bsz, TOP_K, t)[:, :, CHUNK:]
        return a.reshape(bsz, TOP_K, tiles_per_b, tt).transpose(0, 2, 1, 3).reshape(n_tiles, TOP_K, tt)

    h_src = (jnp.arange(bsz, dtype=jnp.int32)[:, None] * t + CHUNK
             + jnp.arange(tiles_per_b, dtype=jnp.int32)[None, :] * tt).reshape(n_tiles, 1, 1)
    tw_cols = seq_tiles(top_w).transpose(0, 2, 1).reshape(n_tiles * tt, TOP_K)
    out = _combine(seq_tiles(pos), tw_cols, h_src, y_rows, h2, norm_final[None], tt)
    return out.reshape(bsz, seq, d)
```

```python
import functools

import numpy as np
import jax
import jax.numpy as jnp
from jax import lax
from jax.experimental import pallas as pl
from jax.experimental.pallas import tpu as pltpu

F32 = jnp.float32
BF16 = jnp.bfloat16

N_META = 16
CHUNK = 64
N_PAD = CHUNK - N_META
EPS = 1e-6

N_HEADS = 4
HG_DK = 128
HG_DV = 128
HG_W = N_HEADS * HG_DV
ML_DK = 64
ML_DV = 128
ML_W = N_HEADS * ML_DV
ML_CONV = 4
TOP_K = 4
SWIGLU_LIMIT = 7.0
SWIGLU_ALPHA = 1.702

LANES = 128
VMEM_LIMIT_BYTES = 56 * 1024 * 1024

HG_LEVELS = (32, 16, 8, 4, 2, 1)
GROUP_ROWS = 8


def _cparams(sem):
    return pltpu.CompilerParams(dimension_semantics=sem, vmem_limit_bytes=VMEM_LIMIT_BYTES)


def _sigmoid(x):
    return 0.5 + 0.5 * jnp.tanh(0.5 * x)


def _split3(x):
    x1 = x.astype(BF16)
    r1 = x - x1.astype(F32)
    x2 = r1.astype(BF16)
    x3 = (r1 - x2.astype(F32)).astype(BF16)
    return x1, x2, x3


def _dot(a, b):
    return jnp.dot(a, b, preferred_element_type=F32)


TOK_ROWS = 8


def _store_token_tiles(ref, x):
    n = x.shape[0]
    for s in range(TOK_ROWS):
        ref[pl.ds(s, n, stride=TOK_ROWS), :] = x[:, s * LANES:(s + 1) * LANES]


def _load_token_tiles(ref, n):
    return jnp.concatenate([ref[pl.ds(s, n, stride=TOK_ROWS), :] for s in range(TOK_ROWS)], axis=1)


def _dot_nt(a, b):
    return lax.dot_general(a, b, (((1,), (1,)), ((), ())), preferred_element_type=F32)


def _dot_exact_lhs(m_bf16, x):
    x1, x2, x3 = _split3(x)
    return _dot(m_bf16, x1) + _dot(m_bf16, x2) + _dot(m_bf16, x3)


def _h_tile(x_ref, head_ref, j):
    x = x_ref[...]
    first = jnp.concatenate([head_ref[...], x[:x.shape[0] - CHUNK]], axis=0)
    return jnp.where(j == 0, first, x)


def _x_tile_spec(tr, d, seq):
    assert seq % 8 == 0 and tr % 8 == 0 and CHUNK % 8 == 0
    return pl.BlockSpec(
        (pl.Element(tr), pl.Element(d)),
        lambda b, j: (pl.multiple_of(b * seq + jnp.maximum(j * tr - CHUNK, 0), 8), 0))


def _rms_bf16(x, gain):
    ms = jnp.mean(x * x, axis=-1, keepdims=True)
    return ((x * lax.rsqrt(ms + EPS)) * gain).astype(BF16)


def _project_ahead(x_ref, gain_ref, w_ref, nxt_ref):
    x = x_ref[...].reshape(x_ref.shape[0] * CHUNK, x_ref.shape[2])
    nxt_ref[...] = _dot(_rms_bf16(x, gain_ref[...]), w_ref[...])


def _with_projection_pipeline(step, x_ref, head_ref, gain_ref, w_ref, proj_a, proj_b, c):
    @pl.when(c == 0)
    def _():
        p0 = _dot(_rms_bf16(head_ref[...], gain_ref[...]), w_ref[...])
        for b in range(x_ref.shape[0]):
            proj_a[b * CHUNK:(b + 1) * CHUNK, :] = p0

    for parity, (cur, nxt) in enumerate(((proj_a, proj_b), (proj_b, proj_a))):
        @pl.when(c % 2 == parity)
        def _():
            step(cur, functools.partial(_project_ahead, x_ref, gain_ref, w_ref, nxt))


def _chunk_ahead_spec(bb, d, nc):
    return pl.BlockSpec((bb, CHUNK, d), lambda i, c: (i, jnp.minimum(c, nc - 2), 0))


def _hgrn2_kernel(x_ref, head_ref, gain_ref, w_ref, tri_ref, lb_ref, nw_ref, y_ref, proj_a, proj_b, *st_refs):
    c = pl.program_id(1)

    @pl.when(c == 0)
    def _():
        for st_ref in st_refs:
            st_ref[...] = jnp.zeros_like(st_ref)

    def step(cur, project):
        bb = x_ref.shape[0]
        for b0 in range(0, bb, GROUP_ROWS):
            _hgrn2_group(c, range(b0, min(b0 + GROUP_ROWS, bb)), tri_ref, lb_ref, nw_ref, y_ref, st_refs, cur,
                         project if b0 == 0 else None)

    _with_projection_pipeline(step, x_ref, head_ref, gain_ref, w_ref, proj_a, proj_b, c)


def _hgrn2_group(c, bs, tri_ref, lb_ref, nw_ref, y_ref, st_refs, cur, project):
    row = lax.broadcasted_iota(jnp.int32, (CHUNK, 1), 0)
    valid = (c * CHUNK + row) >= N_PAD
    ti = lax.broadcasted_iota(jnp.int32, (CHUNK, CHUNK), 0)
    si = lax.broadcasted_iota(jnp.int32, (CHUNK, CHUNK), 1)
    diag_mask = ti == si
    level_masks = {}
    for m in HG_LEVELS:
        same_pair = (ti & ~(2 * m - 1)) == (si & ~(2 * m - 1))
        level_masks[m] = same_pair & ((ti & m) != 0) & ((si & m) == 0)

    lb = lb_ref[...]
    tri = tri_ref[...]
    per_b = {}
    for b in bs:
        rows_b = slice(b * CHUNK, (b + 1) * CHUNK)
        hf = cur[rows_b, HG_W:2 * HG_W]
        f = lb + (1.0 - lb) * _sigmoid(hf)
        f = jnp.where(valid, f, 1.0)
        logf = jnp.log(f)
        k_all = 1.0 - f
        hq = cur[rows_b, 0:HG_W]
        q_all = hq * _sigmoid(hq)
        b_cum = _dot_exact_lhs(tri, logf)
        e_b = jnp.exp(b_cum)
        e_bl = jnp.exp(b_cum[CHUNK - 1:CHUNK] - b_cum)

        q_fac, k_fac = {}, {}
        for m in HG_LEVELS:
            if m >= 4:
                grp = b_cum.reshape(CHUNK // (2 * m), 2 * m, HG_W)
                e = (grp - grp[:, m - 1:m, :]).reshape(CHUNK, HG_W)
                q_fac[m] = jnp.exp(jnp.minimum(e, 0.0))
                k_fac[m] = jnp.exp(jnp.minimum(-e, 0.0))
        f_prev = pltpu.roll(f, 1, 0)
        f_next = pltpu.roll(f, CHUNK - 1, 0)
        r4 = row & 3
        q_fac[2] = jnp.where(r4 == 2, f, jnp.where(r4 == 3, f * f_prev, 1.0))
        k_fac[2] = jnp.where(r4 == 0, f_next, 1.0)
        q_fac[1] = jnp.where((row & 1) == 1, f, 1.0)

        hg = cur[rows_b, 3 * HG_W:4 * HG_W]
        v_all = cur[rows_b, 2 * HG_W:3 * HG_W]
        q_fac = {m: v.astype(BF16) for m, v in q_fac.items()}
        k_fac = {m: v.astype(BF16) for m, v in k_fac.items()}
        per_b[b] = (q_all, k_all, q_fac, k_fac, e_b, e_bl, v_all, hg * _sigmoid(hg))

    if project is not None:
        project()

    pairs = [(b, h) for b in bs for h in range(N_HEADS)]
    sls = [slice(h * HG_DK, (h + 1) * HG_DK) for h in range(N_HEADS)]

    scores_all = []
    for b, h in pairs:
        q_all, k_all, q_fac, k_fac = per_b[b][:4]
        qb = q_all[:, sls[h]].astype(BF16)
        kb = k_all[:, sls[h]].astype(BF16)
        scores = jnp.where(diag_mask, _dot_nt(qb, kb), 0.0)
        for m in HG_LEVELS:
            qd = qb * q_fac[m][:, sls[h]]
            kd = kb * k_fac[m][:, sls[h]] if m in k_fac else kb
            scores = jnp.where(level_masks[m], _dot_nt(qd, kd), scores)
        scores_all.append(scores.astype(BF16))

    sts = [st_refs[b * N_HEADS + h][...] for b, h in pairs]
    outs = []
    for p, (b, h) in enumerate(pairs):
        q_all, e_b, v_all = per_b[b][0], per_b[b][4], per_b[b][6]
        qe = (q_all[:, sls[h]] * e_b[:, sls[h]]).astype(BF16)
        outs.append(_dot(scores_all[p], v_all[:, sls[h]].astype(BF16)) + _dot_nt(qe, sts[p].astype(BF16)))

    for p, (b, h) in enumerate(pairs):
        k_all, e_b, e_bl, v_all = per_b[b][1], per_b[b][4], per_b[b][5], per_b[b][6]
        kl = (k_all[:, sls[h]] * e_bl[:, sls[h]]).astype(BF16)
        vt = v_all[:, sls[h]].T.astype(BF16)
        st_refs[b * N_HEADS + h][...] = e_b[CHUNK - 1:CHUNK, sls[h]] * sts[p] + _dot(vt, kl)

    for p, (b, h) in enumerate(pairs):
        o = outs[p]
        ms = jnp.mean(o * o, axis=-1, keepdims=True)
        y = (o * lax.rsqrt(ms + EPS)) * nw_ref[:, sls[h]] * per_b[b][7][:, sls[h]]
        y_ref[b, :, sls[h]] = y.astype(y_ref.dtype)


def _hgrn2(x3, head, gain, w_hg, tri, lb, norm_w, bb):
    bsz, seq, d = x3.shape
    nc = seq // CHUNK + 1
    blk = pl.BlockSpec((bb, CHUNK, HG_W), lambda i, c: (i, c, 0))
    full = lambda a: pl.BlockSpec(a.shape, lambda i, c: (0, 0))
    return pl.pallas_call(
        _hgrn2_kernel,
        grid=(bsz // bb, nc),
        in_specs=[_chunk_ahead_spec(bb, d, nc), full(head), full(gain), full(w_hg), full(tri), full(lb),
                  full(norm_w)],
        out_specs=blk,
        out_shape=jax.ShapeDtypeStruct((bsz, nc * CHUNK, HG_W), BF16),
        scratch_shapes=([pltpu.VMEM((bb * CHUNK, w_hg.shape[1]), F32)] * 2
                        + [pltpu.VMEM((HG_DV, HG_DK), F32)] * (bb * N_HEADS)),
        compiler_params=_cparams(("parallel", "arbitrary")),
        name="hgrn2",
    )(x3, head, gain, w_hg, tri, lb, norm_w)


def _log_sigmoid(x):
    return jnp.minimum(x, 0.0) - jnp.log(1.0 + jnp.exp(-jnp.abs(x)))


def _mlstm_kernel(x_ref, head_ref, gain_ref, w_ref, tri_ref, cw_ref, cb_ref, wq_ref, wk_ref, wv_ref,
                  gbc_ref, gbr_ref, nw_ref, sk_ref, y_ref, proj_a, proj_b, *scratch):
    c = pl.program_id(1)

    @pl.when(c == 0)
    def _():
        for ref in scratch:
            ref[...] = jnp.zeros_like(ref)

    step = functools.partial(_mlstm_step, c, x_ref.shape[0], tri_ref, cw_ref, cb_ref, wq_ref, wk_ref, wv_ref,
                             gbc_ref, gbr_ref, nw_ref, sk_ref, y_ref, scratch)
    _with_projection_pipeline(step, x_ref, head_ref, gain_ref, w_ref, proj_a, proj_b, c)


def _mlstm_step(c, bb, tri_ref, cw_ref, cb_ref, wq_ref, wk_ref, wv_ref, gbc_ref, gbr_ref, nw_ref, sk_ref, y_ref,
                scratch, cur, project):
    n_pairs = bb * N_HEADS
    s_refs = scratch[:n_pairs]
    m_refs = scratch[n_pairs:2 * n_pairs]
    tail_refs = scratch[2 * n_pairs:]

    pos_c = c * CHUNK + lax.broadcasted_iota(jnp.int32, (CHUNK, 1), 0)
    valid_c = pos_c >= N_PAD
    pos_r = c * CHUNK + lax.broadcasted_iota(jnp.int32, (1, CHUNK), 1)
    valid_r = pos_r >= N_PAD
    ti = lax.broadcasted_iota(jnp.int32, (CHUNK, CHUNK), 0)
    si = lax.broadcasted_iota(jnp.int32, (CHUNK, CHUNK), 1)
    causal = si <= ti
    tri = tri_ref[...]
    ones_v = jnp.ones((CHUNK, ML_DV), BF16)
    neg_inf = -jnp.inf

    per_b = []
    for b in range(bb):
        rows_b = slice(b * CHUNK, (b + 1) * CHUNK)
        mm = jnp.where(valid_c, cur[rows_b, 0:ML_W], 0.0)
        ext = jnp.concatenate([tail_refs[b][...], mm], axis=0)
        tail_refs[b][...] = mm[CHUNK - 8:CHUNK]
        conv = cb_ref[...]
        for j in range(ML_CONV):
            off = 8 - (ML_CONV - 1) + j
            conv = conv + cw_ref[j:j + 1, :] * ext[off:off + CHUNK]
        cact = conv * _sigmoid(conv)
        cact_b = cact.astype(BF16)
        mm_b = mm.astype(BF16)

        graw = cur[rows_b, 2 * ML_W:2 * ML_W + LANES]
        gcol = graw + gbc_ref[...]
        li_col = jnp.where(valid_c, gcol, neg_inf)
        lf_col = jnp.where(valid_c, _log_sigmoid(gcol), 0.0)
        b_col = _dot_exact_lhs(tri, lf_col)
        grow = graw.T[0:2 * N_HEADS] + gbr_ref[...]
        li_row = jnp.where(valid_r, grow, neg_inf)
        lf_row = jnp.where(valid_r, _log_sigmoid(grow), 0.0)
        r1, r2, r3 = _split3(lf_row)
        b_row = _dot_nt(r1, tri) + _dot_nt(r2, tri) + _dot_nt(r3, tri)

        ogate = _sigmoid(cur[rows_b, ML_W:2 * ML_W])
        per_b.append((cact, cact_b, mm_b, li_col, b_col, li_row, b_row, ogate))

    project()

    pairs = [(b, h) for b in range(bb) for h in range(N_HEADS)]
    sls = [slice(h * ML_DV, (h + 1) * ML_DV) for h in range(N_HEADS)]

    qs, ks, vs = [], [], []
    for b, h in pairs:
        cact_b, mm_b = per_b[b][1], per_b[b][2]
        qs.append((_dot(cact_b[:, sls[h]], wq_ref[h]) * (ML_DK ** -0.5)).astype(BF16))
        ks.append(_dot(cact_b[:, sls[h]], wk_ref[h]))
        v = _dot(mm_b[:, sls[h]], wv_ref[h]).astype(BF16)
        vs.append(jnp.concatenate([v, ones_v], axis=1))

    n_p = len(pairs)
    blk = lambda p: slice(p * CHUNK, (p + 1) * CHUNK)
    stack = lambda xs: jnp.concatenate(xs, axis=0)
    bc_all = stack([per_b[b][4][:, N_HEADS + h:N_HEADS + h + 1] for b, h in pairs])
    lic_all = stack([per_b[b][3][:, h:h + 1] for b, h in pairs])
    row_all = stack([jnp.broadcast_to(per_b[b][5][h:h + 1, :] - per_b[b][6][N_HEADS + h:N_HEADS + h + 1, :],
                                      (CHUNK, CHUNK)) for b, h in pairs])
    mprev_all = stack([jnp.broadcast_to(m_refs[p][0:1, 0:1], (CHUNK, 1)) for p in range(n_p)])
    glast_all = stack([jnp.broadcast_to(per_b[b][4][CHUNK - 1:CHUNK, N_HEADS + h:N_HEADS + h + 1], (CHUNK, 1))
                       for b, h in pairs])
    causal_all = stack([causal] * n_p)

    d_all = jnp.where(causal_all, bc_all + row_all, neg_inf)
    a_all = bc_all + mprev_all
    m_t_all = jnp.maximum(a_all, jnp.max(d_all, axis=-1, keepdims=True))
    w_intra_all = jnp.exp(d_all - m_t_all)
    w_inter_all = jnp.exp(a_all - m_t_all)

    qk_all = (stack([_dot_nt(qs[p], ks[p].astype(BF16)) for p in range(n_p)]) * w_intra_all).astype(BF16)
    s_augs = [s_refs[p][...] for p in range(n_p)]
    intra_all = stack([_dot(qk_all[blk(p)], vs[p]) for p in range(n_p)])
    inter_all = stack([_dot(qs[p], s_augs[p].astype(BF16)) for p in range(n_p)])
    numden_all = intra_all + w_inter_all * inter_all
    o_all = numden_all[:, :ML_DV] / jnp.maximum(jnp.abs(numden_all[:, ML_DV:]), jnp.exp(-m_t_all))
    ms_all = jnp.mean(o_all * o_all, axis=-1, keepdims=True)
    on_all = o_all * lax.rsqrt(ms_all + EPS)

    e_all = glast_all - bc_all + lic_all
    gm_all = glast_all + mprev_all
    e_max = jnp.max(e_all.reshape(n_p, CHUNK, 1), axis=1, keepdims=True)
    m_new_all = jnp.maximum(gm_all.reshape(n_p, CHUNK, 1), e_max).reshape(n_p * CHUNK, 1)
    w_s_all = jnp.exp(e_all - m_new_all)
    w_p_all = jnp.exp(gm_all - m_new_all)
    kw_all = stack(ks) * w_s_all
    for p in range(n_p):
        kw_t = kw_all[blk(p)].T.astype(BF16)
        s_refs[p][...] = w_p_all[p * CHUNK:p * CHUNK + 1] * s_augs[p] + _dot(kw_t, vs[p])
        m_refs[p][...] = jnp.broadcast_to(m_new_all[p * CHUNK:p * CHUNK + 1], m_refs[p].shape)

    for p, (b, h) in enumerate(pairs):
        cact, ogate = per_b[b][0], per_b[b][7]
        sl = sls[h]
        y = (on_all[blk(p)] * nw_ref[:, sl] + sk_ref[:, sl] * cact[:, sl]) * ogate[:, sl]
        y_ref[b, :, sl] = y.astype(y_ref.dtype)


def _mlstm(x3, head, gain, w_ml, tri, conv_w, conv_b, wq, wk, wv, gb_col, gb_row, norm_w, skip, bb):
    bsz, seq, d = x3.shape
    nc = seq // CHUNK + 1
    blk = pl.BlockSpec((bb, CHUNK, ML_W), lambda i, c: (i, c, 0))

    def full(a):
        nd = a.ndim
        return pl.BlockSpec(a.shape, lambda i, c: (0,) * nd)

    params = (head, gain, w_ml, tri, conv_w, conv_b, wq, wk, wv, gb_col, gb_row, norm_w, skip)
    return pl.pallas_call(
        _mlstm_kernel,
        grid=(bsz // bb, nc),
        in_specs=[_chunk_ahead_spec(bb, d, nc)] + [full(p) for p in params],
        out_specs=blk,
        out_shape=jax.ShapeDtypeStruct((bsz, nc * CHUNK, ML_W), BF16),
        scratch_shapes=([pltpu.VMEM((bb * CHUNK, w_ml.shape[1]), F32)] * 2
                        + [pltpu.VMEM((ML_DK, 2 * ML_DV), F32)] * (bb * N_HEADS)
                        + [pltpu.VMEM((8, LANES), F32)] * (bb * N_HEADS)
                        + [pltpu.VMEM((8, ML_W), F32)] * bb),
        compiler_params=_cparams(("parallel", "arbitrary")),
        name="mlstm",
    )(x3, *params)


def _merge_route_kernel(x_ref, head_ref, gain_ref, wg_ref, yh_ref, ym_ref, wbh_ref, wbm_ref, wo_ref, nf_ref,
                        rw_ref, rb_ref, h2_ref, xn_ref, te_ref, tw_ref, rk_ref, cnt_ref, *, n_experts):
    d = x_ref.shape[1]
    tq = x_ref.shape[0]
    j = pl.program_id(1)

    @pl.when(jnp.logical_and(pl.program_id(0) == 0, j == 0))
    def _():
        cnt_ref[...] = jnp.zeros_like(cnt_ref)

    p_hg = _dot(yh_ref[...], wbh_ref[...])
    p_ml = _dot(ym_ref[...], wbm_ref[...])
    h = _h_tile(x_ref, head_ref, j)
    hn = _rms_bf16(h, gain_ref[...])
    g0 = _sigmoid(_dot(hn, wg_ref[:, :d]))
    g1 = _sigmoid(_dot(hn, wg_ref[:, d:]))
    merged = g0 * p_hg + g1 * p_ml
    h2 = h + _dot(merged.astype(BF16), wo_ref[...])
    h2_ref[...] = h2
    ms = jnp.mean(h2 * h2, axis=-1, keepdims=True)
    xn = (h2 * lax.rsqrt(ms + EPS)) * nf_ref[...]
    _store_token_tiles(xn_ref, xn)
    x1, x2, _ = _split3(xn)
    logits = (_dot_nt(rw_ref[0], x1) + _dot_nt(rw_ref[1], x1) + _dot_nt(rw_ref[0], x2)) + rb_ref[...]
    sub = lax.broadcasted_iota(jnp.int32, logits.shape, 0)
    work = logits
    vals, idxs = [], []
    for _ in range(TOP_K):
        vmax = jnp.max(work, axis=0, keepdims=True)
        imax = jnp.min(jnp.where(work == vmax, sub, n_experts), axis=0, keepdims=True)
        vals.append(vmax)
        idxs.append(imax)
        work = jnp.where(sub == imax, -jnp.inf, work)
    exps = [jnp.exp(v - vals[0]) for v in vals]
    tot = exps[0] + exps[1] + exps[2] + exps[3]

    krow = lax.broadcasted_iota(jnp.int32, (TOP_K, tq), 0)

    def rows(per_k):
        out = jnp.broadcast_to(per_k[0], (TOP_K, tq))
        for kk in range(1, TOP_K):
            out = jnp.where(krow == kk, per_k[kk], out)
        return out

    te_ref[0] = rows(idxs)
    tw_ref[0] = rows([e / tot for e in exps])

    valid = (j * tq + lax.broadcasted_iota(jnp.int32, (1, tq), 1)) >= N_PAD
    onehots = [jnp.where(jnp.logical_and(sub == idxs[kk], valid), 1.0, 0.0) for kk in range(TOP_K)]
    oh_all = onehots[0] + onehots[1] + onehots[2] + onehots[3]
    ri = lax.broadcasted_iota(jnp.int32, (tq, tq), 0)
    ci = lax.broadcasted_iota(jnp.int32, (tq, tq), 1)
    earlier = jnp.where(ri < ci, 1.0, 0.0).astype(BF16)
    cnt = cnt_ref[:, 0:1]
    before = _dot(oh_all.astype(BF16), earlier) + cnt
    ranks = []
    for kk in range(TOP_K):
        ranks.append(jnp.sum(jnp.where(sub == idxs[kk], before, 0.0), axis=0, keepdims=True))
        before = before + onehots[kk]
    rk_ref[0] = rows(ranks).astype(jnp.int32)
    cnt_ref[...] = jnp.broadcast_to(cnt + jnp.sum(oh_all, axis=1, keepdims=True), cnt_ref.shape)


def _merge_route(x2d, head, gain, w_g, y_hg, y_ml, wbh, wbm, wo, norm_ffn, rw_split, rb, n_experts, bsz, tq):
    d = x2d.shape[1]
    seq = x2d.shape[0] // bsz
    nj = (seq + CHUNK) // tq
    m = bsz * nj * tq
    assert d == TOK_ROWS * LANES, "token-tile layout assumes one (8,128) tile per token"
    row = lambda n: pl.BlockSpec((tq, n), lambda b, j: (b * nj + j, 0))
    tiles = pl.BlockSpec((tq * TOK_ROWS, LANES), lambda b, j: (b * nj + j, 0))
    per_tok = pl.BlockSpec((1, TOP_K, tq), lambda b, j: (b * nj + j, 0, 0))

    def full(a):
        nd = a.ndim
        return pl.BlockSpec(a.shape, lambda b, j: (0,) * nd)

    return pl.pallas_call(
        functools.partial(_merge_route_kernel, n_experts=n_experts),
        grid=(bsz, nj),
        in_specs=[_x_tile_spec(tq, d, seq), full(head), full(gain), full(w_g), row(HG_W), row(ML_W),
                  full(wbh), full(wbm), full(wo), full(norm_ffn), full(rw_split), full(rb)],
        out_specs=[row(d), tiles, per_tok, per_tok, per_tok,
                   pl.BlockSpec((n_experts, LANES), lambda b, j: (0, 0))],
        out_shape=[
            jax.ShapeDtypeStruct((m, d), F32),
            jax.ShapeDtypeStruct((m * TOK_ROWS, LANES), F32),
            jax.ShapeDtypeStruct((bsz * nj, TOP_K, tq), jnp.int32),
            jax.ShapeDtypeStruct((bsz * nj, TOP_K, tq), F32),
            jax.ShapeDtypeStruct((bsz * nj, TOP_K, tq), jnp.int32),
            jax.ShapeDtypeStruct((n_experts, LANES), F32),
        ],
        compiler_params=_cparams(("arbitrary", "arbitrary")),
        name="merge_route",
    )(x2d, head, gain, w_g, y_hg, y_ml, wbh, wbm, wo, norm_ffn, rw_split, rb)


def _toks(first, n=1):
    return pl.ds(pl.multiple_of(first * TOK_ROWS, TOK_ROWS), n * TOK_ROWS)


def _dispatch_kernel(ps_ref, pe_ref, pos_ref, x_hbm, xs_hbm, zbuf, stage, sem_in, sem_out, *, n_experts, tm, nj):
    g = pl.program_id(0)
    n_tiles = pl.num_programs(0)
    tq = stage.shape[1] // TOK_ROWS

    def tile_in(tile, slot):
        return pltpu.make_async_copy(x_hbm.at[_toks(tile * tq, tq), :], stage.at[slot], sem_in.at[slot])

    def wait_out(tile):
        for lo in (N_PAD, 0):
            @pl.when((tile % nj == 0) == (lo == N_PAD))
            def _():
                for _ in range(TOP_K):
                    pltpu.make_async_copy(stage.at[0, _toks(0, tq - lo), :], xs_hbm.at[_toks(0, tq - lo), :],
                                          sem_out.at[tile % 2]).wait()

    @pl.when(g == 0)
    def _():
        zbuf[...] = jnp.zeros_like(zbuf)
        for e in range(n_experts):
            @pl.when(pe_ref[e] > ps_ref[e])
            def _():
                pltpu.make_async_copy(zbuf, xs_hbm.at[_toks(pe_ref[e] - tm, tm), :], sem_out.at[0]).start()
        for e in range(n_experts):
            @pl.when(pe_ref[e] > ps_ref[e])
            def _():
                pltpu.make_async_copy(zbuf, xs_hbm.at[_toks(0, tm), :], sem_out.at[0]).wait()

        def zero_tail(blk, carry):
            cp = pltpu.make_async_copy(zbuf, xs_hbm.at[_toks(blk * tm, tm), :], sem_out.at[0])
            cp.start()
            cp.wait()
            return carry
        lax.fori_loop(pe_ref[n_experts - 1] // tm, xs_hbm.shape[0] // (tm * TOK_ROWS), zero_tail, 0)
        tile_in(0, 0).start()

    slot = g % 3

    @pl.when(g >= 2)
    def _():
        wait_out(g - 2)

    @pl.when(g + 1 < n_tiles)
    def _():
        tile_in(g + 1, (g + 1) % 3).start()

    tile_in(g, slot).wait()

    def scatter_rows(lo):
        def body(r, carry):
            for kk in range(TOP_K):
                dst = pos_ref[0, kk, r]
                pltpu.make_async_copy(stage.at[slot, _toks(r), :], xs_hbm.at[_toks(dst), :],
                                      sem_out.at[g % 2]).start(priority=kk % 2)
            return carry
        lax.fori_loop(lo, tq, body, 0, unroll=4)

    @pl.when(g % nj == 0)
    def _():
        scatter_rows(N_PAD)

    @pl.when(g % nj != 0)
    def _():
        scatter_rows(0)

    @pl.when(g == n_tiles - 1)
    def _():
        @pl.when(g >= 1)
        def _():
            wait_out(g - 1)
        wait_out(g)


def _dispatch(pstart, pend, pos, xn_t, n_rows, bsz, tq, tm):
    m = xn_t.shape[0] // TOK_ROWS
    nj = m // (bsz * tq)
    n_experts = pstart.shape[0]
    grid_spec = pltpu.PrefetchScalarGridSpec(
        num_scalar_prefetch=2,
        grid=(bsz * nj,),
        in_specs=[pl.BlockSpec((1, TOP_K, tq), lambda g, ps, pe: (g, 0, 0), memory_space=pltpu.SMEM),
                  pl.BlockSpec(memory_space=pl.ANY)],
        out_specs=pl.BlockSpec(memory_space=pl.ANY),
        scratch_shapes=[pltpu.VMEM((tm * TOK_ROWS, LANES), F32),
                        pltpu.VMEM((3, tq * TOK_ROWS, LANES), F32),
                        pltpu.SemaphoreType.DMA((3,)), pltpu.SemaphoreType.DMA((2,))],
    )
    return pl.pallas_call(
        functools.partial(_dispatch_kernel, n_experts=n_experts, tm=tm, nj=nj),
        grid_spec=grid_spec,
        out_shape=jax.ShapeDtypeStruct((n_rows * TOK_ROWS, LANES), F32),
        compiler_params=_cparams(("arbitrary",)),
        name="dispatch",
    )(pstart, pend, pos, xn_t)


CAST_ROWS = 256


def _experts_kernel(be_ref, nu_ref, slot_ref, nxt_ref, x_ref, wgu_hbm, bg_ref, bu_ref, wd_hbm, bd_ref, y_ref,
                    wg_s, wu_s, wd_s, wgu_buf, wd_buf, sem):
    i = pl.program_id(0)
    n_used = nu_ref[0]
    dff = wg_s.shape[0]
    new_expert = jnp.logical_or(i == 0, be_ref[i] != be_ref[jnp.maximum(i - 1, 0)])

    def weight_copies(e, slot):
        return (pltpu.make_async_copy(wgu_hbm.at[e], wgu_buf.at[slot], sem.at[slot]),
                pltpu.make_async_copy(wd_hbm.at[e], wd_buf.at[slot], sem.at[slot]))

    @pl.when(i == 0)
    def _():
        for cp in weight_copies(be_ref[0], 0):
            cp.start()

    for slot in range(2):
        @pl.when(jnp.logical_and(jnp.logical_and(new_expert, i < n_used), slot_ref[i] == slot))
        def _():
            for cp in weight_copies(be_ref[i], slot):
                cp.wait()

            @pl.when(nxt_ref[i] >= 0)
            def _():
                for cp in weight_copies(nxt_ref[i], 1 - slot):
                    cp.start()

            for c in range(dff // CAST_ROWS):
                rows = pl.ds(c * CAST_ROWS, CAST_ROWS)
                wd_s[rows, :] = wd_buf[slot, rows, :].astype(BF16)
                for sl in range(wgu_buf.shape[1]):
                    lanes = pl.ds(sl * LANES, LANES)
                    wg_s[rows, lanes] = wgu_buf[
                        slot, sl, pl.ds(2 * c * CAST_ROWS, CAST_ROWS, stride=2), :].astype(BF16)
                    wu_s[rows, lanes] = wgu_buf[
                        slot, sl, pl.ds(2 * c * CAST_ROWS + 1, CAST_ROWS, stride=2), :].astype(BF16)

    @pl.when(i < n_used)
    def _():
        tm = x_ref.shape[0] // TOK_ROWS
        xb = _load_token_tiles(x_ref, tm).astype(BF16)
        g = _dot_nt(xb, wg_s[...]) + bg_ref[0]
        u = _dot_nt(xb, wu_s[...]) + bu_ref[0]
        gate = jnp.minimum(g, SWIGLU_LIMIT)
        up = jnp.clip(u, -SWIGLU_LIMIT, SWIGLU_LIMIT)
        act = (up + 1.0) * gate * _sigmoid(SWIGLU_ALPHA * gate)
        _store_token_tiles(y_ref, _dot(act.astype(BF16), wd_s[...]) + bd_ref[0])

    @pl.when(i >= n_used)
    def _():
        y_ref[...] = jnp.zeros_like(y_ref)


def _experts(block_e, n_used, xs, w_gu_t, b_g, b_u, w_d, b_d, tm):
    n_blocks = block_e.shape[0]
    dff, d = w_d.shape[1:]
    tile_blk = lambda f: pl.BlockSpec((tm * TOK_ROWS, LANES), f)
    assert dff % CAST_ROWS == 0
    blk = jnp.arange(n_blocks, dtype=jnp.int32)
    changed = jnp.concatenate([jnp.zeros((1,), jnp.int32), (block_e[1:] != block_e[:-1]).astype(jnp.int32)])
    w_slot = jnp.cumsum(changed) % 2
    later = jnp.logical_and(jnp.logical_and(blk[None, :] > blk[:, None], blk[None, :] < n_used[0]),
                            block_e[None, :] != block_e[:, None])
    nxt_e = jnp.where(jnp.any(later, axis=1), block_e[jnp.argmax(later, axis=1)], -1).astype(jnp.int32)
    wspec = lambda k, n: pl.BlockSpec((1, k, n), lambda i, be, nu, ws, ne: (be[i], 0, 0))
    grid_spec = pltpu.PrefetchScalarGridSpec(
        num_scalar_prefetch=4,
        grid=(n_blocks,),
        in_specs=[
            tile_blk(lambda i, be, nu, ws, ne: (jnp.minimum(i, nu[0] - 1), 0)),
            pl.BlockSpec(memory_space=pl.ANY),
            wspec(1, dff), wspec(1, dff),
            pl.BlockSpec(memory_space=pl.ANY),
            wspec(1, d),
        ],
        out_specs=tile_blk(lambda i, be, nu, ws, ne: (i, 0)),
        scratch_shapes=[pltpu.VMEM((dff, d), BF16), pltpu.VMEM((dff, d), BF16), pltpu.VMEM((dff, d), BF16),
                        pltpu.VMEM((2,) + w_gu_t.shape[1:], F32), pltpu.VMEM((2, dff, d), F32),
                        pltpu.SemaphoreType.DMA((2,))],
    )
    return pl.pallas_call(
        _experts_kernel,
        grid_spec=grid_spec,
        out_shape=jax.ShapeDtypeStruct((n_blocks * tm * TOK_ROWS, LANES), F32),
        compiler_params=_cparams(("arbitrary",)),
        name="experts",
    )(block_e, n_used, w_slot, nxt_e, xs, w_gu_t, b_g, b_u, w_d, b_d)


N_SLOTS = 3


def _combine_kernel(pos_ref, pos1_ref, pos2_ref, hsrc_ref, hsrc1_ref, hsrc2_ref, tw_ref, y_hbm, h_hbm,
                    nw_ref, o_ref, ybuf, hbuf, sem):
    i = pl.program_id(0)
    n = pl.num_programs(0)
    tt = o_ref.shape[0]
    nt = tt * TOK_ROWS

    def copies(src_pos_ref, dst_slot, r):
        return [pltpu.make_async_copy(
            y_hbm.at[_toks(src_pos_ref[0, kk, r]), :],
            ybuf.at[dst_slot, _toks(kk * tt + r), :],
            sem.at[dst_slot]) for kk in range(TOP_K)]

    def start_h(src_h_ref, dst_slot):
        h_row = pl.multiple_of(src_h_ref[0, 0, 0], 8)
        pltpu.make_async_copy(h_hbm.at[pl.ds(h_row, tt), :], hbuf.at[dst_slot], sem.at[dst_slot]).start()

    @pl.when(i == 0)
    def _():
        for p_ref, h_ref, slot0 in ((pos_ref, hsrc_ref, 0), (pos1_ref, hsrc1_ref, 1)):
            def body(r, carry, p_ref=p_ref, slot0=slot0):
                for kk, cp in enumerate(copies(p_ref, slot0, r)):
                    cp.start(priority=kk % 2)
                return carry
            lax.fori_loop(0, tt, body, 0, unroll=2)
            start_h(h_ref, slot0)

    def wait_slot(s):
        pltpu.make_async_copy(ybuf.at[s], ybuf.at[s], sem.at[s]).wait()
        pltpu.make_async_copy(hbuf.at[s], hbuf.at[s], sem.at[s]).wait()

    def step(slot):
        wait_slot(slot)
        acc = hbuf[slot]
        tw = tw_ref[...]
        for kk in range(TOP_K):
            yk = jnp.concatenate(
                [ybuf[slot, pl.ds(kk * nt + s, tt, stride=TOK_ROWS), :] for s in range(TOK_ROWS)], axis=1)
            acc = acc + tw[:, kk:kk + 1] * yk
        ms = jnp.mean(acc * acc, axis=-1, keepdims=True)
        o_ref[...] = (acc * lax.rsqrt(ms + EPS)) * nw_ref[...]

        nxt = (slot + 2) % N_SLOTS
        for r in range(tt):
            for kk, cp in enumerate(copies(pos2_ref, nxt, r)):
                cp.start(priority=kk % 2)
        start_h(hsrc2_ref, nxt)

        @pl.when(i == n - 1)
        def _():
            wait_slot((slot + 1) % N_SLOTS)
            wait_slot(nxt)

    for rot in range(N_SLOTS):
        pl.when(i % N_SLOTS == rot)(functools.partial(step, rot))


def _combine(pos_seq, tw_seq, h_src, y_rows, h2, norm_w, tt):
    n_tiles = h_src.shape[0]
    assert n_tiles >= 2
    d = h2.shape[1]
    last = n_tiles - 1
    nt = tt * TOK_ROWS
    ahead = lambda k: (lambda i: (jnp.minimum(i + k, last), 0, 0))
    idx_blk = lambda f: pl.BlockSpec((1, TOP_K, tt), f, memory_space=pltpu.SMEM)
    one_blk = lambda f: pl.BlockSpec((1, 1, 1), f, memory_space=pltpu.SMEM)
    return pl.pallas_call(
        _combine_kernel,
        grid=(n_tiles,),
        in_specs=[
            idx_blk(ahead(0)), idx_blk(ahead(1)), idx_blk(ahead(2)),
            one_blk(ahead(0)), one_blk(ahead(1)), one_blk(ahead(2)),
            pl.BlockSpec((tt, TOP_K), lambda i: (i, 0)),
            pl.BlockSpec(memory_space=pl.ANY),
            pl.BlockSpec(memory_space=pl.ANY),
            pl.BlockSpec(norm_w.shape, lambda i: (0, 0)),
        ],
        out_specs=pl.BlockSpec((tt, d), lambda i: (i, 0)),
        scratch_shapes=[pltpu.VMEM((N_SLOTS, TOP_K * nt, LANES), F32), pltpu.VMEM((N_SLOTS, tt, d), F32),
                        pltpu.SemaphoreType.DMA((N_SLOTS,))],
        out_shape=jax.ShapeDtypeStruct((n_tiles * tt, d), F32),
        compiler_params=_cparams(("arbitrary",)),
        name="combine",
    )(pos_seq, pos_seq, pos_seq, h_src, h_src, h_src, tw_seq, y_rows, h2, norm_w)


def _pick(n, prefs):
    for p in prefs:
        if n % p == 0:
            return p
    raise ValueError(f"no tile in {prefs} divides {n}")


def kernel(x, meta_tokens, hg_lb_logits, norm_mix, w_in, hg_norm, ml_conv_w, ml_conv_b, ml_wq, ml_wk, ml_wv,
           ml_gate_b, ml_norm, ml_skip, w_branch_hg, w_branch_ml, w_out, norm_ffn, router_w, router_b,
           exp_w_gu, exp_b_gu, exp_w_down, exp_b_down, norm_final):
    bsz, seq, d = x.shape
    assert norm_mix.shape[0] == 1, "single-layer block"
    assert seq % CHUNK == 0 and d % LANES == 0
    t = CHUNK + seq
    m_rows = bsz * t
    n_experts = router_w.shape[-1]
    dff = exp_w_down.shape[2]
    assert n_experts <= LANES

    head = jnp.concatenate([jnp.zeros((N_PAD, d), x.dtype), meta_tokens.astype(x.dtype)], axis=0)
    x2d = x.reshape(bsz * seq, d)
    lower_bounds = jnp.cumsum(jax.nn.softmax(hg_lb_logits.astype(F32), axis=0), axis=0)

    w = w_in[0]
    n_hg = 4 * HG_W
    n_a = n_hg + 2 * ML_W
    w_hg = w[:, :n_hg].astype(BF16)
    w_m = jnp.pad(w[:, n_a:n_a + 2 * N_HEADS], ((0, 0), (0, LANES - 2 * N_HEADS)))
    w_ml = jnp.concatenate([w[:, n_hg:n_a], w_m], axis=1).astype(BF16)
    w_g = w[:, n_a + 2 * N_HEADS:].astype(BF16)
    gain = norm_mix[0][None]

    bb = _pick(bsz, (8, 4, 2, 1))
    tri = jnp.asarray(np.tril(np.ones((CHUNK, CHUNK), np.float32)), BF16)
    y_hg = _hgrn2(x, head, gain, w_hg, tri, lower_bounds[0][None], hg_norm[0][None], bb)

    gb = ml_gate_b[0].astype(F32)
    gb_col = jnp.pad(gb, (0, LANES - 2 * N_HEADS))[None]
    gb_row = jnp.broadcast_to(gb[:, None], (2 * N_HEADS, CHUNK))
    y_ml = _mlstm(x, head, gain, w_ml, tri, ml_conv_w[0], ml_conv_b[0][None],
                  ml_wq[0].astype(BF16), ml_wk[0].astype(BF16), ml_wv[0].astype(BF16),
                  gb_col, gb_row, ml_norm[0][None], ml_skip[0][None], bb)

    assert n_experts % 8 == 0
    tq = _pick(t, (1056, 704, 192, 64))
    rw = router_w[0].astype(F32).T
    rw1 = rw.astype(BF16)
    rw2 = (rw - rw1.astype(F32)).astype(BF16)
    rw_split = jnp.stack([rw1, rw2])
    rb = jnp.broadcast_to(router_b[0].astype(F32)[:, None], (n_experts, tq))
    h2, xn2, top_e, top_w, rank, cnt = _merge_route(
        x2d, head, gain, w_g, y_hg.reshape(m_rows, HG_W), y_ml.reshape(m_rows, ML_W),
        w_branch_hg[0].astype(BF16), w_branch_ml[0].astype(BF16), w_out[0].astype(BF16),
        norm_ffn[0][None], rw_split, rb, n_experts, bsz, tq)

    tm6 = 512
    n_assign = bsz * (t - N_PAD) * TOP_K
    counts = cnt[:, 0].astype(jnp.int32)
    padded = ((counts + tm6 - 1) // tm6) * tm6
    pend = jnp.cumsum(padded).astype(jnp.int32)
    pstart = pend - padded
    n_blocks = -(-n_assign // tm6) + n_experts
    n_rows = n_blocks * tm6
    blk_start = jnp.arange(n_blocks, dtype=jnp.int32) * tm6
    block_e = jnp.minimum(jnp.sum((blk_start[:, None] >= pend[None, :]).astype(jnp.int32), axis=1),
                          n_experts - 1)
    n_used = (pend[-1] // tm6)[None]
    pos = rank + jnp.sum(jnp.where(top_e[..., None] == jnp.arange(n_experts, dtype=jnp.int32), pstart, 0),
                         axis=-1)
    xs = _dispatch(pstart, pend, pos, xn2, n_rows, bsz, tq, tm6)

    w_gu_t = jnp.swapaxes(exp_w_gu[0].reshape(n_experts, d // LANES, LANES, 2 * dff), 2, 3)
    bgu = exp_b_gu[0]
    y_rows = _experts(block_e, n_used, xs, w_gu_t, bgu[:, None, 0::2], bgu[:, None, 1::2],
                      exp_w_down[0], exp_b_down[0][:, None, :], tm6)

    tt = _pick(seq, (256, 128, 64))
    tiles_per_b = seq // tt
    n_tiles = bsz * tiles_per_b
    nj = t // tq

    def seq_tiles(a):
        a = a.reshape(bsz, nj, TOP_K, tq).transpose(0, 2, 1, 3).reshape(bsz, TOP_K, t)[:, :, CHUNK:]
        return a.reshape(bsz, TOP_K, tiles_per_b, tt).transpose(0, 2, 1, 3).reshape(n_tiles, TOP_K, tt)

    h_src = (jnp.arange(bsz, dtype=jnp.int32)[:, None] * t + CHUNK
             + jnp.arange(tiles_per_b, dtype=jnp.int32)[None, :] * tt).reshape(n_tiles, 1, 1)
    tw_cols = seq_tiles(top_w).transpose(0, 2, 1).reshape(n_tiles * tt, TOP_K)
    out = _combine(seq_tiles(pos), tw_cols, h_src, y_rows, h2, norm_final[None], tt)
    return out.reshape(bsz, seq, d)
```
